```python
import math
import jax
import jax.numpy as jnp
from jax import lax
import numpy as np

D_MODEL = 1024
BATCH = 8
SEQ = 2048
DEPTH = 2
DEC_BATCH = 32
DEC_SEQ = 1
PAST_LEN = 8192
PAGE_SIZE = 128

N_A_LAYERS = (DEPTH + 1) // 2
N_C_LAYERS = DEPTH // 2
HEAD_DIM = 64
N_HEADS = D_MODEL // 128
N_KV = 2
Q_PER_KV = N_HEADS // N_KV
ROPE_THETA = 10000.0
CMP_STRIDE = 16
L_CMP = 2 * CMP_STRIDE
CMP_HIDDEN = 256
SLC_BLOCK = 64
N_SELECT = 16
WINDOW = 512
WIN_BLOCK = 128
SLC_Q_BLOCK = 64
FORCE_SCORE = 1e9
NEG_BIG = -1e30
SSD_HEAD_DIM = 64
SSD_HEADS = D_MODEL // 128
SSD_INNER = SSD_HEADS * SSD_HEAD_DIM
SSD_GROUPS = 2
SSD_STATE = 128
SSD_CHUNK = 128
SSD_CONV_DIM = SSD_INNER + 2 * SSD_GROUPS * SSD_STATE
CONV_WIDTH = 4
LRU_WIDTH = (4 * D_MODEL // 3) // 128 * 128
LRU_BLOCK = 128
LRU_HEADS = LRU_WIDTH // LRU_BLOCK
LRU_C = 8.0
FFN_HIDDEN = -(-(8 * D_MODEL // 3) // 256) * 256
Q_DIM = N_HEADS * HEAD_DIM
KV_DIM = 2 * N_KV * HEAD_DIM
GATE_DIM = 3 * N_HEADS
A_SPLITS = (Q_DIM, KV_DIM, KV_DIM, KV_DIM, GATE_DIM, SSD_INNER, SSD_CONV_DIM, SSD_HEADS)
IN_A_DIM = Q_DIM + 3 * KV_DIM + GATE_DIM + SSD_INNER + SSD_CONV_DIM + SSD_HEADS
RMS_EPS = 1e-6

kernel_name = 'nsa_ssd_rglru_hybrid_step'


def split_last(x, sizes):
    points = [int(p) for p in np.cumsum(sizes)[:-1]]
    return jnp.split(x, points, axis=-1)


def rmsnorm(x, g):
    xf = x.astype(jnp.float32)
    y = xf * lax.rsqrt(jnp.mean(xf * xf, axis=-1, keepdims=True) + RMS_EPS)
    return (y * g.astype(jnp.float32)).astype(x.dtype)


def masked_softmax(s, mask):
    s = jnp.where(mask, s.astype(jnp.float32), NEG_BIG)
    p = jax.nn.softmax(s, axis=-1)
    return jnp.where(mask, p, 0.0)


def rope(x, pos):
    half = HEAD_DIM // 2
    inv_freq = ROPE_THETA ** (-jnp.arange(half, dtype=jnp.float32) / half)
    ang = pos.astype(jnp.float32)[:, None] * inv_freq[None, :]
    shape = (1, pos.shape[0]) + (1,) * (x.ndim - 3) + (half,)
    cos = jnp.cos(ang).reshape(shape)
    sin = jnp.sin(ang).reshape(shape)
    xf = x.astype(jnp.float32)
    x1, x2 = xf[..., :half], xf[..., half:]
    return jnp.concatenate([x1 * cos - x2 * sin, x2 * cos + x1 * sin], axis=-1).astype(x.dtype)


def causal_conv(u_full, w, b_conv):
    T = u_full.shape[1] - (CONV_WIDTH - 1)
    out = u_full[:, 0:T] * w[0]
    for k in range(1, CONV_WIDTH):
        out = out + u_full[:, k:k + T] * w[k]
    return out + b_conv


def compress_rows(rows, pe, w1, w2):
    b, t_keys = rows.shape[:2]
    n_half = t_keys // CMP_STRIDE
    halves = rows[:, :n_half * CMP_STRIDE].reshape(b, n_half, CMP_STRIDE, N_KV, HEAD_DIM)
    w1 = w1.reshape(2, CMP_STRIDE, HEAD_DIM, CMP_HIDDEN)
    pe = pe.reshape(2, CMP_STRIDE, HEAD_DIM)
    lead = jnp.einsum('bnsgd,sdf->bngf', halves[:, :-1] + pe[0][:, None, :], w1[0])
    trail = jnp.einsum('bnsgd,sdf->bngf', halves[:, 1:] + pe[1][:, None, :], w1[1])
    return jnp.einsum('bngf,fd->bngd', jax.nn.silu(lead + trail), w2)


def nsa_compressed(q, pos_q, k_cmp, v_cmp):
    n_cmp = k_cmp.shape[1]
    s = jnp.einsum('bqgrd,bngd->bgrqn', q, k_cmp) * HEAD_DIM ** -0.5
    end = jnp.arange(n_cmp) * CMP_STRIDE + (L_CMP - 1)
    mask = end[None, :] <= pos_q[:, None]
    p = masked_softmax(s, mask)
    o = jnp.einsum('bgrqn,bngd->bqgrd', p.astype(v_cmp.dtype), v_cmp)
    return o, p


def select_blocks(p_cmp, pos_q, n_sel):
    n_cmp = p_cmp.shape[-1]
    c_start = jnp.arange(n_cmp) * CMP_STRIDE
    s_start = jnp.arange(n_sel) * SLC_BLOCK
    overlap = ((c_start[:, None] < s_start[None, :] + SLC_BLOCK)
               & (c_start[:, None] + L_CMP > s_start[None, :])).astype(jnp.float32)
    imp = jnp.einsum('bgrqn,nj->bgqj', p_cmp, overlap)
    j = jnp.arange(n_sel)[None, :]
    cur = (pos_q // SLC_BLOCK)[:, None]
    valid = s_start[None, :] <= pos_q[:, None]
    forced = (j == 0) | (j == cur) | (j == cur - 1)
    imp = jnp.where(valid & forced, FORCE_SCORE, imp)
    imp = jnp.where(valid, imp, -FORCE_SCORE)
    _, idx = lax.top_k(imp, min(N_SELECT, n_sel))
    return idx


def nsa_selected(q, pos_q, idx, k_blk, v_blk):
    b, g, tq, k = idx.shape
    bi = jnp.arange(b)[:, None, None, None]
    gi = jnp.arange(g)[None, :, None, None]
    kg = k_blk[bi, gi, idx].reshape(b, g, tq, k * SLC_BLOCK, HEAD_DIM)
    vg = v_blk[bi, gi, idx].reshape(b, g, tq, k * SLC_BLOCK, HEAD_DIM)
    key_pos = (idx[..., None] * SLC_BLOCK + jnp.arange(SLC_BLOCK)).reshape(b, g, tq, k * SLC_BLOCK)
    mask = key_pos <= pos_q[None, None, :, None]
    s = jnp.einsum('bqgrd,bgqkd->bgrqk', q, kg) * HEAD_DIM ** -0.5
    p = masked_softmax(s, mask[:, :, None])
    return jnp.einsum('bgrqk,bgqkd->bqgrd', p.astype(vg.dtype), vg)


def window_band(q, k, v, q_pos, k_pos):
    s = jnp.einsum('bnqgrd,bnkgd->bngrqk', q, k) * HEAD_DIM ** -0.5
    dist = q_pos[:, :, None] - k_pos[:, None, :]
    mask = (dist >= 0) & (dist < WINDOW) & (k_pos >= 0)[:, None, :]
    p = masked_softmax(s, mask[None, :, None, None])
    return jnp.einsum('bngrqk,bnkgd->bnqgrd', p.astype(v.dtype), v)


def segsum(a):
    l = a.shape[-1]
    cs = jnp.cumsum(a, axis=-1)
    diff = cs[..., :, None] - cs[..., None, :]
    return jnp.where(jnp.tril(jnp.ones((l, l), dtype=bool)), diff, -jnp.inf)


def ssd_scan(x, a, bm, cm, h0):
    b, T, h, pd = x.shape
    lc = SSD_CHUNK if T % SSD_CHUNK == 0 else T
    nc = T // lc
    xc = x.reshape(b, nc, lc, h, pd)
    bc = bm.reshape(b, nc, lc, h, SSD_STATE)
    cc = cm.reshape(b, nc, lc, h, SSD_STATE)
    ac = a.reshape(b, nc, lc, h).transpose(0, 3, 1, 2)
    a_cs = jnp.cumsum(ac, axis=-1)
    scores = jnp.einsum('bclhn,bcshn->bhcls', cc, bc) * jnp.exp(segsum(ac))
    y_diag = jnp.einsum('bhcls,bcshp->bclhp', scores, xc)
    decay_states = jnp.exp(a_cs[..., -1:] - a_cs)
    states = jnp.einsum('bclhn,bhcl,bclhp->bchpn', bc, decay_states, xc)
    states = jnp.concatenate([h0[:, None], states], axis=1)
    decay_chunk = jnp.exp(segsum(jnp.pad(a_cs[..., -1], ((0, 0), (0, 0), (1, 0)))))
    new_states = jnp.einsum('bhzc,bchpn->bzhpn', decay_chunk, states)
    states, h_final = new_states[:, :-1], new_states[:, -1]
    y_off = jnp.einsum('bclhn,bchpn,bhcl->bclhp', cc, states, jnp.exp(a_cs))
    return (y_diag + y_off).reshape(b, T, h, pd), h_final


def linear_scan(a, bx, h0):
    def combine(left, right):
        return (left[0] * right[0], right[0] * left[1] + right[1])
    a_cum, b_cum = lax.associative_scan(combine, (a, bx), axis=1)
    return a_cum * h0[:, None] + b_cum


def swiglu(x, w_gate, w_up, w_down):
    return (jax.nn.silu(x @ w_gate) * (x @ w_up)) @ w_down


def mixer_a(xn, pos, past, w_in, w_out, cmp_pe_k, cmp_w1_k, cmp_w2_k, cmp_pe_v, cmp_w1_v, cmp_w2_v,
            conv_w, conv_b, dt_bias, a_log, d_skip, ssd_norm):
    b, T, _ = xn.shape
    dtype = xn.dtype
    q, kvc, kvs, kvw, gate, z, xbc, dt = split_last(xn @ w_in, A_SPLITS)
    q = q.reshape(b, T, N_KV, Q_PER_KV, HEAD_DIM)
    q_rot = rope(q, pos)
    kvc = kvc.reshape(b, T, 2, N_KV, HEAD_DIM)
    kvs = kvs.reshape(b, T, 2, N_KV, HEAD_DIM)
    kvs = jnp.stack([rope(kvs[:, :, 0], pos), kvs[:, :, 1]], axis=2)
    kvw = kvw.reshape(b, T, 2, N_KV, HEAD_DIM)
    kvw = jnp.stack([rope(kvw[:, :, 0], pos), kvw[:, :, 1]], axis=2)
    if past is None:
        kvc_all, kvs_all = kvc, kvs
        conv_buf = jnp.zeros((b, CONV_WIDTH - 1, SSD_CONV_DIM), dtype)
        h0 = jnp.zeros((b, SSD_HEADS, SSD_HEAD_DIM, SSD_STATE), jnp.float32)
    else:
        past_c, past_s, win_buf, h0, conv_buf = past
        kvc_all = jnp.concatenate([past_c, kvc], axis=1)
        kvs_all = jnp.concatenate([past_s, kvs], axis=1)
    t_keys = kvc_all.shape[1]
    k_cmp = compress_rows(kvc_all[:, :, 0], cmp_pe_k, cmp_w1_k, cmp_w2_k)
    v_cmp = compress_rows(kvc_all[:, :, 1], cmp_pe_v, cmp_w1_v, cmp_w2_v)
    o_cmp, p_cmp = nsa_compressed(q, pos, k_cmp, v_cmp)
    n_sel = -(-t_keys // SLC_BLOCK)
    idx = select_blocks(p_cmp, pos, n_sel)
    kvs_pad = jnp.pad(kvs_all, ((0, 0), (0, n_sel * SLC_BLOCK - t_keys), (0, 0), (0, 0), (0, 0)))
    blocks = kvs_pad.reshape(b, n_sel, SLC_BLOCK, 2, N_KV, HEAD_DIM).transpose(3, 0, 4, 1, 2, 5)
    k_blk, v_blk = blocks[0], blocks[1]
    if past is None:
        nqb = T // SLC_Q_BLOCK
        qb = q_rot.reshape(b, nqb, SLC_Q_BLOCK, N_KV, Q_PER_KV, HEAD_DIM).transpose(1, 0, 2, 3, 4, 5)
        pb = pos.reshape(nqb, SLC_Q_BLOCK)
        ib = idx.reshape(b, N_KV, nqb, SLC_Q_BLOCK, idx.shape[-1]).transpose(2, 0, 1, 3, 4)
        o_slc = lax.map(lambda a: nsa_selected(a[0], a[1], a[2], k_blk, v_blk), (qb, pb, ib))
        o_slc = o_slc.transpose(1, 0, 2, 3, 4, 5).reshape(b, T, N_KV, Q_PER_KV, HEAD_DIM)
        nb = T // WIN_BLOCK
        n_prev = WINDOW // WIN_BLOCK
        kvw_pad = jnp.pad(kvw, ((0, 0), (n_prev * WIN_BLOCK, 0), (0, 0), (0, 0), (0, 0)))
        kvw_pad = kvw_pad.reshape(b, nb + n_prev, WIN_BLOCK, 2, N_KV, HEAD_DIM)
        band = jnp.concatenate([kvw_pad[:, o:o + nb] for o in range(n_prev + 1)], axis=2)
        k_pos = (jnp.arange(nb)[:, None] - n_prev) * WIN_BLOCK + jnp.arange((n_prev + 1) * WIN_BLOCK)[None, :]
        o_win = window_band(q_rot.reshape(b, nb, WIN_BLOCK, N_KV, Q_PER_KV, HEAD_DIM),
                            band[:, :, :, 0], band[:, :, :, 1], pos.reshape(nb, WIN_BLOCK), k_pos)
        win_new = kvw[:, T - min(WINDOW, T):]
    else:
        o_slc = nsa_selected(q_rot, pos, idx, k_blk, v_blk)
        w_buf = win_buf.shape[1]
        kvw_all = jnp.concatenate([win_buf, kvw], axis=1)
        k_pos = pos[0] - w_buf + jnp.arange(w_buf + T)
        o_win = window_band(q_rot[:, None], kvw_all[:, None, :, 0], kvw_all[:, None, :, 1], pos[None], k_pos[None])
        win_new = kvw_all[:, T:]
    o_win = o_win.reshape(b, T, N_KV, Q_PER_KV, HEAD_DIM)
    g = jax.nn.sigmoid(gate.reshape(b, T, N_KV, Q_PER_KV, 3).astype(jnp.float32)).astype(dtype)
    o_nsa = (g[..., 0:1] * o_cmp + g[..., 1:2] * o_slc + g[..., 2:3] * o_win).reshape(b, T, Q_DIM)
    xbc_full = jnp.concatenate([conv_buf.astype(dtype), xbc], axis=1)
    conv_new = xbc_full[:, -(CONV_WIDTH - 1):]
    xbc_c = jax.nn.silu(causal_conv(xbc_full, conv_w, conv_b))
    xs, bm, cm = split_last(xbc_c, (SSD_INNER, SSD_GROUPS * SSD_STATE, SSD_GROUPS * SSD_STATE))
    xh = xs.reshape(b, T, SSD_HEADS, SSD_HEAD_DIM).astype(jnp.float32)
    rep = SSD_HEADS // SSD_GROUPS
    bm = jnp.repeat(bm.reshape(b, T, SSD_GROUPS, SSD_STATE), rep, axis=2).astype(jnp.float32)
    cm = jnp.repeat(cm.reshape(b, T, SSD_GROUPS, SSD_STATE), rep, axis=2).astype(jnp.float32)
    dt = jax.nn.softplus(dt.astype(jnp.float32) + dt_bias.astype(jnp.float32))
    a_head = -jnp.exp(a_log.astype(jnp.float32))
    y, h_new = ssd_scan(xh * dt[..., None], dt * a_head, bm, cm, h0.astype(jnp.float32))
    y = y + d_skip.astype(jnp.float32)[:, None] * xh
    y = y.reshape(b, T, SSD_INNER) * jax.nn.silu(z.astype(jnp.float32))
    y_ssd = rmsnorm(y, ssd_norm).astype(dtype)
    out = jnp.concatenate([o_nsa, y_ssd], axis=-1) @ w_out
    return out, (kvc, kvs, win_new, h_new, conv_new)


def mixer_c(xn, past, w_in_c, conv_w, conv_b, w_a, b_a, w_x, b_x, lam, w_out_c):
    b, T, _ = xn.shape
    dtype = xn.dtype
    gate_br, x_br = jnp.split(xn @ w_in_c, 2, axis=-1)
    if past is None:
        h0 = jnp.zeros((b, LRU_WIDTH), jnp.float32)
        conv_buf = jnp.zeros((b, CONV_WIDTH - 1, LRU_WIDTH), dtype)
    else:
        h0, conv_buf = past
    x_full = jnp.concatenate([conv_buf.astype(dtype), x_br], axis=1)
    conv_new = x_full[:, -(CONV_WIDTH - 1):]
    xc = causal_conv(x_full, conv_w, conv_b).astype(jnp.float32)
    xblk = xc.reshape(b, T, LRU_HEADS, LRU_BLOCK)
    r = jax.nn.sigmoid(jnp.einsum('bthi,hij->bthj', xblk, w_a.astype(jnp.float32)).reshape(b, T, LRU_WIDTH) + b_a)
    i = jax.nn.sigmoid(jnp.einsum('bthi,hij->bthj', xblk, w_x.astype(jnp.float32)).reshape(b, T, LRU_WIDTH) + b_x)
    log_a = -LRU_C * r * jax.nn.softplus(-lam.astype(jnp.float32))
    a = jnp.exp(log_a)
    gx = jnp.sqrt(-jnp.expm1(2.0 * log_a)) * (i * xc)
    h = linear_scan(a, gx, h0.astype(jnp.float32))
    y = (jax.nn.gelu(gate_br.astype(jnp.float32)) * h).astype(dtype) @ w_out_c
    return y, (h[:, -1], conv_new)


def setup_inputs(seed: int = 0) -> dict:
    key = jax.random.key(seed)
    keys = iter(jax.random.split(key, 64))

    def nrm(shape, scale):
        return jax.random.normal(next(keys), shape, jnp.float32) * scale

    def unif(shape, lo, hi):
        return jax.random.uniform(next(keys), shape, jnp.float32, lo, hi)

    n_pages = PAST_LEN // PAGE_SIZE
    n_used = DEC_BATCH * n_pages
    n_phys = n_used + max(1, n_used // 4)
    w_buf = min(WINDOW, PAST_LEN)
    page_table = jax.random.permutation(next(keys), n_phys)[:n_used].reshape(DEC_BATCH, n_pages).astype(jnp.int32)
    dt0 = jnp.exp(unif((N_A_LAYERS, SSD_HEADS), math.log(1e-3), math.log(1e-1)))
    a0 = unif((N_C_LAYERS, LRU_WIDTH), 0.9, 0.999)
    return {
        'x_prompt': nrm((BATCH, SEQ, D_MODEL), 1.0),
        'x_sample': nrm((DEC_BATCH, DEC_SEQ, D_MODEL), 1.0),
        'cache_kv_cmp': nrm((N_A_LAYERS, n_phys, PAGE_SIZE, 2, N_KV, HEAD_DIM), 1.0),
        'cache_kv_slc': nrm((N_A_LAYERS, n_phys, PAGE_SIZE, 2, N_KV, HEAD_DIM), 1.0),
        'cache_kv_win': nrm((N_A_LAYERS, DEC_BATCH, w_buf, 2, N_KV, HEAD_DIM), 1.0),
        'state_ssm': nrm((N_A_LAYERS, DEC_BATCH, SSD_HEADS, SSD_HEAD_DIM, SSD_STATE), 0.3),
        'state_ssd_conv': nrm((N_A_LAYERS, DEC_BATCH, CONV_WIDTH - 1, SSD_CONV_DIM), 1.0),
        'state_lru': nrm((N_C_LAYERS, DEC_BATCH, LRU_WIDTH), 0.5),
        'state_lru_conv': nrm((N_C_LAYERS, DEC_BATCH, CONV_WIDTH - 1, LRU_WIDTH), 1.0),
        'page_table': page_table,
        'norm_mix': 1.0 + nrm((DEPTH, D_MODEL), 0.01),
        'norm_ffn': 1.0 + nrm((DEPTH, D_MODEL), 0.01),
        'norm_final': 1.0 + nrm((D_MODEL,), 0.01),
        'w_ffn_gate': nrm((DEPTH, D_MODEL, FFN_HIDDEN), D_MODEL ** -0.5),
        'w_ffn_up': nrm((DEPTH, D_MODEL, FFN_HIDDEN), D_MODEL ** -0.5),
        'w_ffn_down': nrm((DEPTH, FFN_HIDDEN, D_MODEL), FFN_HIDDEN ** -0.5),
        'w_in_a': nrm((N_A_LAYERS, D_MODEL, IN_A_DIM), D_MODEL ** -0.5),
        'w_out_a': nrm((N_A_LAYERS, Q_DIM + SSD_INNER, D_MODEL), (Q_DIM + SSD_INNER) ** -0.5),
        'cmp_pe_k': nrm((N_A_LAYERS, L_CMP, HEAD_DIM), 0.1),
        'cmp_w1_k': nrm((N_A_LAYERS, L_CMP * HEAD_DIM, CMP_HIDDEN), (L_CMP * HEAD_DIM) ** -0.5),
        'cmp_w2_k': nrm((N_A_LAYERS, CMP_HIDDEN, HEAD_DIM), CMP_HIDDEN ** -0.5),
        'cmp_pe_v': nrm((N_A_LAYERS, L_CMP, HEAD_DIM), 0.1),
        'cmp_w1_v': nrm((N_A_LAYERS, L_CMP * HEAD_DIM, CMP_HIDDEN), (L_CMP * HEAD_DIM) ** -0.5),
        'cmp_w2_v': nrm((N_A_LAYERS, CMP_HIDDEN, HEAD_DIM), CMP_HIDDEN ** -0.5),
        'ssd_conv_w': nrm((N_A_LAYERS, CONV_WIDTH, SSD_CONV_DIM), CONV_WIDTH ** -0.5),
        'ssd_conv_b': nrm((N_A_LAYERS, SSD_CONV_DIM), 0.1),
        'ssd_dt_bias': dt0 + jnp.log(-jnp.expm1(-dt0)),
        'ssd_a_log': jnp.log(unif((N_A_LAYERS, SSD_HEADS), 1.0, 16.0)),
        'ssd_d': 1.0 + nrm((N_A_LAYERS, SSD_HEADS), 0.1),
        'ssd_norm': 1.0 + nrm((N_A_LAYERS, SSD_INNER), 0.01),
        'w_in_c': nrm((N_C_LAYERS, D_MODEL, 2 * LRU_WIDTH), D_MODEL ** -0.5),
        'lru_conv_w': nrm((N_C_LAYERS, CONV_WIDTH, LRU_WIDTH), CONV_WIDTH ** -0.5),
        'lru_conv_b': nrm((N_C_LAYERS, LRU_WIDTH), 0.1),
        'lru_w_a': nrm((N_C_LAYERS, LRU_HEADS, LRU_BLOCK, LRU_BLOCK), LRU_BLOCK ** -0.5),
        'lru_b_a': nrm((N_C_LAYERS, LRU_WIDTH), 0.1),
        'lru_w_x': nrm((N_C_LAYERS, LRU_HEADS, LRU_BLOCK, LRU_BLOCK), LRU_BLOCK ** -0.5),
        'lru_b_x': nrm((N_C_LAYERS, LRU_WIDTH), 0.1),
        'lru_lambda': jnp.log(a0) - jnp.log1p(-a0),
        'w_out_c': nrm((N_C_LAYERS, LRU_WIDTH, D_MODEL), LRU_WIDTH ** -0.5),
    }


def reference(x_prompt, x_sample, cache_kv_cmp, cache_kv_slc, cache_kv_win, state_ssm, state_ssd_conv,
              state_lru, state_lru_conv, page_table, norm_mix, norm_ffn, norm_final, w_ffn_gate, w_ffn_up,
              w_ffn_down, w_in_a, w_out_a, cmp_pe_k, cmp_w1_k, cmp_w2_k, cmp_pe_v, cmp_w1_v, cmp_w2_v,
              ssd_conv_w, ssd_conv_b, ssd_dt_bias, ssd_a_log, ssd_d, ssd_norm, w_in_c, lru_conv_w, lru_conv_b,
              lru_w_a, lru_b_a, lru_w_x, lru_b_x, lru_lambda, w_out_c):
    T = x_prompt.shape[1]
    db, S = x_sample.shape[:2]
    n_pages = page_table.shape[1]
    past_len = n_pages * PAGE_SIZE
    pos_p = jnp.arange(T, dtype=jnp.int32)
    pos_s = past_len + jnp.arange(S, dtype=jnp.int32)
    xp, xs = x_prompt, x_sample
    cmp_p, slc_p, win_p, ssm_p, sconv_p, lru_p, lconv_p = [], [], [], [], [], [], []
    cmp_s, slc_s, win_s, ssm_s, sconv_s, lru_s, lconv_s = [], [], [], [], [], [], []
    for layer in range(DEPTH):
        hp = rmsnorm(xp, norm_mix[layer])
        hs = rmsnorm(xs, norm_mix[layer])
        if layer % 2 == 0:
            ia = layer // 2
            wa = (w_in_a[ia], w_out_a[ia], cmp_pe_k[ia], cmp_w1_k[ia], cmp_w2_k[ia], cmp_pe_v[ia], cmp_w1_v[ia],
                  cmp_w2_v[ia], ssd_conv_w[ia], ssd_conv_b[ia], ssd_dt_bias[ia], ssd_a_log[ia], ssd_d[ia], ssd_norm[ia])
            mp, (c1, c2, c3, c4, c5) = mixer_a(hp, pos_p, None, *wa)
            past_c = cache_kv_cmp[ia][page_table].reshape(db, past_len, 2, N_KV, HEAD_DIM)
            past_s = cache_kv_slc[ia][page_table].reshape(db, past_len, 2, N_KV, HEAD_DIM)
            past = (past_c, past_s, cache_kv_win[ia], state_ssm[ia], state_ssd_conv[ia])
            ms, (d1, d2, d3, d4, d5) = mixer_a(hs, pos_s, past, *wa)
            cmp_p.append(c1); slc_p.append(c2); win_p.append(c3); ssm_p.append(c4); sconv_p.append(c5)
            cmp_s.append(d1); slc_s.append(d2); win_s.append(d3); ssm_s.append(d4); sconv_s.append(d5)
        else:
            ic = layer // 2
            wc = (w_in_c[ic], lru_conv_w[ic], lru_conv_b[ic], lru_w_a[ic], lru_b_a[ic], lru_w_x[ic], lru_b_x[ic],
                  lru_lambda[ic], w_out_c[ic])
            mp, (e1, e2) = mixer_c(hp, None, *wc)
            ms, (f1, f2) = mixer_c(hs, (state_lru[ic], state_lru_conv[ic]), *wc)
            lru_p.append(e1); lconv_p.append(e2)
            lru_s.append(f1); lconv_s.append(f2)
        xp = xp + mp
        xs = xs + ms
        xp = xp + swiglu(rmsnorm(xp, norm_ffn[layer]), w_ffn_gate[layer], w_ffn_up[layer], w_ffn_down[layer])
        xs = xs + swiglu(rmsnorm(xs, norm_ffn[layer]), w_ffn_gate[layer], w_ffn_up[layer], w_ffn_down[layer])
    y_prompt = rmsnorm(xp, norm_final)
    y_sample = rmsnorm(xs, norm_final)
    return (y_prompt, y_sample,
            jnp.stack(cmp_p), jnp.stack(slc_p), jnp.stack(win_p), jnp.stack(ssm_p), jnp.stack(sconv_p),
            jnp.stack(lru_p), jnp.stack(lconv_p),
            jnp.stack(cmp_s), jnp.stack(slc_s), jnp.stack(win_s), jnp.stack(ssm_s), jnp.stack(sconv_s),
            jnp.stack(lru_s), jnp.stack(lconv_s))
```

```python
import functools
import math

import jax
import jax.numpy as jnp
from jax import lax
from jax.experimental import pallas as pl
from jax.experimental.pallas import tpu as pltpu

BF = jnp.bfloat16
F32 = jnp.float32

HEAD_DIM = 64
N_KV = 2
Q_PER_KV = 4
N_HEADS = N_KV * Q_PER_KV
CMP_STRIDE = 16
CMP_HIDDEN = 256
SLC_BLOCK = 64
N_SELECT = 16
WINDOW = 512
PAGE_SIZE = 128
ROPE_THETA = 10000.0
FORCE_SCORE = 1e9
NEG_BIG = -1e30
SSD_HEADS = 8
SSD_HEAD_DIM = 64
SSD_STATE = 128
SSD_INNER = SSD_HEADS * SSD_HEAD_DIM
SSD_CHUNK = 128
CONV_WIDTH = 4
LRU_BLOCK = 128
LRU_C = 8.0
RMS_EPS = 1e-6
SCALE = HEAD_DIM ** -0.5
Q_DIM = N_HEADS * HEAD_DIM
KV_DIM = 2 * N_KV * HEAD_DIM
GATE_DIM = 3 * N_HEADS
DT_COL = GATE_DIM
LANES = 128
VMEM_LIMIT = 56 * 1024 * 1024


def _cparams(sem):
    return pltpu.CompilerParams(dimension_semantics=sem, vmem_limit_bytes=VMEM_LIMIT)


def _dot(a, b):
    return jnp.dot(a, b, preferred_element_type=F32)


def _dot_nt(a, b):
    return lax.dot_general(a, b, (((1,), (1,)), ((), ())), preferred_element_type=F32)


def _dot_tn(a, b):
    return lax.dot_general(a, b, (((0,), (0,)), ((), ())), preferred_element_type=F32)


def _split3(x):
    hi = x.astype(BF)
    r = x - hi.astype(F32)
    mid = r.astype(BF)
    lo = (r - mid.astype(F32)).astype(BF)
    return hi, mid, lo


def _dot3(x, m01):
    hi, mid, lo = _split3(x)
    return _dot(hi, m01) + _dot(mid, m01) + _dot(lo, m01)


def _dot3_left(m01, x):
    hi, mid, lo = _split3(x)
    return _dot(m01, hi) + _dot(m01, mid) + _dot(m01, lo)


def _rms(x, g):
    y = x * lax.rsqrt(jnp.mean(x * x, axis=-1, keepdims=True) + RMS_EPS)
    return y * g


def _silu(x):
    return x * jax.nn.sigmoid(x)


def _softplus(x):
    return jnp.maximum(x, 0.0) + jnp.log1p(jnp.exp(-jnp.abs(x)))


def _msoftmax_parts(s, mask):
    s = jnp.where(mask, s, NEG_BIG)
    m = jnp.max(s, axis=-1, keepdims=True)
    e = jnp.where(mask, jnp.exp(s - m), 0.0)
    return m, e


def _msoftmax(s, mask):
    _, e = _msoftmax_parts(s, mask)
    d = jnp.sum(e, axis=-1, keepdims=True)
    return e / jnp.where(d > 0.0, d, 1.0)


def _rope_tables(pos):
    half = HEAD_DIM // 2
    inv_freq = ROPE_THETA ** (-jnp.arange(half, dtype=F32) / half)
    ang = pos.astype(F32)[:, None] * inv_freq[None, :]
    cos = jnp.cos(ang)
    sin = jnp.sin(ang)
    cos2 = jnp.tile(jnp.concatenate([cos, cos], axis=-1), (1, LANES // HEAD_DIM))
    sin2 = jnp.tile(jnp.concatenate([-sin, sin], axis=-1), (1, LANES // HEAD_DIM))
    return cos2, sin2


def _inproj_a_kernel(x_ref, g_ref, wm_ref, ws_ref, cos_ref, sin_ref,
                     q_ref, qr_ref, kvc_ref, kvs_ref, kvw_ref, zx_ref, sm_ref):
    xn = _rms(x_ref[...], g_ref[...]).astype(BF)
    cos = cos_ref[...]
    sin = sin_ref[...]
    lane = lax.broadcasted_iota(jnp.int32, cos.shape, 1)
    first = (lane % HEAD_DIM) < (HEAD_DIM // 2)

    def rope(v):
        rot = jnp.where(first, pltpu.roll(v, LANES - HEAD_DIM // 2, 1), pltpu.roll(v, HEAD_DIM // 2, 1))
        return v * cos + rot * sin

    q = _dot(xn, wm_ref[:, 0:Q_DIM])
    q_ref[...] = q
    for c in range(Q_DIM // LANES):
        qr_ref[:, c * LANES:(c + 1) * LANES] = rope(q[:, c * LANES:(c + 1) * LANES])
    o = Q_DIM
    kvc_ref[...] = _dot(xn, wm_ref[:, o:o + KV_DIM])
    o += KV_DIM
    kvs = _dot(xn, wm_ref[:, o:o + KV_DIM])
    kvs_ref[:, 0:LANES] = rope(kvs[:, 0:LANES])
    kvs_ref[:, LANES:KV_DIM] = kvs[:, LANES:KV_DIM]
    o += KV_DIM
    kvw = _dot(xn, wm_ref[:, o:o + KV_DIM])
    kvw_ref[:, 0:LANES] = rope(kvw[:, 0:LANES])
    kvw_ref[:, LANES:KV_DIM] = kvw[:, LANES:KV_DIM]
    o += KV_DIM
    zx_ref[...] = _dot(xn, wm_ref[:, o:])
    sm_ref[...] = _dot(xn, ws_ref[...])


def _prep_w_in_a(w_in_a):
    a = Q_DIM + 3 * KV_DIM
    gate = w_in_a[:, a:a + GATE_DIM]
    rest = w_in_a[:, a + GATE_DIM:]
    zx_w = rest[:, :rest.shape[1] - SSD_HEADS]
    dt = rest[:, rest.shape[1] - SSD_HEADS:]
    main = jnp.concatenate([w_in_a[:, :a], zx_w], axis=1).astype(BF)
    small = jnp.concatenate([gate, dt], axis=1)
    small = jnp.pad(small, ((0, 0), (0, LANES - small.shape[1]))).astype(BF)
    return main, small


def _inproj_a(x2d, g, wm, ws, cos2, sin2, tm, n_pos_blocks):
    m, d = x2d.shape
    zx_dim = wm.shape[1] - Q_DIM - 3 * KV_DIM
    row = lambda i: (i, 0)
    const = lambda i: (0, 0)
    outs = [(Q_DIM, F32), (Q_DIM, F32), (KV_DIM, F32), (KV_DIM, F32), (KV_DIM, F32), (zx_dim, F32), (LANES, F32)]
    return pl.pallas_call(
        _inproj_a_kernel,
        grid=(m // tm,),
        in_specs=[
            pl.BlockSpec((tm, d), row),
            pl.BlockSpec((1, d), const),
            pl.BlockSpec(wm.shape, const),
            pl.BlockSpec(ws.shape, const),
            pl.BlockSpec((tm, LANES), lambda i: (i % n_pos_blocks, 0)),
            pl.BlockSpec((tm, LANES), lambda i: (i % n_pos_blocks, 0)),
        ],
        out_specs=[pl.BlockSpec((tm, w), row) for w, _ in outs],
        out_shape=[jax.ShapeDtypeStruct((m, w), dt) for w, dt in outs],
        compiler_params=_cparams(("parallel",)),
        name="inproj_a",
    )(x2d, g.reshape(1, d), wm, ws, cos2, sin2)


def _compress_accumulate(load_rows, pe_ref, w1_ref, col):
    a0 = None
    a1 = None
    for s in range(CMP_STRIDE):
        x = load_rows(s)[:, col:col + HEAD_DIM]
        l = _dot((x + pe_ref[s:s + 1, :]).astype(BF), w1_ref[s])
        t = _dot((x + pe_ref[CMP_STRIDE + s:CMP_STRIDE + s + 1, :]).astype(BF), w1_ref[CMP_STRIDE + s])
        a0 = l if a0 is None else a0 + l
        a1 = t if a1 is None else a1 + t
    return a0, a1


def _compress_prompt_kernel(kc_ref, vc_ref, pek_ref, w1k_ref, w2k_ref, pev_ref, w1v_ref, w2v_ref, out_ref, *,
                            n_half):
    branches = ((kc_ref, pek_ref, w1k_ref, w2k_ref), (vc_ref, pev_ref, w1v_ref, w2v_ref))
    for c, (src_ref, pe_ref, w1_ref, w2_ref) in enumerate(branches):
        def load_rows(s, src_ref=src_ref):
            return src_ref[pl.ds(s, n_half, stride=CMP_STRIDE), :]

        for g in range(N_KV):
            a0, a1 = _compress_accumulate(load_rows, pe_ref, w1_ref, g * HEAD_DIM)
            pre = a0 + pltpu.roll(a1, n_half - 1, 0)
            tok = _dot(_silu(pre).astype(BF), w2_ref[...])
            out_ref[g, :, c * HEAD_DIM:(c + 1) * HEAD_DIM] = tok


def _compress_prompt(kvc2d, b, t, pek, w1k, w2k, pev, w1v, w2v):
    n_half = t // CMP_STRIDE
    full = lambda a: pl.BlockSpec(a.shape, lambda i: (0,) * a.ndim)
    return pl.pallas_call(
        functools.partial(_compress_prompt_kernel, n_half=n_half),
        grid=(b,),
        in_specs=[pl.BlockSpec((t, LANES), lambda i: (i, 0)), pl.BlockSpec((t, LANES), lambda i: (i, 1)),
                  full(pek), full(w1k), full(w2k), full(pev), full(w1v), full(w2v)],
        out_specs=pl.BlockSpec((None, N_KV, n_half, LANES), lambda i: (i, 0, 0, 0)),
        out_shape=jax.ShapeDtypeStruct((b, N_KV, n_half, LANES), F32),
        compiler_params=_cparams(("parallel",)),
        name="compress_prompt",
    )(kvc2d, kvc2d, pek, w1k, w2k, pev, w1v, w2v)


def _prep_cmp_w(pe, w1, w2):
    return pe, w1.reshape(2 * CMP_STRIDE, HEAD_DIM, CMP_HIDDEN).astype(BF), w2.astype(BF)


def _rank_select(imp, n_rows):
    tq = imp.shape[0]
    imp_t = imp.T[0:n_rows, :]
    jrow = lax.broadcasted_iota(jnp.int32, (n_rows, tq), 0)
    cnt = jnp.zeros((n_rows, tq), F32)
    for jp in range(n_rows):
        row = imp_t[jp:jp + 1, :]
        ahead = jnp.where(row == imp_t, jnp.where(jrow > jp, 1.0, 0.0), jnp.where(row > imp_t, 1.0, 0.0))
        cnt = cnt + ahead
    sel_t = jnp.where(cnt < float(N_SELECT), 1.0, 0.0)
    sel_t = jnp.concatenate([sel_t, jnp.zeros((LANES - n_rows, tq), F32)], axis=0)
    return sel_t.T


def _nsa_prompt_kernel(q_ref, qr_ref, cmp_ref, kvs_ref, kvw_ref, sm_ref, ov_ref, e_ref, o_ref, *, t_len, tq):
    t0 = pl.program_id(1) * tq
    trow = t0 + lax.broadcasted_iota(jnp.int32, (tq, 1), 0)
    gates = jax.nn.sigmoid(sm_ref[...])
    lane = lax.broadcasted_iota(jnp.int32, (tq, LANES), 1)
    cmask = (lane * CMP_STRIDE + (2 * CMP_STRIDE - 1)) <= trow
    n_sel = t_len // SLC_BLOCK
    valid = (lane * SLC_BLOCK) <= trow
    cur = trow // SLC_BLOCK
    forced = (lane == 0) | (lane == cur) | (lane == cur - 1)
    kpos = lax.broadcasted_iota(jnp.int32, (tq, t_len), 1)
    causal = kpos <= trow
    n_win = WINDOW + tq
    start = pl.multiple_of(jnp.clip(t0 - WINDOW, 0, t_len - n_win), LANES)
    wpos = start + lax.broadcasted_iota(jnp.int32, (tq, n_win), 1)
    dist = trow - wpos
    wmask = (dist >= 0) & (dist < WINDOW)

    for g in range(N_KV):
        kc = cmp_ref[g, :, 0:HEAD_DIM].astype(BF)
        vc = cmp_ref[g, :, HEAD_DIM:LANES].astype(BF)
        psum = jnp.zeros((tq, LANES), F32)
        o_cmp = []
        for r in range(Q_PER_KV):
            c0 = (g * Q_PER_KV + r) * HEAD_DIM
            qh = q_ref[:, c0:c0 + HEAD_DIM].astype(BF)
            p = _msoftmax(_dot_nt(qh, kc) * SCALE, cmask)
            psum = psum + p
            o_cmp.append(_dot(p.astype(BF), vc))
        imp = _dot3(psum, ov_ref[...])
        imp = jnp.where(valid, jnp.where(forced, FORCE_SCORE, imp), -FORCE_SCORE)
        sel = _rank_select(imp, n_sel)
        sel_k = _dot(sel.astype(BF), e_ref[...])
        smask = causal & (sel_k > 0.5)
        ks = kvs_ref[:, g * HEAD_DIM:(g + 1) * HEAD_DIM].astype(BF)
        vs = kvs_ref[:, LANES + g * HEAD_DIM:LANES + (g + 1) * HEAD_DIM].astype(BF)
        kw = kvw_ref[pl.ds(start, n_win), g * HEAD_DIM:(g + 1) * HEAD_DIM].astype(BF)
        vw = kvw_ref[pl.ds(start, n_win), LANES + g * HEAD_DIM:LANES + (g + 1) * HEAD_DIM].astype(BF)
        for r in range(Q_PER_KV):
            c0 = (g * Q_PER_KV + r) * HEAD_DIM
            qh = qr_ref[:, c0:c0 + HEAD_DIM].astype(BF)
            p = _msoftmax(_dot_nt(qh, ks) * SCALE, smask)
            o_slc = _dot(p.astype(BF), vs)
            p = _msoftmax(_dot_nt(qh, kw) * SCALE, wmask)
            o_win = _dot(p.astype(BF), vw)
            gc = (g * Q_PER_KV + r) * 3
            o = (gates[:, gc:gc + 1] * o_cmp[r] + gates[:, gc + 1:gc + 2] * o_slc
                 + gates[:, gc + 2:gc + 3] * o_win)
            o_ref[:, c0:c0 + HEAD_DIM] = o


def _overlap_matrix(n_rows, n_cols, row_shift):
    n = jnp.arange(n_rows)[:, None] - row_shift
    c_start = n * CMP_STRIDE
    s_start = jnp.arange(n_cols)[None, :] * SLC_BLOCK
    ov = (c_start < s_start + SLC_BLOCK) & (c_start + 2 * CMP_STRIDE > s_start) & (n >= 0)
    return ov.astype(BF)


def _nsa_prompt(q, qr, cmp, kvs, kvw, small, b, t):
    tq = 128
    nq = t // tq
    n_half = t // CMP_STRIDE
    assert n_half == LANES and t // SLC_BLOCK <= LANES
    ov = _overlap_matrix(LANES, LANES, 0)
    expand = (jnp.arange(LANES)[:, None] == (jnp.arange(t)[None, :] // SLC_BLOCK)).astype(BF)
    tile = lambda w: pl.BlockSpec((tq, w), lambda i, j: (i * nq + j, 0))
    seq = lambda w: pl.BlockSpec((t, w), lambda i, j: (i, 0))
    return pl.pallas_call(
        functools.partial(_nsa_prompt_kernel, t_len=t, tq=tq),
        grid=(b, nq),
        in_specs=[tile(Q_DIM), tile(Q_DIM),
                  pl.BlockSpec((None, N_KV, n_half, LANES), lambda i, j: (i, 0, 0, 0)),
                  seq(KV_DIM), seq(KV_DIM), tile(LANES),
                  pl.BlockSpec(ov.shape, lambda i, j: (0, 0)),
                  pl.BlockSpec(expand.shape, lambda i, j: (0, 0))],
        out_specs=tile(Q_DIM),
        out_shape=jax.ShapeDtypeStruct((b * t, Q_DIM), F32),
        compiler_params=_cparams(("parallel", "parallel")),
        name="nsa_prompt",
    )(q, qr, cmp, kvs, kvw, small, ov, expand)


def _ssd_prompt_kernel(zx_ref, sm_ref, cw_ref, cb_ref, dtb_ref, alog_ref, dskip_ref, nrm_ref, tri_ref,
                       y_ref, hout_ref, tail_ref, h_ref):
    c = pl.program_id(1)
    lc = SSD_CHUNK

    @pl.when(c == 0)
    def _():
        tail_ref[...] = jnp.zeros_like(tail_ref)
        h_ref[...] = jnp.zeros_like(h_ref)

    z = zx_ref[:, 0:SSD_INNER]
    xbc = zx_ref[:, SSD_INNER:]
    u = jnp.concatenate([tail_ref[...], xbc], axis=0)
    tail_ref[...] = xbc[lc - 8:lc, :]
    conv = cb_ref[...]
    for k in range(CONV_WIDTH):
        conv = conv + u[5 + k:5 + k + lc, :] * cw_ref[k:k + 1, :]
    xbc_c = _silu(conv)
    xs = xbc_c[:, 0:SSD_INNER]

    dt_full = _softplus(sm_ref[...] + dtb_ref[...])
    a_full = dt_full * (-jnp.exp(alog_ref[...]))
    acs = _dot3_left(tri_ref[...], a_full)
    acs_t = acs.T
    li = lax.broadcasted_iota(jnp.int32, (lc, lc), 0)
    si = lax.broadcasted_iota(jnp.int32, (lc, lc), 1)
    lower = li >= si

    ys = []
    for g in range(2):
        bm = xbc_c[:, SSD_INNER + g * SSD_STATE:SSD_INNER + (g + 1) * SSD_STATE]
        cm = xbc_c[:, SSD_INNER + (2 + g) * SSD_STATE:SSD_INNER + (3 + g) * SSD_STATE]
        bm_b = bm.astype(BF)
        cm_b = cm.astype(BF)
        cb = _dot_nt(cm_b, bm_b)
        for hh in range(SSD_HEADS // 2):
            h = g * (SSD_HEADS // 2) + hh
            col = DT_COL + h
            dt_h = dt_full[:, col:col + 1]
            acs_h = acs[:, col:col + 1]
            acs_row = acs_t[col:col + 1, :]
            acs_last = acs[lc - 1:lc, col:col + 1]
            lmat = jnp.where(lower, jnp.exp(acs_h - acs_row), 0.0)
            x_h = xs[:, h * SSD_HEAD_DIM:(h + 1) * SSD_HEAD_DIM]
            xdt = x_h * dt_h
            y_diag = _dot((cb * lmat).astype(BF), xdt.astype(BF))
            h_prev = h_ref[h]
            y_off = _dot_nt(cm_b, h_prev.astype(BF)) * jnp.exp(acs_h)
            decay = jnp.exp(acs_last - acs_h)
            h_new = jnp.exp(acs_last) * h_prev + _dot_tn((xdt * decay).astype(BF), bm_b)
            h_ref[h] = h_new
            ys.append(y_diag + y_off)
    y = jnp.concatenate(ys, axis=1) + dskip_ref[...] * xs
    y = y * _silu(z)
    y_ref[...] = _rms(y, nrm_ref[...])
    hout_ref[...] = h_ref[...]


def _ssd_params(conv_w, conv_b, dt_bias, a_log, d_skip, ssd_norm):
    pad = lambda v: jnp.pad(v, (DT_COL, LANES - DT_COL - SSD_HEADS)).reshape(1, LANES)
    return (conv_w, conv_b.reshape(1, -1), pad(dt_bias), pad(a_log),
            jnp.repeat(d_skip, SSD_HEAD_DIM).reshape(1, SSD_INNER), ssd_norm.reshape(1, SSD_INNER))


def _ssd_prompt(zx, small, params, b, t):
    lc = SSD_CHUNK
    nc = t // lc
    cw, cb, dtb, alog, dskip, nrm = params
    tri = (jnp.arange(lc)[:, None] >= jnp.arange(lc)[None, :]).astype(BF)
    zx_dim = zx.shape[1]
    conv_dim = zx_dim - SSD_INNER
    full = lambda a: pl.BlockSpec(a.shape, lambda i, j: (0,) * a.ndim)
    tile = lambda w: pl.BlockSpec((lc, w), lambda i, j: (i * nc + j, 0))
    return pl.pallas_call(
        _ssd_prompt_kernel,
        grid=(b, nc),
        in_specs=[tile(zx_dim), tile(LANES), full(cw), full(cb), full(dtb), full(alog), full(dskip), full(nrm),
                  full(tri)],
        out_specs=[tile(SSD_INNER),
                   pl.BlockSpec((None, SSD_HEADS, SSD_HEAD_DIM, SSD_STATE), lambda i, j: (i, 0, 0, 0))],
        out_shape=[jax.ShapeDtypeStruct((b * t, SSD_INNER), F32),
                   jax.ShapeDtypeStruct((b, SSD_HEADS, SSD_HEAD_DIM, SSD_STATE), F32)],
        scratch_shapes=[pltpu.VMEM((8, conv_dim), F32), pltpu.VMEM((SSD_HEADS, SSD_HEAD_DIM, SSD_STATE), F32)],
        compiler_params=_cparams(("parallel", "arbitrary")),
        name="ssd_prompt",
    )(zx, small, cw, cb, dtb, alog, dskip, nrm, tri)


def _outproj_kernel(x_ref, a_ref, b_ref, wa_ref, wb_ref, o_ref):
    o_ref[...] = (x_ref[...] + _dot(a_ref[...].astype(BF), wa_ref[...])
                  + _dot(b_ref[...].astype(BF), wb_ref[...]))


def _outproj(x2d, a, bb, wa, wb, tm):
    m, d = x2d.shape
    row = lambda w: pl.BlockSpec((tm, w), lambda i: (i, 0))
    full = lambda arr: pl.BlockSpec(arr.shape, lambda i: (0, 0))
    return pl.pallas_call(
        _outproj_kernel,
        grid=(m // tm,),
        in_specs=[row(d), row(a.shape[1]), row(bb.shape[1]), full(wa), full(wb)],
        out_specs=row(d),
        out_shape=jax.ShapeDtypeStruct((m, d), F32),
        compiler_params=_cparams(("parallel",)),
        name="outproj_a",
    )(x2d, a, bb, wa, wb)


def _ffn_kernel(x_ref, g_ref, wg_ref, wu_ref, wd_ref, gf_ref, o_ref, *, final_norm):
    x = x_ref[...]
    xn = _rms(x, g_ref[...]).astype(BF)
    h = _silu(_dot(xn, wg_ref[...])) * _dot(xn, wu_ref[...])
    y = x + _dot(h.astype(BF), wd_ref[...])
    if final_norm:
        y = _rms(y, gf_ref[...])
    o_ref[...] = y


def _ffn(x2d, g, wg, wu, wd, gf, tm, final_norm, in_map, out_map, out_2d_shape):
    m, d = x2d.shape[0] * x2d.shape[1] // wg.shape[0], wg.shape[0]
    n_steps = m // tm
    full = lambda arr: pl.BlockSpec(arr.shape, lambda i: (0, 0))
    return pl.pallas_call(
        functools.partial(_ffn_kernel, final_norm=final_norm),
        grid=(n_steps,),
        in_specs=[pl.BlockSpec((tm, d), in_map), full(g), full(wg), full(wu), full(wd), full(gf)],
        out_specs=pl.BlockSpec((tm, d), out_map),
        out_shape=jax.ShapeDtypeStruct(out_2d_shape, F32),
        compiler_params=_cparams(("parallel",)),
        name="ffn_final" if final_norm else "ffn",
    )(x2d, g, wg, wu, wd, gf)


def _lru_gates(xc, wa_ref, ba_ref, wx_ref, bx_ref, sp):
    n_heads = xc.shape[1] // LRU_BLOCK
    a_parts = []
    gx_parts = []
    for h in range(n_heads):
        sl = slice(h * LRU_BLOCK, (h + 1) * LRU_BLOCK)
        xh = xc[:, sl]
        xb = xh.astype(BF)
        r = jax.nn.sigmoid(_dot(xb, wa_ref[h]) + ba_ref[:, sl])
        i = jax.nn.sigmoid(_dot(xb, wx_ref[h]) + bx_ref[:, sl])
        log_a = -LRU_C * r * sp[:, sl]
        a = jnp.exp(log_a)
        a_parts.append(a)
        gx_parts.append(jnp.sqrt(1.0 - a * a) * (i * xh))
    return jnp.concatenate(a_parts, axis=1), jnp.concatenate(gx_parts, axis=1)


def _lru_prompt_kernel(x_ref, g_ref, win_ref, cw_ref, cb_ref, wa_ref, ba_ref, wx_ref, bx_ref, lam_ref, wo_ref,
                       o_ref, hout_ref, cout_ref, prev_ref, h_ref, a_s, gx_s, hs_s, *, nb, tt, w):
    @pl.when(pl.program_id(0) == 0)
    def _():
        prev_ref[...] = jnp.zeros_like(prev_ref)
        h_ref[...] = jnp.zeros_like(h_ref)

    rows = tt * nb
    x = x_ref[...]
    xn = _rms(x, g_ref[...]).astype(BF)
    proj = _dot(xn, win_ref[...])
    gate_br = proj[:, 0:w]
    x_br = proj[:, w:2 * w]
    n_prev = (CONV_WIDTH - 1) * nb
    u = jnp.concatenate([prev_ref[...], x_br], axis=0)
    prev_ref[...] = x_br[rows - n_prev:rows, :]
    cout_ref[...] = x_br[rows - n_prev:rows, :]
    xc = cb_ref[...]
    for k in range(CONV_WIDTH):
        xc = xc + u[k * nb:k * nb + rows, :] * cw_ref[k:k + 1, :]
    sp = _softplus(-lam_ref[...])
    a, gx = _lru_gates(xc, wa_ref, ba_ref, wx_ref, bx_ref, sp)
    a_s[...] = a
    gx_s[...] = gx
    h = h_ref[...]
    for t in range(tt):
        h = a_s[t * nb:(t + 1) * nb, :] * h + gx_s[t * nb:(t + 1) * nb, :]
        hs_s[t * nb:(t + 1) * nb, :] = h
    h_ref[...] = h
    hout_ref[...] = h
    y = (jax.nn.gelu(gate_br) * hs_s[...]).astype(BF)
    o_ref[...] = x + _dot(y, wo_ref[...])


def _lru_prompt(x_tm, g, w_in, cw, cb, wa, ba, wx, bx, lam, wo, nb, tt):
    m, d = x_tm.shape
    w = cw.shape[1]
    rows = tt * nb
    n_prev = (CONV_WIDTH - 1) * nb
    full = lambda a: pl.BlockSpec(a.shape, lambda i: (0,) * a.ndim)
    args = (g, w_in, cw, cb, wa, ba, wx, bx, lam, wo)
    return pl.pallas_call(
        functools.partial(_lru_prompt_kernel, nb=nb, tt=tt, w=w),
        grid=(m // rows,),
        in_specs=[pl.BlockSpec((rows, d), lambda i: (i, 0))] + [full(a) for a in args],
        out_specs=[pl.BlockSpec((rows, d), lambda i: (i, 0)),
                   pl.BlockSpec((nb, w), lambda i: (0, 0)),
                   pl.BlockSpec((n_prev, w), lambda i: (0, 0))],
        out_shape=[jax.ShapeDtypeStruct((m, d), F32),
                   jax.ShapeDtypeStruct((nb, w), F32),
                   jax.ShapeDtypeStruct((n_prev, w), F32)],
        scratch_shapes=[pltpu.VMEM((n_prev, w), F32), pltpu.VMEM((nb, w), F32),
                        pltpu.VMEM((rows, w), F32), pltpu.VMEM((rows, w), F32), pltpu.VMEM((rows, w), F32)],
        compiler_params=_cparams(("arbitrary",)),
        name="lru_prompt",
    )(x_tm, *args)


def _lru_params(norm_g, w_in_c, conv_w, conv_b, w_a, b_a, w_x, b_x, lam, w_out_c):
    r = lambda v: v.reshape(1, -1)
    return (r(norm_g), w_in_c.astype(BF), conv_w, r(conv_b), w_a.astype(BF), r(b_a), w_x.astype(BF), r(b_x),
            r(lam), w_out_c.astype(BF))


def _lru_decode_kernel(x_ref, g_ref, win_ref, cw_ref, cb_ref, wa_ref, ba_ref, wx_ref, bx_ref, lam_ref, wo_ref,
                       c0_ref, c1_ref, c2_ref, h0_ref, o_ref, hout_ref, xbr_ref, *, w):
    x = x_ref[...]
    xn = _rms(x, g_ref[...]).astype(BF)
    proj = _dot(xn, win_ref[...])
    gate_br = proj[:, 0:w]
    x_br = proj[:, w:2 * w]
    xbr_ref[...] = x_br
    xc = (cb_ref[...] + c0_ref[...] * cw_ref[0:1, :] + c1_ref[...] * cw_ref[1:2, :]
          + c2_ref[...] * cw_ref[2:3, :] + x_br * cw_ref[3:4, :])
    sp = _softplus(-lam_ref[...])
    a, gx = _lru_gates(xc, wa_ref, ba_ref, wx_ref, bx_ref, sp)
    h = a * h0_ref[...] + gx
    hout_ref[...] = h
    y = (jax.nn.gelu(gate_br) * h).astype(BF)
    o_ref[...] = x + _dot(y, wo_ref[...])


def _lru_decode(x, params, conv_state, h0):
    m, d = x.shape
    w = h0.shape[1]
    args = (x,) + tuple(params) + (conv_state[:, 0], conv_state[:, 1], conv_state[:, 2], h0)
    full = lambda a: pl.BlockSpec(a.shape, lambda i: (0,) * a.ndim)
    return pl.pallas_call(
        functools.partial(_lru_decode_kernel, w=w),
        grid=(1,),
        in_specs=[full(a) for a in args],
        out_specs=[pl.BlockSpec((m, d), lambda i: (0, 0)), pl.BlockSpec((m, w), lambda i: (0, 0)),
                   pl.BlockSpec((m, w), lambda i: (0, 0))],
        out_shape=[jax.ShapeDtypeStruct((m, d), F32), jax.ShapeDtypeStruct((m, w), F32),
                   jax.ShapeDtypeStruct((m, w), F32)],
        compiler_params=_cparams(("arbitrary",)),
        name="lru_decode",
    )(*args)


PAGES_PER_STEP = 16
HALVES_PER_PAGE = PAGE_SIZE // CMP_STRIDE


def _compress_decode_kernel(pt_ref, *refs):
    del pt_ref
    np_ = PAGES_PER_STEP
    k_pages = refs[0:np_]
    v_pages = refs[np_:2 * np_]
    pek_ref, w1k_ref, w2k_ref, pev_ref, w1v_ref, w2v_ref, out_ref, carry_ref = refs[2 * np_:]
    rows = np_ * HALVES_PER_PAGE

    @pl.when(pl.program_id(1) == 0)
    def _():
        carry_ref[...] = jnp.zeros_like(carry_ref)

    rowi = lax.broadcasted_iota(jnp.int32, (rows, CMP_HIDDEN), 0)
    branches = ((k_pages, pek_ref, w1k_ref, w2k_ref), (v_pages, pev_ref, w1v_ref, w2v_ref))
    for c, (pages, pe_ref, w1_ref, w2_ref) in enumerate(branches):
        def load_rows(s, pages=pages):
            return jnp.concatenate([p[pl.ds(s, HALVES_PER_PAGE, stride=CMP_STRIDE), :] for p in pages], axis=0)

        for g in range(N_KV):
            a0, a1 = _compress_accumulate(load_rows, pe_ref, w1_ref, g * HEAD_DIM)
            slot = c * N_KV + g
            lead = jnp.where(rowi == 0, carry_ref[slot, 7:8, :], pltpu.roll(a0, 1, 0))
            carry_ref[slot] = a0[rows - 8:rows, :]
            tok = _dot(_silu(lead + a1).astype(BF), w2_ref[...])
            out_ref[g, :, c * HEAD_DIM:(c + 1) * HEAD_DIM] = tok


def _compress_decode(cache3d, pt_flat, db, n_pages, pek, w1k, w2k, pev, w1v, w2v):
    np_ = PAGES_PER_STEP
    n_steps = n_pages // np_
    rows = np_ * HALVES_PER_PAGE
    n_half = n_pages * HALVES_PER_PAGE

    def page_spec(k, colblk):
        return pl.BlockSpec((None, PAGE_SIZE, LANES),
                            lambda i, j, pt: (pt[i * n_pages + j * np_ + k], 0, colblk))

    full = lambda a: pl.BlockSpec(a.shape, lambda i, j, pt: (0,) * a.ndim)
    grid_spec = pltpu.PrefetchScalarGridSpec(
        num_scalar_prefetch=1,
        grid=(db, n_steps),
        in_specs=[page_spec(k, 0) for k in range(np_)] + [page_spec(k, 1) for k in range(np_)]
        + [full(pek), full(w1k), full(w2k), full(pev), full(w1v), full(w2v)],
        out_specs=pl.BlockSpec((None, N_KV, rows, LANES), lambda i, j, pt: (i, 0, j, 0)),
        scratch_shapes=[pltpu.VMEM((2 * N_KV, 8, CMP_HIDDEN), F32)],
    )
    return pl.pallas_call(
        _compress_decode_kernel,
        grid_spec=grid_spec,
        out_shape=jax.ShapeDtypeStruct((db, N_KV, n_half, LANES), F32),
        compiler_params=_cparams(("parallel", "arbitrary")),
        name="compress_decode",
    )(pt_flat, *([cache3d] * (2 * np_)), pek, w1k, w2k, pev, w1v, w2v)


def _group_rows(top, a, b):
    return jnp.where(top, a, b)


def _nsa_decode_a_kernel(q_ref, qr_ref, cmp_ref, win_ref, new_ref, ov_ref, ocmp_ref, owin_ref, idx_ref, *,
                         pos, n_blocks):
    nh = N_HEADS
    top = lax.broadcasted_iota(jnp.int32, (nh, 1), 0) < Q_PER_KV
    q = q_ref[...].astype(BF)
    n_tok = cmp_ref.shape[1]
    lane = lax.broadcasted_iota(jnp.int32, (nh, n_tok), 1)
    cmask = (lane >= 1) & (((lane - 1) * CMP_STRIDE + (2 * CMP_STRIDE - 1)) <= pos)
    s = _group_rows(top, _dot_nt(q, cmp_ref[0, :, 0:HEAD_DIM].astype(BF)),
                    _dot_nt(q, cmp_ref[1, :, 0:HEAD_DIM].astype(BF))) * SCALE
    p = _msoftmax(s, cmask)
    pb = p.astype(BF)
    ocmp_ref[...] = _group_rows(top, _dot(pb, cmp_ref[0, :, HEAD_DIM:LANES].astype(BF)),
                                _dot(pb, cmp_ref[1, :, HEAD_DIM:LANES].astype(BF)))

    imp8 = _dot3(p, ov_ref[...])
    nj = ov_ref.shape[1]
    cur = pos // SLC_BLOCK
    j = lax.broadcasted_iota(jnp.int32, (1, nj), 1)
    valid = (j * SLC_BLOCK) <= pos
    forced = (j == 0) | (j == cur) | (j == cur - 1)
    ri = lax.broadcasted_iota(jnp.int32, (nj, nj), 0)
    ci = lax.broadcasted_iota(jnp.int32, (nj, nj), 1)
    k_lane = lax.broadcasted_iota(jnp.int32, (nj, LANES), 1).astype(F32)
    jvals = lax.broadcasted_iota(jnp.int32, (8, nj), 1).astype(F32).astype(BF)
    idx_rows = []
    for g in range(N_KV):
        imp = jnp.sum(imp8[g * Q_PER_KV:(g + 1) * Q_PER_KV, :], axis=0, keepdims=True)
        imp = jnp.where(valid & forced, FORCE_SCORE, imp)
        imp = jnp.where(valid, imp, -FORCE_SCORE)
        imp = jnp.where(j < n_blocks, imp, -3e38)
        impb = jnp.broadcast_to(imp, (nj, nj))
        col = jnp.sum(jnp.where(ri == ci, impb, 0.0), axis=1, keepdims=True)
        ahead = jnp.where(impb == col, jnp.where(ci < ri, 1.0, 0.0), jnp.where(impb > col, 1.0, 0.0))
        rank_col = jnp.sum(ahead, axis=1, keepdims=True)
        onehot = jnp.where(rank_col == k_lane, 1.0, 0.0).astype(BF)
        idx_rows.append(_dot(jvals, onehot)[0:1, :])
    idx = jnp.concatenate(idx_rows + [jnp.zeros((8 - N_KV, LANES), F32)], axis=0)
    idx_ref[...] = idx.astype(jnp.int32)

    qr = qr_ref[...]
    qrb = qr.astype(BF)
    n_win = win_ref.shape[0]
    s = _group_rows(top, _dot_nt(qrb, win_ref[:, 0:HEAD_DIM].astype(BF)),
                    _dot_nt(qrb, win_ref[:, HEAD_DIM:LANES].astype(BF))) * SCALE
    wl = lax.broadcasted_iota(jnp.int32, (nh, n_win), 1)
    dist = n_win - wl
    wmask = (dist >= 0) & (dist < WINDOW)
    new = new_ref[...]
    knew = _group_rows(top, new[:, 0:HEAD_DIM], new[:, HEAD_DIM:LANES])
    vnew = _group_rows(top, new[:, LANES:LANES + HEAD_DIM], new[:, LANES + HEAD_DIM:KV_DIM])
    s_new = jnp.sum(qr * knew, axis=1, keepdims=True) * SCALE
    sm = jnp.where(wmask, s, NEG_BIG)
    m = jnp.maximum(jnp.max(sm, axis=1, keepdims=True), s_new)
    e = jnp.where(wmask, jnp.exp(sm - m), 0.0)
    e_new = jnp.exp(s_new - m)
    d = jnp.sum(e, axis=1, keepdims=True) + e_new
    eb = e.astype(BF)
    o = _group_rows(top, _dot(eb, win_ref[:, LANES:LANES + HEAD_DIM].astype(BF)),
                    _dot(eb, win_ref[:, LANES + HEAD_DIM:KV_DIM].astype(BF))) + e_new * vnew
    owin_ref[...] = o / d


def _nsa_decode_a(q3, qr3, cmp_s, win, kvw_new3, pos):
    db = q3.shape[0]
    n_tok = cmp_s.shape[2]
    n_blocks = -(-(pos + 1) // SLC_BLOCK)
    nj = -(-n_blocks // LANES) * LANES
    ov = _overlap_matrix(n_tok, nj, 1)
    per_b = lambda a: pl.BlockSpec((None,) + a.shape[1:], lambda i: (i,) + (0,) * (a.ndim - 1))
    head = pl.BlockSpec((None, N_HEADS, HEAD_DIM), lambda i: (i, 0, 0))
    return pl.pallas_call(
        functools.partial(_nsa_decode_a_kernel, pos=pos, n_blocks=n_blocks),
        grid=(db,),
        in_specs=[head, head, per_b(cmp_s), per_b(win), per_b(kvw_new3), pl.BlockSpec(ov.shape, lambda i: (0, 0))],
        out_specs=[head, head, pl.BlockSpec((None, 8, LANES), lambda i: (i, 0, 0))],
        out_shape=[jax.ShapeDtypeStruct((db, N_HEADS, HEAD_DIM), F32),
                   jax.ShapeDtypeStruct((db, N_HEADS, HEAD_DIM), F32),
                   jax.ShapeDtypeStruct((db, 8, LANES), jnp.int32)],
        compiler_params=_cparams(("parallel",)),
        name="nsa_decode_select",
    )(q3, qr3, cmp_s, win, kvw_new3, ov)


def _nsa_decode_b_kernel(idx_ref, pt_ref, qr_ref, *refs, n_past_blocks, cur):
    del pt_ref
    nb = N_KV * N_SELECT
    blk_refs = refs[0:nb]
    new_ref, ocmp_ref, owin_ref, sm_ref, o_ref = refs[nb:]
    b = pl.program_id(0)
    nh = N_HEADS
    top = lax.broadcasted_iota(jnp.int32, (nh, 1), 0) < Q_PER_KV
    qr = qr_ref[...]
    qrb = qr.astype(BF)
    n_keys = N_SELECT * SLC_BLOCK
    lane_blk = lax.broadcasted_iota(jnp.int32, (nh, n_keys), 1) // SLC_BLOCK
    s_g = []
    v_g = []
    ok_g = []
    has_new = []
    for g in range(N_KV):
        blks = blk_refs[g * N_SELECT:(g + 1) * N_SELECT]
        kk = jnp.concatenate([r[:, g * HEAD_DIM:(g + 1) * HEAD_DIM] for r in blks], axis=0).astype(BF)
        v_g.append(jnp.concatenate([r[:, LANES + g * HEAD_DIM:LANES + (g + 1) * HEAD_DIM] for r in blks],
                                   axis=0).astype(BF))
        s_g.append(_dot_nt(qrb, kk))
        ok = jnp.zeros((nh, n_keys), F32)
        new_sel = jnp.zeros((), F32)
        for k in range(N_SELECT):
            jk = idx_ref[(b * N_KV + g) * N_SELECT + k]
            ok = jnp.where(lane_blk == k, jnp.where(jk < n_past_blocks, 1.0, 0.0), ok)
            new_sel = jnp.maximum(new_sel, jnp.where(jk == cur, 1.0, 0.0))
        ok_g.append(ok)
        has_new.append(new_sel)
    s = _group_rows(top, s_g[0], s_g[1]) * SCALE
    mask = _group_rows(top, ok_g[0], ok_g[1]) > 0.5
    new_on = _group_rows(top, has_new[0], has_new[1]) > 0.5
    new = new_ref[...]
    knew = _group_rows(top, new[:, 0:HEAD_DIM], new[:, HEAD_DIM:LANES])
    vnew = _group_rows(top, new[:, LANES:LANES + HEAD_DIM], new[:, LANES + HEAD_DIM:KV_DIM])
    s_new = jnp.where(new_on, jnp.sum(qr * knew, axis=1, keepdims=True) * SCALE, NEG_BIG)
    sm = jnp.where(mask, s, NEG_BIG)
    m = jnp.maximum(jnp.max(sm, axis=1, keepdims=True), s_new)
    e = jnp.where(mask, jnp.exp(sm - m), 0.0)
    e_new = jnp.where(new_on, jnp.exp(s_new - m), 0.0)
    d = jnp.sum(e, axis=1, keepdims=True) + e_new
    eb = e.astype(BF)
    o_slc = (_group_rows(top, _dot(eb, v_g[0]), _dot(eb, v_g[1])) + e_new * vnew) / jnp.where(d > 0.0, d, 1.0)

    sig = jnp.broadcast_to(jax.nn.sigmoid(sm_ref[...]), (nh, LANES))
    hl = lax.broadcasted_iota(jnp.int32, (nh, LANES), 1)
    hr = lax.broadcasted_iota(jnp.int32, (nh, LANES), 0)

    def gate(br):
        return jnp.sum(jnp.where(hl == 3 * hr + br, sig, 0.0), axis=1, keepdims=True)

    o_ref[...] = gate(0) * ocmp_ref[...] + gate(1) * o_slc + gate(2) * owin_ref[...]


def _nsa_decode_b(idx_flat, pt_flat, qr3, slc_half_blocks, kvs_new3, o_cmp, o_win, small3, pos, n_pages):
    db = qr3.shape[0]
    n_past_blocks = pos // SLC_BLOCK
    cur = pos // SLC_BLOCK
    per_page = PAGE_SIZE // SLC_BLOCK

    def blk_spec(g, k):
        def imap(i, idx, pt):
            jk = jnp.minimum(idx[(i * N_KV + g) * N_SELECT + k], n_past_blocks - 1)
            return (pt[i * n_pages + jk // per_page] * per_page + jk % per_page, 0, 0)
        return pl.BlockSpec((None, SLC_BLOCK, KV_DIM), imap)

    head = pl.BlockSpec((None, N_HEADS, HEAD_DIM), lambda i, idx, pt: (i, 0, 0))
    row3 = lambda a: pl.BlockSpec((None,) + a.shape[1:], lambda i, idx, pt: (i, 0, 0))
    grid_spec = pltpu.PrefetchScalarGridSpec(
        num_scalar_prefetch=2,
        grid=(db,),
        in_specs=[head] + [blk_spec(g, k) for g in range(N_KV) for k in range(N_SELECT)]
        + [row3(kvs_new3), head, head, row3(small3)],
        out_specs=head,
    )
    return pl.pallas_call(
        functools.partial(_nsa_decode_b_kernel, n_past_blocks=n_past_blocks, cur=cur),
        grid_spec=grid_spec,
        out_shape=jax.ShapeDtypeStruct((db, N_HEADS, HEAD_DIM), F32),
        compiler_params=_cparams(("arbitrary",)),
        name="nsa_decode_attend",
    )(idx_flat, pt_flat, qr3, *([slc_half_blocks] * (N_KV * N_SELECT)), kvs_new3, o_cmp, o_win, small3)


def _ssd_decode_kernel(zx_ref, cst_ref, sm_ref, h0_ref, cw_ref, cb_ref, dtb_ref, alog_ref, dskip_ref, nrm_ref,
                       y_ref, hout_ref):
    z = zx_ref[:, 0:SSD_INNER]
    xbc = zx_ref[:, SSD_INNER:]
    conv = cb_ref[...] + xbc * cw_ref[CONV_WIDTH - 1:CONV_WIDTH, :]
    for k in range(CONV_WIDTH - 1):
        conv = conv + cst_ref[k:k + 1, :] * cw_ref[k:k + 1, :]
    xbc_c = _silu(conv)
    xs = xbc_c[:, 0:SSD_INNER]
    dt_full = _softplus(sm_ref[...] + dtb_ref[...])
    da_full = jnp.exp(dt_full * (-jnp.exp(alog_ref[...])))
    p = SSD_HEAD_DIM
    eye = lax.broadcasted_iota(jnp.int32, (p, p), 0) == lax.broadcasted_iota(jnp.int32, (p, p), 1)
    ys = []
    for h in range(SSD_HEADS):
        g = h // (SSD_HEADS // 2)
        col = DT_COL + h
        xdt = xs[:, h * p:(h + 1) * p] * dt_full[:, col:col + 1]
        xcol = jnp.sum(jnp.where(eye, jnp.broadcast_to(xdt, (p, p)), 0.0), axis=1, keepdims=True)
        bm = xbc_c[:, SSD_INNER + g * SSD_STATE:SSD_INNER + (g + 1) * SSD_STATE]
        cm = xbc_c[:, SSD_INNER + (2 + g) * SSD_STATE:SSD_INNER + (3 + g) * SSD_STATE]
        h_new = da_full[:, col:col + 1] * h0_ref[h] + xcol * bm
        hout_ref[h] = h_new
        ys.append(_dot_nt(jnp.broadcast_to(cm, (8, SSD_STATE)).astype(BF), h_new.astype(BF))[0:1, :])
    y = jnp.concatenate(ys, axis=1) + dskip_ref[...] * xs
    y = y * _silu(z)
    y_ref[...] = _rms(y, nrm_ref[...])


def _ssd_decode(zx3, conv_state, small3, h0, params):
    db = zx3.shape[0]
    cw, cb, dtb, alog, dskip, nrm = params
    per_b = lambda a: pl.BlockSpec((None,) + a.shape[1:], lambda i: (i,) + (0,) * (a.ndim - 1))
    full = lambda a: pl.BlockSpec(a.shape, lambda i: (0,) * a.ndim)
    return pl.pallas_call(
        _ssd_decode_kernel,
        grid=(db,),
        in_specs=[per_b(zx3), per_b(conv_state), per_b(small3), per_b(h0), full(cw), full(cb), full(dtb), full(alog),
                  full(dskip), full(nrm)],
        out_specs=[pl.BlockSpec((None, 1, SSD_INNER), lambda i: (i, 0, 0)), per_b(h0)],
        out_shape=[jax.ShapeDtypeStruct((db, 1, SSD_INNER), F32), jax.ShapeDtypeStruct(h0.shape, F32)],
        compiler_params=_cparams(("parallel",)),
        name="ssd_decode",
    )(zx3, conv_state, small3, h0, cw, cb, dtb, alog, dskip, nrm)


def kernel(x_prompt, x_sample, cache_kv_cmp, cache_kv_slc, cache_kv_win, state_ssm, state_ssd_conv, state_lru,
           state_lru_conv, page_table, norm_mix, norm_ffn, norm_final, w_ffn_gate, w_ffn_up, w_ffn_down, w_in_a,
           w_out_a, cmp_pe_k, cmp_w1_k, cmp_w2_k, cmp_pe_v, cmp_w1_v, cmp_w2_v, ssd_conv_w, ssd_conv_b, ssd_dt_bias,
           ssd_a_log, ssd_d, ssd_norm, w_in_c, lru_conv_w, lru_conv_b, lru_w_a, lru_b_a, lru_w_x, lru_b_x,
           lru_lambda, w_out_c):
    b, t, d = x_prompt.shape
    db = x_sample.shape[0]
    n_pages = page_table.shape[1]
    pos_s = n_pages * PAGE_SIZE
    m = b * t
    kv_shape = (2, N_KV, HEAD_DIM)

    wm, ws = _prep_w_in_a(w_in_a[0])
    wo_nsa = w_out_a[0, :Q_DIM].astype(BF)
    wo_ssd = w_out_a[0, Q_DIM:].astype(BF)
    cmp_k = _prep_cmp_w(cmp_pe_k[0], cmp_w1_k[0], cmp_w2_k[0])
    cmp_v = _prep_cmp_w(cmp_pe_v[0], cmp_w1_v[0], cmp_w2_v[0])
    ssd_par = _ssd_params(ssd_conv_w[0], ssd_conv_b[0], ssd_dt_bias[0], ssd_a_log[0], ssd_d[0], ssd_norm[0])
    lru_par = _lru_params(norm_mix[1], w_in_c[0], lru_conv_w[0], lru_conv_b[0], lru_w_a[0], lru_b_a[0], lru_w_x[0],
                          lru_b_x[0], lru_lambda[0], w_out_c[0])
    ffn_w = [(norm_ffn[l].reshape(1, d), w_ffn_gate[l].astype(BF), w_ffn_up[l].astype(BF), w_ffn_down[l].astype(BF))
             for l in range(2)]
    gfin = norm_final.reshape(1, d)

    tm_proj = 512
    xp = x_prompt.reshape(m, d)
    cos_p, sin_p = _rope_tables(jnp.arange(t, dtype=jnp.int32))
    q, qr, kvc, kvs, kvw, zx, small = _inproj_a(xp, norm_mix[0], wm, ws, cos_p, sin_p, tm_proj, t // tm_proj)
    cmp_p = _compress_prompt(kvc, b, t, *cmp_k, *cmp_v)
    o_nsa = _nsa_prompt(q, qr, cmp_p, kvs, kvw, small, b, t)
    y_ssd, ssm_p = _ssd_prompt(zx, small, ssd_par, b, t)
    x1 = _outproj(xp, o_nsa, y_ssd, wo_nsa, wo_ssd, tm_proj)
    tm_ffn = 256
    nt = t // tm_ffn
    x2 = _ffn(x1, *ffn_w[0], gfin, tm_ffn, False, lambda i: (i, 0), lambda i: (i % nt, i // nt), (t, b * d))
    x3, lru_p, lconv_tail = _lru_prompt(x2.reshape(t * b, d), *lru_par, b, 32)
    y_prompt = _ffn(x3.reshape(t, b * d), *ffn_w[1], gfin, tm_ffn, True, lambda i: (i % nt, i // nt),
                    lambda i: (i, 0), (m, d))
    w_keep = min(WINDOW, t)
    kv_cmp_p = kvc.reshape((1, b, t) + kv_shape)
    kv_slc_p = kvs.reshape((1, b, t) + kv_shape)
    kv_win_p = kvw.reshape(b, t, KV_DIM)[:, t - w_keep:].reshape((1, b, w_keep) + kv_shape)
    ssd_conv_p = zx.reshape(b, t, -1)[:, t - (CONV_WIDTH - 1):, SSD_INNER:][None]
    lru_conv_p = lconv_tail.reshape(CONV_WIDTH - 1, b, -1).transpose(1, 0, 2)[None]

    xs = x_sample.reshape(db, d)
    cos_s, sin_s = _rope_tables(jnp.full((db,), pos_s, dtype=jnp.int32))
    q_s, qr_s, kvc_s, kvs_s, kvw_s, zx_s, small_s = _inproj_a(xs, norm_mix[0], wm, ws, cos_s, sin_s, db, 1)
    pt_flat = page_table.reshape(-1)
    n_phys = cache_kv_cmp.shape[1]
    cmp_s = _compress_decode(cache_kv_cmp[0].reshape(n_phys, PAGE_SIZE, KV_DIM), pt_flat, db, n_pages,
                             *cmp_k, *cmp_v)
    win_buf = cache_kv_win[0].reshape(db, -1, KV_DIM)
    head3 = lambda a: a.reshape(db, N_HEADS, HEAD_DIM)
    o_cmp_s, o_win_s, idx = _nsa_decode_a(head3(q_s), head3(qr_s), cmp_s, win_buf, kvw_s.reshape(db, 1, KV_DIM),
                                          pos_s)
    idx_flat = idx[:, :N_KV, :N_SELECT].reshape(-1)
    slc_half = cache_kv_slc[0].reshape(n_phys * (PAGE_SIZE // SLC_BLOCK), SLC_BLOCK, KV_DIM)
    o_nsa_s = _nsa_decode_b(idx_flat, pt_flat, head3(qr_s), slc_half, kvs_s.reshape(db, 1, KV_DIM), o_cmp_s,
                            o_win_s, small_s.reshape(db, 1, LANES), pos_s, n_pages)
    y_ssd_s, ssm_s = _ssd_decode(zx_s.reshape(db, 1, -1), state_ssd_conv[0], small_s.reshape(db, 1, LANES),
                                 state_ssm[0], ssd_par)
    x1_s = _outproj(xs, o_nsa_s.reshape(db, Q_DIM), y_ssd_s.reshape(db, SSD_INNER), wo_nsa, wo_ssd, db)
    x2_s = _ffn(x1_s, *ffn_w[0], gfin, db, False, lambda i: (i, 0), lambda i: (i, 0), (db, d))
    x3_s, lru_s, xbr_s = _lru_decode(x2_s, lru_par, state_lru_conv[0], state_lru[0])
    y_sample = _ffn(x3_s, *ffn_w[1], gfin, db, True, lambda i: (i, 0), lambda i: (i, 0), (db, d))
    kv_cmp_s = kvc_s.reshape((1, db, 1) + kv_shape)
    kv_slc_s = kvs_s.reshape((1, db, 1) + kv_shape)
    kv_win_s = jnp.concatenate([win_buf[:, 1:], kvw_s[:, None, :]], axis=1).reshape(
        (1, db, win_buf.shape[1]) + kv_shape)
    ssd_conv_s = jnp.concatenate([state_ssd_conv[0][:, 1:], zx_s[:, None, SSD_INNER:]], axis=1)[None]
    lru_conv_s = jnp.concatenate([state_lru_conv[0][:, 1:], xbr_s[:, None, :]], axis=1)[None]

    return (y_prompt.reshape(b, t, d), y_sample.reshape(db, 1, d),
            kv_cmp_p, kv_slc_p, kv_win_p, ssm_p[None], ssd_conv_p, lru_p[None], lru_conv_p,
            kv_cmp_s, kv_slc_s, kv_win_s, ssm_s[None], ssd_conv_s, lru_s[None], lru_conv_s)
```

```python
import functools
import math

import jax
import jax.numpy as jnp
from jax import lax
from jax.experimental import pallas as pl
from jax.experimental.pallas import tpu as pltpu

BF = jnp.bfloat16
F32 = jnp.float32

HEAD_DIM = 64
N_KV = 2
Q_PER_KV = 4
N_HEADS = N_KV * Q_PER_KV
CMP_STRIDE = 16
CMP_HIDDEN = 256
SLC_BLOCK = 64
N_SELECT = 16
WINDOW = 512
PAGE_SIZE = 128
ROPE_THETA = 10000.0
FORCE_SCORE = 1e9
NEG_BIG = -1e30
SSD_HEADS = 8
SSD_HEAD_DIM = 64
SSD_STATE = 128
SSD_INNER = SSD_HEADS * SSD_HEAD_DIM
SSD_CHUNK = 128
CONV_WIDTH = 4
LRU_BLOCK = 128
LRU_C = 8.0
RMS_EPS = 1e-6
SCALE = HEAD_DIM ** -0.5
Q_DIM = N_HEADS * HEAD_DIM
KV_DIM = 2 * N_KV * HEAD_DIM
GATE_DIM = 3 * N_HEADS
DT_COL = GATE_DIM
LANES = 128
VMEM_LIMIT = 56 * 1024 * 1024


def _cparams(sem):
    return pltpu.CompilerParams(dimension_semantics=sem, vmem_limit_bytes=VMEM_LIMIT)


def _dot(a, b):
    return jnp.dot(a, b, preferred_element_type=F32)


def _dot_nt(a, b):
    return lax.dot_general(a, b, (((1,), (1,)), ((), ())), preferred_element_type=F32)


def _dot_tn(a, b):
    return lax.dot_general(a, b, (((0,), (0,)), ((), ())), preferred_element_type=F32)


def _split3(x):
    hi = x.astype(BF)
    r = x - hi.astype(F32)
    mid = r.astype(BF)
    lo = (r - mid.astype(F32)).astype(BF)
    return hi, mid, lo


def _dot3(x, m01):
    hi, mid, lo = _split3(x)
    return _dot(hi, m01) + _dot(mid, m01) + _dot(lo, m01)


def _dot3_left(m01, x):
    hi, mid, lo = _split3(x)
    return _dot(m01, hi) + _dot(m01, mid) + _dot(m01, lo)


def _rms(x, g):
    y = x * lax.rsqrt(jnp.mean(x * x, axis=-1, keepdims=True) + RMS_EPS)
    return y * g


def _silu(x):
    return x * jax.nn.sigmoid(x)


def _softplus(x):
    return jnp.maximum(x, 0.0) + jnp.log1p(jnp.exp(-jnp.abs(x)))


def _msoftmax_parts(s, mask):
    s = jnp.where(mask, s, NEG_BIG)
    m = jnp.max(s, axis=-1, keepdims=True)
    e = jnp.where(mask, jnp.exp(s - m), 0.0)
    return m, e


def _msoftmax(s, mask):
    _, e = _msoftmax_parts(s, mask)
    d = jnp.sum(e, axis=-1, keepdims=True)
    return e / jnp.where(d > 0.0, d, 1.0)


def _rope_tables(pos):
    half = HEAD_DIM // 2
    inv_freq = ROPE_THETA ** (-jnp.arange(half, dtype=F32) / half)
    ang = pos.astype(F32)[:, None] * inv_freq[None, :]
    cos = jnp.cos(ang)
    sin = jnp.sin(ang)
    cos2 = jnp.tile(jnp.concatenate([cos, cos], axis=-1), (1, LANES // HEAD_DIM))
    sin2 = jnp.tile(jnp.concatenate([-sin, sin], axis=-1), (1, LANES // HEAD_DIM))
    return cos2, sin2


def _inproj_a_kernel(x_ref, g_ref, wm_ref, ws_ref, cos_ref, sin_ref,
                     q_ref, qr_ref, kvc_ref, kvs_ref, kvw_ref, zx_ref, sm_ref, *kvc_t_ref, feature_major):
    xn = _rms(x_ref[...], g_ref[...]).astype(BF)
    cos = cos_ref[...]
    sin = sin_ref[...]
    lane = lax.broadcasted_iota(jnp.int32, cos.shape, 1)
    first = (lane % HEAD_DIM) < (HEAD_DIM // 2)

    def rope(v):
        rot = jnp.where(first, pltpu.roll(v, LANES - HEAD_DIM // 2, 1), pltpu.roll(v, HEAD_DIM // 2, 1))
        return v * cos + rot * sin

    def store_kv(ref, k, v):
        if feature_major:
            ref[0:LANES, :] = k.T
            ref[LANES:KV_DIM, :] = v.T
        else:
            ref[:, 0:LANES] = k
            ref[:, LANES:KV_DIM] = v

    q = _dot(xn, wm_ref[:, 0:Q_DIM])
    q_ref[...] = q
    for c in range(Q_DIM // LANES):
        qr_ref[:, c * LANES:(c + 1) * LANES] = rope(q[:, c * LANES:(c + 1) * LANES])
    o = Q_DIM
    kvc = _dot(xn, wm_ref[:, o:o + KV_DIM])
    kvc_ref[...] = kvc
    if feature_major:
        store_kv(kvc_t_ref[0], kvc[:, 0:LANES], kvc[:, LANES:KV_DIM])
    o += KV_DIM
    kvs = _dot(xn, wm_ref[:, o:o + KV_DIM])
    store_kv(kvs_ref, rope(kvs[:, 0:LANES]), kvs[:, LANES:KV_DIM])
    o += KV_DIM
    kvw = _dot(xn, wm_ref[:, o:o + KV_DIM])
    store_kv(kvw_ref, rope(kvw[:, 0:LANES]), kvw[:, LANES:KV_DIM])
    o += KV_DIM
    zx_ref[...] = _dot(xn, wm_ref[:, o:])
    sm_ref[...] = _dot(xn, ws_ref[...])


def _prep_w_in_a(w_in_a):
    a = Q_DIM + 3 * KV_DIM
    gate = w_in_a[:, a:a + GATE_DIM]
    rest = w_in_a[:, a + GATE_DIM:]
    zx_w = rest[:, :rest.shape[1] - SSD_HEADS]
    dt = rest[:, rest.shape[1] - SSD_HEADS:]
    main = jnp.concatenate([w_in_a[:, :a], zx_w], axis=1).astype(BF)
    small = jnp.concatenate([gate, dt], axis=1)
    small = jnp.pad(small, ((0, 0), (0, LANES - small.shape[1]))).astype(BF)
    return main, small


def _inproj_a(x2d, g, wm, ws, cos2, sin2, tm, nb, feature_major):
    m, d = x2d.shape
    nt = m // nb // tm
    zx_dim = wm.shape[1] - Q_DIM - 3 * KV_DIM
    row = lambda w: pl.BlockSpec((tm, w), lambda i, j: (i * nt + j, 0))
    row_shape = lambda w: jax.ShapeDtypeStruct((m, w), F32)
    const = lambda i, j: (0, 0)
    if feature_major:
        kv = pl.BlockSpec((None, KV_DIM, tm), lambda i, j: (i, 0, j))
        kv_shape = jax.ShapeDtypeStruct((nb, KV_DIM, m // nb), F32)
    else:
        kv, kv_shape = row(KV_DIM), row_shape(KV_DIM)
    out_specs = [row(Q_DIM), row(Q_DIM), row(KV_DIM), kv, kv, row(zx_dim), row(LANES)]
    out_shape = [row_shape(Q_DIM), row_shape(Q_DIM), row_shape(KV_DIM), kv_shape, kv_shape, row_shape(zx_dim),
                 row_shape(LANES)]
    if feature_major:
        out_specs.append(kv)
        out_shape.append(kv_shape)
    pos = pl.BlockSpec((tm, LANES), lambda i, j: (j, 0))
    return pl.pallas_call(
        functools.partial(_inproj_a_kernel, feature_major=feature_major),
        grid=(nb, nt),
        in_specs=[row(d), pl.BlockSpec((1, d), const), pl.BlockSpec(wm.shape, const), pl.BlockSpec(ws.shape, const),
                  pos, pos],
        out_specs=out_specs,
        out_shape=out_shape,
        compiler_params=_cparams(("parallel", "parallel")),
        name="inproj_a",
    )(x2d, g.reshape(1, d), wm, ws, cos2, sin2)


def _compress_accumulate(load_rows, pe_ref, w1_ref, col):
    a0 = None
    a1 = None
    for s in range(CMP_STRIDE):
        x = load_rows(s)[:, col:col + HEAD_DIM]
        l = _dot((x + pe_ref[s:s + 1, :]).astype(BF), w1_ref[s])
        t = _dot((x + pe_ref[CMP_STRIDE + s:CMP_STRIDE + s + 1, :]).astype(BF), w1_ref[CMP_STRIDE + s])
        a0 = l if a0 is None else a0 + l
        a1 = t if a1 is None else a1 + t
    return a0, a1


def _compress_prompt_kernel(kc_ref, vc_ref, pek_ref, w1k_ref, w2k_ref, pev_ref, w1v_ref, w2v_ref, out_ref, *,
                            n_half):
    branches = ((kc_ref, pek_ref, w1k_ref, w2k_ref), (vc_ref, pev_ref, w1v_ref, w2v_ref))
    for c, (src_ref, pe_ref, w1_ref, w2_ref) in enumerate(branches):
        def load_rows(s, src_ref=src_ref):
            return src_ref[pl.ds(s, n_half, stride=CMP_STRIDE), :]

        for g in range(N_KV):
            a0, a1 = _compress_accumulate(load_rows, pe_ref, w1_ref, g * HEAD_DIM)
            pre = a0 + pltpu.roll(a1, n_half - 1, 0)
            tok = _dot(_silu(pre).astype(BF), w2_ref[...])
            out_ref[g, :, c * HEAD_DIM:(c + 1) * HEAD_DIM] = tok


def _compress_prompt(kvc2d, b, t, pek, w1k, w2k, pev, w1v, w2v):
    n_half = t // CMP_STRIDE
    full = lambda a: pl.BlockSpec(a.shape, lambda i: (0,) * a.ndim)
    return pl.pallas_call(
        functools.partial(_compress_prompt_kernel, n_half=n_half),
        grid=(b,),
        in_specs=[pl.BlockSpec((t, LANES), lambda i: (i, 0)), pl.BlockSpec((t, LANES), lambda i: (i, 1)),
                  full(pek), full(w1k), full(w2k), full(pev), full(w1v), full(w2v)],
        out_specs=pl.BlockSpec((None, N_KV, n_half, LANES), lambda i: (i, 0, 0, 0)),
        out_shape=jax.ShapeDtypeStruct((b, N_KV, n_half, LANES), F32),
        compiler_params=_cparams(("parallel",)),
        name="compress_prompt",
    )(kvc2d, kvc2d, pek, w1k, w2k, pev, w1v, w2v)


def _prep_cmp_w(pe, w1, w2):
    return pe, w1.reshape(2 * CMP_STRIDE, HEAD_DIM, CMP_HIDDEN).astype(BF), w2.astype(BF)


def _rank_select(imp, n_rows):
    tq = imp.shape[0]
    imp_t = imp.T[0:n_rows, :]
    jrow = lax.broadcasted_iota(jnp.int32, (n_rows, tq), 0)
    cnt = jnp.zeros((n_rows, tq), F32)
    for jp in range(n_rows):
        row = imp_t[jp:jp + 1, :]
        ahead = jnp.where(row == imp_t, jnp.where(jrow > jp, 1.0, 0.0), jnp.where(row > imp_t, 1.0, 0.0))
        cnt = cnt + ahead
    sel_t = jnp.where(cnt < float(N_SELECT), 1.0, 0.0)
    sel_t = jnp.concatenate([sel_t, jnp.zeros((LANES - n_rows, tq), F32)], axis=0)
    return sel_t.T


def _stack_heads(ref, g):
    cols = [(g * Q_PER_KV + r) * HEAD_DIM for r in range(Q_PER_KV)]
    return jnp.concatenate([ref[:, c:c + HEAD_DIM] for c in cols], axis=0).astype(BF)


def _tile_heads(maskf):
    return jnp.concatenate([maskf] * Q_PER_KV, axis=0) > 0.5


def _nsa_prompt_kernel(q_ref, qr_ref, cmp_ref, kvs_ref, kvw_ref, sm_ref, ov_ref, e_ref, o_ref, *, t_len, tq, kc):
    t0 = pl.program_id(1) * tq
    nr = Q_PER_KV * tq
    trow = t0 + lax.broadcasted_iota(jnp.int32, (tq, 1), 0)
    gates = jax.nn.sigmoid(sm_ref[...])
    lane = lax.broadcasted_iota(jnp.int32, (tq, LANES), 1)
    cmask = _tile_heads(jnp.where((lane * CMP_STRIDE + (2 * CMP_STRIDE - 1)) <= trow, 1.0, 0.0))
    n_sel = t_len // SLC_BLOCK
    valid = (lane * SLC_BLOCK) <= trow
    cur = trow // SLC_BLOCK
    forced = (lane == 0) | (lane == cur) | (lane == cur - 1)
    n_win = WINDOW + tq
    start = pl.multiple_of(jnp.clip(t0 - WINDOW, 0, t_len - n_win), LANES)
    dist = trow - (start + lax.broadcasted_iota(jnp.int32, (tq, n_win), 1))
    wmask = _tile_heads(jnp.where(dist >= 0, jnp.where(dist < WINDOW, 1.0, 0.0), 0.0))
    kiota = lax.broadcasted_iota(jnp.int32, (tq, kc), 1)
    n_chunks = (t0 + tq - 1) // kc + 1

    for g in range(N_KV):
        krow = slice(g * HEAD_DIM, (g + 1) * HEAD_DIM)
        vrow = slice(LANES + g * HEAD_DIM, LANES + (g + 1) * HEAD_DIM)
        p = _msoftmax(_dot_nt(_stack_heads(q_ref, g), cmp_ref[g, :, 0:HEAD_DIM].astype(BF)) * SCALE, cmask)
        o_cmp = _dot(p.astype(BF), cmp_ref[g, :, HEAD_DIM:LANES].astype(BF))
        psum = p[0:tq]
        for r in range(1, Q_PER_KV):
            psum = psum + p[r * tq:(r + 1) * tq]
        imp = _dot3(psum, ov_ref[...])
        imp = jnp.where(valid, jnp.where(forced, FORCE_SCORE, imp), -FORCE_SCORE)
        sel_b = _rank_select(imp, n_sel).astype(BF)

        qs = _stack_heads(qr_ref, g)

        def chunk(ci, carry, qs=qs, sel_b=sel_b, krow=krow, vrow=vrow):
            m, l, acc = carry
            k0 = pl.multiple_of(ci * kc, kc)
            kt = kvs_ref[krow, pl.ds(k0, kc)].astype(BF)
            vt = kvs_ref[vrow, pl.ds(k0, kc)].astype(BF)
            s = _dot(qs, kt) * SCALE
            sel_k = _dot(sel_b, e_ref[:, pl.ds(k0, kc)])
            ok = _tile_heads(jnp.where(k0 + kiota <= trow, sel_k, 0.0))
            s = jnp.where(ok, s, NEG_BIG)
            m_new = jnp.maximum(m, jnp.max(s, axis=1, keepdims=True))
            alpha = jnp.exp(m - m_new)
            e = jnp.where(ok, jnp.exp(s - m_new), 0.0)
            l = alpha * l + jnp.sum(e, axis=1, keepdims=True)
            acc = alpha * acc + _dot_nt(e.astype(BF), vt)
            return m_new, l, acc

        init = (jnp.full((nr, 1), NEG_BIG, F32), jnp.zeros((nr, 1), F32), jnp.zeros((nr, HEAD_DIM), F32))
        _, l, acc = lax.fori_loop(0, n_chunks, chunk, init)
        o_slc = acc / jnp.where(l > 0.0, l, 1.0)

        kw = kvw_ref[krow, pl.ds(start, n_win)].astype(BF)
        vw = kvw_ref[vrow, pl.ds(start, n_win)].astype(BF)
        p = _msoftmax(_dot(qs, kw) * SCALE, wmask)
        o_win = _dot_nt(p.astype(BF), vw)

        for r in range(Q_PER_KV):
            h = g * Q_PER_KV + r
            rows = slice(r * tq, (r + 1) * tq)
            o = (gates[:, 3 * h:3 * h + 1] * o_cmp[rows] + gates[:, 3 * h + 1:3 * h + 2] * o_slc[rows]
                 + gates[:, 3 * h + 2:3 * h + 3] * o_win[rows])
            o_ref[:, h * HEAD_DIM:(h + 1) * HEAD_DIM] = o


def _overlap_matrix(n_rows, n_cols, row_shift):
    n = jnp.arange(n_rows)[:, None] - row_shift
    c_start = n * CMP_STRIDE
    s_start = jnp.arange(n_cols)[None, :] * SLC_BLOCK
    ov = (c_start < s_start + SLC_BLOCK) & (c_start + 2 * CMP_STRIDE > s_start) & (n >= 0)
    return ov.astype(BF)


def _nsa_prompt(q, qr, cmp, kvs_t, kvw_t, small, b, t):
    tq = 128
    kc = 512
    nq = t // tq
    n_half = t // CMP_STRIDE
    assert n_half == LANES and t // SLC_BLOCK <= LANES and t % kc == 0
    ov = _overlap_matrix(LANES, LANES, 0)
    expand = (jnp.arange(LANES)[:, None] == (jnp.arange(t)[None, :] // SLC_BLOCK)).astype(BF)
    tile = lambda w: pl.BlockSpec((tq, w), lambda i, j: (i * nq + j, 0))
    seq = pl.BlockSpec((None, KV_DIM, t), lambda i, j: (i, 0, 0))
    return pl.pallas_call(
        functools.partial(_nsa_prompt_kernel, t_len=t, tq=tq, kc=kc),
        grid=(b, nq),
        in_specs=[tile(Q_DIM), tile(Q_DIM),
                  pl.BlockSpec((None, N_KV, n_half, LANES), lambda i, j: (i, 0, 0, 0)),
                  seq, seq, tile(LANES),
                  pl.BlockSpec(ov.shape, lambda i, j: (0, 0)),
                  pl.BlockSpec(expand.shape, lambda i, j: (0, 0))],
        out_specs=tile(Q_DIM),
        out_shape=jax.ShapeDtypeStruct((b * t, Q_DIM), F32),
        compiler_params=_cparams(("parallel", "parallel")),
        name="nsa_prompt",
    )(q, qr, cmp, kvs_t, kvw_t, small, ov, expand)


def _ssd_prompt_kernel(zx_ref, sm_ref, cw_ref, cb_ref, dtb_ref, alog_ref, dskip_ref, nrm_ref, tri_ref,
                       y_ref, hout_ref, tail_ref, h_ref):
    c = pl.program_id(1)
    lc = SSD_CHUNK

    @pl.when(c == 0)
    def _():
        tail_ref[...] = jnp.zeros_like(tail_ref)
        h_ref[...] = jnp.zeros_like(h_ref)

    z = zx_ref[:, 0:SSD_INNER]
    xbc = zx_ref[:, SSD_INNER:]
    u = jnp.concatenate([tail_ref[...], xbc], axis=0)
    tail_ref[...] = xbc[lc - 8:lc, :]
    conv = cb_ref[...]
    for k in range(CONV_WIDTH):
        conv = conv + u[5 + k:5 + k + lc, :] * cw_ref[k:k + 1, :]
    xbc_c = _silu(conv)
    xs = xbc_c[:, 0:SSD_INNER]

    dt_full = _softplus(sm_ref[...] + dtb_ref[...])
    a_full = dt_full * (-jnp.exp(alog_ref[...]))
    acs = _dot3_left(tri_ref[...], a_full)
    acs_t = acs.T
    li = lax.broadcasted_iota(jnp.int32, (lc, lc), 0)
    si = lax.broadcasted_iota(jnp.int32, (lc, lc), 1)
    lower = li >= si

    ys = []
    for g in range(2):
        bm = xbc_c[:, SSD_INNER + g * SSD_STATE:SSD_INNER + (g + 1) * SSD_STATE]
        cm = xbc_c[:, SSD_INNER + (2 + g) * SSD_STATE:SSD_INNER + (3 + g) * SSD_STATE]
        bm_b = bm.astype(BF)
        cm_b = cm.astype(BF)
        cb = _dot_nt(cm_b, bm_b)
        for hh in range(SSD_HEADS // 2):
            h = g * (SSD_HEADS // 2) + hh
            col = DT_COL + h
            dt_h = dt_full[:, col:col + 1]
            acs_h = acs[:, col:col + 1]
            acs_row = acs_t[col:col + 1, :]
            acs_last = acs[lc - 1:lc, col:col + 1]
            lmat = jnp.where(lower, jnp.exp(acs_h - acs_row), 0.0)
            x_h = xs[:, h * SSD_HEAD_DIM:(h + 1) * SSD_HEAD_DIM]
            xdt = x_h * dt_h
            y_diag = _dot((cb * lmat).astype(BF), xdt.astype(BF))
            h_prev = h_ref[h]
            y_off = _dot_nt(cm_b, h_prev.astype(BF)) * jnp.exp(acs_h)
            decay = jnp.exp(acs_last - acs_h)
            h_new = jnp.exp(acs_last) * h_prev + _dot_tn((xdt * decay).astype(BF), bm_b)
            h_ref[h] = h_new
            ys.append(y_diag + y_off)
    y = jnp.concatenate(ys, axis=1) + dskip_ref[...] * xs
    y = y * _silu(z)
    y_ref[...] = _rms(y, nrm_ref[...])
    hout_ref[...] = h_ref[...]


def _ssd_params(conv_w, conv_b, dt_bias, a_log, d_skip, ssd_norm):
    pad = lambda v: jnp.pad(v, (DT_COL, LANES - DT_COL - SSD_HEADS)).reshape(1, LANES)
    return (conv_w, conv_b.reshape(1, -1), pad(dt_bias), pad(a_log),
            jnp.repeat(d_skip, SSD_HEAD_DIM).reshape(1, SSD_INNER), ssd_norm.reshape(1, SSD_INNER))


def _ssd_prompt(zx, small, params, b, t):
    lc = SSD_CHUNK
    nc = t // lc
    cw, cb, dtb, alog, dskip, nrm = params
    tri = (jnp.arange(lc)[:, None] >= jnp.arange(lc)[None, :]).astype(BF)
    zx_dim = zx.shape[1]
    conv_dim = zx_dim - SSD_INNER
    full = lambda a: pl.BlockSpec(a.shape, lambda i, j: (0,) * a.ndim)
    tile = lambda w: pl.BlockSpec((lc, w), lambda i, j: (i * nc + j, 0))
    return pl.pallas_call(
        _ssd_prompt_kernel,
        grid=(b, nc),
        in_specs=[tile(zx_dim), tile(LANES), full(cw), full(cb), full(dtb), full(alog), full(dskip), full(nrm),
                  full(tri)],
        out_specs=[tile(SSD_INNER),
                   pl.BlockSpec((None, SSD_HEADS, SSD_HEAD_DIM, SSD_STATE), lambda i, j: (i, 0, 0, 0))],
        out_shape=[jax.ShapeDtypeStruct((b * t, SSD_INNER), F32),
                   jax.ShapeDtypeStruct((b, SSD_HEADS, SSD_HEAD_DIM, SSD_STATE), F32)],
        scratch_shapes=[pltpu.VMEM((8, conv_dim), F32), pltpu.VMEM((SSD_HEADS, SSD_HEAD_DIM, SSD_STATE), F32)],
        compiler_params=_cparams(("parallel", "arbitrary")),
        name="ssd_prompt",
    )(zx, small, cw, cb, dtb, alog, dskip, nrm, tri)


def _outproj_kernel(x_ref, a_ref, b_ref, wa_ref, wb_ref, o_ref):
    o_ref[...] = (x_ref[...] + _dot(a_ref[...].astype(BF), wa_ref[...])
                  + _dot(b_ref[...].astype(BF), wb_ref[...]))


def _outproj(x2d, a, bb, wa, wb, tm):
    m, d = x2d.shape
    row = lambda w: pl.BlockSpec((tm, w), lambda i: (i, 0))
    full = lambda arr: pl.BlockSpec(arr.shape, lambda i: (0, 0))
    return pl.pallas_call(
        _outproj_kernel,
        grid=(m // tm,),
        in_specs=[row(d), row(a.shape[1]), row(bb.shape[1]), full(wa), full(wb)],
        out_specs=row(d),
        out_shape=jax.ShapeDtypeStruct((m, d), F32),
        compiler_params=_cparams(("parallel",)),
        name="outproj_a",
    )(x2d, a, bb, wa, wb)


def _ffn_kernel(x_ref, g_ref, wg_ref, wu_ref, wd_ref, gf_ref, o_ref, *, final_norm):
    x = x_ref[...]
    xn = _rms(x, g_ref[...]).astype(BF)
    h = _silu(_dot(xn, wg_ref[...])) * _dot(xn, wu_ref[...])
    y = x + _dot(h.astype(BF), wd_ref[...])
    if final_norm:
        y = _rms(y, gf_ref[...])
    o_ref[...] = y


def _ffn(x2d, g, wg, wu, wd, gf, tm, final_norm, in_map, out_map, out_2d_shape):
    m, d = x2d.shape[0] * x2d.shape[1] // wg.shape[0], wg.shape[0]
    n_steps = m // tm
    full = lambda arr: pl.BlockSpec(arr.shape, lambda i: (0, 0))
    return pl.pallas_call(
        functools.partial(_ffn_kernel, final_norm=final_norm),
        grid=(n_steps,),
        in_specs=[pl.BlockSpec((tm, d), in_map), full(g), full(wg), full(wu), full(wd), full(gf)],
        out_specs=pl.BlockSpec((tm, d), out_map),
        out_shape=jax.ShapeDtypeStruct(out_2d_shape, F32),
        compiler_params=_cparams(("parallel",)),
        name="ffn_final" if final_norm else "ffn",
    )(x2d, g, wg, wu, wd, gf)


def _lru_gates(xc, wa_ref, ba_ref, wx_ref, bx_ref, sp):
    n_heads = xc.shape[1] // LRU_BLOCK
    a_parts = []
    gx_parts = []
    for h in range(n_heads):
        sl = slice(h * LRU_BLOCK, (h + 1) * LRU_BLOCK)
        xh = xc[:, sl]
        xb = xh.astype(BF)
        r = jax.nn.sigmoid(_dot(xb, wa_ref[h]) + ba_ref[:, sl])
        i = jax.nn.sigmoid(_dot(xb, wx_ref[h]) + bx_ref[:, sl])
        log_a = -LRU_C * r * sp[:, sl]
        a = jnp.exp(log_a)
        a_parts.append(a)
        gx_parts.append(jnp.sqrt(1.0 - a * a) * (i * xh))
    return jnp.concatenate(a_parts, axis=1), jnp.concatenate(gx_parts, axis=1)


def _lru_prompt_kernel(x_ref, g_ref, win_ref, cw_ref, cb_ref, wa_ref, ba_ref, wx_ref, bx_ref, lam_ref, wo_ref,
                       o_ref, hout_ref, cout_ref, prev_ref, h_ref, a_s, gx_s, hs_s, *, nb, tt, w):
    @pl.when(pl.program_id(0) == 0)
    def _():
        prev_ref[...] = jnp.zeros_like(prev_ref)
        h_ref[...] = jnp.zeros_like(h_ref)

    rows = tt * nb
    x = x_ref[...]
    xn = _rms(x, g_ref[...]).astype(BF)
    proj = _dot(xn, win_ref[...])
    gate_br = proj[:, 0:w]
    x_br = proj[:, w:2 * w]
    n_prev = (CONV_WIDTH - 1) * nb
    u = jnp.concatenate([prev_ref[...], x_br], axis=0)
    prev_ref[...] = x_br[rows - n_prev:rows, :]
    cout_ref[...] = x_br[rows - n_prev:rows, :]
    xc = cb_ref[...]
    for k in range(CONV_WIDTH):
        xc = xc + u[k * nb:k * nb + rows, :] * cw_ref[k:k + 1, :]
    sp = _softplus(-lam_ref[...])
    a, gx = _lru_gates(xc, wa_ref, ba_ref, wx_ref, bx_ref, sp)
    a_s[...] = a
    gx_s[...] = gx
    h = h_ref[...]
    for t in range(tt):
        h = a_s[t * nb:(t + 1) * nb, :] * h + gx_s[t * nb:(t + 1) * nb, :]
        hs_s[t * nb:(t + 1) * nb, :] = h
    h_ref[...] = h
    hout_ref[...] = h
    y = (jax.nn.gelu(gate_br) * hs_s[...]).astype(BF)
    o_ref[...] = x + _dot(y, wo_ref[...])


def _lru_prompt(x_tm, g, w_in, cw, cb, wa, ba, wx, bx, lam, wo, nb, tt):
    m, d = x_tm.shape
    w = cw.shape[1]
    rows = tt * nb
    n_prev = (CONV_WIDTH - 1) * nb
    full = lambda a: pl.BlockSpec(a.shape, lambda i: (0,) * a.ndim)
    args = (g, w_in, cw, cb, wa, ba, wx, bx, lam, wo)
    return pl.pallas_call(
        functools.partial(_lru_prompt_kernel, nb=nb, tt=tt, w=w),
        grid=(m // rows,),
        in_specs=[pl.BlockSpec((rows, d), lambda i: (i, 0))] + [full(a) for a in args],
        out_specs=[pl.BlockSpec((rows, d), lambda i: (i, 0)),
                   pl.BlockSpec((nb, w), lambda i: (0, 0)),
                   pl.BlockSpec((n_prev, w), lambda i: (0, 0))],
        out_shape=[jax.ShapeDtypeStruct((m, d), F32),
                   jax.ShapeDtypeStruct((nb, w), F32),
                   jax.ShapeDtypeStruct((n_prev, w), F32)],
        scratch_shapes=[pltpu.VMEM((n_prev, w), F32), pltpu.VMEM((nb, w), F32),
                        pltpu.VMEM((rows, w), F32), pltpu.VMEM((rows, w), F32), pltpu.VMEM((rows, w), F32)],
        compiler_params=_cparams(("arbitrary",)),
        name="lru_prompt",
    )(x_tm, *args)


def _lru_params(norm_g, w_in_c, conv_w, conv_b, w_a, b_a, w_x, b_x, lam, w_out_c):
    r = lambda v: v.reshape(1, -1)
    return (r(norm_g), w_in_c.astype(BF), conv_w, r(conv_b), w_a.astype(BF), r(b_a), w_x.astype(BF), r(b_x),
            r(lam), w_out_c.astype(BF))


def _lru_decode_kernel(x_ref, g_ref, win_ref, cw_ref, cb_ref, wa_ref, ba_ref, wx_ref, bx_ref, lam_ref, wo_ref,
                       c0_ref, c1_ref, c2_ref, h0_ref, o_ref, hout_ref, xbr_ref, *, w):
    x = x_ref[...]
    xn = _rms(x, g_ref[...]).astype(BF)
    proj = _dot(xn, win_ref[...])
    gate_br = proj[:, 0:w]
    x_br = proj[:, w:2 * w]
    xbr_ref[...] = x_br
    xc = (cb_ref[...] + c0_ref[...] * cw_ref[0:1, :] + c1_ref[...] * cw_ref[1:2, :]
          + c2_ref[...] * cw_ref[2:3, :] + x_br * cw_ref[3:4, :])
    sp = _softplus(-lam_ref[...])
    a, gx = _lru_gates(xc, wa_ref, ba_ref, wx_ref, bx_ref, sp)
    h = a * h0_ref[...] + gx
    hout_ref[...] = h
    y = (jax.nn.gelu(gate_br) * h).astype(BF)
    o_ref[...] = x + _dot(y, wo_ref[...])


def _lru_decode(x, params, conv_state, h0):
    m, d = x.shape
    w = h0.shape[1]
    args = (x,) + tuple(params) + (conv_state[:, 0], conv_state[:, 1], conv_state[:, 2], h0)
    full = lambda a: pl.BlockSpec(a.shape, lambda i: (0,) * a.ndim)
    return pl.pallas_call(
        functools.partial(_lru_decode_kernel, w=w),
        grid=(1,),
        in_specs=[full(a) for a in args],
        out_specs=[pl.BlockSpec((m, d), lambda i: (0, 0)), pl.BlockSpec((m, w), lambda i: (0, 0)),
                   pl.BlockSpec((m, w), lambda i: (0, 0))],
        out_shape=[jax.ShapeDtypeStruct((m, d), F32), jax.ShapeDtypeStruct((m, w), F32),
                   jax.ShapeDtypeStruct((m, w), F32)],
        compiler_params=_cparams(("arbitrary",)),
        name="lru_decode",
    )(*args)


PAGES_PER_STEP = 16
HALVES_PER_PAGE = PAGE_SIZE // CMP_STRIDE


def _compress_decode_kernel(pt_ref, *refs):
    del pt_ref
    np_ = PAGES_PER_STEP
    k_pages = refs[0:np_]
    v_pages = refs[np_:2 * np_]
    pek_ref, w1k_ref, w2k_ref, pev_ref, w1v_ref, w2v_ref, out_ref, carry_ref, rows_ref = refs[2 * np_:]
    rows = np_ * HALVES_PER_PAGE

    @pl.when(pl.program_id(1) == 0)
    def _():
        carry_ref[...] = jnp.zeros_like(carry_ref)

    rowi = lax.broadcasted_iota(jnp.int32, (rows, CMP_HIDDEN), 0)
    branches = ((k_pages, pek_ref, w1k_ref, w2k_ref), (v_pages, pev_ref, w1v_ref, w2v_ref))
    for c, (pages, pe_ref, w1_ref, w2_ref) in enumerate(branches):
        for k, p in enumerate(pages):
            rows_ref[c, k * PAGE_SIZE:(k + 1) * PAGE_SIZE, :] = p[...].T

        def load_rows(s, c=c):
            return rows_ref[c, pl.ds(s, rows, stride=CMP_STRIDE), :]

        for g in range(N_KV):
            a0, a1 = _compress_accumulate(load_rows, pe_ref, w1_ref, g * HEAD_DIM)
            slot = c * N_KV + g
            lead = jnp.where(rowi == 0, carry_ref[slot, 7:8, :], pltpu.roll(a0, 1, 0))
            carry_ref[slot] = a0[rows - 8:rows, :]
            tok = _dot(_silu(lead + a1).astype(BF), w2_ref[...])
            out_ref[g, :, c * HEAD_DIM:(c + 1) * HEAD_DIM] = tok


def _compress_decode(cache_t, pt_flat, db, n_pages, pek, w1k, w2k, pev, w1v, w2v):
    np_ = PAGES_PER_STEP
    n_steps = n_pages // np_
    rows = np_ * HALVES_PER_PAGE
    n_half = n_pages * HALVES_PER_PAGE

    def page_spec(k, rowblk):
        return pl.BlockSpec((None, LANES, PAGE_SIZE),
                            lambda i, j, pt: (pt[i * n_pages + j * np_ + k], rowblk, 0))

    full = lambda a: pl.BlockSpec(a.shape, lambda i, j, pt: (0,) * a.ndim)
    grid_spec = pltpu.PrefetchScalarGridSpec(
        num_scalar_prefetch=1,
        grid=(db, n_steps),
        in_specs=[page_spec(k, 0) for k in range(np_)] + [page_spec(k, 1) for k in range(np_)]
        + [full(pek), full(w1k), full(w2k), full(pev), full(w1v), full(w2v)],
        out_specs=pl.BlockSpec((None, N_KV, rows, LANES), lambda i, j, pt: (i, 0, j, 0)),
        scratch_shapes=[pltpu.VMEM((2 * N_KV, 8, CMP_HIDDEN), F32), pltpu.VMEM((2, np_ * PAGE_SIZE, LANES), F32)],
    )
    return pl.pallas_call(
        _compress_decode_kernel,
        grid_spec=grid_spec,
        out_shape=jax.ShapeDtypeStruct((db, N_KV, n_half, LANES), F32),
        compiler_params=_cparams(("parallel", "arbitrary")),
        name="compress_decode",
    )(pt_flat, *([cache_t] * (2 * np_)), pek, w1k, w2k, pev, w1v, w2v)


def _group_rows(top, a, b):
    return jnp.where(top, a, b)


def _nsa_decode_a_kernel(q_ref, qr_ref, cmp_ref, win_ref, new_ref, ov_ref, ocmp_ref, owin_ref, idx_ref, *,
                         pos, n_blocks):
    nh = N_HEADS
    top = lax.broadcasted_iota(jnp.int32, (nh, 1), 0) < Q_PER_KV
    q = q_ref[...].astype(BF)
    n_tok = cmp_ref.shape[1]
    lane = lax.broadcasted_iota(jnp.int32, (nh, n_tok), 1)
    cmask = (lane >= 1) & (((lane - 1) * CMP_STRIDE + (2 * CMP_STRIDE - 1)) <= pos)
    s = _group_rows(top, _dot_nt(q, cmp_ref[0, :, 0:HEAD_DIM].astype(BF)),
                    _dot_nt(q, cmp_ref[1, :, 0:HEAD_DIM].astype(BF))) * SCALE
    p = _msoftmax(s, cmask)
    pb = p.astype(BF)
    ocmp_ref[...] = _group_rows(top, _dot(pb, cmp_ref[0, :, HEAD_DIM:LANES].astype(BF)),
                                _dot(pb, cmp_ref[1, :, HEAD_DIM:LANES].astype(BF)))

    imp8 = _dot3(p, ov_ref[...])
    nj = ov_ref.shape[1]
    cur = pos // SLC_BLOCK
    j = lax.broadcasted_iota(jnp.int32, (1, nj), 1)
    valid = (j * SLC_BLOCK) <= pos
    forced = (j == 0) | (j == cur) | (j == cur - 1)
    ri = lax.broadcasted_iota(jnp.int32, (nj, nj), 0)
    ci = lax.broadcasted_iota(jnp.int32, (nj, nj), 1)
    k_lane = lax.broadcasted_iota(jnp.int32, (nj, LANES), 1).astype(F32)
    jvals = lax.broadcasted_iota(jnp.int32, (8, nj), 1).astype(F32).astype(BF)
    idx_rows = []
    for g in range(N_KV):
        imp = jnp.sum(imp8[g * Q_PER_KV:(g + 1) * Q_PER_KV, :], axis=0, keepdims=True)
        imp = jnp.where(valid & forced, FORCE_SCORE, imp)
        imp = jnp.where(valid, imp, -FORCE_SCORE)
        imp = jnp.where(j < n_blocks, imp, -3e38)
        impb = jnp.broadcast_to(imp, (nj, nj))
        col = jnp.sum(jnp.where(ri == ci, impb, 0.0), axis=1, keepdims=True)
        ahead = jnp.where(impb == col, jnp.where(ci < ri, 1.0, 0.0), jnp.where(impb > col, 1.0, 0.0))
        rank_col = jnp.sum(ahead, axis=1, keepdims=True)
        onehot = jnp.where(rank_col == k_lane, 1.0, 0.0).astype(BF)
        idx_rows.append(_dot(jvals, onehot)[0:1, :])
    idx = jnp.concatenate(idx_rows + [jnp.zeros((8 - N_KV, LANES), F32)], axis=0)
    idx_ref[...] = idx.astype(jnp.int32)

    qr = qr_ref[...]
    qrb = qr.astype(BF)
    n_win = win_ref.shape[1]
    s = _group_rows(top, _dot(qrb, win_ref[0:HEAD_DIM, :].astype(BF)),
                    _dot(qrb, win_ref[HEAD_DIM:LANES, :].astype(BF))) * SCALE
    wl = lax.broadcasted_iota(jnp.int32, (nh, n_win), 1)
    dist = n_win - wl
    wmask = (dist >= 0) & (dist < WINDOW)
    new = new_ref[...]
    knew = _group_rows(top, new[:, 0:HEAD_DIM], new[:, HEAD_DIM:LANES])
    vnew = _group_rows(top, new[:, LANES:LANES + HEAD_DIM], new[:, LANES + HEAD_DIM:KV_DIM])
    s_new = jnp.sum(qr * knew, axis=1, keepdims=True) * SCALE
    sm = jnp.where(wmask, s, NEG_BIG)
    m = jnp.maximum(jnp.max(sm, axis=1, keepdims=True), s_new)
    e = jnp.where(wmask, jnp.exp(sm - m), 0.0)
    e_new = jnp.exp(s_new - m)
    d = jnp.sum(e, axis=1, keepdims=True) + e_new
    eb = e.astype(BF)
    o = _group_rows(top, _dot_nt(eb, win_ref[LANES:LANES + HEAD_DIM, :].astype(BF)),
                    _dot_nt(eb, win_ref[LANES + HEAD_DIM:KV_DIM, :].astype(BF))) + e_new * vnew
    owin_ref[...] = o / d


def _nsa_decode_a(q3, qr3, cmp_s, win, kvw_new3, pos):
    db = q3.shape[0]
    n_tok = cmp_s.shape[2]
    n_blocks = -(-(pos + 1) // SLC_BLOCK)
    nj = -(-n_blocks // LANES) * LANES
    ov = _overlap_matrix(n_tok, nj, 1)
    per_b = lambda a: pl.BlockSpec((None,) + a.shape[1:], lambda i: (i,) + (0,) * (a.ndim - 1))
    head = pl.BlockSpec((None, N_HEADS, HEAD_DIM), lambda i: (i, 0, 0))
    return pl.pallas_call(
        functools.partial(_nsa_decode_a_kernel, pos=pos, n_blocks=n_blocks),
        grid=(db,),
        in_specs=[head, head, per_b(cmp_s), per_b(win), per_b(kvw_new3), pl.BlockSpec(ov.shape, lambda i: (0, 0))],
        out_specs=[head, head, pl.BlockSpec((None, 8, LANES), lambda i: (i, 0, 0))],
        out_shape=[jax.ShapeDtypeStruct((db, N_HEADS, HEAD_DIM), F32),
                   jax.ShapeDtypeStruct((db, N_HEADS, HEAD_DIM), F32),
                   jax.ShapeDtypeStruct((db, 8, LANES), jnp.int32)],
        compiler_params=_cparams(("parallel",)),
        name="nsa_decode_select",
    )(q3, qr3, cmp_s, win, kvw_new3, ov)


def _nsa_decode_b_kernel(idx_ref, pt_ref, qr_ref, *refs, n_past_blocks, cur):
    del pt_ref
    nb = N_KV * N_SELECT
    blk_refs = refs[0:nb]
    new_ref, ocmp_ref, owin_ref, sm_ref, o_ref = refs[nb:]
    b = pl.program_id(0)
    nh = N_HEADS
    top = lax.broadcasted_iota(jnp.int32, (nh, 1), 0) < Q_PER_KV
    qr = qr_ref[...]
    qrb = qr.astype(BF)
    per_page = PAGE_SIZE // SLC_BLOCK
    n_keys = N_SELECT * PAGE_SIZE
    lane = lax.broadcasted_iota(jnp.int32, (nh, n_keys), 1)
    lane_page = lane // PAGE_SIZE
    lane_sub = (lane // SLC_BLOCK) % per_page
    s_g = []
    v_g = []
    ok_g = []
    has_new = []
    for g in range(N_KV):
        blks = blk_refs[g * N_SELECT:(g + 1) * N_SELECT]
        kt = jnp.concatenate([r[g * HEAD_DIM:(g + 1) * HEAD_DIM, :] for r in blks], axis=1).astype(BF)
        v_g.append(jnp.concatenate([r[LANES + g * HEAD_DIM:LANES + (g + 1) * HEAD_DIM, :] for r in blks],
                                   axis=1).astype(BF))
        s_g.append(_dot(qrb, kt))
        ok = jnp.zeros((nh, n_keys), F32)
        new_sel = jnp.zeros((), F32)
        for k in range(N_SELECT):
            jk = idx_ref[(b * N_KV + g) * N_SELECT + k]
            hit = (lane_page == k) & (lane_sub == jk % per_page)
            ok = jnp.where(hit, jnp.where(jk < n_past_blocks, 1.0, 0.0), ok)
            new_sel = jnp.maximum(new_sel, jnp.where(jk == cur, 1.0, 0.0))
        ok_g.append(ok)
        has_new.append(new_sel)
    s = _group_rows(top, s_g[0], s_g[1]) * SCALE
    mask = _group_rows(top, ok_g[0], ok_g[1]) > 0.5
    new_on = _group_rows(top, has_new[0], has_new[1]) > 0.5
    new = new_ref[...]
    knew = _group_rows(top, new[:, 0:HEAD_DIM], new[:, HEAD_DIM:LANES])
    vnew = _group_rows(top, new[:, LANES:LANES + HEAD_DIM], new[:, LANES + HEAD_DIM:KV_DIM])
    s_new = jnp.where(new_on, jnp.sum(qr * knew, axis=1, keepdims=True) * SCALE, NEG_BIG)
    sm = jnp.where(mask, s, NEG_BIG)
    m = jnp.maximum(jnp.max(sm, axis=1, keepdims=True), s_new)
    e = jnp.where(mask, jnp.exp(sm - m), 0.0)
    e_new = jnp.where(new_on, jnp.exp(s_new - m), 0.0)
    d = jnp.sum(e, axis=1, keepdims=True) + e_new
    eb = e.astype(BF)
    o_slc = ((_group_rows(top, _dot_nt(eb, v_g[0]), _dot_nt(eb, v_g[1])) + e_new * vnew)
             / jnp.where(d > 0.0, d, 1.0))

    sig = jnp.broadcast_to(jax.nn.sigmoid(sm_ref[...]), (nh, LANES))
    hl = lax.broadcasted_iota(jnp.int32, (nh, LANES), 1)
    hr = lax.broadcasted_iota(jnp.int32, (nh, LANES), 0)

    def gate(br):
        return jnp.sum(jnp.where(hl == 3 * hr + br, sig, 0.0), axis=1, keepdims=True)

    o_ref[...] = gate(0) * ocmp_ref[...] + gate(1) * o_slc + gate(2) * owin_ref[...]


def _nsa_decode_b(idx_flat, pt_flat, qr3, slc_pages_t, kvs_new3, o_cmp, o_win, small3, pos, n_pages):
    db = qr3.shape[0]
    n_past_blocks = pos // SLC_BLOCK
    cur = pos // SLC_BLOCK
    per_page = PAGE_SIZE // SLC_BLOCK

    def blk_spec(g, k):
        def imap(i, idx, pt):
            jk = jnp.minimum(idx[(i * N_KV + g) * N_SELECT + k], n_past_blocks - 1)
            return (pt[i * n_pages + jk // per_page], 0, 0)
        return pl.BlockSpec((None, KV_DIM, PAGE_SIZE), imap)

    head = pl.BlockSpec((None, N_HEADS, HEAD_DIM), lambda i, idx, pt: (i, 0, 0))
    row3 = lambda a: pl.BlockSpec((None,) + a.shape[1:], lambda i, idx, pt: (i, 0, 0))
    grid_spec = pltpu.PrefetchScalarGridSpec(
        num_scalar_prefetch=2,
        grid=(db,),
        in_specs=[head] + [blk_spec(g, k) for g in range(N_KV) for k in range(N_SELECT)]
        + [row3(kvs_new3), head, head, row3(small3)],
        out_specs=head,
    )
    return pl.pallas_call(
        functools.partial(_nsa_decode_b_kernel, n_past_blocks=n_past_blocks, cur=cur),
        grid_spec=grid_spec,
        out_shape=jax.ShapeDtypeStruct((db, N_HEADS, HEAD_DIM), F32),
        compiler_params=_cparams(("arbitrary",)),
        name="nsa_decode_attend",
    )(idx_flat, pt_flat, qr3, *([slc_pages_t] * (N_KV * N_SELECT)), kvs_new3, o_cmp, o_win, small3)


def _ssd_decode_kernel(zx_ref, cst_ref, sm_ref, h0_ref, cw_ref, cb_ref, dtb_ref, alog_ref, dskip_ref, nrm_ref,
                       y_ref, hout_ref):
    z = zx_ref[:, 0:SSD_INNER]
    xbc = zx_ref[:, SSD_INNER:]
    conv = cb_ref[...] + xbc * cw_ref[CONV_WIDTH - 1:CONV_WIDTH, :]
    for k in range(CONV_WIDTH - 1):
        conv = conv + cst_ref[k:k + 1, :] * cw_ref[k:k + 1, :]
    xbc_c = _silu(conv)
    xs = xbc_c[:, 0:SSD_INNER]
    dt_full = _softplus(sm_ref[...] + dtb_ref[...])
    da_full = jnp.exp(dt_full * (-jnp.exp(alog_ref[...])))
    p = SSD_HEAD_DIM
    eye = lax.broadcasted_iota(jnp.int32, (p, p), 0) == lax.broadcasted_iota(jnp.int32, (p, p), 1)
    ys = []
    for h in range(SSD_HEADS):
        g = h // (SSD_HEADS // 2)
        col = DT_COL + h
        xdt = xs[:, h * p:(h + 1) * p] * dt_full[:, col:col + 1]
        xcol = jnp.sum(jnp.where(eye, jnp.broadcast_to(xdt, (p, p)), 0.0), axis=1, keepdims=True)
        bm = xbc_c[:, SSD_INNER + g * SSD_STATE:SSD_INNER + (g + 1) * SSD_STATE]
        cm = xbc_c[:, SSD_INNER + (2 + g) * SSD_STATE:SSD_INNER + (3 + g) * SSD_STATE]
        h_new = da_full[:, col:col + 1] * h0_ref[h] + xcol * bm
        hout_ref[h] = h_new
        ys.append(_dot_nt(jnp.broadcast_to(cm, (8, SSD_STATE)).astype(BF), h_new.astype(BF))[0:1, :])
    y = jnp.concatenate(ys, axis=1) + dskip_ref[...] * xs
    y = y * _silu(z)
    y_ref[...] = _rms(y, nrm_ref[...])


def _ssd_decode(zx3, conv_state, small3, h0, params):
    db = zx3.shape[0]
    cw, cb, dtb, alog, dskip, nrm = params
    per_b = lambda a: pl.BlockSpec((None,) + a.shape[1:], lambda i: (i,) + (0,) * (a.ndim - 1))
    full = lambda a: pl.BlockSpec(a.shape, lambda i: (0,) * a.ndim)
    return pl.pallas_call(
        _ssd_decode_kernel,
        grid=(db,),
        in_specs=[per_b(zx3), per_b(conv_state), per_b(small3), per_b(h0), full(cw), full(cb), full(dtb), full(alog),
                  full(dskip), full(nrm)],
        out_specs=[pl.BlockSpec((None, 1, SSD_INNER), lambda i: (i, 0, 0)), per_b(h0)],
        out_shape=[jax.ShapeDtypeStruct((db, 1, SSD_INNER), F32), jax.ShapeDtypeStruct(h0.shape, F32)],
        compiler_params=_cparams(("parallel",)),
        name="ssd_decode",
    )(zx3, conv_state, small3, h0, cw, cb, dtb, alog, dskip, nrm)


def kernel(x_prompt, x_sample, cache_kv_cmp, cache_kv_slc, cache_kv_win, state_ssm, state_ssd_conv, state_lru,
           state_lru_conv, page_table, norm_mix, norm_ffn, norm_final, w_ffn_gate, w_ffn_up, w_ffn_down, w_in_a,
           w_out_a, cmp_pe_k, cmp_w1_k, cmp_w2_k, cmp_pe_v, cmp_w1_v, cmp_w2_v, ssd_conv_w, ssd_conv_b, ssd_dt_bias,
           ssd_a_log, ssd_d, ssd_norm, w_in_c, lru_conv_w, lru_conv_b, lru_w_a, lru_b_a, lru_w_x, lru_b_x,
           lru_lambda, w_out_c):
    b, t, d = x_prompt.shape
    db = x_sample.shape[0]
    n_pages = page_table.shape[1]
    pos_s = n_pages * PAGE_SIZE
    m = b * t
    kv_shape = (2, N_KV, HEAD_DIM)

    wm, ws = _prep_w_in_a(w_in_a[0])
    wo_nsa = w_out_a[0, :Q_DIM].astype(BF)
    wo_ssd = w_out_a[0, Q_DIM:].astype(BF)
    cmp_k = _prep_cmp_w(cmp_pe_k[0], cmp_w1_k[0], cmp_w2_k[0])
    cmp_v = _prep_cmp_w(cmp_pe_v[0], cmp_w1_v[0], cmp_w2_v[0])
    ssd_par = _ssd_params(ssd_conv_w[0], ssd_conv_b[0], ssd_dt_bias[0], ssd_a_log[0], ssd_d[0], ssd_norm[0])
    lru_par = _lru_params(norm_mix[1], w_in_c[0], lru_conv_w[0], lru_conv_b[0], lru_w_a[0], lru_b_a[0], lru_w_x[0],
                          lru_b_x[0], lru_lambda[0], w_out_c[0])
    ffn_w = [(norm_ffn[l].reshape(1, d), w_ffn_gate[l].astype(BF), w_ffn_up[l].astype(BF), w_ffn_down[l].astype(BF))
             for l in range(2)]
    gfin = norm_final.reshape(1, d)

    tm_proj = 512
    xp = x_prompt.reshape(m, d)
    cos_p, sin_p = _rope_tables(jnp.arange(t, dtype=jnp.int32))
    q, qr, kvc, kvs_t, kvw_t, zx, small, kvc_t = _inproj_a(xp, norm_mix[0], wm, ws, cos_p, sin_p, tm_proj, b, True)
    cmp_p = _compress_prompt(kvc, b, t, *cmp_k, *cmp_v)
    o_nsa = _nsa_prompt(q, qr, cmp_p, kvs_t, kvw_t, small, b, t)
    y_ssd, ssm_p = _ssd_prompt(zx, small, ssd_par, b, t)
    x1 = _outproj(xp, o_nsa, y_ssd, wo_nsa, wo_ssd, tm_proj)
    tm_ffn = 256
    nt = t // tm_ffn
    x2 = _ffn(x1, *ffn_w[0], gfin, tm_ffn, False, lambda i: (i, 0), lambda i: (i % nt, i // nt), (t, b * d))
    x3, lru_p, lconv_tail = _lru_prompt(x2.reshape(t * b, d), *lru_par, b, 32)
    y_prompt = _ffn(x3.reshape(t, b * d), *ffn_w[1], gfin, tm_ffn, True, lambda i: (i % nt, i // nt),
                    lambda i: (i, 0), (m, d))
    w_keep = min(WINDOW, t)
    to_cache = lambda a: a.reshape((b,) + kv_shape + (a.shape[-1],)).transpose(0, 4, 1, 2, 3)[None]
    kv_cmp_p = to_cache(kvc_t)
    kv_slc_p = to_cache(kvs_t)
    kv_win_p = to_cache(kvw_t[:, :, t - w_keep:])
    ssd_conv_p = zx.reshape(b, t, -1)[:, t - (CONV_WIDTH - 1):, SSD_INNER:][None]
    lru_conv_p = lconv_tail.reshape(CONV_WIDTH - 1, b, -1).transpose(1, 0, 2)[None]

    xs = x_sample.reshape(db, d)
    cos_s, sin_s = _rope_tables(jnp.full((db,), pos_s, dtype=jnp.int32))
    q_s, qr_s, kvc_s, kvs_s, kvw_s, zx_s, small_s = _inproj_a(xs, norm_mix[0], wm, ws, cos_s, sin_s, db, 1, False)
    pt_flat = page_table.reshape(-1)
    n_phys = cache_kv_cmp.shape[1]
    feature_major = lambda a: a.transpose(0, 2, 3, 4, 1).reshape(a.shape[0], KV_DIM, a.shape[1])
    cmp_s = _compress_decode(feature_major(cache_kv_cmp[0]), pt_flat, db, n_pages, *cmp_k, *cmp_v)
    win_buf = cache_kv_win[0].reshape(db, -1, KV_DIM)
    head3 = lambda a: a.reshape(db, N_HEADS, HEAD_DIM)
    o_cmp_s, o_win_s, idx = _nsa_decode_a(head3(q_s), head3(qr_s), cmp_s, feature_major(cache_kv_win[0]),
                                          kvw_s.reshape(db, 1, KV_DIM), pos_s)
    idx_flat = idx[:, :N_KV, :N_SELECT].reshape(-1)
    o_nsa_s = _nsa_decode_b(idx_flat, pt_flat, head3(qr_s), feature_major(cache_kv_slc[0]),
                            kvs_s.reshape(db, 1, KV_DIM), o_cmp_s, o_win_s, small_s.reshape(db, 1, LANES), pos_s,
                            n_pages)
    y_ssd_s, ssm_s = _ssd_decode(zx_s.reshape(db, 1, -1), state_ssd_conv[0], small_s.reshape(db, 1, LANES),
                                 state_ssm[0], ssd_par)
    x1_s = _outproj(xs, o_nsa_s.reshape(db, Q_DIM), y_ssd_s.reshape(db, SSD_INNER), wo_nsa, wo_ssd, db)
    x2_s = _ffn(x1_s, *ffn_w[0], gfin, db, False, lambda i: (i, 0), lambda i: (i, 0), (db, d))
    x3_s, lru_s, xbr_s = _lru_decode(x2_s, lru_par, state_lru_conv[0], state_lru[0])
    y_sample = _ffn(x3_s, *ffn_w[1], gfin, db, True, lambda i: (i, 0), lambda i: (i, 0), (db, d))
    kv_cmp_s = kvc_s.reshape((1, db, 1) + kv_shape)
    kv_slc_s = kvs_s.reshape((1, db, 1) + kv_shape)
    kv_win_s = jnp.concatenate([win_buf[:, 1:], kvw_s[:, None, :]], axis=1).reshape(
        (1, db, win_buf.shape[1]) + kv_shape)
    ssd_conv_s = jnp.concatenate([state_ssd_conv[0][:, 1:], zx_s[:, None, SSD_INNER:]], axis=1)[None]
    lru_conv_s = jnp.concatenate([state_lru_conv[0][:, 1:], xbr_s[:, None, :]], axis=1)[None]

    return (y_prompt.reshape(b, t, d), y_sample.reshape(db, 1, d),
            kv_cmp_p, kv_slc_p, kv_win_p, ssm_p[None], ssd_conv_p, lru_p[None], lru_conv_p,
            kv_cmp_s, kv_slc_s, kv_win_s, ssm_s[None], ssd_conv_s, lru_s[None], lru_conv_s)
```

```python
import functools
import math

import jax
import jax.numpy as jnp
from jax import lax
from jax.experimental import pallas as pl
from jax.experimental.pallas import tpu as pltpu

BF = jnp.bfloat16
F32 = jnp.float32

HEAD_DIM = 64
N_KV = 2
Q_PER_KV = 4
N_HEADS = N_KV * Q_PER_KV
CMP_STRIDE = 16
CMP_HIDDEN = 256
SLC_BLOCK = 64
N_SELECT = 16
WINDOW = 512
PAGE_SIZE = 128
ROPE_THETA = 10000.0
FORCE_SCORE = 1e9
NEG_BIG = -1e30
SSD_HEADS = 8
SSD_HEAD_DIM = 64
SSD_STATE = 128
SSD_INNER = SSD_HEADS * SSD_HEAD_DIM
SSD_CHUNK = 128
CONV_WIDTH = 4
LRU_BLOCK = 128
LRU_C = 8.0
RMS_EPS = 1e-6
SCALE = HEAD_DIM ** -0.5
Q_DIM = N_HEADS * HEAD_DIM
KV_DIM = 2 * N_KV * HEAD_DIM
GATE_DIM = 3 * N_HEADS
DT_COL = GATE_DIM
LANES = 128
VMEM_LIMIT = 56 * 1024 * 1024


def _cparams(sem):
    return pltpu.CompilerParams(dimension_semantics=sem, vmem_limit_bytes=VMEM_LIMIT)


def _dot(a, b):
    return jnp.dot(a, b, preferred_element_type=F32)


def _dot_nt(a, b):
    return lax.dot_general(a, b, (((1,), (1,)), ((), ())), preferred_element_type=F32)


def _dot_tn(a, b):
    return lax.dot_general(a, b, (((0,), (0,)), ((), ())), preferred_element_type=F32)


def _split3(x):
    hi = x.astype(BF)
    r = x - hi.astype(F32)
    mid = r.astype(BF)
    lo = (r - mid.astype(F32)).astype(BF)
    return hi, mid, lo


def _dot3(x, m01):
    hi, mid, lo = _split3(x)
    return _dot(hi, m01) + _dot(mid, m01) + _dot(lo, m01)


def _dot3_left(m01, x):
    hi, mid, lo = _split3(x)
    return _dot(m01, hi) + _dot(m01, mid) + _dot(m01, lo)


def _rms(x, g):
    y = x * lax.rsqrt(jnp.mean(x * x, axis=-1, keepdims=True) + RMS_EPS)
    return y * g


def _silu(x):
    return x * jax.nn.sigmoid(x)


def _softplus(x):
    return jnp.maximum(x, 0.0) + jnp.log1p(jnp.exp(-jnp.abs(x)))


def _msoftmax_parts(s, mask):
    s = jnp.where(mask, s, NEG_BIG)
    m = jnp.max(s, axis=-1, keepdims=True)
    e = jnp.where(mask, jnp.exp(s - m), 0.0)
    return m, e


def _msoftmax(s, mask):
    _, e = _msoftmax_parts(s, mask)
    d = jnp.sum(e, axis=-1, keepdims=True)
    return e / jnp.where(d > 0.0, d, 1.0)


def _rope_tables(pos):
    half = HEAD_DIM // 2
    inv_freq = ROPE_THETA ** (-jnp.arange(half, dtype=F32) / half)
    ang = pos.astype(F32)[:, None] * inv_freq[None, :]
    cos = jnp.cos(ang)
    sin = jnp.sin(ang)
    cos2 = jnp.tile(jnp.concatenate([cos, cos], axis=-1), (1, LANES // HEAD_DIM))
    sin2 = jnp.tile(jnp.concatenate([-sin, sin], axis=-1), (1, LANES // HEAD_DIM))
    return cos2, sin2


def _inproj_a_kernel(x_ref, g_ref, wm_ref, ws_ref, cos_ref, sin_ref,
                     q_ref, qr_ref, kvc_ref, kvs_ref, kvw_ref, zx_ref, sm_ref, *kvc_t_ref, feature_major):
    xn = _rms(x_ref[...], g_ref[...]).astype(BF)
    cos = cos_ref[...]
    sin = sin_ref[...]
    lane = lax.broadcasted_iota(jnp.int32, cos.shape, 1)
    first = (lane % HEAD_DIM) < (HEAD_DIM // 2)

    def rope(v):
        rot = jnp.where(first, pltpu.roll(v, LANES - HEAD_DIM // 2, 1), pltpu.roll(v, HEAD_DIM // 2, 1))
        return v * cos + rot * sin

    def store_kv(ref, k, v):
        if feature_major:
            ref[0:LANES, :] = k.T
            ref[LANES:KV_DIM, :] = v.T
        else:
            ref[:, 0:LANES] = k
            ref[:, LANES:KV_DIM] = v

    q = _dot(xn, wm_ref[:, 0:Q_DIM])
    for c in range(Q_DIM // LANES):
        cols = slice(c * LANES, (c + 1) * LANES)
        if feature_major:
            q_ref[cols, :] = q[:, cols].T
            qr_ref[cols, :] = rope(q[:, cols]).T
        else:
            q_ref[:, cols] = q[:, cols]
            qr_ref[:, cols] = rope(q[:, cols])
    o = Q_DIM
    kvc = _dot(xn, wm_ref[:, o:o + KV_DIM])
    kvc_ref[...] = kvc
    if feature_major:
        store_kv(kvc_t_ref[0], kvc[:, 0:LANES], kvc[:, LANES:KV_DIM])
    o += KV_DIM
    kvs = _dot(xn, wm_ref[:, o:o + KV_DIM])
    store_kv(kvs_ref, rope(kvs[:, 0:LANES]), kvs[:, LANES:KV_DIM])
    o += KV_DIM
    kvw = _dot(xn, wm_ref[:, o:o + KV_DIM])
    store_kv(kvw_ref, rope(kvw[:, 0:LANES]), kvw[:, LANES:KV_DIM])
    o += KV_DIM
    zx_ref[...] = _dot(xn, wm_ref[:, o:])
    sm_ref[...] = _dot(xn, ws_ref[...])


def _prep_w_in_a(w_in_a):
    a = Q_DIM + 3 * KV_DIM
    gate = w_in_a[:, a:a + GATE_DIM]
    rest = w_in_a[:, a + GATE_DIM:]
    zx_w = rest[:, :rest.shape[1] - SSD_HEADS]
    dt = rest[:, rest.shape[1] - SSD_HEADS:]
    main = jnp.concatenate([w_in_a[:, :a], zx_w], axis=1).astype(BF)
    small = jnp.concatenate([gate, dt], axis=1)
    small = jnp.pad(small, ((0, 0), (0, LANES - small.shape[1]))).astype(BF)
    return main, small


def _inproj_a(x2d, g, wm, ws, cos2, sin2, tm, nb, feature_major):
    m, d = x2d.shape
    nt = m // nb // tm
    zx_dim = wm.shape[1] - Q_DIM - 3 * KV_DIM
    row = lambda w: pl.BlockSpec((tm, w), lambda i, j: (i * nt + j, 0))
    row_shape = lambda w: jax.ShapeDtypeStruct((m, w), F32)
    const = lambda i, j: (0, 0)
    if feature_major:
        fm = lambda w: pl.BlockSpec((None, w, tm), lambda i, j: (i, 0, j))
        fm_shape = lambda w: jax.ShapeDtypeStruct((nb, w, m // nb), F32)
    else:
        fm, fm_shape = row, row_shape
    kv, kv_shape = fm(KV_DIM), fm_shape(KV_DIM)
    out_specs = [fm(Q_DIM), fm(Q_DIM), row(KV_DIM), kv, kv, row(zx_dim), row(LANES)]
    out_shape = [fm_shape(Q_DIM), fm_shape(Q_DIM), row_shape(KV_DIM), kv_shape, kv_shape, row_shape(zx_dim),
                 row_shape(LANES)]
    if feature_major:
        out_specs.append(kv)
        out_shape.append(kv_shape)
    pos = pl.BlockSpec((tm, LANES), lambda i, j: (j, 0))
    return pl.pallas_call(
        functools.partial(_inproj_a_kernel, feature_major=feature_major),
        grid=(nb, nt),
        in_specs=[row(d), pl.BlockSpec((1, d), const), pl.BlockSpec(wm.shape, const), pl.BlockSpec(ws.shape, const),
                  pos, pos],
        out_specs=out_specs,
        out_shape=out_shape,
        compiler_params=_cparams(("parallel", "parallel")),
        name="inproj_a",
    )(x2d, g.reshape(1, d), wm, ws, cos2, sin2)


def _compress_accumulate(load_rows, pe_ref, w1_ref, col):
    a0 = None
    a1 = None
    for s in range(CMP_STRIDE):
        x = load_rows(s)[:, col:col + HEAD_DIM]
        l = _dot((x + pe_ref[s:s + 1, :]).astype(BF), w1_ref[s])
        t = _dot((x + pe_ref[CMP_STRIDE + s:CMP_STRIDE + s + 1, :]).astype(BF), w1_ref[CMP_STRIDE + s])
        a0 = l if a0 is None else a0 + l
        a1 = t if a1 is None else a1 + t
    return a0, a1


def _compress_prompt_kernel(kc_ref, vc_ref, pek_ref, w1k_ref, w2k_ref, pev_ref, w1v_ref, w2v_ref, out_ref, *,
                            n_half):
    branches = ((kc_ref, pek_ref, w1k_ref, w2k_ref), (vc_ref, pev_ref, w1v_ref, w2v_ref))
    for c, (src_ref, pe_ref, w1_ref, w2_ref) in enumerate(branches):
        def load_rows(s, src_ref=src_ref):
            return src_ref[pl.ds(s, n_half, stride=CMP_STRIDE), :]

        for g in range(N_KV):
            a0, a1 = _compress_accumulate(load_rows, pe_ref, w1_ref, g * HEAD_DIM)
            pre = a0 + pltpu.roll(a1, n_half - 1, 0)
            tok = _dot(_silu(pre).astype(BF), w2_ref[...])
            out_ref[g, :, c * HEAD_DIM:(c + 1) * HEAD_DIM] = tok


def _compress_prompt(kvc2d, b, t, pek, w1k, w2k, pev, w1v, w2v):
    n_half = t // CMP_STRIDE
    full = lambda a: pl.BlockSpec(a.shape, lambda i: (0,) * a.ndim)
    return pl.pallas_call(
        functools.partial(_compress_prompt_kernel, n_half=n_half),
        grid=(b,),
        in_specs=[pl.BlockSpec((t, LANES), lambda i: (i, 0)), pl.BlockSpec((t, LANES), lambda i: (i, 1)),
                  full(pek), full(w1k), full(w2k), full(pev), full(w1v), full(w2v)],
        out_specs=pl.BlockSpec((None, N_KV, n_half, LANES), lambda i: (i, 0, 0, 0)),
        out_shape=jax.ShapeDtypeStruct((b, N_KV, n_half, LANES), F32),
        compiler_params=_cparams(("parallel",)),
        name="compress_prompt",
    )(kvc2d, kvc2d, pek, w1k, w2k, pev, w1v, w2v)


def _prep_cmp_w(pe, w1, w2):
    return pe, w1.reshape(2 * CMP_STRIDE, HEAD_DIM, CMP_HIDDEN).astype(BF), w2.astype(BF)


def _rank_select(imp_t):
    n_rows, tq = imp_t.shape
    jrow = lax.broadcasted_iota(jnp.int32, (n_rows, tq), 0)
    cnt = jnp.zeros((n_rows, tq), F32)
    for jp in range(n_rows):
        row = imp_t[jp:jp + 1, :]
        ahead = jnp.where(row == imp_t, jnp.where(jrow > jp, 1.0, 0.0), jnp.where(row > imp_t, 1.0, 0.0))
        cnt = cnt + ahead
    return jnp.where(cnt < float(N_SELECT), 1.0, 0.0)


def _heads_on_lanes(ref, g):
    rows = [(g * Q_PER_KV + r) * HEAD_DIM for r in range(Q_PER_KV)]
    return (jnp.concatenate([ref[c:c + HEAD_DIM, :] for c in rows], axis=1) * SCALE).astype(BF)


def _per_head(x):
    return jnp.concatenate([x] * Q_PER_KV, axis=1)


def _nsa_prompt_kernel(q_ref, qr_ref, cmp_ref, kvs_ref, kvw_ref, sm_ref, ovt_ref, o_ref, sel_ref, *, t_len, tq, kc):
    t0 = pl.program_id(1) * tq
    hq = Q_PER_KV
    nq = hq * tq
    tcol = t0 + lax.broadcasted_iota(jnp.int32, (1, tq), 1)
    gates_t = jax.nn.sigmoid(sm_ref[...]).T
    tok = lax.broadcasted_iota(jnp.int32, (LANES, tq), 0)
    cmask = _per_head(jnp.where((tok * CMP_STRIDE + (2 * CMP_STRIDE - 1)) <= tcol, 1.0, 0.0)) > 0.5
    n_sel = t_len // SLC_BLOCK
    blk = lax.broadcasted_iota(jnp.int32, (n_sel, tq), 0)
    valid = (blk * SLC_BLOCK) <= tcol
    cur = tcol // SLC_BLOCK
    forced = (blk == 0) | (blk == cur) | (blk == cur - 1)
    n_win = WINDOW + tq
    start = pl.multiple_of(jnp.clip(t0 - WINDOW, 0, t_len - n_win), LANES)
    dist = tcol - (start + lax.broadcasted_iota(jnp.int32, (n_win, tq), 0))
    wbias = _per_head(jnp.where(dist >= 0, jnp.where(dist < WINDOW, 0.0, NEG_BIG), NEG_BIG))
    krel = lax.broadcasted_iota(jnp.int32, (kc, tq), 0)
    n_chunks = (t0 + tq - 1) // kc + 1
    blocks_per_chunk = kc // SLC_BLOCK

    krows = [slice(g * HEAD_DIM, (g + 1) * HEAD_DIM) for g in range(N_KV)]
    vrows = [slice(LANES + g * HEAD_DIM, LANES + (g + 1) * HEAD_DIM) for g in range(N_KV)]

    o_cmp = []
    for g in range(N_KV):
        s = jnp.where(cmask, _dot(cmp_ref[g, :, 0:HEAD_DIM].astype(BF), _heads_on_lanes(q_ref, g)), NEG_BIG)
        e = jnp.where(cmask, jnp.exp(s - jnp.max(s, axis=0, keepdims=True)), 0.0)
        d = jnp.sum(e, axis=0, keepdims=True)
        p = e / jnp.where(d > 0.0, d, 1.0)
        o_cmp.append(_dot_tn(cmp_ref[g, :, HEAD_DIM:LANES].astype(BF), p.astype(BF)))
        psum = p[:, 0:tq]
        for r in range(1, hq):
            psum = psum + p[:, r * tq:(r + 1) * tq]
        imp = _dot3_left(ovt_ref[...], psum)[0:n_sel, :]
        imp = jnp.where(valid, jnp.where(forced, FORCE_SCORE, imp), -FORCE_SCORE)
        sel_ref[g] = _rank_select(imp)

    qs = [_heads_on_lanes(qr_ref, g) for g in range(N_KV)]

    def chunk(ci, carry):
        k0 = pl.multiple_of(ci * kc, kc)
        causal = k0 + krel <= tcol
        j0 = pl.multiple_of(ci * blocks_per_chunk, blocks_per_chunk)
        out = []
        for g in range(N_KV):
            m, l, acc = carry[g]
            kt = kvs_ref[krows[g], pl.ds(k0, kc)].astype(BF)
            vt = kvs_ref[vrows[g], pl.ds(k0, kc)].astype(BF)
            sel_c = sel_ref[g, pl.ds(j0, blocks_per_chunk), :]
            sel_k = jnp.concatenate([jnp.broadcast_to(sel_c[i:i + 1, :], (SLC_BLOCK, tq))
                                     for i in range(blocks_per_chunk)], axis=0)
            bias = jnp.where(causal, jnp.where(sel_k > 0.5, 0.0, NEG_BIG), NEG_BIG)
            s = _dot_tn(kt, qs[g]) + _per_head(bias)
            m_new = jnp.maximum(m, jnp.max(s, axis=0, keepdims=True))
            alpha = jnp.exp(m - m_new)
            e = jnp.exp(s - m_new)
            l = alpha * l + jnp.sum(e, axis=0, keepdims=True)
            out.append((m_new, l, alpha * acc + _dot(vt, e.astype(BF))))
        return tuple(out)

    init = (jnp.full((1, nq), NEG_BIG, F32), jnp.zeros((1, nq), F32), jnp.zeros((HEAD_DIM, nq), F32))
    slc = lax.fori_loop(0, n_chunks, chunk, (init,) * N_KV)

    for g in range(N_KV):
        _, l, acc = slc[g]
        o_slc = acc / l
        kw = kvw_ref[krows[g], pl.ds(start, n_win)].astype(BF)
        vw = kvw_ref[vrows[g], pl.ds(start, n_win)].astype(BF)
        s = _dot_tn(kw, qs[g]) + wbias
        e = jnp.exp(s - jnp.max(s, axis=0, keepdims=True))
        o_win = _dot(vw, e.astype(BF)) / jnp.sum(e, axis=0, keepdims=True)
        for r in range(hq):
            h = g * hq + r
            cols = slice(r * tq, (r + 1) * tq)
            o_ref[h * HEAD_DIM:(h + 1) * HEAD_DIM, :] = (
                gates_t[3 * h:3 * h + 1, :] * o_cmp[g][:, cols] + gates_t[3 * h + 1:3 * h + 2, :] * o_slc[:, cols]
                + gates_t[3 * h + 2:3 * h + 3, :] * o_win[:, cols])


def _overlap_matrix(n_rows, n_cols, row_shift):
    n = jnp.arange(n_rows)[:, None] - row_shift
    c_start = n * CMP_STRIDE
    s_start = jnp.arange(n_cols)[None, :] * SLC_BLOCK
    ov = (c_start < s_start + SLC_BLOCK) & (c_start + 2 * CMP_STRIDE > s_start) & (n >= 0)
    return ov.astype(BF)


def _nsa_prompt(q_t, qr_t, cmp, kvs_t, kvw_t, small, b, t):
    tq = 128
    kc = 512
    nq = t // tq
    n_half = t // CMP_STRIDE
    n_sel = t // SLC_BLOCK
    assert n_half == LANES and n_sel <= LANES and t % kc == 0
    ov_t = _overlap_matrix(LANES, LANES, 0).T
    qtile = pl.BlockSpec((None, Q_DIM, tq), lambda i, j: (i, 0, j))
    seq = pl.BlockSpec((None, KV_DIM, t), lambda i, j: (i, 0, 0))
    return pl.pallas_call(
        functools.partial(_nsa_prompt_kernel, t_len=t, tq=tq, kc=kc),
        grid=(b, nq),
        in_specs=[qtile, qtile,
                  pl.BlockSpec((None, N_KV, n_half, LANES), lambda i, j: (i, 0, 0, 0)),
                  seq, seq, pl.BlockSpec((tq, LANES), lambda i, j: (i * nq + j, 0)),
                  pl.BlockSpec(ov_t.shape, lambda i, j: (0, 0))],
        out_specs=qtile,
        out_shape=jax.ShapeDtypeStruct((b, Q_DIM, t), F32),
        scratch_shapes=[pltpu.VMEM((N_KV, n_sel, tq), F32)],
        compiler_params=_cparams(("parallel", "parallel")),
        name="nsa_prompt",
    )(q_t, qr_t, cmp, kvs_t, kvw_t, small, ov_t)


def _ssd_prompt_kernel(zx_ref, sm_ref, cw_ref, cb_ref, dtb_ref, alog_ref, dskip_ref, nrm_ref, tri_ref,
                       y_ref, hout_ref, tail_ref, h_ref):
    c = pl.program_id(1)
    lc = SSD_CHUNK

    @pl.when(c == 0)
    def _():
        tail_ref[...] = jnp.zeros_like(tail_ref)
        h_ref[...] = jnp.zeros_like(h_ref)

    z = zx_ref[:, 0:SSD_INNER]
    xbc = zx_ref[:, SSD_INNER:]
    u = jnp.concatenate([tail_ref[...], xbc], axis=0)
    tail_ref[...] = xbc[lc - 8:lc, :]
    conv = cb_ref[...]
    for k in range(CONV_WIDTH):
        conv = conv + u[5 + k:5 + k + lc, :] * cw_ref[k:k + 1, :]
    xbc_c = _silu(conv)
    xs = xbc_c[:, 0:SSD_INNER]

    dt_full = _softplus(sm_ref[...] + dtb_ref[...])
    a_full = dt_full * (-jnp.exp(alog_ref[...]))
    acs = _dot3_left(tri_ref[...], a_full)
    acs_t = acs.T
    li = lax.broadcasted_iota(jnp.int32, (lc, lc), 0)
    si = lax.broadcasted_iota(jnp.int32, (lc, lc), 1)
    lower = li >= si

    ys = []
    for g in range(2):
        bm = xbc_c[:, SSD_INNER + g * SSD_STATE:SSD_INNER + (g + 1) * SSD_STATE]
        cm = xbc_c[:, SSD_INNER + (2 + g) * SSD_STATE:SSD_INNER + (3 + g) * SSD_STATE]
        bm_b = bm.astype(BF)
        cm_b = cm.astype(BF)
        cb = _dot_nt(cm_b, bm_b)
        for hh in range(SSD_HEADS // 2):
            h = g * (SSD_HEADS // 2) + hh
            col = DT_COL + h
            dt_h = dt_full[:, col:col + 1]
            acs_h = acs[:, col:col + 1]
            acs_row = acs_t[col:col + 1, :]
            acs_last = acs[lc - 1:lc, col:col + 1]
            lmat = jnp.where(lower, jnp.exp(acs_h - acs_row), 0.0)
            x_h = xs[:, h * SSD_HEAD_DIM:(h + 1) * SSD_HEAD_DIM]
            xdt = x_h * dt_h
            y_diag = _dot((cb * lmat).astype(BF), xdt.astype(BF))
            h_prev = h_ref[h]
            y_off = _dot_nt(cm_b, h_prev.astype(BF)) * jnp.exp(acs_h)
            decay = jnp.exp(acs_last - acs_h)
            h_new = jnp.exp(acs_last) * h_prev + _dot_tn((xdt * decay).astype(BF), bm_b)
            h_ref[h] = h_new
            ys.append(y_diag + y_off)
    y = jnp.concatenate(ys, axis=1) + dskip_ref[...] * xs
    y = y * _silu(z)
    y_ref[...] = _rms(y, nrm_ref[...])
    hout_ref[...] = h_ref[...]


def _ssd_params(conv_w, conv_b, dt_bias, a_log, d_skip, ssd_norm):
    pad = lambda v: jnp.pad(v, (DT_COL, LANES - DT_COL - SSD_HEADS)).reshape(1, LANES)
    return (conv_w, conv_b.reshape(1, -1), pad(dt_bias), pad(a_log),
            jnp.repeat(d_skip, SSD_HEAD_DIM).reshape(1, SSD_INNER), ssd_norm.reshape(1, SSD_INNER))


def _ssd_prompt(zx, small, params, b, t):
    lc = SSD_CHUNK
    nc = t // lc
    cw, cb, dtb, alog, dskip, nrm = params
    tri = (jnp.arange(lc)[:, None] >= jnp.arange(lc)[None, :]).astype(BF)
    zx_dim = zx.shape[1]
    conv_dim = zx_dim - SSD_INNER
    full = lambda a: pl.BlockSpec(a.shape, lambda i, j: (0,) * a.ndim)
    tile = lambda w: pl.BlockSpec((lc, w), lambda i, j: (i * nc + j, 0))
    return pl.pallas_call(
        _ssd_prompt_kernel,
        grid=(b, nc),
        in_specs=[tile(zx_dim), tile(LANES), full(cw), full(cb), full(dtb), full(alog), full(dskip), full(nrm),
                  full(tri)],
        out_specs=[tile(SSD_INNER),
                   pl.BlockSpec((None, SSD_HEADS, SSD_HEAD_DIM, SSD_STATE), lambda i, j: (i, 0, 0, 0))],
        out_shape=[jax.ShapeDtypeStruct((b * t, SSD_INNER), F32),
                   jax.ShapeDtypeStruct((b, SSD_HEADS, SSD_HEAD_DIM, SSD_STATE), F32)],
        scratch_shapes=[pltpu.VMEM((8, conv_dim), F32), pltpu.VMEM((SSD_HEADS, SSD_HEAD_DIM, SSD_STATE), F32)],
        compiler_params=_cparams(("parallel", "arbitrary")),
        name="ssd_prompt",
    )(zx, small, cw, cb, dtb, alog, dskip, nrm, tri)


def _ffn_kernel(x_ref, *refs, final_norm, mixer_proj, a_feature_major):
    x = x_ref[...]
    if mixer_proj:
        a_ref, b_ref, wa_ref, wb_ref = refs[:4]
        refs = refs[4:]
        a = a_ref[...].astype(BF)
        x = x + (_dot_tn(a, wa_ref[...]) if a_feature_major else _dot(a, wa_ref[...]))
        x = x + _dot(b_ref[...].astype(BF), wb_ref[...])
    g_ref, wg_ref, wu_ref, wd_ref, gf_ref, o_ref = refs
    xn = _rms(x, g_ref[...]).astype(BF)
    h = _silu(_dot(xn, wg_ref[...])) * _dot(xn, wu_ref[...])
    y = x + _dot(h.astype(BF), wd_ref[...])
    if final_norm:
        y = _rms(y, gf_ref[...])
    o_ref[...] = y


def _ffn(x2d, g, wg, wu, wd, gf, tm, final_norm, in_map, out_map, out_2d_shape, mixer=None):
    m, d = x2d.shape[0] * x2d.shape[1] // wg.shape[0], wg.shape[0]
    n_steps = m // tm
    full = lambda arr: pl.BlockSpec(arr.shape, lambda i: (0, 0))
    args = [x2d]
    in_specs = [pl.BlockSpec((tm, d), in_map)]
    a_feature_major = mixer is not None and mixer[0].ndim == 3
    if mixer is not None:
        a, bb, wa, wb = mixer
        args += [a, bb, wa, wb]
        if a_feature_major:
            nt = a.shape[2] // tm
            a_spec = pl.BlockSpec((None, a.shape[1], tm), lambda i: (i // nt, 0, i % nt))
        else:
            a_spec = pl.BlockSpec((tm, a.shape[1]), in_map)
        in_specs += [a_spec, pl.BlockSpec((tm, bb.shape[1]), in_map), full(wa), full(wb)]
    args += [g, wg, wu, wd, gf]
    in_specs += [full(g), full(wg), full(wu), full(wd), full(gf)]
    return pl.pallas_call(
        functools.partial(_ffn_kernel, final_norm=final_norm, mixer_proj=mixer is not None,
                          a_feature_major=a_feature_major),
        grid=(n_steps,),
        in_specs=in_specs,
        out_specs=pl.BlockSpec((tm, d), out_map),
        out_shape=jax.ShapeDtypeStruct(out_2d_shape, F32),
        compiler_params=_cparams(("parallel",)),
        name="ffn_final" if final_norm else "ffn",
    )(*args)


def _lru_gates(xc, wa_ref, ba_ref, wx_ref, bx_ref, sp):
    n_heads = xc.shape[1] // LRU_BLOCK
    a_parts = []
    gx_parts = []
    for h in range(n_heads):
        sl = slice(h * LRU_BLOCK, (h + 1) * LRU_BLOCK)
        xh = xc[:, sl]
        xb = xh.astype(BF)
        r = jax.nn.sigmoid(_dot(xb, wa_ref[h]) + ba_ref[:, sl])
        i = jax.nn.sigmoid(_dot(xb, wx_ref[h]) + bx_ref[:, sl])
        log_a = -LRU_C * r * sp[:, sl]
        a = jnp.exp(log_a)
        a_parts.append(a)
        gx_parts.append(jnp.sqrt(1.0 - a * a) * (i * xh))
    return jnp.concatenate(a_parts, axis=1), jnp.concatenate(gx_parts, axis=1)


def _lru_prompt_kernel(x_ref, g_ref, win_ref, cw_ref, cb_ref, wa_ref, ba_ref, wx_ref, bx_ref, lam_ref, wo_ref,
                       o_ref, hout_ref, cout_ref, prev_ref, h_ref, a_s, gx_s, hs_s, *, nb, tt, w):
    @pl.when(pl.program_id(0) == 0)
    def _():
        prev_ref[...] = jnp.zeros_like(prev_ref)
        h_ref[...] = jnp.zeros_like(h_ref)

    rows = tt * nb
    x = x_ref[...]
    xn = _rms(x, g_ref[...]).astype(BF)
    proj = _dot(xn, win_ref[...])
    gate_br = proj[:, 0:w]
    x_br = proj[:, w:2 * w]
    n_prev = (CONV_WIDTH - 1) * nb
    u = jnp.concatenate([prev_ref[...], x_br], axis=0)
    prev_ref[...] = x_br[rows - n_prev:rows, :]
    cout_ref[...] = x_br[rows - n_prev:rows, :]
    xc = cb_ref[...]
    for k in range(CONV_WIDTH):
        xc = xc + u[k * nb:k * nb + rows, :] * cw_ref[k:k + 1, :]
    sp = _softplus(-lam_ref[...])
    a, gx = _lru_gates(xc, wa_ref, ba_ref, wx_ref, bx_ref, sp)
    a_s[...] = a
    gx_s[...] = gx
    h = h_ref[...]
    for t in range(tt):
        h = a_s[t * nb:(t + 1) * nb, :] * h + gx_s[t * nb:(t + 1) * nb, :]
        hs_s[t * nb:(t + 1) * nb, :] = h
    h_ref[...] = h
    hout_ref[...] = h
    y = (jax.nn.gelu(gate_br) * hs_s[...]).astype(BF)
    o_ref[...] = x + _dot(y, wo_ref[...])


def _lru_prompt(x_tm, g, w_in, cw, cb, wa, ba, wx, bx, lam, wo, nb, tt):
    m, d = x_tm.shape
    w = cw.shape[1]
    rows = tt * nb
    n_prev = (CONV_WIDTH - 1) * nb
    full = lambda a: pl.BlockSpec(a.shape, lambda i: (0,) * a.ndim)
    args = (g, w_in, cw, cb, wa, ba, wx, bx, lam, wo)
    return pl.pallas_call(
        functools.partial(_lru_prompt_kernel, nb=nb, tt=tt, w=w),
        grid=(m // rows,),
        in_specs=[pl.BlockSpec((rows, d), lambda i: (i, 0))] + [full(a) for a in args],
        out_specs=[pl.BlockSpec((rows, d), lambda i: (i, 0)),
                   pl.BlockSpec((nb, w), lambda i: (0, 0)),
                   pl.BlockSpec((n_prev, w), lambda i: (0, 0))],
        out_shape=[jax.ShapeDtypeStruct((m, d), F32),
                   jax.ShapeDtypeStruct((nb, w), F32),
                   jax.ShapeDtypeStruct((n_prev, w), F32)],
        scratch_shapes=[pltpu.VMEM((n_prev, w), F32), pltpu.VMEM((nb, w), F32),
                        pltpu.VMEM((rows, w), F32), pltpu.VMEM((rows, w), F32), pltpu.VMEM((rows, w), F32)],
        compiler_params=_cparams(("arbitrary",)),
        name="lru_prompt",
    )(x_tm, *args)


def _lru_params(norm_g, w_in_c, conv_w, conv_b, w_a, b_a, w_x, b_x, lam, w_out_c):
    r = lambda v: v.reshape(1, -1)
    return (r(norm_g), w_in_c.astype(BF), conv_w, r(conv_b), w_a.astype(BF), r(b_a), w_x.astype(BF), r(b_x),
            r(lam), w_out_c.astype(BF))


def _lru_decode_kernel(x_ref, g_ref, win_ref, cw_ref, cb_ref, wa_ref, ba_ref, wx_ref, bx_ref, lam_ref, wo_ref,
                       c0_ref, c1_ref, c2_ref, h0_ref, o_ref, hout_ref, xbr_ref, *, w):
    x = x_ref[...]
    xn = _rms(x, g_ref[...]).astype(BF)
    proj = _dot(xn, win_ref[...])
    gate_br = proj[:, 0:w]
    x_br = proj[:, w:2 * w]
    xbr_ref[...] = x_br
    xc = (cb_ref[...] + c0_ref[...] * cw_ref[0:1, :] + c1_ref[...] * cw_ref[1:2, :]
          + c2_ref[...] * cw_ref[2:3, :] + x_br * cw_ref[3:4, :])
    sp = _softplus(-lam_ref[...])
    a, gx = _lru_gates(xc, wa_ref, ba_ref, wx_ref, bx_ref, sp)
    h = a * h0_ref[...] + gx
    hout_ref[...] = h
    y = (jax.nn.gelu(gate_br) * h).astype(BF)
    o_ref[...] = x + _dot(y, wo_ref[...])


def _lru_decode(x, params, conv_state, h0):
    m, d = x.shape
    w = h0.shape[1]
    args = (x,) + tuple(params) + (conv_state[:, 0], conv_state[:, 1], conv_state[:, 2], h0)
    full = lambda a: pl.BlockSpec(a.shape, lambda i: (0,) * a.ndim)
    return pl.pallas_call(
        functools.partial(_lru_decode_kernel, w=w),
        grid=(1,),
        in_specs=[full(a) for a in args],
        out_specs=[pl.BlockSpec((m, d), lambda i: (0, 0)), pl.BlockSpec((m, w), lambda i: (0, 0)),
                   pl.BlockSpec((m, w), lambda i: (0, 0))],
        out_shape=[jax.ShapeDtypeStruct((m, d), F32), jax.ShapeDtypeStruct((m, w), F32),
                   jax.ShapeDtypeStruct((m, w), F32)],
        compiler_params=_cparams(("arbitrary",)),
        name="lru_decode",
    )(*args)


PAGES_PER_STEP = 32
HALVES_PER_PAGE = PAGE_SIZE // CMP_STRIDE
CMP_PACK = 2 * LANES // HEAD_DIM


def _compress_decode_kernel(pt_ref, *refs):
    del pt_ref
    np_ = PAGES_PER_STEP
    k_pages = refs[0:np_]
    v_pages = refs[np_:2 * np_]
    pek_ref, w1k_ref, w2k_ref, pev_ref, w1v_ref, w2v_ref, out_ref, carry_ref, rows_ref = refs[2 * np_:]
    rows = np_ * HALVES_PER_PAGE

    @pl.when(pl.program_id(1) == 0)
    def _():
        carry_ref[...] = jnp.zeros_like(carry_ref)

    rowi = lax.broadcasted_iota(jnp.int32, (rows, CMP_HIDDEN), 0)
    low = lax.broadcasted_iota(jnp.int32, (rows, LANES), 1) < HEAD_DIM

    def split_groups(a, b):
        return (jnp.where(low, a, pltpu.roll(b, HEAD_DIM, 1)), jnp.where(low, pltpu.roll(a, HEAD_DIM, 1), b))

    branches = ((k_pages, pek_ref, w1k_ref, w2k_ref), (v_pages, pev_ref, w1v_ref, w2v_ref))
    for c, (pages, pe_ref, w1_ref, w2_ref) in enumerate(branches):
        for k, p in enumerate(pages):
            rows_ref[c, k * PAGE_SIZE:(k + 1) * PAGE_SIZE, :] = p[...].T

        lead = None
        trail = None
        for u in range(CMP_STRIDE // CMP_PACK):
            xs = [rows_ref[c, pl.ds(u * CMP_PACK + j, rows, stride=CMP_STRIDE), :] for j in range(CMP_PACK)]
            g01 = split_groups(xs[0], xs[1])
            g23 = split_groups(xs[2], xs[3])
            x = jnp.concatenate([jnp.concatenate([g01[0], g23[0]], axis=1),
                                 jnp.concatenate([g01[1], g23[1]], axis=1)], axis=0)
            dl = _dot((x + pe_ref[0, u]).astype(BF), w1_ref[0, u])
            dt = _dot((x + pe_ref[1, u]).astype(BF), w1_ref[1, u])
            lead = dl if lead is None else lead + dl
            trail = dt if trail is None else trail + dt
        for g in range(N_KV):
            a0 = lead[g * rows:(g + 1) * rows]
            a1 = trail[g * rows:(g + 1) * rows]
            slot = c * N_KV + g
            prev = jnp.where(rowi == 0, carry_ref[slot, 7:8, :], pltpu.roll(a0, 1, 0))
            carry_ref[slot] = a0[rows - 8:rows, :]
            tok = _dot(_silu(prev + a1).astype(BF), w2_ref[...])
            out_ref[g, :, c * HEAD_DIM:(c + 1) * HEAD_DIM] = tok


def _compress_decode(cache_t, pt_flat, db, n_pages, pek, w1k, w2k, pev, w1v, w2v):
    np_ = PAGES_PER_STEP
    n_steps = n_pages // np_
    rows = np_ * HALVES_PER_PAGE
    n_half = n_pages * HALVES_PER_PAGE
    n_quads = CMP_STRIDE // CMP_PACK
    pack_pe = lambda pe: pe.reshape(2, n_quads, 1, CMP_PACK * HEAD_DIM)
    pack_w1 = lambda w1: w1.reshape(2, n_quads, CMP_PACK * HEAD_DIM, CMP_HIDDEN)
    pek, w1k, pev, w1v = pack_pe(pek), pack_w1(w1k), pack_pe(pev), pack_w1(w1v)

    def page_spec(k, rowblk):
        return pl.BlockSpec((None, LANES, PAGE_SIZE),
                            lambda i, j, pt: (pt[i * n_pages + j * np_ + k], rowblk, 0))

    full = lambda a: pl.BlockSpec(a.shape, lambda i, j, pt: (0,) * a.ndim)
    grid_spec = pltpu.PrefetchScalarGridSpec(
        num_scalar_prefetch=1,
        grid=(db, n_steps),
        in_specs=[page_spec(k, 0) for k in range(np_)] + [page_spec(k, 1) for k in range(np_)]
        + [full(pek), full(w1k), full(w2k), full(pev), full(w1v), full(w2v)],
        out_specs=pl.BlockSpec((None, N_KV, rows, LANES), lambda i, j, pt: (i, 0, j, 0)),
        scratch_shapes=[pltpu.VMEM((2 * N_KV, 8, CMP_HIDDEN), F32), pltpu.VMEM((2, np_ * PAGE_SIZE, LANES), F32)],
    )
    return pl.pallas_call(
        _compress_decode_kernel,
        grid_spec=grid_spec,
        out_shape=jax.ShapeDtypeStruct((db, N_KV, n_half, LANES), F32),
        compiler_params=_cparams(("parallel", "arbitrary")),
        name="compress_decode",
    )(pt_flat, *([cache_t] * (2 * np_)), pek, w1k, w2k, pev, w1v, w2v)


def _group_rows(top, a, b):
    return jnp.where(top, a, b)


def _nsa_decode_a_kernel(q_ref, qr_ref, cmp_ref, win_ref, new_ref, ov_ref, ocmp_ref, owin_ref, idx_ref, *,
                         pos, n_blocks):
    nh = N_HEADS
    top = lax.broadcasted_iota(jnp.int32, (nh, 1), 0) < Q_PER_KV
    q = q_ref[...].astype(BF)
    n_tok = cmp_ref.shape[1]
    lane = lax.broadcasted_iota(jnp.int32, (nh, n_tok), 1)
    cmask = (lane >= 1) & (((lane - 1) * CMP_STRIDE + (2 * CMP_STRIDE - 1)) <= pos)
    s = _group_rows(top, _dot_nt(q, cmp_ref[0, :, 0:HEAD_DIM].astype(BF)),
                    _dot_nt(q, cmp_ref[1, :, 0:HEAD_DIM].astype(BF))) * SCALE
    p = _msoftmax(s, cmask)
    pb = p.astype(BF)
    ocmp_ref[...] = _group_rows(top, _dot(pb, cmp_ref[0, :, HEAD_DIM:LANES].astype(BF)),
                                _dot(pb, cmp_ref[1, :, HEAD_DIM:LANES].astype(BF)))

    imp8 = _dot3(p, ov_ref[...])
    nj = ov_ref.shape[1]
    cur = pos // SLC_BLOCK
    j = lax.broadcasted_iota(jnp.int32, (1, nj), 1)
    valid = (j * SLC_BLOCK) <= pos
    forced = (j == 0) | (j == cur) | (j == cur - 1)
    ri = lax.broadcasted_iota(jnp.int32, (nj, nj), 0)
    ci = lax.broadcasted_iota(jnp.int32, (nj, nj), 1)
    k_lane = lax.broadcasted_iota(jnp.int32, (nj, LANES), 1).astype(F32)
    jvals = lax.broadcasted_iota(jnp.int32, (8, nj), 1).astype(F32).astype(BF)
    idx_rows = []
    for g in range(N_KV):
        imp = jnp.sum(imp8[g * Q_PER_KV:(g + 1) * Q_PER_KV, :], axis=0, keepdims=True)
        imp = jnp.where(valid & forced, FORCE_SCORE, imp)
        imp = jnp.where(valid, imp, -FORCE_SCORE)
        imp = jnp.where(j < n_blocks, imp, -3e38)
        impb = jnp.broadcast_to(imp, (nj, nj))
        col = jnp.sum(jnp.where(ri == ci, impb, 0.0), axis=1, keepdims=True)
        ahead = jnp.where(impb == col, jnp.where(ci < ri, 1.0, 0.0), jnp.where(impb > col, 1.0, 0.0))
        rank_col = jnp.sum(ahead, axis=1, keepdims=True)
        onehot = jnp.where(rank_col == k_lane, 1.0, 0.0).astype(BF)
        idx_rows.append(_dot(jvals, onehot)[0:1, :])
    idx = jnp.concatenate(idx_rows + [jnp.zeros((8 - N_KV, LANES), F32)], axis=0)
    idx_ref[...] = idx.astype(jnp.int32)

    qr = qr_ref[...]
    qrb = qr.astype(BF)
    n_win = win_ref.shape[1]
    s = _group_rows(top, _dot(qrb, win_ref[0:HEAD_DIM, :].astype(BF)),
                    _dot(qrb, win_ref[HEAD_DIM:LANES, :].astype(BF))) * SCALE
    wl = lax.broadcasted_iota(jnp.int32, (nh, n_win), 1)
    dist = n_win - wl
    wmask = (dist >= 0) & (dist < WINDOW)
    new = new_ref[...]
    knew = _group_rows(top, new[:, 0:HEAD_DIM], new[:, HEAD_DIM:LANES])
    vnew = _group_rows(top, new[:, LANES:LANES + HEAD_DIM], new[:, LANES + HEAD_DIM:KV_DIM])
    s_new = jnp.sum(qr * knew, axis=1, keepdims=True) * SCALE
    sm = jnp.where(wmask, s, NEG_BIG)
    m = jnp.maximum(jnp.max(sm, axis=1, keepdims=True), s_new)
    e = jnp.where(wmask, jnp.exp(sm - m), 0.0)
    e_new = jnp.exp(s_new - m)
    d = jnp.sum(e, axis=1, keepdims=True) + e_new
    eb = e.astype(BF)
    o = _group_rows(top, _dot_nt(eb, win_ref[LANES:LANES + HEAD_DIM, :].astype(BF)),
                    _dot_nt(eb, win_ref[LANES + HEAD_DIM:KV_DIM, :].astype(BF))) + e_new * vnew
    owin_ref[...] = o / d


def _nsa_decode_a(q3, qr3, cmp_s, win, kvw_new3, pos):
    db = q3.shape[0]
    n_tok = cmp_s.shape[2]
    n_blocks = -(-(pos + 1) // SLC_BLOCK)
    nj = -(-n_blocks // LANES) * LANES
    ov = _overlap_matrix(n_tok, nj, 1)
    per_b = lambda a: pl.BlockSpec((None,) + a.shape[1:], lambda i: (i,) + (0,) * (a.ndim - 1))
    head = pl.BlockSpec((None, N_HEADS, HEAD_DIM), lambda i: (i, 0, 0))
    return pl.pallas_call(
        functools.partial(_nsa_decode_a_kernel, pos=pos, n_blocks=n_blocks),
        grid=(db,),
        in_specs=[head, head, per_b(cmp_s), per_b(win), per_b(kvw_new3), pl.BlockSpec(ov.shape, lambda i: (0, 0))],
        out_specs=[head, head, pl.BlockSpec((None, 8, LANES), lambda i: (i, 0, 0))],
        out_shape=[jax.ShapeDtypeStruct((db, N_HEADS, HEAD_DIM), F32),
                   jax.ShapeDtypeStruct((db, N_HEADS, HEAD_DIM), F32),
                   jax.ShapeDtypeStruct((db, 8, LANES), jnp.int32)],
        compiler_params=_cparams(("parallel",)),
        name="nsa_decode_select",
    )(q3, qr3, cmp_s, win, kvw_new3, ov)


def _nsa_decode_b_kernel(idx_ref, pt_ref, qr_ref, *refs, n_past_blocks, cur):
    del pt_ref
    nb = N_KV * N_SELECT
    blk_refs = refs[0:nb]
    new_ref, ocmp_ref, owin_ref, sm_ref, o_ref = refs[nb:]
    b = pl.program_id(0)
    nh = N_HEADS
    top = lax.broadcasted_iota(jnp.int32, (nh, 1), 0) < Q_PER_KV
    qr = qr_ref[...]
    qrb = qr.astype(BF)
    per_page = PAGE_SIZE // SLC_BLOCK
    n_keys = N_SELECT * PAGE_SIZE
    lane = lax.broadcasted_iota(jnp.int32, (nh, n_keys), 1)
    lane_page = lane // PAGE_SIZE
    lane_sub = (lane // SLC_BLOCK) % per_page
    s_g = []
    v_g = []
    ok_g = []
    has_new = []
    for g in range(N_KV):
        blks = blk_refs[g * N_SELECT:(g + 1) * N_SELECT]
        kt = jnp.concatenate([r[g * HEAD_DIM:(g + 1) * HEAD_DIM, :] for r in blks], axis=1).astype(BF)
        v_g.append(jnp.concatenate([r[LANES + g * HEAD_DIM:LANES + (g + 1) * HEAD_DIM, :] for r in blks],
                                   axis=1).astype(BF))
        s_g.append(_dot(qrb, kt))
        ok = jnp.zeros((nh, n_keys), F32)
        new_sel = jnp.zeros((), F32)
        for k in range(N_SELECT):
            jk = idx_ref[(b * N_KV + g) * N_SELECT + k]
            hit = (lane_page == k) & (lane_sub == jk % per_page)
            ok = jnp.where(hit, jnp.where(jk < n_past_blocks, 1.0, 0.0), ok)
            new_sel = jnp.maximum(new_sel, jnp.where(jk == cur, 1.0, 0.0))
        ok_g.append(ok)
        has_new.append(new_sel)
    s = _group_rows(top, s_g[0], s_g[1]) * SCALE
    mask = _group_rows(top, ok_g[0], ok_g[1]) > 0.5
    new_on = _group_rows(top, has_new[0], has_new[1]) > 0.5
    new = new_ref[...]
    knew = _group_rows(top, new[:, 0:HEAD_DIM], new[:, HEAD_DIM:LANES])
    vnew = _group_rows(top, new[:, LANES:LANES + HEAD_DIM], new[:, LANES + HEAD_DIM:KV_DIM])
    s_new = jnp.where(new_on, jnp.sum(qr * knew, axis=1, keepdims=True) * SCALE, NEG_BIG)
    sm = jnp.where(mask, s, NEG_BIG)
    m = jnp.maximum(jnp.max(sm, axis=1, keepdims=True), s_new)
    e = jnp.where(mask, jnp.exp(sm - m), 0.0)
    e_new = jnp.where(new_on, jnp.exp(s_new - m), 0.0)
    d = jnp.sum(e, axis=1, keepdims=True) + e_new
    eb = e.astype(BF)
    o_slc = ((_group_rows(top, _dot_nt(eb, v_g[0]), _dot_nt(eb, v_g[1])) + e_new * vnew)
             / jnp.where(d > 0.0, d, 1.0))

    sig = jnp.broadcast_to(jax.nn.sigmoid(sm_ref[...]), (nh, LANES))
    hl = lax.broadcasted_iota(jnp.int32, (nh, LANES), 1)
    hr = lax.broadcasted_iota(jnp.int32, (nh, LANES), 0)

    def gate(br):
        return jnp.sum(jnp.where(hl == 3 * hr + br, sig, 0.0), axis=1, keepdims=True)

    o_ref[...] = gate(0) * ocmp_ref[...] + gate(1) * o_slc + gate(2) * owin_ref[...]


def _nsa_decode_b(idx_flat, pt_flat, qr3, slc_pages_t, kvs_new3, o_cmp, o_win, small3, pos, n_pages):
    db = qr3.shape[0]
    n_past_blocks = pos // SLC_BLOCK
    cur = pos // SLC_BLOCK
    per_page = PAGE_SIZE // SLC_BLOCK

    def blk_spec(g, k):
        def imap(i, idx, pt):
            jk = jnp.minimum(idx[(i * N_KV + g) * N_SELECT + k], n_past_blocks - 1)
            return (pt[i * n_pages + jk // per_page], 0, 0)
        return pl.BlockSpec((None, KV_DIM, PAGE_SIZE), imap)

    head = pl.BlockSpec((None, N_HEADS, HEAD_DIM), lambda i, idx, pt: (i, 0, 0))
    row3 = lambda a: pl.BlockSpec((None,) + a.shape[1:], lambda i, idx, pt: (i, 0, 0))
    grid_spec = pltpu.PrefetchScalarGridSpec(
        num_scalar_prefetch=2,
        grid=(db,),
        in_specs=[head] + [blk_spec(g, k) for g in range(N_KV) for k in range(N_SELECT)]
        + [row3(kvs_new3), head, head, row3(small3)],
        out_specs=head,
    )
    return pl.pallas_call(
        functools.partial(_nsa_decode_b_kernel, n_past_blocks=n_past_blocks, cur=cur),
        grid_spec=grid_spec,
        out_shape=jax.ShapeDtypeStruct((db, N_HEADS, HEAD_DIM), F32),
        compiler_params=_cparams(("arbitrary",)),
        name="nsa_decode_attend",
    )(idx_flat, pt_flat, qr3, *([slc_pages_t] * (N_KV * N_SELECT)), kvs_new3, o_cmp, o_win, small3)


def _ssd_decode_kernel(zx_ref, cst_ref, sm_ref, h0_ref, cw_ref, cb_ref, dtb_ref, alog_ref, dskip_ref, nrm_ref,
                       y_ref, hout_ref):
    z = zx_ref[:, 0:SSD_INNER]
    xbc = zx_ref[:, SSD_INNER:]
    conv = cb_ref[...] + xbc * cw_ref[CONV_WIDTH - 1:CONV_WIDTH, :]
    for k in range(CONV_WIDTH - 1):
        conv = conv + cst_ref[k:k + 1, :] * cw_ref[k:k + 1, :]
    xbc_c = _silu(conv)
    xs = xbc_c[:, 0:SSD_INNER]
    dt_full = _softplus(sm_ref[...] + dtb_ref[...])
    da_full = jnp.exp(dt_full * (-jnp.exp(alog_ref[...])))
    p = SSD_HEAD_DIM
    eye = lax.broadcasted_iota(jnp.int32, (p, p), 0) == lax.broadcasted_iota(jnp.int32, (p, p), 1)
    ys = []
    for h in range(SSD_HEADS):
        g = h // (SSD_HEADS // 2)
        col = DT_COL + h
        xdt = xs[:, h * p:(h + 1) * p] * dt_full[:, col:col + 1]
        xcol = jnp.sum(jnp.where(eye, jnp.broadcast_to(xdt, (p, p)), 0.0), axis=1, keepdims=True)
        bm = xbc_c[:, SSD_INNER + g * SSD_STATE:SSD_INNER + (g + 1) * SSD_STATE]
        cm = xbc_c[:, SSD_INNER + (2 + g) * SSD_STATE:SSD_INNER + (3 + g) * SSD_STATE]
        h_new = da_full[:, col:col + 1] * h0_ref[h] + xcol * bm
        hout_ref[h] = h_new
        ys.append(_dot_nt(jnp.broadcast_to(cm, (8, SSD_STATE)).astype(BF), h_new.astype(BF))[0:1, :])
    y = jnp.concatenate(ys, axis=1) + dskip_ref[...] * xs
    y = y * _silu(z)
    y_ref[...] = _rms(y, nrm_ref[...])


def _ssd_decode(zx3, conv_state, small3, h0, params):
    db = zx3.shape[0]
    cw, cb, dtb, alog, dskip, nrm = params
    per_b = lambda a: pl.BlockSpec((None,) + a.shape[1:], lambda i: (i,) + (0,) * (a.ndim - 1))
    full = lambda a: pl.BlockSpec(a.shape, lambda i: (0,) * a.ndim)
    return pl.pallas_call(
        _ssd_decode_kernel,
        grid=(db,),
        in_specs=[per_b(zx3), per_b(conv_state), per_b(small3), per_b(h0), full(cw), full(cb), full(dtb), full(alog),
                  full(dskip), full(nrm)],
        out_specs=[pl.BlockSpec((None, 1, SSD_INNER), lambda i: (i, 0, 0)), per_b(h0)],
        out_shape=[jax.ShapeDtypeStruct((db, 1, SSD_INNER), F32), jax.ShapeDtypeStruct(h0.shape, F32)],
        compiler_params=_cparams(("parallel",)),
        name="ssd_decode",
    )(zx3, conv_state, small3, h0, cw, cb, dtb, alog, dskip, nrm)


def kernel(x_prompt, x_sample, cache_kv_cmp, cache_kv_slc, cache_kv_win, state_ssm, state_ssd_conv, state_lru,
           state_lru_conv, page_table, norm_mix, norm_ffn, norm_final, w_ffn_gate, w_ffn_up, w_ffn_down, w_in_a,
           w_out_a, cmp_pe_k, cmp_w1_k, cmp_w2_k, cmp_pe_v, cmp_w1_v, cmp_w2_v, ssd_conv_w, ssd_conv_b, ssd_dt_bias,
           ssd_a_log, ssd_d, ssd_norm, w_in_c, lru_conv_w, lru_conv_b, lru_w_a, lru_b_a, lru_w_x, lru_b_x,
           lru_lambda, w_out_c):
    b, t, d = x_prompt.shape
    db = x_sample.shape[0]
    n_pages = page_table.shape[1]
    pos_s = n_pages * PAGE_SIZE
    m = b * t
    kv_shape = (2, N_KV, HEAD_DIM)

    wm, ws = _prep_w_in_a(w_in_a[0])
    wo_nsa = w_out_a[0, :Q_DIM].astype(BF)
    wo_ssd = w_out_a[0, Q_DIM:].astype(BF)
    cmp_k = _prep_cmp_w(cmp_pe_k[0], cmp_w1_k[0], cmp_w2_k[0])
    cmp_v = _prep_cmp_w(cmp_pe_v[0], cmp_w1_v[0], cmp_w2_v[0])
    ssd_par = _ssd_params(ssd_conv_w[0], ssd_conv_b[0], ssd_dt_bias[0], ssd_a_log[0], ssd_d[0], ssd_norm[0])
    lru_par = _lru_params(norm_mix[1], w_in_c[0], lru_conv_w[0], lru_conv_b[0], lru_w_a[0], lru_b_a[0], lru_w_x[0],
                          lru_b_x[0], lru_lambda[0], w_out_c[0])
    ffn_w = [(norm_ffn[l].reshape(1, d), w_ffn_gate[l].astype(BF), w_ffn_up[l].astype(BF), w_ffn_down[l].astype(BF))
             for l in range(2)]
    gfin = norm_final.reshape(1, d)

    tm_proj = 512
    xp = x_prompt.reshape(m, d)
    cos_p, sin_p = _rope_tables(jnp.arange(t, dtype=jnp.int32))
    q, qr, kvc, kvs_t, kvw_t, zx, small, kvc_t = _inproj_a(xp, norm_mix[0], wm, ws, cos_p, sin_p, tm_proj, b, True)
    cmp_p = _compress_prompt(kvc, b, t, *cmp_k, *cmp_v)
    o_nsa = _nsa_prompt(q, qr, cmp_p, kvs_t, kvw_t, small, b, t)
    y_ssd, ssm_p = _ssd_prompt(zx, small, ssd_par, b, t)
    tm_ffn = 256
    nt = t // tm_ffn
    x2 = _ffn(xp, *ffn_w[0], gfin, tm_ffn, False, lambda i: (i, 0), lambda i: (i % nt, i // nt), (t, b * d),
              mixer=(o_nsa, y_ssd, wo_nsa, wo_ssd))
    x3, lru_p, lconv_tail = _lru_prompt(x2.reshape(t * b, d), *lru_par, b, 32)
    y_prompt = _ffn(x3.reshape(t, b * d), *ffn_w[1], gfin, tm_ffn, True, lambda i: (i % nt, i // nt),
                    lambda i: (i, 0), (m, d))
    w_keep = min(WINDOW, t)
    to_cache = lambda a: a.reshape((b,) + kv_shape + (a.shape[-1],)).transpose(0, 4, 1, 2, 3)[None]
    kv_cmp_p = to_cache(kvc_t)
    kv_slc_p = to_cache(kvs_t)
    kv_win_p = to_cache(kvw_t[:, :, t - w_keep:])
    ssd_conv_p = zx.reshape(b, t, -1)[:, t - (CONV_WIDTH - 1):, SSD_INNER:][None]
    lru_conv_p = lconv_tail.reshape(CONV_WIDTH - 1, b, -1).transpose(1, 0, 2)[None]

    xs = x_sample.reshape(db, d)
    cos_s, sin_s = _rope_tables(jnp.full((db,), pos_s, dtype=jnp.int32))
    q_s, qr_s, kvc_s, kvs_s, kvw_s, zx_s, small_s = _inproj_a(xs, norm_mix[0], wm, ws, cos_s, sin_s, db, 1, False)
    pt_flat = page_table.reshape(-1)
    n_phys = cache_kv_cmp.shape[1]
    feature_major = lambda a: a.transpose(0, 2, 3, 4, 1).reshape(a.shape[0], KV_DIM, a.shape[1])
    cmp_s = _compress_decode(feature_major(cache_kv_cmp[0]), pt_flat, db, n_pages, *cmp_k, *cmp_v)
    win_buf = cache_kv_win[0].reshape(db, -1, KV_DIM)
    head3 = lambda a: a.reshape(db, N_HEADS, HEAD_DIM)
    o_cmp_s, o_win_s, idx = _nsa_decode_a(head3(q_s), head3(qr_s), cmp_s, feature_major(cache_kv_win[0]),
                                          kvw_s.reshape(db, 1, KV_DIM), pos_s)
    idx_flat = idx[:, :N_KV, :N_SELECT].reshape(-1)
    o_nsa_s = _nsa_decode_b(idx_flat, pt_flat, head3(qr_s), feature_major(cache_kv_slc[0]),
                            kvs_s.reshape(db, 1, KV_DIM), o_cmp_s, o_win_s, small_s.reshape(db, 1, LANES), pos_s,
                            n_pages)
    y_ssd_s, ssm_s = _ssd_decode(zx_s.reshape(db, 1, -1), state_ssd_conv[0], small_s.reshape(db, 1, LANES),
                                 state_ssm[0], ssd_par)
    x2_s = _ffn(xs, *ffn_w[0], gfin, db, False, lambda i: (i, 0), lambda i: (i, 0), (db, d),
                mixer=(o_nsa_s.reshape(db, Q_DIM), y_ssd_s.reshape(db, SSD_INNER), wo_nsa, wo_ssd))
    x3_s, lru_s, xbr_s = _lru_decode(x2_s, lru_par, state_lru_conv[0], state_lru[0])
    y_sample = _ffn(x3_s, *ffn_w[1], gfin, db, True, lambda i: (i, 0), lambda i: (i, 0), (db, d))
    kv_cmp_s = kvc_s.reshape((1, db, 1) + kv_shape)
    kv_slc_s = kvs_s.reshape((1, db, 1) + kv_shape)
    kv_win_s = jnp.concatenate([win_buf[:, 1:], kvw_s[:, None, :]], axis=1).reshape(
        (1, db, win_buf.shape[1]) + kv_shape)
    ssd_conv_s = jnp.concatenate([state_ssd_conv[0][:, 1:], zx_s[:, None, SSD_INNER:]], axis=1)[None]
    lru_conv_s = jnp.concatenate([state_lru_conv[0][:, 1:], xbr_s[:, None, :]], axis=1)[None]

    return (y_prompt.reshape(b, t, d), y_sample.reshape(db, 1, d),
            kv_cmp_p, kv_slc_p, kv_win_p, ssm_p[None], ssd_conv_p, lru_p[None], lru_conv_p,
            kv_cmp_s, kv_slc_s, kv_win_s, ssm_s[None], ssd_conv_s, lru_s[None], lru_conv_s)
```

```python
import functools
import math

import jax
import jax.numpy as jnp
from jax import lax
from jax.experimental import pallas as pl
from jax.experimental.pallas import tpu as pltpu

BF = jnp.bfloat16
F32 = jnp.float32

HEAD_DIM = 64
N_KV = 2
Q_PER_KV = 4
N_HEADS = N_KV * Q_PER_KV
CMP_STRIDE = 16
CMP_HIDDEN = 256
SLC_BLOCK = 64
N_SELECT = 16
WINDOW = 512
PAGE_SIZE = 128
ROPE_THETA = 10000.0
FORCE_SCORE = 1e9
NEG_BIG = -1e30
SSD_HEADS = 8
SSD_HEAD_DIM = 64
SSD_STATE = 128
SSD_INNER = SSD_HEADS * SSD_HEAD_DIM
SSD_CHUNK = 128
CONV_WIDTH = 4
LRU_BLOCK = 128
LRU_C = 8.0
RMS_EPS = 1e-6
SCALE = HEAD_DIM ** -0.5
LOG2E = math.log2(math.e)
Q_DIM = N_HEADS * HEAD_DIM
KV_DIM = 2 * N_KV * HEAD_DIM
GATE_DIM = 3 * N_HEADS
DT_COL = GATE_DIM
LANES = 128
VMEM_LIMIT = 56 * 1024 * 1024


def _cparams(sem):
    return pltpu.CompilerParams(dimension_semantics=sem, vmem_limit_bytes=VMEM_LIMIT)


def _dot(a, b):
    return jnp.dot(a, b, preferred_element_type=F32)


def _dot_nt(a, b):
    return lax.dot_general(a, b, (((1,), (1,)), ((), ())), preferred_element_type=F32)


def _dot_tn(a, b):
    return lax.dot_general(a, b, (((0,), (0,)), ((), ())), preferred_element_type=F32)


def _split3(x):
    hi = x.astype(BF)
    r = x - hi.astype(F32)
    mid = r.astype(BF)
    lo = (r - mid.astype(F32)).astype(BF)
    return hi, mid, lo


def _dot3(x, m01):
    hi, mid, lo = _split3(x)
    return _dot(hi, m01) + _dot(mid, m01) + _dot(lo, m01)


def _dot3_left(m01, x):
    hi, mid, lo = _split3(x)
    return _dot(m01, hi) + _dot(m01, mid) + _dot(m01, lo)


def _rms(x, g):
    y = x * lax.rsqrt(jnp.mean(x * x, axis=-1, keepdims=True) + RMS_EPS)
    return y * g


def _silu(x):
    return x * jax.nn.sigmoid(x)


def _softplus(x):
    return jnp.maximum(x, 0.0) + jnp.log1p(jnp.exp(-jnp.abs(x)))


def _msoftmax_parts(s, mask):
    s = jnp.where(mask, s, NEG_BIG)
    m = jnp.max(s, axis=-1, keepdims=True)
    e = jnp.where(mask, jnp.exp(s - m), 0.0)
    return m, e


def _msoftmax(s, mask):
    _, e = _msoftmax_parts(s, mask)
    d = jnp.sum(e, axis=-1, keepdims=True)
    return e / jnp.where(d > 0.0, d, 1.0)


def _rope_tables(pos):
    half = HEAD_DIM // 2
    inv_freq = ROPE_THETA ** (-jnp.arange(half, dtype=F32) / half)
    ang = pos.astype(F32)[:, None] * inv_freq[None, :]
    cos = jnp.cos(ang)
    sin = jnp.sin(ang)
    cos2 = jnp.tile(jnp.concatenate([cos, cos], axis=-1), (1, LANES // HEAD_DIM))
    sin2 = jnp.tile(jnp.concatenate([-sin, sin], axis=-1), (1, LANES // HEAD_DIM))
    return cos2, sin2


def _inproj_a_kernel(x_ref, g_ref, wm_ref, ws_ref, cos_ref, sin_ref,
                     q_ref, qr_ref, kvc_ref, kvs_ref, kvw_ref, zx_ref, sm_ref, *kvc_t_ref, feature_major):
    xn = _rms(x_ref[...], g_ref[...]).astype(BF)
    cos = cos_ref[...]
    sin = sin_ref[...]
    lane = lax.broadcasted_iota(jnp.int32, cos.shape, 1)
    first = (lane % HEAD_DIM) < (HEAD_DIM // 2)

    def rope(v):
        rot = jnp.where(first, pltpu.roll(v, LANES - HEAD_DIM // 2, 1), pltpu.roll(v, HEAD_DIM // 2, 1))
        return v * cos + rot * sin

    def store_kv(ref, k, v):
        if feature_major:
            ref[0:LANES, :] = k.T
            ref[LANES:KV_DIM, :] = v.T
        else:
            ref[:, 0:LANES] = k
            ref[:, LANES:KV_DIM] = v

    q = _dot(xn, wm_ref[:, 0:Q_DIM])
    for c in range(Q_DIM // LANES):
        cols = slice(c * LANES, (c + 1) * LANES)
        if feature_major:
            q_ref[cols, :] = q[:, cols].T
            qr_ref[cols, :] = rope(q[:, cols]).T
        else:
            q_ref[:, cols] = q[:, cols]
            qr_ref[:, cols] = rope(q[:, cols])
    o = Q_DIM
    kvc = _dot(xn, wm_ref[:, o:o + KV_DIM])
    kvc_ref[...] = kvc
    if feature_major:
        store_kv(kvc_t_ref[0], kvc[:, 0:LANES], kvc[:, LANES:KV_DIM])
    o += KV_DIM
    kvs = _dot(xn, wm_ref[:, o:o + KV_DIM])
    store_kv(kvs_ref, rope(kvs[:, 0:LANES]), kvs[:, LANES:KV_DIM])
    o += KV_DIM
    kvw = _dot(xn, wm_ref[:, o:o + KV_DIM])
    store_kv(kvw_ref, rope(kvw[:, 0:LANES]), kvw[:, LANES:KV_DIM])
    o += KV_DIM
    zx_ref[...] = _dot(xn, wm_ref[:, o:])
    sm_ref[...] = _dot(xn, ws_ref[...])


def _prep_w_in_a(w_in_a):
    a = Q_DIM + 3 * KV_DIM
    gate = w_in_a[:, a:a + GATE_DIM]
    rest = w_in_a[:, a + GATE_DIM:]
    zx_w = rest[:, :rest.shape[1] - SSD_HEADS]
    dt = rest[:, rest.shape[1] - SSD_HEADS:]
    main = jnp.concatenate([w_in_a[:, :a], zx_w], axis=1).astype(BF)
    small = jnp.concatenate([gate, dt], axis=1)
    small = jnp.pad(small, ((0, 0), (0, LANES - small.shape[1]))).astype(BF)
    return main, small


def _inproj_a(x2d, g, wm, ws, cos2, sin2, tm, nb, feature_major):
    m, d = x2d.shape
    nt = m // nb // tm
    zx_dim = wm.shape[1] - Q_DIM - 3 * KV_DIM
    row = lambda w: pl.BlockSpec((tm, w), lambda i, j: (i * nt + j, 0))
    row_shape = lambda w: jax.ShapeDtypeStruct((m, w), F32)
    const = lambda i, j: (0, 0)
    if feature_major:
        fm = lambda w: pl.BlockSpec((None, w, tm), lambda i, j: (i, 0, j))
        fm_shape = lambda w: jax.ShapeDtypeStruct((nb, w, m // nb), F32)
    else:
        fm, fm_shape = row, row_shape
    kv, kv_shape = fm(KV_DIM), fm_shape(KV_DIM)
    out_specs = [fm(Q_DIM), fm(Q_DIM), row(KV_DIM), kv, kv, row(zx_dim), row(LANES)]
    out_shape = [fm_shape(Q_DIM), fm_shape(Q_DIM), row_shape(KV_DIM), kv_shape, kv_shape, row_shape(zx_dim),
                 row_shape(LANES)]
    if feature_major:
        out_specs.append(kv)
        out_shape.append(kv_shape)
    pos = pl.BlockSpec((tm, LANES), lambda i, j: (j, 0))
    return pl.pallas_call(
        functools.partial(_inproj_a_kernel, feature_major=feature_major),
        grid=(nb, nt),
        in_specs=[row(d), pl.BlockSpec((1, d), const), pl.BlockSpec(wm.shape, const), pl.BlockSpec(ws.shape, const),
                  pos, pos],
        out_specs=out_specs,
        out_shape=out_shape,
        compiler_params=_cparams(("parallel", "parallel")),
        name="inproj_a",
    )(x2d, g.reshape(1, d), wm, ws, cos2, sin2)


def _compress_accumulate(load_rows, pe_ref, w1_ref, col):
    a0 = None
    a1 = None
    for s in range(CMP_STRIDE):
        x = load_rows(s)[:, col:col + HEAD_DIM]
        l = _dot((x + pe_ref[s:s + 1, :]).astype(BF), w1_ref[s])
        t = _dot((x + pe_ref[CMP_STRIDE + s:CMP_STRIDE + s + 1, :]).astype(BF), w1_ref[CMP_STRIDE + s])
        a0 = l if a0 is None else a0 + l
        a1 = t if a1 is None else a1 + t
    return a0, a1


def _compress_prompt_kernel(kc_ref, vc_ref, pek_ref, w1k_ref, w2k_ref, pev_ref, w1v_ref, w2v_ref, out_ref, *,
                            n_half):
    branches = ((kc_ref, pek_ref, w1k_ref, w2k_ref), (vc_ref, pev_ref, w1v_ref, w2v_ref))
    for c, (src_ref, pe_ref, w1_ref, w2_ref) in enumerate(branches):
        def load_rows(s, src_ref=src_ref):
            return src_ref[pl.ds(s, n_half, stride=CMP_STRIDE), :]

        for g in range(N_KV):
            a0, a1 = _compress_accumulate(load_rows, pe_ref, w1_ref, g * HEAD_DIM)
            pre = a0 + pltpu.roll(a1, n_half - 1, 0)
            tok = _dot(_silu(pre).astype(BF), w2_ref[...])
            out_ref[g, :, c * HEAD_DIM:(c + 1) * HEAD_DIM] = tok


def _compress_prompt(kvc2d, b, t, pek, w1k, w2k, pev, w1v, w2v):
    n_half = t // CMP_STRIDE
    full = lambda a: pl.BlockSpec(a.shape, lambda i: (0,) * a.ndim)
    return pl.pallas_call(
        functools.partial(_compress_prompt_kernel, n_half=n_half),
        grid=(b,),
        in_specs=[pl.BlockSpec((t, LANES), lambda i: (i, 0)), pl.BlockSpec((t, LANES), lambda i: (i, 1)),
                  full(pek), full(w1k), full(w2k), full(pev), full(w1v), full(w2v)],
        out_specs=pl.BlockSpec((None, N_KV, n_half, LANES), lambda i: (i, 0, 0, 0)),
        out_shape=jax.ShapeDtypeStruct((b, N_KV, n_half, LANES), F32),
        compiler_params=_cparams(("parallel",)),
        name="compress_prompt",
    )(kvc2d, kvc2d, pek, w1k, w2k, pev, w1v, w2v)


def _prep_cmp_w(pe, w1, w2):
    return pe, w1.reshape(2 * CMP_STRIDE, HEAD_DIM, CMP_HIDDEN).astype(BF), w2.astype(BF)


def _rank_select(imp_t):
    n_rows, tq = imp_t.shape
    jrow = lax.broadcasted_iota(jnp.int32, (n_rows, tq), 0)
    cnt = jnp.zeros((n_rows, tq), F32)
    for jp in range(n_rows):
        row = imp_t[jp:jp + 1, :]
        ahead = jnp.where(row == imp_t, jnp.where(jrow > jp, 1.0, 0.0), jnp.where(row > imp_t, 1.0, 0.0))
        cnt = cnt + ahead
    return jnp.where(cnt < float(N_SELECT), 1.0, 0.0)


def _heads_on_lanes(ref, g):
    rows = [(g * Q_PER_KV + r) * HEAD_DIM for r in range(Q_PER_KV)]
    return (jnp.concatenate([ref[c:c + HEAD_DIM, :] for c in rows], axis=1) * (SCALE * LOG2E)).astype(BF)


def _per_head(x):
    return jnp.concatenate([x] * Q_PER_KV, axis=1)


def _nsa_prompt_kernel(q_ref, qr_ref, cmp_ref, kvs_ref, kvw_ref, sm_ref, ovt_ref, o_ref, sel_ref, *, t_len, tq, kc):
    t0 = pl.program_id(1) * tq
    hq = Q_PER_KV
    nq = hq * tq
    tcol = t0 + lax.broadcasted_iota(jnp.int32, (1, tq), 1)
    gates_t = jax.nn.sigmoid(sm_ref[...]).T
    tok = lax.broadcasted_iota(jnp.int32, (LANES, tq), 0)
    cmask = _per_head(jnp.where((tok * CMP_STRIDE + (2 * CMP_STRIDE - 1)) <= tcol, 1.0, 0.0)) > 0.5
    n_sel = t_len // SLC_BLOCK
    blk = lax.broadcasted_iota(jnp.int32, (n_sel, tq), 0)
    valid = (blk * SLC_BLOCK) <= tcol
    cur = tcol // SLC_BLOCK
    forced = (blk == 0) | (blk == cur) | (blk == cur - 1)
    n_win = WINDOW + tq
    start = pl.multiple_of(jnp.clip(t0 - WINDOW, 0, t_len - n_win), LANES)
    dist = tcol - (start + lax.broadcasted_iota(jnp.int32, (n_win, tq), 0))
    wbias = _per_head(jnp.where(dist >= 0, jnp.where(dist < WINDOW, 0.0, NEG_BIG), NEG_BIG))
    krel = lax.broadcasted_iota(jnp.int32, (kc, tq), 0)
    n_chunks = (t0 + tq - 1) // kc + 1
    blocks_per_chunk = kc // SLC_BLOCK

    krows = [slice(g * HEAD_DIM, (g + 1) * HEAD_DIM) for g in range(N_KV)]
    vrows = [slice(LANES + g * HEAD_DIM, LANES + (g + 1) * HEAD_DIM) for g in range(N_KV)]

    o_cmp = []
    for g in range(N_KV):
        s = jnp.where(cmask, _dot(cmp_ref[g, :, 0:HEAD_DIM].astype(BF), _heads_on_lanes(q_ref, g)), NEG_BIG)
        e = jnp.where(cmask, jnp.exp2(s - jnp.max(s, axis=0, keepdims=True)), 0.0)
        d = jnp.sum(e, axis=0, keepdims=True)
        p = e / jnp.where(d > 0.0, d, 1.0)
        o_cmp.append(_dot_tn(cmp_ref[g, :, HEAD_DIM:LANES].astype(BF), p.astype(BF)))
        psum = p[:, 0:tq]
        for r in range(1, hq):
            psum = psum + p[:, r * tq:(r + 1) * tq]
        imp = _dot3_left(ovt_ref[...], psum)[0:n_sel, :]
        imp = jnp.where(valid, jnp.where(forced, FORCE_SCORE, imp), -FORCE_SCORE)
        sel_ref[g] = _rank_select(imp)

    qs = [_heads_on_lanes(qr_ref, g) for g in range(N_KV)]

    def chunk(ci, carry):
        k0 = pl.multiple_of(ci * kc, kc)
        causal = k0 + krel <= tcol
        j0 = pl.multiple_of(ci * blocks_per_chunk, blocks_per_chunk)
        out = []
        for g in range(N_KV):
            m, l, acc = carry[g]
            kt = kvs_ref[krows[g], pl.ds(k0, kc)].astype(BF)
            vt = kvs_ref[vrows[g], pl.ds(k0, kc)].astype(BF)
            sel_c = sel_ref[g, pl.ds(j0, blocks_per_chunk), :]
            sel_k = jnp.concatenate([jnp.broadcast_to(sel_c[i:i + 1, :], (SLC_BLOCK, tq))
                                     for i in range(blocks_per_chunk)], axis=0)
            bias = jnp.where(causal, jnp.where(sel_k > 0.5, 0.0, NEG_BIG), NEG_BIG)
            s = _dot_tn(kt, qs[g]) + _per_head(bias)
            m_new = jnp.maximum(m, jnp.max(s, axis=0, keepdims=True))
            alpha = jnp.exp2(m - m_new)
            e = jnp.exp2(s - m_new)
            l = alpha * l + jnp.sum(e, axis=0, keepdims=True)
            out.append((m_new, l, alpha * acc + _dot(vt, e.astype(BF))))
        return tuple(out)

    init = (jnp.full((1, nq), NEG_BIG, F32), jnp.zeros((1, nq), F32), jnp.zeros((HEAD_DIM, nq), F32))
    slc = lax.fori_loop(0, n_chunks, chunk, (init,) * N_KV)

    for g in range(N_KV):
        _, l, acc = slc[g]
        o_slc = acc / l
        kw = kvw_ref[krows[g], pl.ds(start, n_win)].astype(BF)
        vw = kvw_ref[vrows[g], pl.ds(start, n_win)].astype(BF)
        s = _dot_tn(kw, qs[g]) + wbias
        e = jnp.exp2(s - jnp.max(s, axis=0, keepdims=True))
        o_win = _dot(vw, e.astype(BF)) / jnp.sum(e, axis=0, keepdims=True)
        for r in range(hq):
            h = g * hq + r
            cols = slice(r * tq, (r + 1) * tq)
            o_ref[h * HEAD_DIM:(h + 1) * HEAD_DIM, :] = (
                gates_t[3 * h:3 * h + 1, :] * o_cmp[g][:, cols] + gates_t[3 * h + 1:3 * h + 2, :] * o_slc[:, cols]
                + gates_t[3 * h + 2:3 * h + 3, :] * o_win[:, cols])


def _overlap_matrix(n_rows, n_cols, row_shift):
    n = jnp.arange(n_rows)[:, None] - row_shift
    c_start = n * CMP_STRIDE
    s_start = jnp.arange(n_cols)[None, :] * SLC_BLOCK
    ov = (c_start < s_start + SLC_BLOCK) & (c_start + 2 * CMP_STRIDE > s_start) & (n >= 0)
    return ov.astype(BF)


def _nsa_prompt(q_t, qr_t, cmp, kvs_t, kvw_t, small, b, t):
    tq = 128
    kc = 512
    nq = t // tq
    n_half = t // CMP_STRIDE
    n_sel = t // SLC_BLOCK
    assert n_half == LANES and n_sel <= LANES and t % kc == 0
    ov_t = _overlap_matrix(LANES, LANES, 0).T
    qtile = pl.BlockSpec((None, Q_DIM, tq), lambda i, j: (i, 0, j))
    seq = pl.BlockSpec((None, KV_DIM, t), lambda i, j: (i, 0, 0))
    return pl.pallas_call(
        functools.partial(_nsa_prompt_kernel, t_len=t, tq=tq, kc=kc),
        grid=(b, nq),
        in_specs=[qtile, qtile,
                  pl.BlockSpec((None, N_KV, n_half, LANES), lambda i, j: (i, 0, 0, 0)),
                  seq, seq, pl.BlockSpec((tq, LANES), lambda i, j: (i * nq + j, 0)),
                  pl.BlockSpec(ov_t.shape, lambda i, j: (0, 0))],
        out_specs=qtile,
        out_shape=jax.ShapeDtypeStruct((b, Q_DIM, t), F32),
        scratch_shapes=[pltpu.VMEM((N_KV, n_sel, tq), F32)],
        compiler_params=_cparams(("parallel", "parallel")),
        name="nsa_prompt",
    )(q_t, qr_t, cmp, kvs_t, kvw_t, small, ov_t)


def _ssd_prompt_kernel(zx_ref, sm_ref, cw_ref, cb_ref, dtb_ref, alog_ref, dskip_ref, nrm_ref, tri_ref,
                       y_ref, hout_ref, tail_ref, h_ref):
    c = pl.program_id(1)
    lc = SSD_CHUNK

    @pl.when(c == 0)
    def _():
        tail_ref[...] = jnp.zeros_like(tail_ref)
        h_ref[...] = jnp.zeros_like(h_ref)

    z = zx_ref[:, 0:SSD_INNER]
    xbc = zx_ref[:, SSD_INNER:]
    u = jnp.concatenate([tail_ref[...], xbc], axis=0)
    tail_ref[...] = xbc[lc - 8:lc, :]
    conv = cb_ref[...]
    for k in range(CONV_WIDTH):
        conv = conv + u[5 + k:5 + k + lc, :] * cw_ref[k:k + 1, :]
    xbc_c = _silu(conv)
    xs = xbc_c[:, 0:SSD_INNER]

    dt_full = _softplus(sm_ref[...] + dtb_ref[...])
    a_full = dt_full * (-jnp.exp(alog_ref[...]))
    acs = _dot3_left(tri_ref[...], a_full)
    acs_t = acs.T
    li = lax.broadcasted_iota(jnp.int32, (lc, lc), 0)
    si = lax.broadcasted_iota(jnp.int32, (lc, lc), 1)
    lower = li >= si

    ys = []
    for g in range(2):
        bm = xbc_c[:, SSD_INNER + g * SSD_STATE:SSD_INNER + (g + 1) * SSD_STATE]
        cm = xbc_c[:, SSD_INNER + (2 + g) * SSD_STATE:SSD_INNER + (3 + g) * SSD_STATE]
        bm_b = bm.astype(BF)
        cm_b = cm.astype(BF)
        cb = _dot_nt(cm_b, bm_b)
        for hh in range(SSD_HEADS // 2):
            h = g * (SSD_HEADS // 2) + hh
            col = DT_COL + h
            dt_h = dt_full[:, col:col + 1]
            acs_h = acs[:, col:col + 1]
            acs_row = acs_t[col:col + 1, :]
            acs_last = acs[lc - 1:lc, col:col + 1]
            lmat = jnp.where(lower, jnp.exp(acs_h - acs_row), 0.0)
            x_h = xs[:, h * SSD_HEAD_DIM:(h + 1) * SSD_HEAD_DIM]
            xdt = x_h * dt_h
            y_diag = _dot((cb * lmat).astype(BF), xdt.astype(BF))
            h_prev = h_ref[h]
            y_off = _dot_nt(cm_b, h_prev.astype(BF)) * jnp.exp(acs_h)
            decay = jnp.exp(acs_last - acs_h)
            h_new = jnp.exp(acs_last) * h_prev + _dot_tn((xdt * decay).astype(BF), bm_b)
            h_ref[h] = h_new
            ys.append(y_diag + y_off)
    y = jnp.concatenate(ys, axis=1) + dskip_ref[...] * xs
    y = y * _silu(z)
    y_ref[...] = _rms(y, nrm_ref[...])
    hout_ref[...] = h_ref[...]


def _ssd_params(conv_w, conv_b, dt_bias, a_log, d_skip, ssd_norm):
    pad = lambda v: jnp.pad(v, (DT_COL, LANES - DT_COL - SSD_HEADS)).reshape(1, LANES)
    return (conv_w, conv_b.reshape(1, -1), pad(dt_bias), pad(a_log),
            jnp.repeat(d_skip, SSD_HEAD_DIM).reshape(1, SSD_INNER), ssd_norm.reshape(1, SSD_INNER))


def _ssd_prompt(zx, small, params, b, t):
    lc = SSD_CHUNK
    nc = t // lc
    cw, cb, dtb, alog, dskip, nrm = params
    tri = (jnp.arange(lc)[:, None] >= jnp.arange(lc)[None, :]).astype(BF)
    zx_dim = zx.shape[1]
    conv_dim = zx_dim - SSD_INNER
    full = lambda a: pl.BlockSpec(a.shape, lambda i, j: (0,) * a.ndim)
    tile = lambda w: pl.BlockSpec((lc, w), lambda i, j: (i * nc + j, 0))
    return pl.pallas_call(
        _ssd_prompt_kernel,
        grid=(b, nc),
        in_specs=[tile(zx_dim), tile(LANES), full(cw), full(cb), full(dtb), full(alog), full(dskip), full(nrm),
                  full(tri)],
        out_specs=[tile(SSD_INNER),
                   pl.BlockSpec((None, SSD_HEADS, SSD_HEAD_DIM, SSD_STATE), lambda i, j: (i, 0, 0, 0))],
        out_shape=[jax.ShapeDtypeStruct((b * t, SSD_INNER), F32),
                   jax.ShapeDtypeStruct((b, SSD_HEADS, SSD_HEAD_DIM, SSD_STATE), F32)],
        scratch_shapes=[pltpu.VMEM((8, conv_dim), F32), pltpu.VMEM((SSD_HEADS, SSD_HEAD_DIM, SSD_STATE), F32)],
        compiler_params=_cparams(("parallel", "arbitrary")),
        name="ssd_prompt",
    )(zx, small, cw, cb, dtb, alog, dskip, nrm, tri)


def _ffn_kernel(x_ref, *refs, final_norm, mixer_proj, a_feature_major):
    x = x_ref[...]
    if mixer_proj:
        a_ref, b_ref, wa_ref, wb_ref = refs[:4]
        refs = refs[4:]
        a = a_ref[...].astype(BF)
        x = x + (_dot_tn(a, wa_ref[...]) if a_feature_major else _dot(a, wa_ref[...]))
        x = x + _dot(b_ref[...].astype(BF), wb_ref[...])
    g_ref, wg_ref, wu_ref, wd_ref, gf_ref, o_ref = refs
    xn = _rms(x, g_ref[...]).astype(BF)
    h = _silu(_dot(xn, wg_ref[...])) * _dot(xn, wu_ref[...])
    y = x + _dot(h.astype(BF), wd_ref[...])
    if final_norm:
        y = _rms(y, gf_ref[...])
    o_ref[...] = y


def _ffn(x2d, g, wg, wu, wd, gf, tm, final_norm, in_map, out_map, out_2d_shape, mixer=None):
    m, d = x2d.shape[0] * x2d.shape[1] // wg.shape[0], wg.shape[0]
    n_steps = m // tm
    full = lambda arr: pl.BlockSpec(arr.shape, lambda i: (0, 0), pipeline_mode=pl.Buffered(1))
    args = [x2d]
    in_specs = [pl.BlockSpec((tm, d), in_map)]
    a_feature_major = mixer is not None and mixer[0].ndim == 3
    if mixer is not None:
        a, bb, wa, wb = mixer
        args += [a, bb, wa, wb]
        if a_feature_major:
            nt = a.shape[2] // tm
            a_spec = pl.BlockSpec((None, a.shape[1], tm), lambda i: (i // nt, 0, i % nt))
        else:
            a_spec = pl.BlockSpec((tm, a.shape[1]), in_map)
        in_specs += [a_spec, pl.BlockSpec((tm, bb.shape[1]), in_map), full(wa), full(wb)]
    args += [g, wg, wu, wd, gf]
    in_specs += [full(g), full(wg), full(wu), full(wd), full(gf)]
    return pl.pallas_call(
        functools.partial(_ffn_kernel, final_norm=final_norm, mixer_proj=mixer is not None,
                          a_feature_major=a_feature_major),
        grid=(n_steps,),
        in_specs=in_specs,
        out_specs=pl.BlockSpec((tm, d), out_map),
        out_shape=jax.ShapeDtypeStruct(out_2d_shape, F32),
        compiler_params=_cparams(("parallel",)),
        name="ffn_final" if final_norm else "ffn",
    )(*args)


def _lru_gates(xc, wa_ref, ba_ref, wx_ref, bx_ref, sp):
    n_heads = xc.shape[1] // LRU_BLOCK
    a_parts = []
    gx_parts = []
    for h in range(n_heads):
        sl = slice(h * LRU_BLOCK, (h + 1) * LRU_BLOCK)
        xh = xc[:, sl]
        xb = xh.astype(BF)
        r = jax.nn.sigmoid(_dot(xb, wa_ref[h]) + ba_ref[:, sl])
        i = jax.nn.sigmoid(_dot(xb, wx_ref[h]) + bx_ref[:, sl])
        log_a = -LRU_C * r * sp[:, sl]
        a = jnp.exp(log_a)
        a_parts.append(a)
        gx_parts.append(jnp.sqrt(1.0 - a * a) * (i * xh))
    return jnp.concatenate(a_parts, axis=1), jnp.concatenate(gx_parts, axis=1)


SUBLANES = 8


def _lru_prompt_kernel(x_ref, g_ref, win_ref, cw_ref, cb_ref, wa_ref, ba_ref, wx_ref, bx_ref, lam_ref, wo_ref,
                       o_ref, hout_ref, cout_ref, tail_ref, h_ref, a_s, gx_s, hs_s, *, rows, w):
    @pl.when(pl.program_id(1) == 0)
    def _():
        tail_ref[...] = jnp.zeros_like(tail_ref)
        h_ref[...] = jnp.zeros_like(h_ref)

    x = x_ref[...]
    xn = _rms(x, g_ref[...]).astype(BF)
    proj = _dot(xn, win_ref[...])
    gate_br = proj[:, 0:w]
    x_br = proj[:, w:2 * w]
    u = jnp.concatenate([tail_ref[...], x_br], axis=0)
    tail_ref[...] = x_br[rows - SUBLANES:rows, :]
    cout_ref[...] = x_br[rows - (CONV_WIDTH - 1):rows, :]
    xc = cb_ref[...]
    for k in range(CONV_WIDTH):
        off = SUBLANES - (CONV_WIDTH - 1) + k
        xc = xc + u[off:off + rows, :] * cw_ref[k:k + 1, :]
    sp = _softplus(-lam_ref[...])
    a, gx = _lru_gates(xc, wa_ref, ba_ref, wx_ref, bx_ref, sp)
    a_s[...] = a
    gx_s[...] = gx
    sub = lax.broadcasted_iota(jnp.int32, (SUBLANES, w), 0)
    h = h_ref[...]
    for i in range(rows // SUBLANES):
        blk = slice(i * SUBLANES, (i + 1) * SUBLANES)
        ac = a_s[blk, :]
        bc = gx_s[blk, :]
        for dd in (1, 2, 4):
            a_sh = jnp.where(sub >= dd, pltpu.roll(ac, dd, 0), 1.0)
            b_sh = jnp.where(sub >= dd, pltpu.roll(bc, dd, 0), 0.0)
            bc = ac * b_sh + bc
            ac = ac * a_sh
        hb = ac * h + bc
        hs_s[blk, :] = hb
        h = jnp.broadcast_to(hb[SUBLANES - 1:SUBLANES, :], (SUBLANES, w))
    h_ref[...] = h
    hout_ref[...] = h[0:1, :]
    y = (jax.nn.gelu(gate_br) * hs_s[...]).astype(BF)
    o_ref[...] = x + _dot(y, wo_ref[...])


def _lru_prompt(x2d, g, w_in, cw, cb, wa, ba, wx, bx, lam, wo, nb, rows):
    m, d = x2d.shape
    w = cw.shape[1]
    nt = m // nb // rows
    full = lambda a: pl.BlockSpec(a.shape, lambda i, j: (0,) * a.ndim)
    tile = pl.BlockSpec((rows, d), lambda i, j: (i * nt + j, 0))
    args = (g, w_in, cw, cb, wa, ba, wx, bx, lam, wo)
    return pl.pallas_call(
        functools.partial(_lru_prompt_kernel, rows=rows, w=w),
        grid=(nb, nt),
        in_specs=[tile] + [full(a) for a in args],
        out_specs=[tile,
                   pl.BlockSpec((None, 1, w), lambda i, j: (i, 0, 0)),
                   pl.BlockSpec((None, CONV_WIDTH - 1, w), lambda i, j: (i, 0, 0))],
        out_shape=[jax.ShapeDtypeStruct((m, d), F32),
                   jax.ShapeDtypeStruct((nb, 1, w), F32),
                   jax.ShapeDtypeStruct((nb, CONV_WIDTH - 1, w), F32)],
        scratch_shapes=[pltpu.VMEM((SUBLANES, w), F32), pltpu.VMEM((SUBLANES, w), F32),
                        pltpu.VMEM((rows, w), F32), pltpu.VMEM((rows, w), F32), pltpu.VMEM((rows, w), F32)],
        compiler_params=_cparams(("parallel", "arbitrary")),
        name="lru_prompt",
    )(x2d, *args)


def _lru_params(norm_g, w_in_c, conv_w, conv_b, w_a, b_a, w_x, b_x, lam, w_out_c):
    r = lambda v: v.reshape(1, -1)
    return (r(norm_g), w_in_c.astype(BF), conv_w, r(conv_b), w_a.astype(BF), r(b_a), w_x.astype(BF), r(b_x),
            r(lam), w_out_c.astype(BF))


def _lru_decode_kernel(x_ref, g_ref, win_ref, cw_ref, cb_ref, wa_ref, ba_ref, wx_ref, bx_ref, lam_ref, wo_ref,
                       c0_ref, c1_ref, c2_ref, h0_ref, o_ref, hout_ref, xbr_ref, *, w):
    x = x_ref[...]
    xn = _rms(x, g_ref[...]).astype(BF)
    proj = _dot(xn, win_ref[...])
    gate_br = proj[:, 0:w]
    x_br = proj[:, w:2 * w]
    xbr_ref[...] = x_br
    xc = (cb_ref[...] + c0_ref[...] * cw_ref[0:1, :] + c1_ref[...] * cw_ref[1:2, :]
          + c2_ref[...] * cw_ref[2:3, :] + x_br * cw_ref[3:4, :])
    sp = _softplus(-lam_ref[...])
    a, gx = _lru_gates(xc, wa_ref, ba_ref, wx_ref, bx_ref, sp)
    h = a * h0_ref[...] + gx
    hout_ref[...] = h
    y = (jax.nn.gelu(gate_br) * h).astype(BF)
    o_ref[...] = x + _dot(y, wo_ref[...])


def _lru_decode(x, params, conv_state, h0):
    m, d = x.shape
    w = h0.shape[1]
    args = (x,) + tuple(params) + (conv_state[:, 0], conv_state[:, 1], conv_state[:, 2], h0)
    full = lambda a: pl.BlockSpec(a.shape, lambda i: (0,) * a.ndim)
    return pl.pallas_call(
        functools.partial(_lru_decode_kernel, w=w),
        grid=(1,),
        in_specs=[full(a) for a in args],
        out_specs=[pl.BlockSpec((m, d), lambda i: (0, 0)), pl.BlockSpec((m, w), lambda i: (0, 0)),
                   pl.BlockSpec((m, w), lambda i: (0, 0))],
        out_shape=[jax.ShapeDtypeStruct((m, d), F32), jax.ShapeDtypeStruct((m, w), F32),
                   jax.ShapeDtypeStruct((m, w), F32)],
        compiler_params=_cparams(("arbitrary",)),
        name="lru_decode",
    )(*args)


PAGES_PER_STEP = 32
HALVES_PER_PAGE = PAGE_SIZE // CMP_STRIDE
CMP_PACK = 2 * LANES // HEAD_DIM


def _compress_decode_kernel(pt_ref, *refs):
    del pt_ref
    np_ = PAGES_PER_STEP
    k_pages = refs[0:np_]
    v_pages = refs[np_:2 * np_]
    pek_ref, w1k_ref, w2k_ref, pev_ref, w1v_ref, w2v_ref, out_ref, carry_ref, rows_ref = refs[2 * np_:]
    rows = np_ * HALVES_PER_PAGE

    @pl.when(pl.program_id(1) == 0)
    def _():
        carry_ref[...] = jnp.zeros_like(carry_ref)

    rowi = lax.broadcasted_iota(jnp.int32, (rows, CMP_HIDDEN), 0)
    low = lax.broadcasted_iota(jnp.int32, (rows, LANES), 1) < HEAD_DIM

    def split_groups(a, b):
        return (jnp.where(low, a, pltpu.roll(b, HEAD_DIM, 1)), jnp.where(low, pltpu.roll(a, HEAD_DIM, 1), b))

    branches = ((k_pages, pek_ref, w1k_ref, w2k_ref), (v_pages, pev_ref, w1v_ref, w2v_ref))
    for c, (pages, pe_ref, w1_ref, w2_ref) in enumerate(branches):
        for k, p in enumerate(pages):
            rows_ref[c, k * PAGE_SIZE:(k + 1) * PAGE_SIZE, :] = p[...].T

        lead = None
        trail = None
        for u in range(CMP_STRIDE // CMP_PACK):
            xs = [rows_ref[c, pl.ds(u * CMP_PACK + j, rows, stride=CMP_STRIDE), :] for j in range(CMP_PACK)]
            g01 = split_groups(xs[0], xs[1])
            g23 = split_groups(xs[2], xs[3])
            x = jnp.concatenate([jnp.concatenate([g01[0], g23[0]], axis=1),
                                 jnp.concatenate([g01[1], g23[1]], axis=1)], axis=0)
            dl = _dot((x + pe_ref[0, u]).astype(BF), w1_ref[0, u])
            dt = _dot((x + pe_ref[1, u]).astype(BF), w1_ref[1, u])
            lead = dl if lead is None else lead + dl
            trail = dt if trail is None else trail + dt
        for g in range(N_KV):
            a0 = lead[g * rows:(g + 1) * rows]
            a1 = trail[g * rows:(g + 1) * rows]
            slot = c * N_KV + g
            prev = jnp.where(rowi == 0, carry_ref[slot, 7:8, :], pltpu.roll(a0, 1, 0))
            carry_ref[slot] = a0[rows - 8:rows, :]
            tok = _dot(_silu(prev + a1).astype(BF), w2_ref[...])
            out_ref[g, :, c * HEAD_DIM:(c + 1) * HEAD_DIM] = tok


def _compress_decode(cache_t, pt_flat, db, n_pages, pek, w1k, w2k, pev, w1v, w2v):
    np_ = PAGES_PER_STEP
    n_steps = n_pages // np_
    rows = np_ * HALVES_PER_PAGE
    n_half = n_pages * HALVES_PER_PAGE
    n_quads = CMP_STRIDE // CMP_PACK
    pack_pe = lambda pe: pe.reshape(2, n_quads, 1, CMP_PACK * HEAD_DIM)
    pack_w1 = lambda w1: w1.reshape(2, n_quads, CMP_PACK * HEAD_DIM, CMP_HIDDEN)
    pek, w1k, pev, w1v = pack_pe(pek), pack_w1(w1k), pack_pe(pev), pack_w1(w1v)

    def page_spec(k, rowblk):
        return pl.BlockSpec((None, LANES, PAGE_SIZE),
                            lambda i, j, pt: (pt[i * n_pages + j * np_ + k], rowblk, 0))

    full = lambda a: pl.BlockSpec(a.shape, lambda i, j, pt: (0,) * a.ndim)
    grid_spec = pltpu.PrefetchScalarGridSpec(
        num_scalar_prefetch=1,
        grid=(db, n_steps),
        in_specs=[page_spec(k, 0) for k in range(np_)] + [page_spec(k, 1) for k in range(np_)]
        + [full(pek), full(w1k), full(w2k), full(pev), full(w1v), full(w2v)],
        out_specs=pl.BlockSpec((None, N_KV, rows, LANES), lambda i, j, pt: (i, 0, j, 0)),
        scratch_shapes=[pltpu.VMEM((2 * N_KV, 8, CMP_HIDDEN), F32), pltpu.VMEM((2, np_ * PAGE_SIZE, LANES), F32)],
    )
    return pl.pallas_call(
        _compress_decode_kernel,
        grid_spec=grid_spec,
        out_shape=jax.ShapeDtypeStruct((db, N_KV, n_half, LANES), F32),
        compiler_params=_cparams(("parallel", "arbitrary")),
        name="compress_decode",
    )(pt_flat, *([cache_t] * (2 * np_)), pek, w1k, w2k, pev, w1v, w2v)


def _group_rows(top, a, b):
    return jnp.where(top, a, b)


def _nsa_decode_a_kernel(q_ref, qr_ref, cmp_ref, win_ref, new_ref, ov_ref, ocmp_ref, owin_ref, idx_ref, *,
                         pos, n_blocks):
    nh = N_HEADS
    top = lax.broadcasted_iota(jnp.int32, (nh, 1), 0) < Q_PER_KV
    q = q_ref[...].astype(BF)
    n_tok = cmp_ref.shape[1]
    lane = lax.broadcasted_iota(jnp.int32, (nh, n_tok), 1)
    cmask = (lane >= 1) & (((lane - 1) * CMP_STRIDE + (2 * CMP_STRIDE - 1)) <= pos)
    s = _group_rows(top, _dot_nt(q, cmp_ref[0, :, 0:HEAD_DIM].astype(BF)),
                    _dot_nt(q, cmp_ref[1, :, 0:HEAD_DIM].astype(BF))) * SCALE
    p = _msoftmax(s, cmask)
    pb = p.astype(BF)
    ocmp_ref[...] = _group_rows(top, _dot(pb, cmp_ref[0, :, HEAD_DIM:LANES].astype(BF)),
                                _dot(pb, cmp_ref[1, :, HEAD_DIM:LANES].astype(BF)))

    imp8 = _dot3(p, ov_ref[...])
    nj = ov_ref.shape[1]
    cur = pos // SLC_BLOCK
    j = lax.broadcasted_iota(jnp.int32, (1, nj), 1)
    valid = (j * SLC_BLOCK) <= pos
    forced = (j == 0) | (j == cur) | (j == cur - 1)
    ri = lax.broadcasted_iota(jnp.int32, (nj, nj), 0)
    ci = lax.broadcasted_iota(jnp.int32, (nj, nj), 1)
    k_lane = lax.broadcasted_iota(jnp.int32, (nj, LANES), 1).astype(F32)
    jvals = lax.broadcasted_iota(jnp.int32, (8, nj), 1).astype(F32).astype(BF)
    idx_rows = []
    for g in range(N_KV):
        imp = jnp.sum(imp8[g * Q_PER_KV:(g + 1) * Q_PER_KV, :], axis=0, keepdims=True)
        imp = jnp.where(valid & forced, FORCE_SCORE, imp)
        imp = jnp.where(valid, imp, -FORCE_SCORE)
        imp = jnp.where(j < n_blocks, imp, -3e38)
        impb = jnp.broadcast_to(imp, (nj, nj))
        col = jnp.sum(jnp.where(ri == ci, impb, 0.0), axis=1, keepdims=True)
        ahead = jnp.where(impb == col, jnp.where(ci < ri, 1.0, 0.0), jnp.where(impb > col, 1.0, 0.0))
        rank_col = jnp.sum(ahead, axis=1, keepdims=True)
        onehot = jnp.where(rank_col == k_lane, 1.0, 0.0).astype(BF)
        idx_rows.append(_dot(jvals, onehot)[0:1, :])
    idx = jnp.concatenate(idx_rows + [jnp.zeros((8 - N_KV, LANES), F32)], axis=0)
    idx_ref[...] = idx.astype(jnp.int32)

    qr = qr_ref[...]
    qrb = qr.astype(BF)
    n_win = win_ref.shape[1]
    s = _group_rows(top, _dot(qrb, win_ref[0:HEAD_DIM, :].astype(BF)),
                    _dot(qrb, win_ref[HEAD_DIM:LANES, :].astype(BF))) * SCALE
    wl = lax.broadcasted_iota(jnp.int32, (nh, n_win), 1)
    dist = n_win - wl
    wmask = (dist >= 0) & (dist < WINDOW)
    new = new_ref[...]
    knew = _group_rows(top, new[:, 0:HEAD_DIM], new[:, HEAD_DIM:LANES])
    vnew = _group_rows(top, new[:, LANES:LANES + HEAD_DIM], new[:, LANES + HEAD_DIM:KV_DIM])
    s_new = jnp.sum(qr * knew, axis=1, keepdims=True) * SCALE
    sm = jnp.where(wmask, s, NEG_BIG)
    m = jnp.maximum(jnp.max(sm, axis=1, keepdims=True), s_new)
    e = jnp.where(wmask, jnp.exp(sm - m), 0.0)
    e_new = jnp.exp(s_new - m)
    d = jnp.sum(e, axis=1, keepdims=True) + e_new
    eb = e.astype(BF)
    o = _group_rows(top, _dot_nt(eb, win_ref[LANES:LANES + HEAD_DIM, :].astype(BF)),
                    _dot_nt(eb, win_ref[LANES + HEAD_DIM:KV_DIM, :].astype(BF))) + e_new * vnew
    owin_ref[...] = o / d


def _nsa_decode_a(q3, qr3, cmp_s, win, kvw_new3, pos):
    db = q3.shape[0]
    n_tok = cmp_s.shape[2]
    n_blocks = -(-(pos + 1) // SLC_BLOCK)
    nj = -(-n_blocks // LANES) * LANES
    ov = _overlap_matrix(n_tok, nj, 1)
    per_b = lambda a: pl.BlockSpec((None,) + a.shape[1:], lambda i: (i,) + (0,) * (a.ndim - 1))
    head = pl.BlockSpec((None, N_HEADS, HEAD_DIM), lambda i: (i, 0, 0))
    return pl.pallas_call(
        functools.partial(_nsa_decode_a_kernel, pos=pos, n_blocks=n_blocks),
        grid=(db,),
        in_specs=[head, head, per_b(cmp_s), per_b(win), per_b(kvw_new3), pl.BlockSpec(ov.shape, lambda i: (0, 0))],
        out_specs=[head, head, pl.BlockSpec((None, 8, LANES), lambda i: (i, 0, 0))],
        out_shape=[jax.ShapeDtypeStruct((db, N_HEADS, HEAD_DIM), F32),
                   jax.ShapeDtypeStruct((db, N_HEADS, HEAD_DIM), F32),
                   jax.ShapeDtypeStruct((db, 8, LANES), jnp.int32)],
        compiler_params=_cparams(("parallel",)),
        name="nsa_decode_select",
    )(q3, qr3, cmp_s, win, kvw_new3, ov)


def _nsa_decode_b_kernel(idx_ref, pt_ref, qr_ref, *refs, n_past_blocks, cur):
    del pt_ref
    nb = N_KV * N_SELECT
    blk_refs = refs[0:nb]
    new_ref, ocmp_ref, owin_ref, sm_ref, o_ref = refs[nb:]
    b = pl.program_id(0)
    nh = N_HEADS
    top = lax.broadcasted_iota(jnp.int32, (nh, 1), 0) < Q_PER_KV
    qr = qr_ref[...]
    qrb = qr.astype(BF)
    per_page = PAGE_SIZE // SLC_BLOCK
    n_keys = N_SELECT * PAGE_SIZE
    lane = lax.broadcasted_iota(jnp.int32, (nh, n_keys), 1)
    lane_page = lane // PAGE_SIZE
    lane_sub = (lane // SLC_BLOCK) % per_page
    s_g = []
    v_g = []
    ok_g = []
    has_new = []
    for g in range(N_KV):
        blks = blk_refs[g * N_SELECT:(g + 1) * N_SELECT]
        kt = jnp.concatenate([r[g * HEAD_DIM:(g + 1) * HEAD_DIM, :] for r in blks], axis=1).astype(BF)
        v_g.append(jnp.concatenate([r[LANES + g * HEAD_DIM:LANES + (g + 1) * HEAD_DIM, :] for r in blks],
                                   axis=1).astype(BF))
        s_g.append(_dot(qrb, kt))
        ok = jnp.zeros((nh, n_keys), F32)
        new_sel = jnp.zeros((), F32)
        for k in range(N_SELECT):
            jk = idx_ref[(b * N_KV + g) * N_SELECT + k]
            hit = (lane_page == k) & (lane_sub == jk % per_page)
            ok = jnp.where(hit, jnp.where(jk < n_past_blocks, 1.0, 0.0), ok)
            new_sel = jnp.maximum(new_sel, jnp.where(jk == cur, 1.0, 0.0))
        ok_g.append(ok)
        has_new.append(new_sel)
    s = _group_rows(top, s_g[0], s_g[1]) * SCALE
    mask = _group_rows(top, ok_g[0], ok_g[1]) > 0.5
    new_on = _group_rows(top, has_new[0], has_new[1]) > 0.5
    new = new_ref[...]
    knew = _group_rows(top, new[:, 0:HEAD_DIM], new[:, HEAD_DIM:LANES])
    vnew = _group_rows(top, new[:, LANES:LANES + HEAD_DIM], new[:, LANES + HEAD_DIM:KV_DIM])
    s_new = jnp.where(new_on, jnp.sum(qr * knew, axis=1, keepdims=True) * SCALE, NEG_BIG)
    sm = jnp.where(mask, s, NEG_BIG)
    m = jnp.maximum(jnp.max(sm, axis=1, keepdims=True), s_new)
    e = jnp.where(mask, jnp.exp(sm - m), 0.0)
    e_new = jnp.where(new_on, jnp.exp(s_new - m), 0.0)
    d = jnp.sum(e, axis=1, keepdims=True) + e_new
    eb = e.astype(BF)
    o_slc = ((_group_rows(top, _dot_nt(eb, v_g[0]), _dot_nt(eb, v_g[1])) + e_new * vnew)
             / jnp.where(d > 0.0, d, 1.0))

    sig = jnp.broadcast_to(jax.nn.sigmoid(sm_ref[...]), (nh, LANES))
    hl = lax.broadcasted_iota(jnp.int32, (nh, LANES), 1)
    hr = lax.broadcasted_iota(jnp.int32, (nh, LANES), 0)

    def gate(br):
        return jnp.sum(jnp.where(hl == 3 * hr + br, sig, 0.0), axis=1, keepdims=True)

    o_ref[...] = gate(0) * ocmp_ref[...] + gate(1) * o_slc + gate(2) * owin_ref[...]


def _nsa_decode_b(idx_flat, pt_flat, qr3, slc_pages_t, kvs_new3, o_cmp, o_win, small3, pos, n_pages):
    db = qr3.shape[0]
    n_past_blocks = pos // SLC_BLOCK
    cur = pos // SLC_BLOCK
    per_page = PAGE_SIZE // SLC_BLOCK

    def blk_spec(g, k):
        def imap(i, idx, pt):
            jk = jnp.minimum(idx[(i * N_KV + g) * N_SELECT + k], n_past_blocks - 1)
            return (pt[i * n_pages + jk // per_page], 0, 0)
        return pl.BlockSpec((None, KV_DIM, PAGE_SIZE), imap)

    head = pl.BlockSpec((None, N_HEADS, HEAD_DIM), lambda i, idx, pt: (i, 0, 0))
    row3 = lambda a: pl.BlockSpec((None,) + a.shape[1:], lambda i, idx, pt: (i, 0, 0))
    grid_spec = pltpu.PrefetchScalarGridSpec(
        num_scalar_prefetch=2,
        grid=(db,),
        in_specs=[head] + [blk_spec(g, k) for g in range(N_KV) for k in range(N_SELECT)]
        + [row3(kvs_new3), head, head, row3(small3)],
        out_specs=head,
    )
    return pl.pallas_call(
        functools.partial(_nsa_decode_b_kernel, n_past_blocks=n_past_blocks, cur=cur),
        grid_spec=grid_spec,
        out_shape=jax.ShapeDtypeStruct((db, N_HEADS, HEAD_DIM), F32),
        compiler_params=_cparams(("arbitrary",)),
        name="nsa_decode_attend",
    )(idx_flat, pt_flat, qr3, *([slc_pages_t] * (N_KV * N_SELECT)), kvs_new3, o_cmp, o_win, small3)


def _ssd_decode_kernel(zx_ref, cst_ref, sm_ref, h0_ref, cw_ref, cb_ref, dtb_ref, alog_ref, dskip_ref, nrm_ref,
                       y_ref, hout_ref):
    z = zx_ref[:, 0:SSD_INNER]
    xbc = zx_ref[:, SSD_INNER:]
    conv = cb_ref[...] + xbc * cw_ref[CONV_WIDTH - 1:CONV_WIDTH, :]
    for k in range(CONV_WIDTH - 1):
        conv = conv + cst_ref[k:k + 1, :] * cw_ref[k:k + 1, :]
    xbc_c = _silu(conv)
    xs = xbc_c[:, 0:SSD_INNER]
    dt_full = _softplus(sm_ref[...] + dtb_ref[...])
    da_full = jnp.exp(dt_full * (-jnp.exp(alog_ref[...])))
    p = SSD_HEAD_DIM
    eye = lax.broadcasted_iota(jnp.int32, (p, p), 0) == lax.broadcasted_iota(jnp.int32, (p, p), 1)
    ys = []
    for h in range(SSD_HEADS):
        g = h // (SSD_HEADS // 2)
        col = DT_COL + h
        xdt = xs[:, h * p:(h + 1) * p] * dt_full[:, col:col + 1]
        xcol = jnp.sum(jnp.where(eye, jnp.broadcast_to(xdt, (p, p)), 0.0), axis=1, keepdims=True)
        bm = xbc_c[:, SSD_INNER + g * SSD_STATE:SSD_INNER + (g + 1) * SSD_STATE]
        cm = xbc_c[:, SSD_INNER + (2 + g) * SSD_STATE:SSD_INNER + (3 + g) * SSD_STATE]
        h_new = da_full[:, col:col + 1] * h0_ref[h] + xcol * bm
        hout_ref[h] = h_new
        ys.append(_dot_nt(jnp.broadcast_to(cm, (8, SSD_STATE)).astype(BF), h_new.astype(BF))[0:1, :])
    y = jnp.concatenate(ys, axis=1) + dskip_ref[...] * xs
    y = y * _silu(z)
    y_ref[...] = _rms(y, nrm_ref[...])


def _ssd_decode(zx3, conv_state, small3, h0, params):
    db = zx3.shape[0]
    cw, cb, dtb, alog, dskip, nrm = params
    per_b = lambda a: pl.BlockSpec((None,) + a.shape[1:], lambda i: (i,) + (0,) * (a.ndim - 1))
    full = lambda a: pl.BlockSpec(a.shape, lambda i: (0,) * a.ndim)
    return pl.pallas_call(
        _ssd_decode_kernel,
        grid=(db,),
        in_specs=[per_b(zx3), per_b(conv_state), per_b(small3), per_b(h0), full(cw), full(cb), full(dtb), full(alog),
                  full(dskip), full(nrm)],
        out_specs=[pl.BlockSpec((None, 1, SSD_INNER), lambda i: (i, 0, 0)), per_b(h0)],
        out_shape=[jax.ShapeDtypeStruct((db, 1, SSD_INNER), F32), jax.ShapeDtypeStruct(h0.shape, F32)],
        compiler_params=_cparams(("parallel",)),
        name="ssd_decode",
    )(zx3, conv_state, small3, h0, cw, cb, dtb, alog, dskip, nrm)


def kernel(x_prompt, x_sample, cache_kv_cmp, cache_kv_slc, cache_kv_win, state_ssm, state_ssd_conv, state_lru,
           state_lru_conv, page_table, norm_mix, norm_ffn, norm_final, w_ffn_gate, w_ffn_up, w_ffn_down, w_in_a,
           w_out_a, cmp_pe_k, cmp_w1_k, cmp_w2_k, cmp_pe_v, cmp_w1_v, cmp_w2_v, ssd_conv_w, ssd_conv_b, ssd_dt_bias,
           ssd_a_log, ssd_d, ssd_norm, w_in_c, lru_conv_w, lru_conv_b, lru_w_a, lru_b_a, lru_w_x, lru_b_x,
           lru_lambda, w_out_c):
    b, t, d = x_prompt.shape
    db = x_sample.shape[0]
    n_pages = page_table.shape[1]
    pos_s = n_pages * PAGE_SIZE
    m = b * t
    kv_shape = (2, N_KV, HEAD_DIM)

    wm, ws = _prep_w_in_a(w_in_a[0])
    wo_nsa = w_out_a[0, :Q_DIM].astype(BF)
    wo_ssd = w_out_a[0, Q_DIM:].astype(BF)
    cmp_k = _prep_cmp_w(cmp_pe_k[0], cmp_w1_k[0], cmp_w2_k[0])
    cmp_v = _prep_cmp_w(cmp_pe_v[0], cmp_w1_v[0], cmp_w2_v[0])
    ssd_par = _ssd_params(ssd_conv_w[0], ssd_conv_b[0], ssd_dt_bias[0], ssd_a_log[0], ssd_d[0], ssd_norm[0])
    lru_par = _lru_params(norm_mix[1], w_in_c[0], lru_conv_w[0], lru_conv_b[0], lru_w_a[0], lru_b_a[0], lru_w_x[0],
                          lru_b_x[0], lru_lambda[0], w_out_c[0])
    ffn_w = [(norm_ffn[l].reshape(1, d), w_ffn_gate[l].astype(BF), w_ffn_up[l].astype(BF), w_ffn_down[l].astype(BF))
             for l in range(2)]
    gfin = norm_final.reshape(1, d)

    tm_proj = 512
    xp = x_prompt.reshape(m, d)
    cos_p, sin_p = _rope_tables(jnp.arange(t, dtype=jnp.int32))
    q, qr, kvc, kvs_t, kvw_t, zx, small, kvc_t = _inproj_a(xp, norm_mix[0], wm, ws, cos_p, sin_p, tm_proj, b, True)
    cmp_p = _compress_prompt(kvc, b, t, *cmp_k, *cmp_v)
    o_nsa = _nsa_prompt(q, qr, cmp_p, kvs_t, kvw_t, small, b, t)
    y_ssd, ssm_p = _ssd_prompt(zx, small, ssd_par, b, t)
    tm_ffn = 512
    rowmap = lambda i: (i, 0)
    x2 = _ffn(xp, *ffn_w[0], gfin, tm_ffn, False, rowmap, rowmap, (m, d), mixer=(o_nsa, y_ssd, wo_nsa, wo_ssd))
    x3, lru_p, lru_conv_p = _lru_prompt(x2, *lru_par, b, 512)
    y_prompt = _ffn(x3, *ffn_w[1], gfin, tm_ffn, True, rowmap, rowmap, (m, d))
    w_keep = min(WINDOW, t)
    to_cache = lambda a: a.reshape((b,) + kv_shape + (a.shape[-1],)).transpose(0, 4, 1, 2, 3)[None]
    kv_cmp_p = to_cache(kvc_t)
    kv_slc_p = to_cache(kvs_t)
    kv_win_p = to_cache(kvw_t[:, :, t - w_keep:])
    ssd_conv_p = zx.reshape(b, t, -1)[:, t - (CONV_WIDTH - 1):, SSD_INNER:][None]
    lru_p = lru_p.reshape(b, -1)
    lru_conv_p = lru_conv_p[None]

    xs = x_sample.reshape(db, d)
    cos_s, sin_s = _rope_tables(jnp.full((db,), pos_s, dtype=jnp.int32))
    q_s, qr_s, kvc_s, kvs_s, kvw_s, zx_s, small_s = _inproj_a(xs, norm_mix[0], wm, ws, cos_s, sin_s, db, 1, False)
    pt_flat = page_table.reshape(-1)
    n_phys = cache_kv_cmp.shape[1]
    feature_major = lambda a: a.transpose(0, 2, 3, 4, 1).reshape(a.shape[0], KV_DIM, a.shape[1])
    cmp_s = _compress_decode(feature_major(cache_kv_cmp[0]), pt_flat, db, n_pages, *cmp_k, *cmp_v)
    win_buf = cache_kv_win[0].reshape(db, -1, KV_DIM)
    head3 = lambda a: a.reshape(db, N_HEADS, HEAD_DIM)
    o_cmp_s, o_win_s, idx = _nsa_decode_a(head3(q_s), head3(qr_s), cmp_s, feature_major(cache_kv_win[0]),
                                          kvw_s.reshape(db, 1, KV_DIM), pos_s)
    idx_flat = idx[:, :N_KV, :N_SELECT].reshape(-1)
    o_nsa_s = _nsa_decode_b(idx_flat, pt_flat, head3(qr_s), feature_major(cache_kv_slc[0]),
                            kvs_s.reshape(db, 1, KV_DIM), o_cmp_s, o_win_s, small_s.reshape(db, 1, LANES), pos_s,
                            n_pages)
    y_ssd_s, ssm_s = _ssd_decode(zx_s.reshape(db, 1, -1), state_ssd_conv[0], small_s.reshape(db, 1, LANES),
                                 state_ssm[0], ssd_par)
    x2_s = _ffn(xs, *ffn_w[0], gfin, db, False, lambda i: (i, 0), lambda i: (i, 0), (db, d),
                mixer=(o_nsa_s.reshape(db, Q_DIM), y_ssd_s.reshape(db, SSD_INNER), wo_nsa, wo_ssd))
    x3_s, lru_s, xbr_s = _lru_decode(x2_s, lru_par, state_lru_conv[0], state_lru[0])
    y_sample = _ffn(x3_s, *ffn_w[1], gfin, db, True, lambda i: (i, 0), lambda i: (i, 0), (db, d))
    kv_cmp_s = kvc_s.reshape((1, db, 1) + kv_shape)
    kv_slc_s = kvs_s.reshape((1, db, 1) + kv_shape)
    kv_win_s = jnp.concatenate([win_buf[:, 1:], kvw_s[:, None, :]], axis=1).reshape(
        (1, db, win_buf.shape[1]) + kv_shape)
    ssd_conv_s = jnp.concatenate([state_ssd_conv[0][:, 1:], zx_s[:, None, SSD_INNER:]], axis=1)[None]
    lru_conv_s = jnp.concatenate([state_lru_conv[0][:, 1:], xbr_s[:, None, :]], axis=1)[None]

    return (y_prompt.reshape(b, t, d), y_sample.reshape(db, 1, d),
            kv_cmp_p, kv_slc_p, kv_win_p, ssm_p[None], ssd_conv_p, lru_p[None], lru_conv_p,
            kv_cmp_s, kv_slc_s, kv_win_s, ssm_s[None], ssd_conv_s, lru_s[None], lru_conv_s)
```

```python
import functools
import math

import jax
import jax.numpy as jnp
from jax import lax
from jax.experimental import pallas as pl
from jax.experimental.pallas import tpu as pltpu

BF = jnp.bfloat16
F32 = jnp.float32

HEAD_DIM = 64
N_KV = 2
Q_PER_KV = 4
N_HEADS = N_KV * Q_PER_KV
CMP_STRIDE = 16
CMP_HIDDEN = 256
SLC_BLOCK = 64
N_SELECT = 16
WINDOW = 512
PAGE_SIZE = 128
ROPE_THETA = 10000.0
FORCE_SCORE = 1e9
NEG_BIG = -1e30
SSD_HEADS = 8
SSD_HEAD_DIM = 64
SSD_STATE = 128
SSD_INNER = SSD_HEADS * SSD_HEAD_DIM
SSD_CHUNK = 128
CONV_WIDTH = 4
LRU_BLOCK = 128
LRU_C = 8.0
RMS_EPS = 1e-6
SCALE = HEAD_DIM ** -0.5
LOG2E = math.log2(math.e)
Q_DIM = N_HEADS * HEAD_DIM
KV_DIM = 2 * N_KV * HEAD_DIM
GATE_DIM = 3 * N_HEADS
DT_COL = GATE_DIM
LANES = 128
VMEM_LIMIT = 56 * 1024 * 1024


def _cparams(sem):
    return pltpu.CompilerParams(dimension_semantics=sem, vmem_limit_bytes=VMEM_LIMIT)


def _dot(a, b):
    return jnp.dot(a, b, preferred_element_type=F32)


def _dot_nt(a, b):
    return lax.dot_general(a, b, (((1,), (1,)), ((), ())), preferred_element_type=F32)


def _dot_tn(a, b):
    return lax.dot_general(a, b, (((0,), (0,)), ((), ())), preferred_element_type=F32)


def _split3(x):
    hi = x.astype(BF)
    r = x - hi.astype(F32)
    mid = r.astype(BF)
    lo = (r - mid.astype(F32)).astype(BF)
    return hi, mid, lo


def _dot3(x, m01):
    hi, mid, lo = _split3(x)
    return _dot(hi, m01) + _dot(mid, m01) + _dot(lo, m01)


def _dot3_left(m01, x):
    hi, mid, lo = _split3(x)
    return _dot(m01, hi) + _dot(m01, mid) + _dot(m01, lo)


def _rms(x, g):
    y = x * lax.rsqrt(jnp.mean(x * x, axis=-1, keepdims=True) + RMS_EPS)
    return y * g


def _silu(x):
    return x * jax.nn.sigmoid(x)


def _softplus(x):
    return jnp.maximum(x, 0.0) + jnp.log1p(jnp.exp(-jnp.abs(x)))


def _msoftmax_parts(s, mask):
    s = jnp.where(mask, s, NEG_BIG)
    m = jnp.max(s, axis=-1, keepdims=True)
    e = jnp.where(mask, jnp.exp(s - m), 0.0)
    return m, e


def _msoftmax(s, mask):
    _, e = _msoftmax_parts(s, mask)
    d = jnp.sum(e, axis=-1, keepdims=True)
    return e / jnp.where(d > 0.0, d, 1.0)


def _rope_tables(pos):
    half = HEAD_DIM // 2
    inv_freq = ROPE_THETA ** (-jnp.arange(half, dtype=F32) / half)
    ang = pos.astype(F32)[:, None] * inv_freq[None, :]
    cos = jnp.cos(ang)
    sin = jnp.sin(ang)
    cos2 = jnp.tile(jnp.concatenate([cos, cos], axis=-1), (1, LANES // HEAD_DIM))
    sin2 = jnp.tile(jnp.concatenate([-sin, sin], axis=-1), (1, LANES // HEAD_DIM))
    return cos2, sin2


def _inproj_a_kernel(x_ref, g_ref, wm_ref, ws_ref, cos_ref, sin_ref,
                     q_ref, qr_ref, kvc_ref, kvs_ref, kvw_ref, zx_ref, sm_ref, *kvc_t_ref, feature_major):
    xn = _rms(x_ref[...], g_ref[...]).astype(BF)
    cos = cos_ref[...]
    sin = sin_ref[...]
    lane = lax.broadcasted_iota(jnp.int32, cos.shape, 1)
    first = (lane % HEAD_DIM) < (HEAD_DIM // 2)

    def rope(v):
        rot = jnp.where(first, pltpu.roll(v, LANES - HEAD_DIM // 2, 1), pltpu.roll(v, HEAD_DIM // 2, 1))
        return v * cos + rot * sin

    def store_kv(ref, k, v):
        if feature_major:
            ref[0:LANES, :] = k.T
            ref[LANES:KV_DIM, :] = v.T
        else:
            ref[:, 0:LANES] = k
            ref[:, LANES:KV_DIM] = v

    q = _dot(xn, wm_ref[:, 0:Q_DIM])
    for c in range(Q_DIM // LANES):
        cols = slice(c * LANES, (c + 1) * LANES)
        if feature_major:
            q_ref[cols, :] = q[:, cols].T
            qr_ref[cols, :] = rope(q[:, cols]).T
        else:
            q_ref[:, cols] = q[:, cols]
            qr_ref[:, cols] = rope(q[:, cols])
    o = Q_DIM
    kvc = _dot(xn, wm_ref[:, o:o + KV_DIM])
    kvc_ref[...] = kvc
    if feature_major:
        store_kv(kvc_t_ref[0], kvc[:, 0:LANES], kvc[:, LANES:KV_DIM])
    o += KV_DIM
    kvs = _dot(xn, wm_ref[:, o:o + KV_DIM])
    store_kv(kvs_ref, rope(kvs[:, 0:LANES]), kvs[:, LANES:KV_DIM])
    o += KV_DIM
    kvw = _dot(xn, wm_ref[:, o:o + KV_DIM])
    store_kv(kvw_ref, rope(kvw[:, 0:LANES]), kvw[:, LANES:KV_DIM])
    o += KV_DIM
    zx_ref[...] = _dot(xn, wm_ref[:, o:])
    sm_ref[...] = _dot(xn, ws_ref[...])


def _prep_w_in_a(w_in_a):
    a = Q_DIM + 3 * KV_DIM
    gate = w_in_a[:, a:a + GATE_DIM]
    rest = w_in_a[:, a + GATE_DIM:]
    zx_w = rest[:, :rest.shape[1] - SSD_HEADS]
    dt = rest[:, rest.shape[1] - SSD_HEADS:]
    main = jnp.concatenate([w_in_a[:, :a], zx_w], axis=1).astype(BF)
    small = jnp.concatenate([gate, dt], axis=1)
    small = jnp.pad(small, ((0, 0), (0, LANES - small.shape[1]))).astype(BF)
    return main, small


def _inproj_a(x2d, g, wm, ws, cos2, sin2, tm, nb, feature_major):
    m, d = x2d.shape
    nt = m // nb // tm
    zx_dim = wm.shape[1] - Q_DIM - 3 * KV_DIM
    row = lambda w: pl.BlockSpec((tm, w), lambda i, j: (i * nt + j, 0))
    row_shape = lambda w: jax.ShapeDtypeStruct((m, w), F32)
    const = lambda i, j: (0, 0)
    if feature_major:
        fm = lambda w: pl.BlockSpec((None, w, tm), lambda i, j: (i, 0, j))
        fm_shape = lambda w: jax.ShapeDtypeStruct((nb, w, m // nb), F32)
    else:
        fm, fm_shape = row, row_shape
    kv, kv_shape = fm(KV_DIM), fm_shape(KV_DIM)
    out_specs = [fm(Q_DIM), fm(Q_DIM), row(KV_DIM), kv, kv, row(zx_dim), row(LANES)]
    out_shape = [fm_shape(Q_DIM), fm_shape(Q_DIM), row_shape(KV_DIM), kv_shape, kv_shape, row_shape(zx_dim),
                 row_shape(LANES)]
    if feature_major:
        out_specs.append(kv)
        out_shape.append(kv_shape)
    pos = pl.BlockSpec((tm, LANES), lambda i, j: (j, 0))
    return pl.pallas_call(
        functools.partial(_inproj_a_kernel, feature_major=feature_major),
        grid=(nb, nt),
        in_specs=[row(d), pl.BlockSpec((1, d), const), pl.BlockSpec(wm.shape, const), pl.BlockSpec(ws.shape, const),
                  pos, pos],
        out_specs=out_specs,
        out_shape=out_shape,
        compiler_params=_cparams(("parallel", "parallel")),
        name="inproj_a",
    )(x2d, g.reshape(1, d), wm, ws, cos2, sin2)


def _compress_accumulate(load_rows, pe_ref, w1_ref, col):
    a0 = None
    a1 = None
    for s in range(CMP_STRIDE):
        x = load_rows(s)[:, col:col + HEAD_DIM]
        l = _dot((x + pe_ref[s:s + 1, :]).astype(BF), w1_ref[s])
        t = _dot((x + pe_ref[CMP_STRIDE + s:CMP_STRIDE + s + 1, :]).astype(BF), w1_ref[CMP_STRIDE + s])
        a0 = l if a0 is None else a0 + l
        a1 = t if a1 is None else a1 + t
    return a0, a1


def _compress_prompt_kernel(kc_ref, vc_ref, pek_ref, w1k_ref, w2k_ref, pev_ref, w1v_ref, w2v_ref, out_ref, *,
                            n_half):
    branches = ((kc_ref, pek_ref, w1k_ref, w2k_ref), (vc_ref, pev_ref, w1v_ref, w2v_ref))
    for c, (src_ref, pe_ref, w1_ref, w2_ref) in enumerate(branches):
        def load_rows(s, src_ref=src_ref):
            return src_ref[pl.ds(s, n_half, stride=CMP_STRIDE), :]

        for g in range(N_KV):
            a0, a1 = _compress_accumulate(load_rows, pe_ref, w1_ref, g * HEAD_DIM)
            pre = a0 + pltpu.roll(a1, n_half - 1, 0)
            tok = _dot(_silu(pre).astype(BF), w2_ref[...])
            out_ref[g, :, c * HEAD_DIM:(c + 1) * HEAD_DIM] = tok


def _compress_prompt(kvc2d, b, t, pek, w1k, w2k, pev, w1v, w2v):
    n_half = t // CMP_STRIDE
    full = lambda a: pl.BlockSpec(a.shape, lambda i: (0,) * a.ndim)
    return pl.pallas_call(
        functools.partial(_compress_prompt_kernel, n_half=n_half),
        grid=(b,),
        in_specs=[pl.BlockSpec((t, LANES), lambda i: (i, 0)), pl.BlockSpec((t, LANES), lambda i: (i, 1)),
                  full(pek), full(w1k), full(w2k), full(pev), full(w1v), full(w2v)],
        out_specs=pl.BlockSpec((None, N_KV, n_half, LANES), lambda i: (i, 0, 0, 0)),
        out_shape=jax.ShapeDtypeStruct((b, N_KV, n_half, LANES), F32),
        compiler_params=_cparams(("parallel",)),
        name="compress_prompt",
    )(kvc2d, kvc2d, pek, w1k, w2k, pev, w1v, w2v)


def _prep_cmp_w(pe, w1, w2):
    return pe, w1.reshape(2 * CMP_STRIDE, HEAD_DIM, CMP_HIDDEN).astype(BF), w2.astype(BF)


def _rank_select(imp_t):
    n_rows, tq = imp_t.shape
    jrow = lax.broadcasted_iota(jnp.int32, (n_rows, tq), 0)
    cnt = jnp.zeros((n_rows, tq), F32)
    for jp in range(n_rows):
        row = imp_t[jp:jp + 1, :]
        ahead = jnp.where(row == imp_t, jnp.where(jrow > jp, 1.0, 0.0), jnp.where(row > imp_t, 1.0, 0.0))
        cnt = cnt + ahead
    return jnp.where(cnt < float(N_SELECT), 1.0, 0.0)


def _heads_on_lanes(ref, g):
    rows = [(g * Q_PER_KV + r) * HEAD_DIM for r in range(Q_PER_KV)]
    return (jnp.concatenate([ref[c:c + HEAD_DIM, :] for c in rows], axis=1) * (SCALE * LOG2E)).astype(BF)


def _per_head(x):
    return jnp.concatenate([x] * Q_PER_KV, axis=1)


def _nsa_prompt_kernel(q_ref, qr_ref, cmp_ref, kvs_ref, kvw_ref, sm_ref, ovt_ref, o_ref, sel_ref, *, t_len, tq, kc):
    t0 = pl.program_id(1) * tq
    hq = Q_PER_KV
    nq = hq * tq
    tcol = t0 + lax.broadcasted_iota(jnp.int32, (1, tq), 1)
    gates_t = jax.nn.sigmoid(sm_ref[...]).T
    tok = lax.broadcasted_iota(jnp.int32, (LANES, tq), 0)
    cmask = _per_head(jnp.where((tok * CMP_STRIDE + (2 * CMP_STRIDE - 1)) <= tcol, 1.0, 0.0)) > 0.5
    n_sel = t_len // SLC_BLOCK
    blk = lax.broadcasted_iota(jnp.int32, (n_sel, tq), 0)
    valid = (blk * SLC_BLOCK) <= tcol
    cur = tcol // SLC_BLOCK
    forced = (blk == 0) | (blk == cur) | (blk == cur - 1)
    n_win = WINDOW + tq
    start = pl.multiple_of(jnp.clip(t0 - WINDOW, 0, t_len - n_win), LANES)
    dist = tcol - (start + lax.broadcasted_iota(jnp.int32, (n_win, tq), 0))
    wbias = _per_head(jnp.where(dist >= 0, jnp.where(dist < WINDOW, 0.0, NEG_BIG), NEG_BIG))
    krel = lax.broadcasted_iota(jnp.int32, (kc, tq), 0)
    n_chunks = (t0 + tq - 1) // kc + 1
    blocks_per_chunk = kc // SLC_BLOCK

    krows = [slice(g * HEAD_DIM, (g + 1) * HEAD_DIM) for g in range(N_KV)]
    vrows = [slice(LANES + g * HEAD_DIM, LANES + (g + 1) * HEAD_DIM) for g in range(N_KV)]

    o_cmp = []
    for g in range(N_KV):
        s = jnp.where(cmask, _dot(cmp_ref[g, :, 0:HEAD_DIM].astype(BF), _heads_on_lanes(q_ref, g)), NEG_BIG)
        e = jnp.where(cmask, jnp.exp2(s - jnp.max(s, axis=0, keepdims=True)), 0.0)
        d = jnp.sum(e, axis=0, keepdims=True)
        p = e / jnp.where(d > 0.0, d, 1.0)
        o_cmp.append(_dot_tn(cmp_ref[g, :, HEAD_DIM:LANES].astype(BF), p.astype(BF)))
        psum = p[:, 0:tq]
        for r in range(1, hq):
            psum = psum + p[:, r * tq:(r + 1) * tq]
        imp = _dot3_left(ovt_ref[...], psum)[0:n_sel, :]
        imp = jnp.where(valid, jnp.where(forced, FORCE_SCORE, imp), -FORCE_SCORE)
        sel_ref[g] = _rank_select(imp)

    qs = [_heads_on_lanes(qr_ref, g) for g in range(N_KV)]

    def chunk(ci, state):
        k0 = pl.multiple_of(ci * kc, kc)
        causal = k0 + krel <= tcol
        j0 = pl.multiple_of(ci * blocks_per_chunk, blocks_per_chunk)
        scores = [_dot_tn(kvs_ref[krows[g], pl.ds(k0, kc)].astype(BF), qs[g]) for g in range(N_KV)]
        out = []
        for g in range(N_KV):
            m, l, acc = state[g]
            vt = kvs_ref[vrows[g], pl.ds(k0, kc)].astype(BF)
            sel_c = sel_ref[g, pl.ds(j0, blocks_per_chunk), :]
            sel_k = jnp.concatenate([jnp.broadcast_to(sel_c[i:i + 1, :], (SLC_BLOCK, tq))
                                     for i in range(blocks_per_chunk)], axis=0)
            bias = jnp.where(causal, jnp.where(sel_k > 0.5, 0.0, NEG_BIG), NEG_BIG)
            s = scores[g] + _per_head(bias)
            m_new = jnp.maximum(m, jnp.max(s, axis=0, keepdims=True))
            alpha = jnp.exp2(m - m_new)
            e = jnp.exp2(s - m_new)
            l = alpha * l + jnp.sum(e, axis=0, keepdims=True)
            out.append((m_new, l, alpha * acc + _dot(vt, e.astype(BF))))
        return tuple(out)

    init = (jnp.full((1, nq), NEG_BIG, F32), jnp.zeros((1, nq), F32), jnp.zeros((HEAD_DIM, nq), F32))
    slc = lax.fori_loop(0, n_chunks, chunk, (init,) * N_KV)

    wscores = [_dot_tn(kvw_ref[krows[g], pl.ds(start, n_win)].astype(BF), qs[g]) for g in range(N_KV)]
    for g in range(N_KV):
        _, l, acc = slc[g]
        o_slc = acc / l
        vw = kvw_ref[vrows[g], pl.ds(start, n_win)].astype(BF)
        s = wscores[g] + wbias
        e = jnp.exp2(s - jnp.max(s, axis=0, keepdims=True))
        o_win = _dot(vw, e.astype(BF)) / jnp.sum(e, axis=0, keepdims=True)
        for r in range(hq):
            h = g * hq + r
            cols = slice(r * tq, (r + 1) * tq)
            o_ref[h * HEAD_DIM:(h + 1) * HEAD_DIM, :] = (
                gates_t[3 * h:3 * h + 1, :] * o_cmp[g][:, cols] + gates_t[3 * h + 1:3 * h + 2, :] * o_slc[:, cols]
                + gates_t[3 * h + 2:3 * h + 3, :] * o_win[:, cols])


def _overlap_matrix(n_rows, n_cols, row_shift):
    n = jnp.arange(n_rows)[:, None] - row_shift
    c_start = n * CMP_STRIDE
    s_start = jnp.arange(n_cols)[None, :] * SLC_BLOCK
    ov = (c_start < s_start + SLC_BLOCK) & (c_start + 2 * CMP_STRIDE > s_start) & (n >= 0)
    return ov.astype(BF)


def _nsa_prompt(q_t, qr_t, cmp, kvs_t, kvw_t, small, b, t):
    tq = 256
    kc = 512
    nq = t // tq
    n_half = t // CMP_STRIDE
    n_sel = t // SLC_BLOCK
    assert n_half == LANES and n_sel <= LANES and t % kc == 0
    ov_t = _overlap_matrix(LANES, LANES, 0).T
    qtile = pl.BlockSpec((None, Q_DIM, tq), lambda i, j: (i, 0, j))
    seq = pl.BlockSpec((None, KV_DIM, t), lambda i, j: (i, 0, 0))
    return pl.pallas_call(
        functools.partial(_nsa_prompt_kernel, t_len=t, tq=tq, kc=kc),
        grid=(b, nq),
        in_specs=[qtile, qtile,
                  pl.BlockSpec((None, N_KV, n_half, LANES), lambda i, j: (i, 0, 0, 0)),
                  seq, seq, pl.BlockSpec((tq, LANES), lambda i, j: (i * nq + j, 0)),
                  pl.BlockSpec(ov_t.shape, lambda i, j: (0, 0))],
        out_specs=qtile,
        out_shape=jax.ShapeDtypeStruct((b, Q_DIM, t), F32),
        scratch_shapes=[pltpu.VMEM((N_KV, n_sel, tq), F32)],
        compiler_params=_cparams(("parallel", "parallel")),
        name="nsa_prompt",
    )(q_t, qr_t, cmp, kvs_t, kvw_t, small, ov_t)


def _ssd_prompt_kernel(zx_ref, sm_ref, cw_ref, cb_ref, dtb_ref, alog_ref, dskip_ref, nrm_ref, tri_ref,
                       y_ref, hout_ref, tail_ref, h_ref):
    c = pl.program_id(1)
    lc = SSD_CHUNK

    @pl.when(c == 0)
    def _():
        tail_ref[...] = jnp.zeros_like(tail_ref)
        h_ref[...] = jnp.zeros_like(h_ref)

    z = zx_ref[:, 0:SSD_INNER]
    xbc = zx_ref[:, SSD_INNER:]
    u = jnp.concatenate([tail_ref[...], xbc], axis=0)
    tail_ref[...] = xbc[lc - 8:lc, :]
    conv = cb_ref[...]
    for k in range(CONV_WIDTH):
        conv = conv + u[5 + k:5 + k + lc, :] * cw_ref[k:k + 1, :]
    xbc_c = _silu(conv)
    xs = xbc_c[:, 0:SSD_INNER]

    dt_full = _softplus(sm_ref[...] + dtb_ref[...])
    a_full = dt_full * (-jnp.exp(alog_ref[...]))
    acs = _dot3_left(tri_ref[...], a_full)
    acs_t = acs.T
    li = lax.broadcasted_iota(jnp.int32, (lc, lc), 0)
    si = lax.broadcasted_iota(jnp.int32, (lc, lc), 1)
    lower = li >= si

    heads_per_group = SSD_HEADS // 2
    bm_b = [xbc_c[:, SSD_INNER + g * SSD_STATE:SSD_INNER + (g + 1) * SSD_STATE].astype(BF) for g in range(2)]
    cm_b = [xbc_c[:, SSD_INNER + (2 + g) * SSD_STATE:SSD_INNER + (3 + g) * SSD_STATE].astype(BF) for g in range(2)]
    cb = [_dot_nt(cm_b[g], bm_b[g]) for g in range(2)]
    prep = []
    for h in range(SSD_HEADS):
        col = DT_COL + h
        acs_h = acs[:, col:col + 1]
        acs_last = acs[lc - 1:lc, col:col + 1]
        lmat = jnp.where(lower, jnp.exp(acs_h - acs_t[col:col + 1, :]), 0.0)
        xdt = xs[:, h * SSD_HEAD_DIM:(h + 1) * SSD_HEAD_DIM] * dt_full[:, col:col + 1]
        prep.append(((cb[h // heads_per_group] * lmat).astype(BF), xdt.astype(BF),
                     (xdt * jnp.exp(acs_last - acs_h)).astype(BF), jnp.exp(acs_h), jnp.exp(acs_last)))
    ys = []
    for h in range(SSD_HEADS):
        g = h // heads_per_group
        m_h, xdt_b, xdec_b, grow, glast = prep[h]
        h_prev = h_ref[h]
        ys.append(_dot(m_h, xdt_b) + _dot_nt(cm_b[g], h_prev.astype(BF)) * grow)
        h_ref[h] = glast * h_prev + _dot_tn(xdec_b, bm_b[g])
    y = jnp.concatenate(ys, axis=1) + dskip_ref[...] * xs
    y = y * _silu(z)
    y_ref[...] = _rms(y, nrm_ref[...])
    hout_ref[...] = h_ref[...]


def _ssd_params(conv_w, conv_b, dt_bias, a_log, d_skip, ssd_norm):
    pad = lambda v: jnp.pad(v, (DT_COL, LANES - DT_COL - SSD_HEADS)).reshape(1, LANES)
    return (conv_w, conv_b.reshape(1, -1), pad(dt_bias), pad(a_log),
            jnp.repeat(d_skip, SSD_HEAD_DIM).reshape(1, SSD_INNER), ssd_norm.reshape(1, SSD_INNER))


def _ssd_prompt(zx, small, params, b, t):
    lc = SSD_CHUNK
    nc = t // lc
    cw, cb, dtb, alog, dskip, nrm = params
    tri = (jnp.arange(lc)[:, None] >= jnp.arange(lc)[None, :]).astype(BF)
    zx_dim = zx.shape[1]
    conv_dim = zx_dim - SSD_INNER
    full = lambda a: pl.BlockSpec(a.shape, lambda i, j: (0,) * a.ndim)
    tile = lambda w: pl.BlockSpec((lc, w), lambda i, j: (i * nc + j, 0))
    return pl.pallas_call(
        _ssd_prompt_kernel,
        grid=(b, nc),
        in_specs=[tile(zx_dim), tile(LANES), full(cw), full(cb), full(dtb), full(alog), full(dskip), full(nrm),
                  full(tri)],
        out_specs=[tile(SSD_INNER),
                   pl.BlockSpec((None, SSD_HEADS, SSD_HEAD_DIM, SSD_STATE), lambda i, j: (i, 0, 0, 0))],
        out_shape=[jax.ShapeDtypeStruct((b * t, SSD_INNER), F32),
                   jax.ShapeDtypeStruct((b, SSD_HEADS, SSD_HEAD_DIM, SSD_STATE), F32)],
        scratch_shapes=[pltpu.VMEM((8, conv_dim), F32), pltpu.VMEM((SSD_HEADS, SSD_HEAD_DIM, SSD_STATE), F32)],
        compiler_params=_cparams(("parallel", "arbitrary")),
        name="ssd_prompt",
    )(zx, small, cw, cb, dtb, alog, dskip, nrm, tri)


def _ffn_kernel(x_ref, *refs, final_norm, mixer_proj, a_feature_major):
    x = x_ref[...]
    if mixer_proj:
        a_ref, b_ref, wa_ref, wb_ref = refs[:4]
        refs = refs[4:]
        a = a_ref[...].astype(BF)
        x = x + (_dot_tn(a, wa_ref[...]) if a_feature_major else _dot(a, wa_ref[...]))
        x = x + _dot(b_ref[...].astype(BF), wb_ref[...])
    g_ref, wg_ref, wu_ref, wd_ref, gf_ref, o_ref = refs
    xn = _rms(x, g_ref[...]).astype(BF)
    h = _silu(_dot(xn, wg_ref[...])) * _dot(xn, wu_ref[...])
    y = x + _dot(h.astype(BF), wd_ref[...])
    if final_norm:
        y = _rms(y, gf_ref[...])
    o_ref[...] = y


def _ffn(x2d, g, wg, wu, wd, gf, tm, final_norm, in_map, out_map, out_2d_shape, mixer=None):
    m, d = x2d.shape[0] * x2d.shape[1] // wg.shape[0], wg.shape[0]
    n_steps = m // tm
    full = lambda arr: pl.BlockSpec(arr.shape, lambda i: (0, 0), pipeline_mode=pl.Buffered(1))
    args = [x2d]
    in_specs = [pl.BlockSpec((tm, d), in_map)]
    a_feature_major = mixer is not None and mixer[0].ndim == 3
    if mixer is not None:
        a, bb, wa, wb = mixer
        args += [a, bb, wa, wb]
        if a_feature_major:
            nt = a.shape[2] // tm
            a_spec = pl.BlockSpec((None, a.shape[1], tm), lambda i: (i // nt, 0, i % nt))
        else:
            a_spec = pl.BlockSpec((tm, a.shape[1]), in_map)
        in_specs += [a_spec, pl.BlockSpec((tm, bb.shape[1]), in_map), full(wa), full(wb)]
    args += [g, wg, wu, wd, gf]
    in_specs += [full(g), full(wg), full(wu), full(wd), full(gf)]
    return pl.pallas_call(
        functools.partial(_ffn_kernel, final_norm=final_norm, mixer_proj=mixer is not None,
                          a_feature_major=a_feature_major),
        grid=(n_steps,),
        in_specs=in_specs,
        out_specs=pl.BlockSpec((tm, d), out_map),
        out_shape=jax.ShapeDtypeStruct(out_2d_shape, F32),
        compiler_params=_cparams(("parallel",)),
        name="ffn_final" if final_norm else "ffn",
    )(*args)


def _lru_gates(xc, wa_ref, ba_ref, wx_ref, bx_ref, sp):
    n_heads = xc.shape[1] // LRU_BLOCK
    a_parts = []
    gx_parts = []
    for h in range(n_heads):
        sl = slice(h * LRU_BLOCK, (h + 1) * LRU_BLOCK)
        xh = xc[:, sl]
        xb = xh.astype(BF)
        r = jax.nn.sigmoid(_dot(xb, wa_ref[h]) + ba_ref[:, sl])
        i = jax.nn.sigmoid(_dot(xb, wx_ref[h]) + bx_ref[:, sl])
        log_a = -LRU_C * r * sp[:, sl]
        a = jnp.exp(log_a)
        a_parts.append(a)
        gx_parts.append(jnp.sqrt(1.0 - a * a) * (i * xh))
    return jnp.concatenate(a_parts, axis=1), jnp.concatenate(gx_parts, axis=1)


SUBLANES = 8


def _lru_prompt_kernel(x_ref, g_ref, win_ref, cw_ref, cb_ref, wa_ref, ba_ref, wx_ref, bx_ref, lam_ref, wo_ref,
                       o_ref, hout_ref, cout_ref, tail_ref, h_ref, a_s, gx_s, hs_s, u_s, *, rows, w):
    @pl.when(pl.program_id(1) == 0)
    def _():
        tail_ref[...] = jnp.zeros_like(tail_ref)
        h_ref[...] = jnp.zeros_like(h_ref)

    x = x_ref[...]
    xn = _rms(x, g_ref[...]).astype(BF)
    proj = _dot(xn, win_ref[...])
    gate_br = proj[:, 0:w]
    x_br = proj[:, w:2 * w]
    u_s[0:SUBLANES, :] = tail_ref[...]
    u_s[SUBLANES:SUBLANES + rows, :] = x_br
    tail_ref[...] = x_br[rows - SUBLANES:rows, :]
    cout_ref[...] = x_br[rows - (CONV_WIDTH - 1):rows, :]
    xc = cb_ref[...]
    for k in range(CONV_WIDTH):
        off = SUBLANES - (CONV_WIDTH - 1) + k
        xc = xc + u_s[off:off + rows, :] * cw_ref[k:k + 1, :]
    sp = _softplus(-lam_ref[...])
    a, gx = _lru_gates(xc, wa_ref, ba_ref, wx_ref, bx_ref, sp)
    a_s[...] = a
    gx_s[...] = gx
    sub = lax.broadcasted_iota(jnp.int32, (SUBLANES, w), 0)
    h = h_ref[...]
    for i in range(rows // SUBLANES):
        blk = slice(i * SUBLANES, (i + 1) * SUBLANES)
        ac = a_s[blk, :]
        bc = gx_s[blk, :]
        for dd in (1, 2, 4):
            a_sh = jnp.where(sub >= dd, pltpu.roll(ac, dd, 0), 1.0)
            b_sh = jnp.where(sub >= dd, pltpu.roll(bc, dd, 0), 0.0)
            bc = ac * b_sh + bc
            ac = ac * a_sh
        hb = ac * h + bc
        hs_s[blk, :] = hb
        h = jnp.broadcast_to(hb[SUBLANES - 1:SUBLANES, :], (SUBLANES, w))
    h_ref[...] = h
    hout_ref[...] = h[0:1, :]
    y = (jax.nn.gelu(gate_br) * hs_s[...]).astype(BF)
    o_ref[...] = x + _dot(y, wo_ref[...])


def _lru_prompt(x2d, g, w_in, cw, cb, wa, ba, wx, bx, lam, wo, nb, rows):
    m, d = x2d.shape
    w = cw.shape[1]
    nt = m // nb // rows
    full = lambda a: pl.BlockSpec(a.shape, lambda i, j: (0,) * a.ndim)
    tile = pl.BlockSpec((rows, d), lambda i, j: (i * nt + j, 0))
    args = (g, w_in, cw, cb, wa, ba, wx, bx, lam, wo)
    return pl.pallas_call(
        functools.partial(_lru_prompt_kernel, rows=rows, w=w),
        grid=(nb, nt),
        in_specs=[tile] + [full(a) for a in args],
        out_specs=[tile,
                   pl.BlockSpec((None, 1, w), lambda i, j: (i, 0, 0)),
                   pl.BlockSpec((None, CONV_WIDTH - 1, w), lambda i, j: (i, 0, 0))],
        out_shape=[jax.ShapeDtypeStruct((m, d), F32),
                   jax.ShapeDtypeStruct((nb, 1, w), F32),
                   jax.ShapeDtypeStruct((nb, CONV_WIDTH - 1, w), F32)],
        scratch_shapes=[pltpu.VMEM((SUBLANES, w), F32), pltpu.VMEM((SUBLANES, w), F32),
                        pltpu.VMEM((rows, w), F32), pltpu.VMEM((rows, w), F32), pltpu.VMEM((rows, w), F32),
                        pltpu.VMEM((SUBLANES + rows, w), F32)],
        compiler_params=_cparams(("parallel", "arbitrary")),
        name="lru_prompt",
    )(x2d, *args)


def _lru_params(norm_g, w_in_c, conv_w, conv_b, w_a, b_a, w_x, b_x, lam, w_out_c):
    r = lambda v: v.reshape(1, -1)
    return (r(norm_g), w_in_c.astype(BF), conv_w, r(conv_b), w_a.astype(BF), r(b_a), w_x.astype(BF), r(b_x),
            r(lam), w_out_c.astype(BF))


def _lru_decode_kernel(x_ref, g_ref, win_ref, cw_ref, cb_ref, wa_ref, ba_ref, wx_ref, bx_ref, lam_ref, wo_ref,
                       c0_ref, c1_ref, c2_ref, h0_ref, o_ref, hout_ref, xbr_ref, *, w):
    x = x_ref[...]
    xn = _rms(x, g_ref[...]).astype(BF)
    proj = _dot(xn, win_ref[...])
    gate_br = proj[:, 0:w]
    x_br = proj[:, w:2 * w]
    xbr_ref[...] = x_br
    xc = (cb_ref[...] + c0_ref[...] * cw_ref[0:1, :] + c1_ref[...] * cw_ref[1:2, :]
          + c2_ref[...] * cw_ref[2:3, :] + x_br * cw_ref[3:4, :])
    sp = _softplus(-lam_ref[...])
    a, gx = _lru_gates(xc, wa_ref, ba_ref, wx_ref, bx_ref, sp)
    h = a * h0_ref[...] + gx
    hout_ref[...] = h
    y = (jax.nn.gelu(gate_br) * h).astype(BF)
    o_ref[...] = x + _dot(y, wo_ref[...])


def _lru_decode(x, params, conv_state, h0):
    m, d = x.shape
    w = h0.shape[1]
    args = (x,) + tuple(params) + (conv_state[:, 0], conv_state[:, 1], conv_state[:, 2], h0)
    full = lambda a: pl.BlockSpec(a.shape, lambda i: (0,) * a.ndim)
    return pl.pallas_call(
        functools.partial(_lru_decode_kernel, w=w),
        grid=(1,),
        in_specs=[full(a) for a in args],
        out_specs=[pl.BlockSpec((m, d), lambda i: (0, 0)), pl.BlockSpec((m, w), lambda i: (0, 0)),
                   pl.BlockSpec((m, w), lambda i: (0, 0))],
        out_shape=[jax.ShapeDtypeStruct((m, d), F32), jax.ShapeDtypeStruct((m, w), F32),
                   jax.ShapeDtypeStruct((m, w), F32)],
        compiler_params=_cparams(("arbitrary",)),
        name="lru_decode",
    )(*args)


PAGES_PER_STEP = 32
HALVES_PER_PAGE = PAGE_SIZE // CMP_STRIDE
CMP_PACK = 2 * LANES // HEAD_DIM


def _compress_decode_kernel(pt_ref, *refs):
    del pt_ref
    np_ = PAGES_PER_STEP
    k_pages = refs[0:np_]
    v_pages = refs[np_:2 * np_]
    pek_ref, w1k_ref, w2k_ref, pev_ref, w1v_ref, w2v_ref, out_ref, carry_ref, rows_ref = refs[2 * np_:]
    rows = np_ * HALVES_PER_PAGE

    @pl.when(pl.program_id(1) == 0)
    def _():
        carry_ref[...] = jnp.zeros_like(carry_ref)

    rowi = lax.broadcasted_iota(jnp.int32, (rows, CMP_HIDDEN), 0)
    low = lax.broadcasted_iota(jnp.int32, (rows, LANES), 1) < HEAD_DIM

    def split_groups(a, b):
        return (jnp.where(low, a, pltpu.roll(b, HEAD_DIM, 1)), jnp.where(low, pltpu.roll(a, HEAD_DIM, 1), b))

    branches = ((k_pages, pek_ref, w1k_ref, w2k_ref), (v_pages, pev_ref, w1v_ref, w2v_ref))
    for c, (pages, pe_ref, w1_ref, w2_ref) in enumerate(branches):
        for k, p in enumerate(pages):
            rows_ref[c, k * PAGE_SIZE:(k + 1) * PAGE_SIZE, :] = p[...].T

        lead = None
        trail = None
        for u in range(CMP_STRIDE // CMP_PACK):
            xs = [rows_ref[c, pl.ds(u * CMP_PACK + j, rows, stride=CMP_STRIDE), :] for j in range(CMP_PACK)]
            g01 = split_groups(xs[0], xs[1])
            g23 = split_groups(xs[2], xs[3])
            x = jnp.concatenate([jnp.concatenate([g01[0], g23[0]], axis=1),
                                 jnp.concatenate([g01[1], g23[1]], axis=1)], axis=0)
            dl = _dot((x + pe_ref[0, u]).astype(BF), w1_ref[0, u])
            dt = _dot((x + pe_ref[1, u]).astype(BF), w1_ref[1, u])
            lead = dl if lead is None else lead + dl
            trail = dt if trail is None else trail + dt
        for g in range(N_KV):
            a0 = lead[g * rows:(g + 1) * rows]
            a1 = trail[g * rows:(g + 1) * rows]
            slot = c * N_KV + g
            prev = jnp.where(rowi == 0, carry_ref[slot, 7:8, :], pltpu.roll(a0, 1, 0))
            carry_ref[slot] = a0[rows - 8:rows, :]
            tok = _dot(_silu(prev + a1).astype(BF), w2_ref[...])
            out_ref[g, :, c * HEAD_DIM:(c + 1) * HEAD_DIM] = tok


def _compress_decode(cache_t, pt_flat, db, n_pages, pek, w1k, w2k, pev, w1v, w2v):
    np_ = PAGES_PER_STEP
    n_steps = n_pages // np_
    rows = np_ * HALVES_PER_PAGE
    n_half = n_pages * HALVES_PER_PAGE
    n_quads = CMP_STRIDE // CMP_PACK
    pack_pe = lambda pe: pe.reshape(2, n_quads, 1, CMP_PACK * HEAD_DIM)
    pack_w1 = lambda w1: w1.reshape(2, n_quads, CMP_PACK * HEAD_DIM, CMP_HIDDEN)
    pek, w1k, pev, w1v = pack_pe(pek), pack_w1(w1k), pack_pe(pev), pack_w1(w1v)

    def page_spec(k, rowblk):
        return pl.BlockSpec((None, LANES, PAGE_SIZE),
                            lambda i, j, pt: (pt[i * n_pages + j * np_ + k], rowblk, 0))

    full = lambda a: pl.BlockSpec(a.shape, lambda i, j, pt: (0,) * a.ndim)
    grid_spec = pltpu.PrefetchScalarGridSpec(
        num_scalar_prefetch=1,
        grid=(db, n_steps),
        in_specs=[page_spec(k, 0) for k in range(np_)] + [page_spec(k, 1) for k in range(np_)]
        + [full(pek), full(w1k), full(w2k), full(pev), full(w1v), full(w2v)],
        out_specs=pl.BlockSpec((None, N_KV, rows, LANES), lambda i, j, pt: (i, 0, j, 0)),
        scratch_shapes=[pltpu.VMEM((2 * N_KV, 8, CMP_HIDDEN), F32), pltpu.VMEM((2, np_ * PAGE_SIZE, LANES), F32)],
    )
    return pl.pallas_call(
        _compress_decode_kernel,
        grid_spec=grid_spec,
        out_shape=jax.ShapeDtypeStruct((db, N_KV, n_half, LANES), F32),
        compiler_params=_cparams(("parallel", "arbitrary")),
        name="compress_decode",
    )(pt_flat, *([cache_t] * (2 * np_)), pek, w1k, w2k, pev, w1v, w2v)


def _group_rows(top, a, b):
    return jnp.where(top, a, b)


def _nsa_decode_a_kernel(q_ref, qr_ref, cmp_ref, win_ref, new_ref, ov_ref, ocmp_ref, owin_ref, idx_ref, *,
                         pos, n_blocks):
    nh = N_HEADS
    top = lax.broadcasted_iota(jnp.int32, (nh, 1), 0) < Q_PER_KV
    q = q_ref[...].astype(BF)
    n_tok = cmp_ref.shape[1]
    lane = lax.broadcasted_iota(jnp.int32, (nh, n_tok), 1)
    cmask = (lane >= 1) & (((lane - 1) * CMP_STRIDE + (2 * CMP_STRIDE - 1)) <= pos)
    s = _group_rows(top, _dot_nt(q, cmp_ref[0, :, 0:HEAD_DIM].astype(BF)),
                    _dot_nt(q, cmp_ref[1, :, 0:HEAD_DIM].astype(BF))) * SCALE
    p = _msoftmax(s, cmask)
    pb = p.astype(BF)
    ocmp_ref[...] = _group_rows(top, _dot(pb, cmp_ref[0, :, HEAD_DIM:LANES].astype(BF)),
                                _dot(pb, cmp_ref[1, :, HEAD_DIM:LANES].astype(BF)))

    imp8 = _dot3(p, ov_ref[...])
    nj = ov_ref.shape[1]
    cur = pos // SLC_BLOCK
    j = lax.broadcasted_iota(jnp.int32, (1, nj), 1)
    valid = (j * SLC_BLOCK) <= pos
    forced = (j == 0) | (j == cur) | (j == cur - 1)
    ri = lax.broadcasted_iota(jnp.int32, (nj, nj), 0)
    ci = lax.broadcasted_iota(jnp.int32, (nj, nj), 1)
    k_lane = lax.broadcasted_iota(jnp.int32, (nj, LANES), 1).astype(F32)
    jvals = lax.broadcasted_iota(jnp.int32, (8, nj), 1).astype(F32).astype(BF)
    idx_rows = []
    for g in range(N_KV):
        imp = jnp.sum(imp8[g * Q_PER_KV:(g + 1) * Q_PER_KV, :], axis=0, keepdims=True)
        imp = jnp.where(valid & forced, FORCE_SCORE, imp)
        imp = jnp.where(valid, imp, -FORCE_SCORE)
        imp = jnp.where(j < n_blocks, imp, -3e38)
        impb = jnp.broadcast_to(imp, (nj, nj))
        col = jnp.sum(jnp.where(ri == ci, impb, 0.0), axis=1, keepdims=True)
        ahead = jnp.where(impb == col, jnp.where(ci < ri, 1.0, 0.0), jnp.where(impb > col, 1.0, 0.0))
        rank_col = jnp.sum(ahead, axis=1, keepdims=True)
        onehot = jnp.where(rank_col == k_lane, 1.0, 0.0).astype(BF)
        idx_rows.append(_dot(jvals, onehot)[0:1, :])
    idx = jnp.concatenate(idx_rows + [jnp.zeros((8 - N_KV, LANES), F32)], axis=0)
    idx_ref[...] = idx.astype(jnp.int32)

    qr = qr_ref[...]
    qrb = qr.astype(BF)
    n_win = win_ref.shape[1]
    s = _group_rows(top, _dot(qrb, win_ref[0:HEAD_DIM, :].astype(BF)),
                    _dot(qrb, win_ref[HEAD_DIM:LANES, :].astype(BF))) * SCALE
    wl = lax.broadcasted_iota(jnp.int32, (nh, n_win), 1)
    dist = n_win - wl
    wmask = (dist >= 0) & (dist < WINDOW)
    new = new_ref[...]
    knew = _group_rows(top, new[:, 0:HEAD_DIM], new[:, HEAD_DIM:LANES])
    vnew = _group_rows(top, new[:, LANES:LANES + HEAD_DIM], new[:, LANES + HEAD_DIM:KV_DIM])
    s_new = jnp.sum(qr * knew, axis=1, keepdims=True) * SCALE
    sm = jnp.where(wmask, s, NEG_BIG)
    m = jnp.maximum(jnp.max(sm, axis=1, keepdims=True), s_new)
    e = jnp.where(wmask, jnp.exp(sm - m), 0.0)
    e_new = jnp.exp(s_new - m)
    d = jnp.sum(e, axis=1, keepdims=True) + e_new
    eb = e.astype(BF)
    o = _group_rows(top, _dot_nt(eb, win_ref[LANES:LANES + HEAD_DIM, :].astype(BF)),
                    _dot_nt(eb, win_ref[LANES + HEAD_DIM:KV_DIM, :].astype(BF))) + e_new * vnew
    owin_ref[...] = o / d


def _nsa_decode_a(q3, qr3, cmp_s, win, kvw_new3, pos):
    db = q3.shape[0]
    n_tok = cmp_s.shape[2]
    n_blocks = -(-(pos + 1) // SLC_BLOCK)
    nj = -(-n_blocks // LANES) * LANES
    ov = _overlap_matrix(n_tok, nj, 1)
    per_b = lambda a: pl.BlockSpec((None,) + a.shape[1:], lambda i: (i,) + (0,) * (a.ndim - 1))
    head = pl.BlockSpec((None, N_HEADS, HEAD_DIM), lambda i: (i, 0, 0))
    return pl.pallas_call(
        functools.partial(_nsa_decode_a_kernel, pos=pos, n_blocks=n_blocks),
        grid=(db,),
        in_specs=[head, head, per_b(cmp_s), per_b(win), per_b(kvw_new3), pl.BlockSpec(ov.shape, lambda i: (0, 0))],
        out_specs=[head, head, pl.BlockSpec((None, 8, LANES), lambda i: (i, 0, 0))],
        out_shape=[jax.ShapeDtypeStruct((db, N_HEADS, HEAD_DIM), F32),
                   jax.ShapeDtypeStruct((db, N_HEADS, HEAD_DIM), F32),
                   jax.ShapeDtypeStruct((db, 8, LANES), jnp.int32)],
        compiler_params=_cparams(("parallel",)),
        name="nsa_decode_select",
    )(q3, qr3, cmp_s, win, kvw_new3, ov)


def _nsa_decode_b_kernel(idx_ref, pt_ref, qr_ref, *refs, n_past_blocks, cur):
    del pt_ref
    nb = N_KV * N_SELECT
    blk_refs = refs[0:nb]
    new_ref, ocmp_ref, owin_ref, sm_ref, o_ref = refs[nb:]
    b = pl.program_id(0)
    nh = N_HEADS
    top = lax.broadcasted_iota(jnp.int32, (nh, 1), 0) < Q_PER_KV
    qr = qr_ref[...]
    qrb = qr.astype(BF)
    per_page = PAGE_SIZE // SLC_BLOCK
    n_keys = N_SELECT * PAGE_SIZE
    lane = lax.broadcasted_iota(jnp.int32, (nh, n_keys), 1)
    lane_page = lane // PAGE_SIZE
    lane_sub = (lane // SLC_BLOCK) % per_page
    s_g = []
    v_g = []
    ok_g = []
    has_new = []
    for g in range(N_KV):
        blks = blk_refs[g * N_SELECT:(g + 1) * N_SELECT]
        kt = jnp.concatenate([r[g * HEAD_DIM:(g + 1) * HEAD_DIM, :] for r in blks], axis=1).astype(BF)
        v_g.append(jnp.concatenate([r[LANES + g * HEAD_DIM:LANES + (g + 1) * HEAD_DIM, :] for r in blks],
                                   axis=1).astype(BF))
        s_g.append(_dot(qrb, kt))
        ok = jnp.zeros((nh, n_keys), F32)
        new_sel = jnp.zeros((), F32)
        for k in range(N_SELECT):
            jk = idx_ref[(b * N_KV + g) * N_SELECT + k]
            hit = (lane_page == k) & (lane_sub == jk % per_page)
            ok = jnp.where(hit, jnp.where(jk < n_past_blocks, 1.0, 0.0), ok)
            new_sel = jnp.maximum(new_sel, jnp.where(jk == cur, 1.0, 0.0))
        ok_g.append(ok)
        has_new.append(new_sel)
    s = _group_rows(top, s_g[0], s_g[1]) * SCALE
    mask = _group_rows(top, ok_g[0], ok_g[1]) > 0.5
    new_on = _group_rows(top, has_new[0], has_new[1]) > 0.5
    new = new_ref[...]
    knew = _group_rows(top, new[:, 0:HEAD_DIM], new[:, HEAD_DIM:LANES])
    vnew = _group_rows(top, new[:, LANES:LANES + HEAD_DIM], new[:, LANES + HEAD_DIM:KV_DIM])
    s_new = jnp.where(new_on, jnp.sum(qr * knew, axis=1, keepdims=True) * SCALE, NEG_BIG)
    sm = jnp.where(mask, s, NEG_BIG)
    m = jnp.maximum(jnp.max(sm, axis=1, keepdims=True), s_new)
    e = jnp.where(mask, jnp.exp(sm - m), 0.0)
    e_new = jnp.where(new_on, jnp.exp(s_new - m), 0.0)
    d = jnp.sum(e, axis=1, keepdims=True) + e_new
    eb = e.astype(BF)
    o_slc = ((_group_rows(top, _dot_nt(eb, v_g[0]), _dot_nt(eb, v_g[1])) + e_new * vnew)
             / jnp.where(d > 0.0, d, 1.0))

    sig = jnp.broadcast_to(jax.nn.sigmoid(sm_ref[...]), (nh, LANES))
    hl = lax.broadcasted_iota(jnp.int32, (nh, LANES), 1)
    hr = lax.broadcasted_iota(jnp.int32, (nh, LANES), 0)

    def gate(br):
        return jnp.sum(jnp.where(hl == 3 * hr + br, sig, 0.0), axis=1, keepdims=True)

    o_ref[...] = gate(0) * ocmp_ref[...] + gate(1) * o_slc + gate(2) * owin_ref[...]


def _nsa_decode_b(idx_flat, pt_flat, qr3, slc_pages_t, kvs_new3, o_cmp, o_win, small3, pos, n_pages):
    db = qr3.shape[0]
    n_past_blocks = pos // SLC_BLOCK
    cur = pos // SLC_BLOCK
    per_page = PAGE_SIZE // SLC_BLOCK

    def blk_spec(g, k):
        def imap(i, idx, pt):
            jk = jnp.minimum(idx[(i * N_KV + g) * N_SELECT + k], n_past_blocks - 1)
            return (pt[i * n_pages + jk // per_page], 0, 0)
        return pl.BlockSpec((None, KV_DIM, PAGE_SIZE), imap)

    head = pl.BlockSpec((None, N_HEADS, HEAD_DIM), lambda i, idx, pt: (i, 0, 0))
    row3 = lambda a: pl.BlockSpec((None,) + a.shape[1:], lambda i, idx, pt: (i, 0, 0))
    grid_spec = pltpu.PrefetchScalarGridSpec(
        num_scalar_prefetch=2,
        grid=(db,),
        in_specs=[head] + [blk_spec(g, k) for g in range(N_KV) for k in range(N_SELECT)]
        + [row3(kvs_new3), head, head, row3(small3)],
        out_specs=head,
    )
    return pl.pallas_call(
        functools.partial(_nsa_decode_b_kernel, n_past_blocks=n_past_blocks, cur=cur),
        grid_spec=grid_spec,
        out_shape=jax.ShapeDtypeStruct((db, N_HEADS, HEAD_DIM), F32),
        compiler_params=_cparams(("arbitrary",)),
        name="nsa_decode_attend",
    )(idx_flat, pt_flat, qr3, *([slc_pages_t] * (N_KV * N_SELECT)), kvs_new3, o_cmp, o_win, small3)


def _ssd_decode_kernel(zx_ref, cst_ref, sm_ref, h0_ref, cw_ref, cb_ref, dtb_ref, alog_ref, dskip_ref, nrm_ref,
                       y_ref, hout_ref):
    z = zx_ref[:, 0:SSD_INNER]
    xbc = zx_ref[:, SSD_INNER:]
    conv = cb_ref[...] + xbc * cw_ref[CONV_WIDTH - 1:CONV_WIDTH, :]
    for k in range(CONV_WIDTH - 1):
        conv = conv + cst_ref[k:k + 1, :] * cw_ref[k:k + 1, :]
    xbc_c = _silu(conv)
    xs = xbc_c[:, 0:SSD_INNER]
    dt_full = _softplus(sm_ref[...] + dtb_ref[...])
    da_full = jnp.exp(dt_full * (-jnp.exp(alog_ref[...])))
    p = SSD_HEAD_DIM
    eye = lax.broadcasted_iota(jnp.int32, (p, p), 0) == lax.broadcasted_iota(jnp.int32, (p, p), 1)
    ys = []
    for h in range(SSD_HEADS):
        g = h // (SSD_HEADS // 2)
        col = DT_COL + h
        xdt = xs[:, h * p:(h + 1) * p] * dt_full[:, col:col + 1]
        xcol = jnp.sum(jnp.where(eye, jnp.broadcast_to(xdt, (p, p)), 0.0), axis=1, keepdims=True)
        bm = xbc_c[:, SSD_INNER + g * SSD_STATE:SSD_INNER + (g + 1) * SSD_STATE]
        cm = xbc_c[:, SSD_INNER + (2 + g) * SSD_STATE:SSD_INNER + (3 + g) * SSD_STATE]
        h_new = da_full[:, col:col + 1] * h0_ref[h] + xcol * bm
        hout_ref[h] = h_new
        ys.append(_dot_nt(jnp.broadcast_to(cm, (8, SSD_STATE)).astype(BF), h_new.astype(BF))[0:1, :])
    y = jnp.concatenate(ys, axis=1) + dskip_ref[...] * xs
    y = y * _silu(z)
    y_ref[...] = _rms(y, nrm_ref[...])


def _ssd_decode(zx3, conv_state, small3, h0, params):
    db = zx3.shape[0]
    cw, cb, dtb, alog, dskip, nrm = params
    per_b = lambda a: pl.BlockSpec((None,) + a.shape[1:], lambda i: (i,) + (0,) * (a.ndim - 1))
    full = lambda a: pl.BlockSpec(a.shape, lambda i: (0,) * a.ndim)
    return pl.pallas_call(
        _ssd_decode_kernel,
        grid=(db,),
        in_specs=[per_b(zx3), per_b(conv_state), per_b(small3), per_b(h0), full(cw), full(cb), full(dtb), full(alog),
                  full(dskip), full(nrm)],
        out_specs=[pl.BlockSpec((None, 1, SSD_INNER), lambda i: (i, 0, 0)), per_b(h0)],
        out_shape=[jax.ShapeDtypeStruct((db, 1, SSD_INNER), F32), jax.ShapeDtypeStruct(h0.shape, F32)],
        compiler_params=_cparams(("parallel",)),
        name="ssd_decode",
    )(zx3, conv_state, small3, h0, cw, cb, dtb, alog, dskip, nrm)


def kernel(x_prompt, x_sample, cache_kv_cmp, cache_kv_slc, cache_kv_win, state_ssm, state_ssd_conv, state_lru,
           state_lru_conv, page_table, norm_mix, norm_ffn, norm_final, w_ffn_gate, w_ffn_up, w_ffn_down, w_in_a,
           w_out_a, cmp_pe_k, cmp_w1_k, cmp_w2_k, cmp_pe_v, cmp_w1_v, cmp_w2_v, ssd_conv_w, ssd_conv_b, ssd_dt_bias,
           ssd_a_log, ssd_d, ssd_norm, w_in_c, lru_conv_w, lru_conv_b, lru_w_a, lru_b_a, lru_w_x, lru_b_x,
           lru_lambda, w_out_c):
    b, t, d = x_prompt.shape
    db = x_sample.shape[0]
    n_pages = page_table.shape[1]
    pos_s = n_pages * PAGE_SIZE
    m = b * t
    kv_shape = (2, N_KV, HEAD_DIM)

    wm, ws = _prep_w_in_a(w_in_a[0])
    wo_nsa = w_out_a[0, :Q_DIM].astype(BF)
    wo_ssd = w_out_a[0, Q_DIM:].astype(BF)
    cmp_k = _prep_cmp_w(cmp_pe_k[0], cmp_w1_k[0], cmp_w2_k[0])
    cmp_v = _prep_cmp_w(cmp_pe_v[0], cmp_w1_v[0], cmp_w2_v[0])
    ssd_par = _ssd_params(ssd_conv_w[0], ssd_conv_b[0], ssd_dt_bias[0], ssd_a_log[0], ssd_d[0], ssd_norm[0])
    lru_par = _lru_params(norm_mix[1], w_in_c[0], lru_conv_w[0], lru_conv_b[0], lru_w_a[0], lru_b_a[0], lru_w_x[0],
                          lru_b_x[0], lru_lambda[0], w_out_c[0])
    ffn_w = [(norm_ffn[l].reshape(1, d), w_ffn_gate[l].astype(BF), w_ffn_up[l].astype(BF), w_ffn_down[l].astype(BF))
             for l in range(2)]
    gfin = norm_final.reshape(1, d)

    tm_proj = 512
    xp = x_prompt.reshape(m, d)
    cos_p, sin_p = _rope_tables(jnp.arange(t, dtype=jnp.int32))
    q, qr, kvc, kvs_t, kvw_t, zx, small, kvc_t = _inproj_a(xp, norm_mix[0], wm, ws, cos_p, sin_p, tm_proj, b, True)
    cmp_p = _compress_prompt(kvc, b, t, *cmp_k, *cmp_v)
    o_nsa = _nsa_prompt(q, qr, cmp_p, kvs_t, kvw_t, small, b, t)
    y_ssd, ssm_p = _ssd_prompt(zx, small, ssd_par, b, t)
    tm_ffn = 512
    rowmap = lambda i: (i, 0)
    x2 = _ffn(xp, *ffn_w[0], gfin, tm_ffn, False, rowmap, rowmap, (m, d), mixer=(o_nsa, y_ssd, wo_nsa, wo_ssd))
    x3, lru_p, lru_conv_p = _lru_prompt(x2, *lru_par, b, 512)
    y_prompt = _ffn(x3, *ffn_w[1], gfin, tm_ffn, True, rowmap, rowmap, (m, d))
    w_keep = min(WINDOW, t)
    to_cache = lambda a: a.reshape((b,) + kv_shape + (a.shape[-1],)).transpose(0, 4, 1, 2, 3)[None]
    kv_cmp_p = to_cache(kvc_t)
    kv_slc_p = to_cache(kvs_t)
    kv_win_p = to_cache(kvw_t[:, :, t - w_keep:])
    ssd_conv_p = zx.reshape(b, t, -1)[:, t - (CONV_WIDTH - 1):, SSD_INNER:][None]
    lru_p = lru_p.reshape(b, -1)
    lru_conv_p = lru_conv_p[None]

    xs = x_sample.reshape(db, d)
    cos_s, sin_s = _rope_tables(jnp.full((db,), pos_s, dtype=jnp.int32))
    q_s, qr_s, kvc_s, kvs_s, kvw_s, zx_s, small_s = _inproj_a(xs, norm_mix[0], wm, ws, cos_s, sin_s, db, 1, False)
    pt_flat = page_table.reshape(-1)
    n_phys = cache_kv_cmp.shape[1]
    feature_major = lambda a: a.transpose(0, 2, 3, 4, 1).reshape(a.shape[0], KV_DIM, a.shape[1])
    cmp_s = _compress_decode(feature_major(cache_kv_cmp[0]), pt_flat, db, n_pages, *cmp_k, *cmp_v)
    win_buf = cache_kv_win[0].reshape(db, -1, KV_DIM)
    head3 = lambda a: a.reshape(db, N_HEADS, HEAD_DIM)
    o_cmp_s, o_win_s, idx = _nsa_decode_a(head3(q_s), head3(qr_s), cmp_s, feature_major(cache_kv_win[0]),
                                          kvw_s.reshape(db, 1, KV_DIM), pos_s)
    idx_flat = idx[:, :N_KV, :N_SELECT].reshape(-1)
    o_nsa_s = _nsa_decode_b(idx_flat, pt_flat, head3(qr_s), feature_major(cache_kv_slc[0]),
                            kvs_s.reshape(db, 1, KV_DIM), o_cmp_s, o_win_s, small_s.reshape(db, 1, LANES), pos_s,
                            n_pages)
    y_ssd_s, ssm_s = _ssd_decode(zx_s.reshape(db, 1, -1), state_ssd_conv[0], small_s.reshape(db, 1, LANES),
                                 state_ssm[0], ssd_par)
    x2_s = _ffn(xs, *ffn_w[0], gfin, db, False, lambda i: (i, 0), lambda i: (i, 0), (db, d),
                mixer=(o_nsa_s.reshape(db, Q_DIM), y_ssd_s.reshape(db, SSD_INNER), wo_nsa, wo_ssd))
    x3_s, lru_s, xbr_s = _lru_decode(x2_s, lru_par, state_lru_conv[0], state_lru[0])
    y_sample = _ffn(x3_s, *ffn_w[1], gfin, db, True, lambda i: (i, 0), lambda i: (i, 0), (db, d))
    kv_cmp_s = kvc_s.reshape((1, db, 1) + kv_shape)
    kv_slc_s = kvs_s.reshape((1, db, 1) + kv_shape)
    kv_win_s = jnp.concatenate([win_buf[:, 1:], kvw_s[:, None, :]], axis=1).reshape(
        (1, db, win_buf.shape[1]) + kv_shape)
    ssd_conv_s = jnp.concatenate([state_ssd_conv[0][:, 1:], zx_s[:, None, SSD_INNER:]], axis=1)[None]
    lru_conv_s = jnp.concatenate([state_lru_conv[0][:, 1:], xbr_s[:, None, :]], axis=1)[None]

    return (y_prompt.reshape(b, t, d), y_sample.reshape(db, 1, d),
            kv_cmp_p, kv_slc_p, kv_win_p, ssm_p[None], ssd_conv_p, lru_p[None], lru_conv_p,
            kv_cmp_s, kv_slc_s, kv_win_s, ssm_s[None], ssd_conv_s, lru_s[None], lru_conv_s)
```

```python
import functools
import math

import jax
import jax.numpy as jnp
from jax import lax
from jax.experimental import pallas as pl
from jax.experimental.pallas import tpu as pltpu

BF = jnp.bfloat16
F32 = jnp.float32

HEAD_DIM = 64
N_KV = 2
Q_PER_KV = 4
N_HEADS = N_KV * Q_PER_KV
CMP_STRIDE = 16
CMP_HIDDEN = 256
SLC_BLOCK = 64
N_SELECT = 16
WINDOW = 512
PAGE_SIZE = 128
ROPE_THETA = 10000.0
FORCE_SCORE = 1e9
NEG_BIG = -1e30
SSD_HEADS = 8
SSD_HEAD_DIM = 64
SSD_STATE = 128
SSD_INNER = SSD_HEADS * SSD_HEAD_DIM
SSD_CHUNK = 128
CONV_WIDTH = 4
LRU_BLOCK = 128
LRU_C = 8.0
RMS_EPS = 1e-6
SCALE = HEAD_DIM ** -0.5
LOG2E = math.log2(math.e)
Q_DIM = N_HEADS * HEAD_DIM
KV_DIM = 2 * N_KV * HEAD_DIM
GATE_DIM = 3 * N_HEADS
DT_COL = GATE_DIM
LANES = 128
VMEM_LIMIT = 56 * 1024 * 1024


def _cparams(sem):
    return pltpu.CompilerParams(dimension_semantics=sem, vmem_limit_bytes=VMEM_LIMIT)


def _dot(a, b):
    return jnp.dot(a, b, preferred_element_type=F32)


def _dot_nt(a, b):
    return lax.dot_general(a, b, (((1,), (1,)), ((), ())), preferred_element_type=F32)


def _dot_tn(a, b):
    return lax.dot_general(a, b, (((0,), (0,)), ((), ())), preferred_element_type=F32)


def _split3(x):
    hi = x.astype(BF)
    r = x - hi.astype(F32)
    mid = r.astype(BF)
    lo = (r - mid.astype(F32)).astype(BF)
    return hi, mid, lo


def _dot3(x, m01):
    hi, mid, lo = _split3(x)
    return _dot(hi, m01) + _dot(mid, m01) + _dot(lo, m01)


def _dot3_left(m01, x):
    hi, mid, lo = _split3(x)
    return _dot(m01, hi) + _dot(m01, mid) + _dot(m01, lo)


def _rms(x, g):
    y = x * lax.rsqrt(jnp.mean(x * x, axis=-1, keepdims=True) + RMS_EPS)
    return y * g


def _silu(x):
    return x * jax.nn.sigmoid(x)


def _softplus(x):
    return jnp.maximum(x, 0.0) + jnp.log1p(jnp.exp(-jnp.abs(x)))


def _msoftmax_parts(s, mask):
    s = jnp.where(mask, s, NEG_BIG)
    m = jnp.max(s, axis=-1, keepdims=True)
    e = jnp.where(mask, jnp.exp(s - m), 0.0)
    return m, e


def _msoftmax(s, mask):
    _, e = _msoftmax_parts(s, mask)
    d = jnp.sum(e, axis=-1, keepdims=True)
    return e / jnp.where(d > 0.0, d, 1.0)


def _rope_tables(pos):
    half = HEAD_DIM // 2
    inv_freq = ROPE_THETA ** (-jnp.arange(half, dtype=F32) / half)
    ang = pos.astype(F32)[:, None] * inv_freq[None, :]
    cos = jnp.cos(ang)
    sin = jnp.sin(ang)
    cos2 = jnp.tile(jnp.concatenate([cos, cos], axis=-1), (1, LANES // HEAD_DIM))
    sin2 = jnp.tile(jnp.concatenate([-sin, sin], axis=-1), (1, LANES // HEAD_DIM))
    return cos2, sin2


def _inproj_a_kernel(x_ref, g_ref, wm_ref, ws_ref, cos_ref, sin_ref,
                     q_ref, qr_ref, kvc_ref, kvs_ref, kvw_ref, zx_ref, sm_ref, *kvc_t_ref, feature_major):
    xn = _rms(x_ref[...], g_ref[...]).astype(BF)
    cos = cos_ref[...]
    sin = sin_ref[...]
    lane = lax.broadcasted_iota(jnp.int32, cos.shape, 1)
    first = (lane % HEAD_DIM) < (HEAD_DIM // 2)

    def rope(v):
        rot = jnp.where(first, pltpu.roll(v, LANES - HEAD_DIM // 2, 1), pltpu.roll(v, HEAD_DIM // 2, 1))
        return v * cos + rot * sin

    def store_kv(ref, k, v):
        if feature_major:
            ref[0:LANES, :] = k.T
            ref[LANES:KV_DIM, :] = v.T
        else:
            ref[:, 0:LANES] = k
            ref[:, LANES:KV_DIM] = v

    q = _dot(xn, wm_ref[:, 0:Q_DIM])
    for c in range(Q_DIM // LANES):
        cols = slice(c * LANES, (c + 1) * LANES)
        if feature_major:
            q_ref[cols, :] = q[:, cols].T
            qr_ref[cols, :] = rope(q[:, cols]).T
        else:
            q_ref[:, cols] = q[:, cols]
            qr_ref[:, cols] = rope(q[:, cols])
    o = Q_DIM
    kvc = _dot(xn, wm_ref[:, o:o + KV_DIM])
    kvc_ref[...] = kvc
    if feature_major:
        store_kv(kvc_t_ref[0], kvc[:, 0:LANES], kvc[:, LANES:KV_DIM])
    o += KV_DIM
    kvs = _dot(xn, wm_ref[:, o:o + KV_DIM])
    store_kv(kvs_ref, rope(kvs[:, 0:LANES]), kvs[:, LANES:KV_DIM])
    o += KV_DIM
    kvw = _dot(xn, wm_ref[:, o:o + KV_DIM])
    store_kv(kvw_ref, rope(kvw[:, 0:LANES]), kvw[:, LANES:KV_DIM])
    o += KV_DIM
    zx_ref[...] = _dot(xn, wm_ref[:, o:])
    sm_ref[...] = _dot(xn, ws_ref[...])


def _prep_w_in_a(w_in_a):
    a = Q_DIM + 3 * KV_DIM
    gate = w_in_a[:, a:a + GATE_DIM]
    rest = w_in_a[:, a + GATE_DIM:]
    zx_w = rest[:, :rest.shape[1] - SSD_HEADS]
    dt = rest[:, rest.shape[1] - SSD_HEADS:]
    main = jnp.concatenate([w_in_a[:, :a], zx_w], axis=1).astype(BF)
    small = jnp.concatenate([gate, dt], axis=1)
    small = jnp.pad(small, ((0, 0), (0, LANES - small.shape[1]))).astype(BF)
    return main, small


def _inproj_a(x2d, g, wm, ws, cos2, sin2, tm, nb, feature_major):
    m, d = x2d.shape
    nt = m // nb // tm
    zx_dim = wm.shape[1] - Q_DIM - 3 * KV_DIM
    row = lambda w: pl.BlockSpec((tm, w), lambda i, j: (i * nt + j, 0))
    row_shape = lambda w: jax.ShapeDtypeStruct((m, w), F32)
    const = lambda i, j: (0, 0)
    if feature_major:
        fm = lambda w: pl.BlockSpec((None, w, tm), lambda i, j: (i, 0, j))
        fm_shape = lambda w: jax.ShapeDtypeStruct((nb, w, m // nb), F32)
    else:
        fm, fm_shape = row, row_shape
    kv, kv_shape = fm(KV_DIM), fm_shape(KV_DIM)
    out_specs = [fm(Q_DIM), fm(Q_DIM), row(KV_DIM), kv, kv, row(zx_dim), row(LANES)]
    out_shape = [fm_shape(Q_DIM), fm_shape(Q_DIM), row_shape(KV_DIM), kv_shape, kv_shape, row_shape(zx_dim),
                 row_shape(LANES)]
    if feature_major:
        out_specs.append(kv)
        out_shape.append(kv_shape)
    pos = pl.BlockSpec((tm, LANES), lambda i, j: (j, 0))
    return pl.pallas_call(
        functools.partial(_inproj_a_kernel, feature_major=feature_major),
        grid=(nb, nt),
        in_specs=[row(d), pl.BlockSpec((1, d), const), pl.BlockSpec(wm.shape, const), pl.BlockSpec(ws.shape, const),
                  pos, pos],
        out_specs=out_specs,
        out_shape=out_shape,
        compiler_params=_cparams(("parallel", "parallel")),
        name="inproj_a",
    )(x2d, g.reshape(1, d), wm, ws, cos2, sin2)


def _compress_accumulate(load_rows, pe_ref, w1_ref, col):
    a0 = None
    a1 = None
    for s in range(CMP_STRIDE):
        x = load_rows(s)[:, col:col + HEAD_DIM]
        l = _dot((x + pe_ref[s:s + 1, :]).astype(BF), w1_ref[s])
        t = _dot((x + pe_ref[CMP_STRIDE + s:CMP_STRIDE + s + 1, :]).astype(BF), w1_ref[CMP_STRIDE + s])
        a0 = l if a0 is None else a0 + l
        a1 = t if a1 is None else a1 + t
    return a0, a1


def _compress_prompt_kernel(kc_ref, vc_ref, pek_ref, w1k_ref, w2k_ref, pev_ref, w1v_ref, w2v_ref, out_ref, *,
                            n_half):
    branches = ((kc_ref, pek_ref, w1k_ref, w2k_ref), (vc_ref, pev_ref, w1v_ref, w2v_ref))
    for c, (src_ref, pe_ref, w1_ref, w2_ref) in enumerate(branches):
        def load_rows(s, src_ref=src_ref):
            return src_ref[pl.ds(s, n_half, stride=CMP_STRIDE), :]

        for g in range(N_KV):
            a0, a1 = _compress_accumulate(load_rows, pe_ref, w1_ref, g * HEAD_DIM)
            pre = a0 + pltpu.roll(a1, n_half - 1, 0)
            tok = _dot(_silu(pre).astype(BF), w2_ref[...])
            out_ref[g, :, c * HEAD_DIM:(c + 1) * HEAD_DIM] = tok


def _compress_prompt(kvc2d, b, t, pek, w1k, w2k, pev, w1v, w2v):
    n_half = t // CMP_STRIDE
    full = lambda a: pl.BlockSpec(a.shape, lambda i: (0,) * a.ndim)
    return pl.pallas_call(
        functools.partial(_compress_prompt_kernel, n_half=n_half),
        grid=(b,),
        in_specs=[pl.BlockSpec((t, LANES), lambda i: (i, 0)), pl.BlockSpec((t, LANES), lambda i: (i, 1)),
                  full(pek), full(w1k), full(w2k), full(pev), full(w1v), full(w2v)],
        out_specs=pl.BlockSpec((None, N_KV, n_half, LANES), lambda i: (i, 0, 0, 0)),
        out_shape=jax.ShapeDtypeStruct((b, N_KV, n_half, LANES), F32),
        compiler_params=_cparams(("parallel",)),
        name="compress_prompt",
    )(kvc2d, kvc2d, pek, w1k, w2k, pev, w1v, w2v)


def _prep_cmp_w(pe, w1, w2):
    return pe, w1.reshape(2 * CMP_STRIDE, HEAD_DIM, CMP_HIDDEN).astype(BF), w2.astype(BF)


def _rank_select(imp_t):
    n_rows, tq = imp_t.shape
    jrow = lax.broadcasted_iota(jnp.int32, (n_rows, tq), 0)
    cnt = jnp.zeros((n_rows, tq), F32)
    for jp in range(n_rows):
        row = imp_t[jp:jp + 1, :]
        ahead = jnp.where(row == imp_t, jnp.where(jrow > jp, 1.0, 0.0), jnp.where(row > imp_t, 1.0, 0.0))
        cnt = cnt + ahead
    return jnp.where(cnt < float(N_SELECT), 1.0, 0.0)


def _heads_on_lanes(ref, g):
    rows = [(g * Q_PER_KV + r) * HEAD_DIM for r in range(Q_PER_KV)]
    return (jnp.concatenate([ref[c:c + HEAD_DIM, :] for c in rows], axis=1) * (SCALE * LOG2E)).astype(BF)


def _per_head(x):
    return jnp.concatenate([x] * Q_PER_KV, axis=1)


def _nsa_prompt_kernel(q_ref, qr_ref, cmp_ref, kvs_ref, kvw_ref, sm_ref, ovt_ref, o_ref, sel_ref, *, t_len, tq, kc):
    t0 = pl.program_id(1) * tq
    hq = Q_PER_KV
    nq = hq * tq
    tcol = t0 + lax.broadcasted_iota(jnp.int32, (1, tq), 1)
    gates_t = jax.nn.sigmoid(sm_ref[...]).T
    tok = lax.broadcasted_iota(jnp.int32, (LANES, tq), 0)
    cmask = _per_head(jnp.where((tok * CMP_STRIDE + (2 * CMP_STRIDE - 1)) <= tcol, 1.0, 0.0)) > 0.5
    n_sel = t_len // SLC_BLOCK
    blk = lax.broadcasted_iota(jnp.int32, (n_sel, tq), 0)
    valid = (blk * SLC_BLOCK) <= tcol
    cur = tcol // SLC_BLOCK
    forced = (blk == 0) | (blk == cur) | (blk == cur - 1)
    krel = lax.broadcasted_iota(jnp.int32, (kc, tq), 0)
    n_chunks = (t0 + tq - 1) // kc + 1
    blocks_per_chunk = kc // SLC_BLOCK

    krows = [slice(g * HEAD_DIM, (g + 1) * HEAD_DIM) for g in range(N_KV)]
    vrows = [slice(LANES + g * HEAD_DIM, LANES + (g + 1) * HEAD_DIM) for g in range(N_KV)]

    o_cmp = []
    for g in range(N_KV):
        s = jnp.where(cmask, _dot(cmp_ref[g, :, 0:HEAD_DIM].astype(BF), _heads_on_lanes(q_ref, g)), NEG_BIG)
        e = jnp.where(cmask, jnp.exp2(s - jnp.max(s, axis=0, keepdims=True)), 0.0)
        d = jnp.sum(e, axis=0, keepdims=True)
        p = e / jnp.where(d > 0.0, d, 1.0)
        o_cmp.append(_dot_tn(cmp_ref[g, :, HEAD_DIM:LANES].astype(BF), p.astype(BF)))
        psum = p[:, 0:tq]
        for r in range(1, hq):
            psum = psum + p[:, r * tq:(r + 1) * tq]
        imp = _dot3_left(ovt_ref[...], psum)[0:n_sel, :]
        imp = jnp.where(valid, jnp.where(forced, FORCE_SCORE, imp), -FORCE_SCORE)
        sel_ref[g] = _rank_select(imp)

    qs = [_heads_on_lanes(qr_ref, g) for g in range(N_KV)]

    def attend(state, kv_ref, k0, n_keys, biases):
        scores = [_dot_tn(kv_ref[krows[g], pl.ds(k0, n_keys)].astype(BF), qs[g]) for g in range(N_KV)]
        out = []
        for g in range(N_KV):
            m, l, acc = state[g]
            vt = kv_ref[vrows[g], pl.ds(k0, n_keys)].astype(BF)
            s = scores[g] + _per_head(biases[g])
            m_new = jnp.maximum(m, jnp.max(s, axis=0, keepdims=True))
            alpha = jnp.exp2(m - m_new)
            e = jnp.exp2(s - m_new)
            l = alpha * l + jnp.sum(e, axis=0, keepdims=True)
            out.append((m_new, l, alpha * acc + _dot(vt, e.astype(BF))))
        return tuple(out)

    def chunk(ci, state):
        k0 = pl.multiple_of(ci * kc, kc)
        causal = k0 + krel <= tcol
        j0 = pl.multiple_of(ci * blocks_per_chunk, blocks_per_chunk)
        biases = []
        for g in range(N_KV):
            sel_c = sel_ref[g, pl.ds(j0, blocks_per_chunk), :]
            sel_k = jnp.concatenate([jnp.broadcast_to(sel_c[i:i + 1, :], (SLC_BLOCK, tq))
                                     for i in range(blocks_per_chunk)], axis=0)
            biases.append(jnp.where(causal, jnp.where(sel_k > 0.5, 0.0, NEG_BIG), NEG_BIG))
        return attend(state, kvs_ref, k0, kc, biases)

    init = (jnp.full((1, nq), NEG_BIG, F32), jnp.zeros((1, nq), F32), jnp.zeros((HEAD_DIM, nq), F32))
    slc = lax.fori_loop(0, n_chunks, chunk, (init,) * N_KV)

    tw = LANES
    n_win = WINDOW + tw
    o_win_parts = [[] for _ in range(N_KV)]
    for sub in range(tq // tw):
        start = pl.multiple_of(jnp.clip(t0 + sub * tw - WINDOW, 0, t_len - n_win), LANES)
        dist = tcol[:, sub * tw:(sub + 1) * tw] - (start + lax.broadcasted_iota(jnp.int32, (n_win, tw), 0))
        wbias = _per_head(jnp.where(dist >= 0, jnp.where(dist < WINDOW, 0.0, NEG_BIG), NEG_BIG))
        qsub = [jnp.concatenate([qs[g][:, r * tq + sub * tw:r * tq + (sub + 1) * tw] for r in range(hq)], axis=1)
                for g in range(N_KV)]
        wscores = [_dot_tn(kvw_ref[krows[g], pl.ds(start, n_win)].astype(BF), qsub[g]) for g in range(N_KV)]
        for g in range(N_KV):
            s = wscores[g] + wbias
            e = jnp.exp2(s - jnp.max(s, axis=0, keepdims=True))
            o_win_parts[g].append(_dot(kvw_ref[vrows[g], pl.ds(start, n_win)].astype(BF), e.astype(BF))
                                  / jnp.sum(e, axis=0, keepdims=True))
    for g in range(N_KV):
        o_slc = slc[g][2] / slc[g][1]
        o_win = jnp.concatenate([o_win_parts[g][sub][:, r * tw:(r + 1) * tw]
                                 for r in range(hq) for sub in range(tq // tw)], axis=1)
        for r in range(hq):
            h = g * hq + r
            cols = slice(r * tq, (r + 1) * tq)
            o_ref[h * HEAD_DIM:(h + 1) * HEAD_DIM, :] = (
                gates_t[3 * h:3 * h + 1, :] * o_cmp[g][:, cols] + gates_t[3 * h + 1:3 * h + 2, :] * o_slc[:, cols]
                + gates_t[3 * h + 2:3 * h + 3, :] * o_win[:, cols])


def _overlap_matrix(n_rows, n_cols, row_shift):
    n = jnp.arange(n_rows)[:, None] - row_shift
    c_start = n * CMP_STRIDE
    s_start = jnp.arange(n_cols)[None, :] * SLC_BLOCK
    ov = (c_start < s_start + SLC_BLOCK) & (c_start + 2 * CMP_STRIDE > s_start) & (n >= 0)
    return ov.astype(BF)


def _nsa_prompt(q_t, qr_t, cmp, kvs_t, kvw_t, small, b, t):
    tq = 256
    kc = 512
    nq = t // tq
    n_half = t // CMP_STRIDE
    n_sel = t // SLC_BLOCK
    assert n_half == LANES and n_sel <= LANES and t % kc == 0
    ov_t = _overlap_matrix(LANES, LANES, 0).T
    qtile = pl.BlockSpec((None, Q_DIM, tq), lambda i, j: (i, 0, j))
    seq = pl.BlockSpec((None, KV_DIM, t), lambda i, j: (i, 0, 0))
    return pl.pallas_call(
        functools.partial(_nsa_prompt_kernel, t_len=t, tq=tq, kc=kc),
        grid=(b, nq),
        in_specs=[qtile, qtile,
                  pl.BlockSpec((None, N_KV, n_half, LANES), lambda i, j: (i, 0, 0, 0)),
                  seq, seq, pl.BlockSpec((tq, LANES), lambda i, j: (i * nq + j, 0)),
                  pl.BlockSpec(ov_t.shape, lambda i, j: (0, 0))],
        out_specs=qtile,
        out_shape=jax.ShapeDtypeStruct((b, Q_DIM, t), F32),
        scratch_shapes=[pltpu.VMEM((N_KV, n_sel, tq), F32)],
        compiler_params=_cparams(("parallel", "parallel")),
        name="nsa_prompt",
    )(q_t, qr_t, cmp, kvs_t, kvw_t, small, ov_t)


def _ssd_prompt_kernel(zx_ref, sm_ref, cw_ref, cb_ref, dtb_ref, alog_ref, dskip_ref, nrm_ref, tri_ref,
                       y_ref, hout_ref, tail_ref, h_ref, *, n_sub):
    lc = SSD_CHUNK

    @pl.when(pl.program_id(1) == 0)
    def _():
        tail_ref[...] = jnp.zeros_like(tail_ref)
        h_ref[...] = jnp.zeros_like(h_ref)

    li = lax.broadcasted_iota(jnp.int32, (lc, lc), 0)
    si = lax.broadcasted_iota(jnp.int32, (lc, lc), 1)
    lower = li >= si
    heads_per_group = SSD_HEADS // 2
    tail = tail_ref[...]
    state = [h_ref[h] for h in range(SSD_HEADS)]
    for sc in range(n_sub):
        rows = slice(sc * lc, (sc + 1) * lc)
        z = zx_ref[rows, 0:SSD_INNER]
        xbc = zx_ref[rows, SSD_INNER:]
        u = jnp.concatenate([tail, xbc], axis=0)
        tail = xbc[lc - 8:lc, :]
        conv = cb_ref[...]
        for k in range(CONV_WIDTH):
            conv = conv + u[5 + k:5 + k + lc, :] * cw_ref[k:k + 1, :]
        xbc_c = _silu(conv)
        xs = xbc_c[:, 0:SSD_INNER]

        dt_full = _softplus(sm_ref[rows, :] + dtb_ref[...])
        a_full = dt_full * (-jnp.exp(alog_ref[...]))
        acs = _dot3_left(tri_ref[...], a_full)
        acs_t = acs.T

        bm_b = [xbc_c[:, SSD_INNER + g * SSD_STATE:SSD_INNER + (g + 1) * SSD_STATE].astype(BF) for g in range(2)]
        cm_b = [xbc_c[:, SSD_INNER + (2 + g) * SSD_STATE:SSD_INNER + (3 + g) * SSD_STATE].astype(BF)
                for g in range(2)]
        cb = [_dot_nt(cm_b[g], bm_b[g]) for g in range(2)]
        prep = []
        for h in range(SSD_HEADS):
            col = DT_COL + h
            acs_h = acs[:, col:col + 1]
            acs_last = acs[lc - 1:lc, col:col + 1]
            lmat = jnp.where(lower, jnp.exp(acs_h - acs_t[col:col + 1, :]), 0.0)
            xdt = xs[:, h * SSD_HEAD_DIM:(h + 1) * SSD_HEAD_DIM] * dt_full[:, col:col + 1]
            prep.append(((cb[h // heads_per_group] * lmat).astype(BF), xdt.astype(BF),
                         (xdt * jnp.exp(acs_last - acs_h)).astype(BF), jnp.exp(acs_h), jnp.exp(acs_last)))
        ys = []
        for h in range(SSD_HEADS):
            g = h // heads_per_group
            m_h, xdt_b, xdec_b, grow, glast = prep[h]
            ys.append(_dot(m_h, xdt_b) + _dot_nt(cm_b[g], state[h].astype(BF)) * grow)
            state[h] = glast * state[h] + _dot_tn(xdec_b, bm_b[g])
        y = jnp.concatenate(ys, axis=1) + dskip_ref[...] * xs
        y = y * _silu(z)
        y_ref[rows, :] = _rms(y, nrm_ref[...])
    tail_ref[...] = tail
    for h in range(SSD_HEADS):
        h_ref[h] = state[h]
        hout_ref[h] = state[h]


def _ssd_params(conv_w, conv_b, dt_bias, a_log, d_skip, ssd_norm):
    pad = lambda v: jnp.pad(v, (DT_COL, LANES - DT_COL - SSD_HEADS)).reshape(1, LANES)
    return (conv_w, conv_b.reshape(1, -1), pad(dt_bias), pad(a_log),
            jnp.repeat(d_skip, SSD_HEAD_DIM).reshape(1, SSD_INNER), ssd_norm.reshape(1, SSD_INNER))


def _ssd_prompt(zx, small, params, b, t):
    lc = SSD_CHUNK
    n_sub = 4
    nc = t // (lc * n_sub)
    cw, cb, dtb, alog, dskip, nrm = params
    tri = (jnp.arange(lc)[:, None] >= jnp.arange(lc)[None, :]).astype(BF)
    zx_dim = zx.shape[1]
    conv_dim = zx_dim - SSD_INNER
    full = lambda a: pl.BlockSpec(a.shape, lambda i, j: (0,) * a.ndim)
    tile = lambda w: pl.BlockSpec((lc * n_sub, w), lambda i, j: (i * nc + j, 0))
    return pl.pallas_call(
        functools.partial(_ssd_prompt_kernel, n_sub=n_sub),
        grid=(b, nc),
        in_specs=[tile(zx_dim), tile(LANES), full(cw), full(cb), full(dtb), full(alog), full(dskip), full(nrm),
                  full(tri)],
        out_specs=[tile(SSD_INNER),
                   pl.BlockSpec((None, SSD_HEADS, SSD_HEAD_DIM, SSD_STATE), lambda i, j: (i, 0, 0, 0))],
        out_shape=[jax.ShapeDtypeStruct((b * t, SSD_INNER), F32),
                   jax.ShapeDtypeStruct((b, SSD_HEADS, SSD_HEAD_DIM, SSD_STATE), F32)],
        scratch_shapes=[pltpu.VMEM((8, conv_dim), F32), pltpu.VMEM((SSD_HEADS, SSD_HEAD_DIM, SSD_STATE), F32)],
        compiler_params=_cparams(("parallel", "arbitrary")),
        name="ssd_prompt",
    )(zx, small, cw, cb, dtb, alog, dskip, nrm, tri)


def _ffn_kernel(x_ref, *refs, final_norm, mixer_proj, a_feature_major):
    x = x_ref[...]
    if mixer_proj:
        a_ref, b_ref, wa_ref, wb_ref = refs[:4]
        refs = refs[4:]
        a = a_ref[...].astype(BF)
        x = x + (_dot_tn(a, wa_ref[...]) if a_feature_major else _dot(a, wa_ref[...]))
        x = x + _dot(b_ref[...].astype(BF), wb_ref[...])
    g_ref, wg_ref, wu_ref, wd_ref, gf_ref, o_ref = refs
    xn = _rms(x, g_ref[...]).astype(BF)
    h = _silu(_dot(xn, wg_ref[...])) * _dot(xn, wu_ref[...])
    y = x + _dot(h.astype(BF), wd_ref[...])
    if final_norm:
        y = _rms(y, gf_ref[...])
    o_ref[...] = y


def _ffn(x2d, g, wg, wu, wd, gf, tm, final_norm, in_map, out_map, out_2d_shape, mixer=None):
    m, d = x2d.shape[0] * x2d.shape[1] // wg.shape[0], wg.shape[0]
    n_steps = m // tm
    full = lambda arr: pl.BlockSpec(arr.shape, lambda i: (0, 0), pipeline_mode=pl.Buffered(1))
    args = [x2d]
    in_specs = [pl.BlockSpec((tm, d), in_map)]
    a_feature_major = mixer is not None and mixer[0].ndim == 3
    if mixer is not None:
        a, bb, wa, wb = mixer
        args += [a, bb, wa, wb]
        if a_feature_major:
            nt = a.shape[2] // tm
            a_spec = pl.BlockSpec((None, a.shape[1], tm), lambda i: (i // nt, 0, i % nt))
        else:
            a_spec = pl.BlockSpec((tm, a.shape[1]), in_map)
        in_specs += [a_spec, pl.BlockSpec((tm, bb.shape[1]), in_map), full(wa), full(wb)]
    args += [g, wg, wu, wd, gf]
    in_specs += [full(g), full(wg), full(wu), full(wd), full(gf)]
    return pl.pallas_call(
        functools.partial(_ffn_kernel, final_norm=final_norm, mixer_proj=mixer is not None,
                          a_feature_major=a_feature_major),
        grid=(n_steps,),
        in_specs=in_specs,
        out_specs=pl.BlockSpec((tm, d), out_map),
        out_shape=jax.ShapeDtypeStruct(out_2d_shape, F32),
        compiler_params=_cparams(("parallel",)),
        name="ffn_final" if final_norm else "ffn",
    )(*args)


def _lru_gates(xc, wa_ref, ba_ref, wx_ref, bx_ref, sp):
    n_heads = xc.shape[1] // LRU_BLOCK
    a_parts = []
    gx_parts = []
    for h in range(n_heads):
        sl = slice(h * LRU_BLOCK, (h + 1) * LRU_BLOCK)
        xh = xc[:, sl]
        xb = xh.astype(BF)
        r = jax.nn.sigmoid(_dot(xb, wa_ref[h]) + ba_ref[:, sl])
        i = jax.nn.sigmoid(_dot(xb, wx_ref[h]) + bx_ref[:, sl])
        log_a = -LRU_C * r * sp[:, sl]
        a = jnp.exp(log_a)
        a_parts.append(a)
        gx_parts.append(jnp.sqrt(1.0 - a * a) * (i * xh))
    return jnp.concatenate(a_parts, axis=1), jnp.concatenate(gx_parts, axis=1)


SUBLANES = 8


def _lru_prompt_kernel(x_ref, g_ref, win_ref, cw_ref, cb_ref, wa_ref, ba_ref, wx_ref, bx_ref, lam_ref, wo_ref,
                       o_ref, hout_ref, cout_ref, tail_ref, h_ref, a_s, gx_s, hs_s, u_s, *, rows, w):
    @pl.when(pl.program_id(1) == 0)
    def _():
        tail_ref[...] = jnp.zeros_like(tail_ref)
        h_ref[...] = jnp.zeros_like(h_ref)

    x = x_ref[...]
    xn = _rms(x, g_ref[...]).astype(BF)
    proj = _dot(xn, win_ref[...])
    gate_br = proj[:, 0:w]
    x_br = proj[:, w:2 * w]
    u_s[0:SUBLANES, :] = tail_ref[...]
    u_s[SUBLANES:SUBLANES + rows, :] = x_br
    tail_ref[...] = x_br[rows - SUBLANES:rows, :]
    cout_ref[...] = x_br[rows - (CONV_WIDTH - 1):rows, :]
    xc = cb_ref[...]
    for k in range(CONV_WIDTH):
        off = SUBLANES - (CONV_WIDTH - 1) + k
        xc = xc + u_s[off:off + rows, :] * cw_ref[k:k + 1, :]
    sp = _softplus(-lam_ref[...])
    a, gx = _lru_gates(xc, wa_ref, ba_ref, wx_ref, bx_ref, sp)
    a_s[...] = a
    gx_s[...] = gx
    sub = lax.broadcasted_iota(jnp.int32, (SUBLANES, w), 0)
    h = h_ref[...]
    for i in range(rows // SUBLANES):
        blk = slice(i * SUBLANES, (i + 1) * SUBLANES)
        ac = a_s[blk, :]
        bc = gx_s[blk, :]
        for dd in (1, 2, 4):
            a_sh = jnp.where(sub >= dd, pltpu.roll(ac, dd, 0), 1.0)
            b_sh = jnp.where(sub >= dd, pltpu.roll(bc, dd, 0), 0.0)
            bc = ac * b_sh + bc
            ac = ac * a_sh
        hb = ac * h + bc
        hs_s[blk, :] = hb
        h = jnp.broadcast_to(hb[SUBLANES - 1:SUBLANES, :], (SUBLANES, w))
    h_ref[...] = h
    hout_ref[...] = h[0:1, :]
    y = (jax.nn.gelu(gate_br) * hs_s[...]).astype(BF)
    o_ref[...] = x + _dot(y, wo_ref[...])


def _lru_prompt(x2d, g, w_in, cw, cb, wa, ba, wx, bx, lam, wo, nb, rows):
    m, d = x2d.shape
    w = cw.shape[1]
    nt = m // nb // rows
    full = lambda a: pl.BlockSpec(a.shape, lambda i, j: (0,) * a.ndim)
    tile = pl.BlockSpec((rows, d), lambda i, j: (i * nt + j, 0))
    args = (g, w_in, cw, cb, wa, ba, wx, bx, lam, wo)
    return pl.pallas_call(
        functools.partial(_lru_prompt_kernel, rows=rows, w=w),
        grid=(nb, nt),
        in_specs=[tile] + [full(a) for a in args],
        out_specs=[tile,
                   pl.BlockSpec((None, 1, w), lambda i, j: (i, 0, 0)),
                   pl.BlockSpec((None, CONV_WIDTH - 1, w), lambda i, j: (i, 0, 0))],
        out_shape=[jax.ShapeDtypeStruct((m, d), F32),
                   jax.ShapeDtypeStruct((nb, 1, w), F32),
                   jax.ShapeDtypeStruct((nb, CONV_WIDTH - 1, w), F32)],
        scratch_shapes=[pltpu.VMEM((SUBLANES, w), F32), pltpu.VMEM((SUBLANES, w), F32),
                        pltpu.VMEM((rows, w), F32), pltpu.VMEM((rows, w), F32), pltpu.VMEM((rows, w), F32),
                        pltpu.VMEM((SUBLANES + rows, w), F32)],
        compiler_params=_cparams(("parallel", "arbitrary")),
        name="lru_prompt",
    )(x2d, *args)


def _lru_params(norm_g, w_in_c, conv_w, conv_b, w_a, b_a, w_x, b_x, lam, w_out_c):
    r = lambda v: v.reshape(1, -1)
    return (r(norm_g), w_in_c.astype(BF), conv_w, r(conv_b), w_a.astype(BF), r(b_a), w_x.astype(BF), r(b_x),
            r(lam), w_out_c.astype(BF))


def _lru_decode_kernel(x_ref, g_ref, win_ref, cw_ref, cb_ref, wa_ref, ba_ref, wx_ref, bx_ref, lam_ref, wo_ref,
                       c0_ref, c1_ref, c2_ref, h0_ref, o_ref, hout_ref, xbr_ref, *, w):
    x = x_ref[...]
    xn = _rms(x, g_ref[...]).astype(BF)
    proj = _dot(xn, win_ref[...])
    gate_br = proj[:, 0:w]
    x_br = proj[:, w:2 * w]
    xbr_ref[...] = x_br
    xc = (cb_ref[...] + c0_ref[...] * cw_ref[0:1, :] + c1_ref[...] * cw_ref[1:2, :]
          + c2_ref[...] * cw_ref[2:3, :] + x_br * cw_ref[3:4, :])
    sp = _softplus(-lam_ref[...])
    a, gx = _lru_gates(xc, wa_ref, ba_ref, wx_ref, bx_ref, sp)
    h = a * h0_ref[...] + gx
    hout_ref[...] = h
    y = (jax.nn.gelu(gate_br) * h).astype(BF)
    o_ref[...] = x + _dot(y, wo_ref[...])


def _lru_decode(x, params, conv_state, h0):
    m, d = x.shape
    w = h0.shape[1]
    args = (x,) + tuple(params) + (conv_state[:, 0], conv_state[:, 1], conv_state[:, 2], h0)
    full = lambda a: pl.BlockSpec(a.shape, lambda i: (0,) * a.ndim)
    return pl.pallas_call(
        functools.partial(_lru_decode_kernel, w=w),
        grid=(1,),
        in_specs=[full(a) for a in args],
        out_specs=[pl.BlockSpec((m, d), lambda i: (0, 0)), pl.BlockSpec((m, w), lambda i: (0, 0)),
                   pl.BlockSpec((m, w), lambda i: (0, 0))],
        out_shape=[jax.ShapeDtypeStruct((m, d), F32), jax.ShapeDtypeStruct((m, w), F32),
                   jax.ShapeDtypeStruct((m, w), F32)],
        compiler_params=_cparams(("arbitrary",)),
        name="lru_decode",
    )(*args)


PAGES_PER_STEP = 32
HALVES_PER_PAGE = PAGE_SIZE // CMP_STRIDE
CMP_PACK = 2 * LANES // HEAD_DIM


def _compress_decode_kernel(pt_ref, *refs):
    del pt_ref
    np_ = PAGES_PER_STEP
    k_pages = refs[0:np_]
    v_pages = refs[np_:2 * np_]
    pek_ref, w1k_ref, w2k_ref, pev_ref, w1v_ref, w2v_ref, out_ref, carry_ref, rows_ref = refs[2 * np_:]
    rows = np_ * HALVES_PER_PAGE

    @pl.when(pl.program_id(1) == 0)
    def _():
        carry_ref[...] = jnp.zeros_like(carry_ref)

    rowi = lax.broadcasted_iota(jnp.int32, (rows, CMP_HIDDEN), 0)
    low = lax.broadcasted_iota(jnp.int32, (rows, LANES), 1) < HEAD_DIM

    def split_groups(a, b):
        return (jnp.where(low, a, pltpu.roll(b, HEAD_DIM, 1)), jnp.where(low, pltpu.roll(a, HEAD_DIM, 1), b))

    branches = ((k_pages, pek_ref, w1k_ref, w2k_ref), (v_pages, pev_ref, w1v_ref, w2v_ref))
    for c, (pages, pe_ref, w1_ref, w2_ref) in enumerate(branches):
        for k, p in enumerate(pages):
            rows_ref[c, k * PAGE_SIZE:(k + 1) * PAGE_SIZE, :] = p[...].T

        lead = None
        trail = None
        for u in range(CMP_STRIDE // CMP_PACK):
            xs = [rows_ref[c, pl.ds(u * CMP_PACK + j, rows, stride=CMP_STRIDE), :] for j in range(CMP_PACK)]
            g01 = split_groups(xs[0], xs[1])
            g23 = split_groups(xs[2], xs[3])
            x = jnp.concatenate([jnp.concatenate([g01[0], g23[0]], axis=1),
                                 jnp.concatenate([g01[1], g23[1]], axis=1)], axis=0)
            dl = _dot((x + pe_ref[0, u]).astype(BF), w1_ref[0, u])
            dt = _dot((x + pe_ref[1, u]).astype(BF), w1_ref[1, u])
            lead = dl if lead is None else lead + dl
            trail = dt if trail is None else trail + dt
        for g in range(N_KV):
            a0 = lead[g * rows:(g + 1) * rows]
            a1 = trail[g * rows:(g + 1) * rows]
            slot = c * N_KV + g
            prev = jnp.where(rowi == 0, carry_ref[slot, 7:8, :], pltpu.roll(a0, 1, 0))
            carry_ref[slot] = a0[rows - 8:rows, :]
            tok = _dot(_silu(prev + a1).astype(BF), w2_ref[...])
            out_ref[g, :, c * HEAD_DIM:(c + 1) * HEAD_DIM] = tok


def _compress_decode(cache_t, pt_flat, db, n_pages, pek, w1k, w2k, pev, w1v, w2v):
    np_ = PAGES_PER_STEP
    n_steps = n_pages // np_
    rows = np_ * HALVES_PER_PAGE
    n_half = n_pages * HALVES_PER_PAGE
    n_quads = CMP_STRIDE // CMP_PACK
    pack_pe = lambda pe: pe.reshape(2, n_quads, 1, CMP_PACK * HEAD_DIM)
    pack_w1 = lambda w1: w1.reshape(2, n_quads, CMP_PACK * HEAD_DIM, CMP_HIDDEN)
    pek, w1k, pev, w1v = pack_pe(pek), pack_w1(w1k), pack_pe(pev), pack_w1(w1v)

    def page_spec(k, rowblk):
        return pl.BlockSpec((None, LANES, PAGE_SIZE),
                            lambda i, j, pt: (pt[i * n_pages + j * np_ + k], rowblk, 0))

    full = lambda a: pl.BlockSpec(a.shape, lambda i, j, pt: (0,) * a.ndim)
    grid_spec = pltpu.PrefetchScalarGridSpec(
        num_scalar_prefetch=1,
        grid=(db, n_steps),
        in_specs=[page_spec(k, 0) for k in range(np_)] + [page_spec(k, 1) for k in range(np_)]
        + [full(pek), full(w1k), full(w2k), full(pev), full(w1v), full(w2v)],
        out_specs=pl.BlockSpec((None, N_KV, rows, LANES), lambda i, j, pt: (i, 0, j, 0)),
        scratch_shapes=[pltpu.VMEM((2 * N_KV, 8, CMP_HIDDEN), F32), pltpu.VMEM((2, np_ * PAGE_SIZE, LANES), F32)],
    )
    return pl.pallas_call(
        _compress_decode_kernel,
        grid_spec=grid_spec,
        out_shape=jax.ShapeDtypeStruct((db, N_KV, n_half, LANES), F32),
        compiler_params=_cparams(("parallel", "arbitrary")),
        name="compress_decode",
    )(pt_flat, *([cache_t] * (2 * np_)), pek, w1k, w2k, pev, w1v, w2v)


def _group_rows(top, a, b):
    return jnp.where(top, a, b)


def _nsa_decode_a_kernel(q_ref, qr_ref, cmp_ref, win_ref, new_ref, ov_ref, ocmp_ref, owin_ref, idx_ref, *,
                         pos, n_blocks):
    nh = N_HEADS
    top = lax.broadcasted_iota(jnp.int32, (nh, 1), 0) < Q_PER_KV
    q = q_ref[...].astype(BF)
    n_tok = cmp_ref.shape[1]
    lane = lax.broadcasted_iota(jnp.int32, (nh, n_tok), 1)
    cmask = (lane >= 1) & (((lane - 1) * CMP_STRIDE + (2 * CMP_STRIDE - 1)) <= pos)
    s = _group_rows(top, _dot_nt(q, cmp_ref[0, :, 0:HEAD_DIM].astype(BF)),
                    _dot_nt(q, cmp_ref[1, :, 0:HEAD_DIM].astype(BF))) * SCALE
    p = _msoftmax(s, cmask)
    pb = p.astype(BF)
    ocmp_ref[...] = _group_rows(top, _dot(pb, cmp_ref[0, :, HEAD_DIM:LANES].astype(BF)),
                                _dot(pb, cmp_ref[1, :, HEAD_DIM:LANES].astype(BF)))

    imp8 = _dot3(p, ov_ref[...])
    nj = ov_ref.shape[1]
    cur = pos // SLC_BLOCK
    j = lax.broadcasted_iota(jnp.int32, (1, nj), 1)
    valid = (j * SLC_BLOCK) <= pos
    forced = (j == 0) | (j == cur) | (j == cur - 1)
    ri = lax.broadcasted_iota(jnp.int32, (nj, nj), 0)
    ci = lax.broadcasted_iota(jnp.int32, (nj, nj), 1)
    k_lane = lax.broadcasted_iota(jnp.int32, (nj, LANES), 1).astype(F32)
    jvals = lax.broadcasted_iota(jnp.int32, (8, nj), 1).astype(F32).astype(BF)
    idx_rows = []
    for g in range(N_KV):
        imp = jnp.sum(imp8[g * Q_PER_KV:(g + 1) * Q_PER_KV, :], axis=0, keepdims=True)
        imp = jnp.where(valid & forced, FORCE_SCORE, imp)
        imp = jnp.where(valid, imp, -FORCE_SCORE)
        imp = jnp.where(j < n_blocks, imp, -3e38)
        impb = jnp.broadcast_to(imp, (nj, nj))
        col = jnp.sum(jnp.where(ri == ci, impb, 0.0), axis=1, keepdims=True)
        ahead = jnp.where(impb == col, jnp.where(ci < ri, 1.0, 0.0), jnp.where(impb > col, 1.0, 0.0))
        rank_col = jnp.sum(ahead, axis=1, keepdims=True)
        onehot = jnp.where(rank_col == k_lane, 1.0, 0.0).astype(BF)
        idx_rows.append(_dot(jvals, onehot)[0:1, :])
    idx = jnp.concatenate(idx_rows + [jnp.zeros((8 - N_KV, LANES), F32)], axis=0)
    idx_ref[...] = idx.astype(jnp.int32)

    qr = qr_ref[...]
    qrb = qr.astype(BF)
    n_win = win_ref.shape[1]
    s = _group_rows(top, _dot(qrb, win_ref[0:HEAD_DIM, :].astype(BF)),
                    _dot(qrb, win_ref[HEAD_DIM:LANES, :].astype(BF))) * SCALE
    wl = lax.broadcasted_iota(jnp.int32, (nh, n_win), 1)
    dist = n_win - wl
    wmask = (dist >= 0) & (dist < WINDOW)
    new = new_ref[...]
    knew = _group_rows(top, new[:, 0:HEAD_DIM], new[:, HEAD_DIM:LANES])
    vnew = _group_rows(top, new[:, LANES:LANES + HEAD_DIM], new[:, LANES + HEAD_DIM:KV_DIM])
    s_new = jnp.sum(qr * knew, axis=1, keepdims=True) * SCALE
    sm = jnp.where(wmask, s, NEG_BIG)
    m = jnp.maximum(jnp.max(sm, axis=1, keepdims=True), s_new)
    e = jnp.where(wmask, jnp.exp(sm - m), 0.0)
    e_new = jnp.exp(s_new - m)
    d = jnp.sum(e, axis=1, keepdims=True) + e_new
    eb = e.astype(BF)
    o = _group_rows(top, _dot_nt(eb, win_ref[LANES:LANES + HEAD_DIM, :].astype(BF)),
                    _dot_nt(eb, win_ref[LANES + HEAD_DIM:KV_DIM, :].astype(BF))) + e_new * vnew
    owin_ref[...] = o / d


def _nsa_decode_a(q3, qr3, cmp_s, win, kvw_new3, pos):
    db = q3.shape[0]
    n_tok = cmp_s.shape[2]
    n_blocks = -(-(pos + 1) // SLC_BLOCK)
    nj = -(-n_blocks // LANES) * LANES
    ov = _overlap_matrix(n_tok, nj, 1)
    per_b = lambda a: pl.BlockSpec((None,) + a.shape[1:], lambda i: (i,) + (0,) * (a.ndim - 1))
    head = pl.BlockSpec((None, N_HEADS, HEAD_DIM), lambda i: (i, 0, 0))
    return pl.pallas_call(
        functools.partial(_nsa_decode_a_kernel, pos=pos, n_blocks=n_blocks),
        grid=(db,),
        in_specs=[head, head, per_b(cmp_s), per_b(win), per_b(kvw_new3), pl.BlockSpec(ov.shape, lambda i: (0, 0))],
        out_specs=[head, head, pl.BlockSpec((None, 8, LANES), lambda i: (i, 0, 0))],
        out_shape=[jax.ShapeDtypeStruct((db, N_HEADS, HEAD_DIM), F32),
                   jax.ShapeDtypeStruct((db, N_HEADS, HEAD_DIM), F32),
                   jax.ShapeDtypeStruct((db, 8, LANES), jnp.int32)],
        compiler_params=_cparams(("parallel",)),
        name="nsa_decode_select",
    )(q3, qr3, cmp_s, win, kvw_new3, ov)


def _nsa_decode_b_kernel(idx_ref, pt_ref, qr_ref, *refs, n_past_blocks, cur):
    del pt_ref
    nb = N_KV * N_SELECT
    blk_refs = refs[0:nb]
    new_ref, ocmp_ref, owin_ref, sm_ref, o_ref = refs[nb:]
    b = pl.program_id(0)
    nh = N_HEADS
    top = lax.broadcasted_iota(jnp.int32, (nh, 1), 0) < Q_PER_KV
    qr = qr_ref[...]
    qrb = qr.astype(BF)
    per_page = PAGE_SIZE // SLC_BLOCK
    n_keys = N_SELECT * PAGE_SIZE
    lane = lax.broadcasted_iota(jnp.int32, (nh, n_keys), 1)
    lane_page = lane // PAGE_SIZE
    lane_sub = (lane // SLC_BLOCK) % per_page
    s_g = []
    v_g = []
    ok_g = []
    has_new = []
    for g in range(N_KV):
        blks = blk_refs[g * N_SELECT:(g + 1) * N_SELECT]
        kt = jnp.concatenate([r[g * HEAD_DIM:(g + 1) * HEAD_DIM, :] for r in blks], axis=1).astype(BF)
        v_g.append(jnp.concatenate([r[LANES + g * HEAD_DIM:LANES + (g + 1) * HEAD_DIM, :] for r in blks],
                                   axis=1).astype(BF))
        s_g.append(_dot(qrb, kt))
        ok = jnp.zeros((nh, n_keys), F32)
        new_sel = jnp.zeros((), F32)
        for k in range(N_SELECT):
            jk = idx_ref[(b * N_KV + g) * N_SELECT + k]
            hit = (lane_page == k) & (lane_sub == jk % per_page)
            ok = jnp.where(hit, jnp.where(jk < n_past_blocks, 1.0, 0.0), ok)
            new_sel = jnp.maximum(new_sel, jnp.where(jk == cur, 1.0, 0.0))
        ok_g.append(ok)
        has_new.append(new_sel)
    s = _group_rows(top, s_g[0], s_g[1]) * SCALE
    mask = _group_rows(top, ok_g[0], ok_g[1]) > 0.5
    new_on = _group_rows(top, has_new[0], has_new[1]) > 0.5
    new = new_ref[...]
    knew = _group_rows(top, new[:, 0:HEAD_DIM], new[:, HEAD_DIM:LANES])
    vnew = _group_rows(top, new[:, LANES:LANES + HEAD_DIM], new[:, LANES + HEAD_DIM:KV_DIM])
    s_new = jnp.where(new_on, jnp.sum(qr * knew, axis=1, keepdims=True) * SCALE, NEG_BIG)
    sm = jnp.where(mask, s, NEG_BIG)
    m = jnp.maximum(jnp.max(sm, axis=1, keepdims=True), s_new)
    e = jnp.where(mask, jnp.exp(sm - m), 0.0)
    e_new = jnp.where(new_on, jnp.exp(s_new - m), 0.0)
    d = jnp.sum(e, axis=1, keepdims=True) + e_new
    eb = e.astype(BF)
    o_slc = ((_group_rows(top, _dot_nt(eb, v_g[0]), _dot_nt(eb, v_g[1])) + e_new * vnew)
             / jnp.where(d > 0.0, d, 1.0))

    sig = jnp.broadcast_to(jax.nn.sigmoid(sm_ref[...]), (nh, LANES))
    hl = lax.broadcasted_iota(jnp.int32, (nh, LANES), 1)
    hr = lax.broadcasted_iota(jnp.int32, (nh, LANES), 0)

    def gate(br):
        return jnp.sum(jnp.where(hl == 3 * hr + br, sig, 0.0), axis=1, keepdims=True)

    o_ref[...] = gate(0) * ocmp_ref[...] + gate(1) * o_slc + gate(2) * owin_ref[...]


def _nsa_decode_b(idx_flat, pt_flat, qr3, slc_pages_t, kvs_new3, o_cmp, o_win, small3, pos, n_pages):
    db = qr3.shape[0]
    n_past_blocks = pos // SLC_BLOCK
    cur = pos // SLC_BLOCK
    per_page = PAGE_SIZE // SLC_BLOCK

    def blk_spec(g, k):
        def imap(i, idx, pt):
            jk = jnp.minimum(idx[(i * N_KV + g) * N_SELECT + k], n_past_blocks - 1)
            return (pt[i * n_pages + jk // per_page], 0, 0)
        return pl.BlockSpec((None, KV_DIM, PAGE_SIZE), imap)

    head = pl.BlockSpec((None, N_HEADS, HEAD_DIM), lambda i, idx, pt: (i, 0, 0))
    row3 = lambda a: pl.BlockSpec((None,) + a.shape[1:], lambda i, idx, pt: (i, 0, 0))
    grid_spec = pltpu.PrefetchScalarGridSpec(
        num_scalar_prefetch=2,
        grid=(db,),
        in_specs=[head] + [blk_spec(g, k) for g in range(N_KV) for k in range(N_SELECT)]
        + [row3(kvs_new3), head, head, row3(small3)],
        out_specs=head,
    )
    return pl.pallas_call(
        functools.partial(_nsa_decode_b_kernel, n_past_blocks=n_past_blocks, cur=cur),
        grid_spec=grid_spec,
        out_shape=jax.ShapeDtypeStruct((db, N_HEADS, HEAD_DIM), F32),
        compiler_params=_cparams(("arbitrary",)),
        name="nsa_decode_attend",
    )(idx_flat, pt_flat, qr3, *([slc_pages_t] * (N_KV * N_SELECT)), kvs_new3, o_cmp, o_win, small3)


def _ssd_decode_kernel(zx_ref, cst_ref, sm_ref, h0_ref, cw_ref, cb_ref, dtb_ref, alog_ref, dskip_ref, nrm_ref,
                       y_ref, hout_ref):
    z = zx_ref[:, 0:SSD_INNER]
    xbc = zx_ref[:, SSD_INNER:]
    conv = cb_ref[...] + xbc * cw_ref[CONV_WIDTH - 1:CONV_WIDTH, :]
    for k in range(CONV_WIDTH - 1):
        conv = conv + cst_ref[k:k + 1, :] * cw_ref[k:k + 1, :]
    xbc_c = _silu(conv)
    xs = xbc_c[:, 0:SSD_INNER]
    dt_full = _softplus(sm_ref[...] + dtb_ref[...])
    da_full = jnp.exp(dt_full * (-jnp.exp(alog_ref[...])))
    p = SSD_HEAD_DIM
    eye = lax.broadcasted_iota(jnp.int32, (p, p), 0) == lax.broadcasted_iota(jnp.int32, (p, p), 1)
    ys = []
    for h in range(SSD_HEADS):
        g = h // (SSD_HEADS // 2)
        col = DT_COL + h
        xdt = xs[:, h * p:(h + 1) * p] * dt_full[:, col:col + 1]
        xcol = jnp.sum(jnp.where(eye, jnp.broadcast_to(xdt, (p, p)), 0.0), axis=1, keepdims=True)
        bm = xbc_c[:, SSD_INNER + g * SSD_STATE:SSD_INNER + (g + 1) * SSD_STATE]
        cm = xbc_c[:, SSD_INNER + (2 + g) * SSD_STATE:SSD_INNER + (3 + g) * SSD_STATE]
        h_new = da_full[:, col:col + 1] * h0_ref[h] + xcol * bm
        hout_ref[h] = h_new
        ys.append(_dot_nt(jnp.broadcast_to(cm, (8, SSD_STATE)).astype(BF), h_new.astype(BF))[0:1, :])
    y = jnp.concatenate(ys, axis=1) + dskip_ref[...] * xs
    y = y * _silu(z)
    y_ref[...] = _rms(y, nrm_ref[...])


def _ssd_decode(zx3, conv_state, small3, h0, params):
    db = zx3.shape[0]
    cw, cb, dtb, alog, dskip, nrm = params
    per_b = lambda a: pl.BlockSpec((None,) + a.shape[1:], lambda i: (i,) + (0,) * (a.ndim - 1))
    full = lambda a: pl.BlockSpec(a.shape, lambda i: (0,) * a.ndim)
    return pl.pallas_call(
        _ssd_decode_kernel,
        grid=(db,),
        in_specs=[per_b(zx3), per_b(conv_state), per_b(small3), per_b(h0), full(cw), full(cb), full(dtb), full(alog),
                  full(dskip), full(nrm)],
        out_specs=[pl.BlockSpec((None, 1, SSD_INNER), lambda i: (i, 0, 0)), per_b(h0)],
        out_shape=[jax.ShapeDtypeStruct((db, 1, SSD_INNER), F32), jax.ShapeDtypeStruct(h0.shape, F32)],
        compiler_params=_cparams(("parallel",)),
        name="ssd_decode",
    )(zx3, conv_state, small3, h0, cw, cb, dtb, alog, dskip, nrm)


def kernel(x_prompt, x_sample, cache_kv_cmp, cache_kv_slc, cache_kv_win, state_ssm, state_ssd_conv, state_lru,
           state_lru_conv, page_table, norm_mix, norm_ffn, norm_final, w_ffn_gate, w_ffn_up, w_ffn_down, w_in_a,
           w_out_a, cmp_pe_k, cmp_w1_k, cmp_w2_k, cmp_pe_v, cmp_w1_v, cmp_w2_v, ssd_conv_w, ssd_conv_b, ssd_dt_bias,
           ssd_a_log, ssd_d, ssd_norm, w_in_c, lru_conv_w, lru_conv_b, lru_w_a, lru_b_a, lru_w_x, lru_b_x,
           lru_lambda, w_out_c):
    b, t, d = x_prompt.shape
    db = x_sample.shape[0]
    n_pages = page_table.shape[1]
    pos_s = n_pages * PAGE_SIZE
    m = b * t
    kv_shape = (2, N_KV, HEAD_DIM)

    wm, ws = _prep_w_in_a(w_in_a[0])
    wo_nsa = w_out_a[0, :Q_DIM].astype(BF)
    wo_ssd = w_out_a[0, Q_DIM:].astype(BF)
    cmp_k = _prep_cmp_w(cmp_pe_k[0], cmp_w1_k[0], cmp_w2_k[0])
    cmp_v = _prep_cmp_w(cmp_pe_v[0], cmp_w1_v[0], cmp_w2_v[0])
    ssd_par = _ssd_params(ssd_conv_w[0], ssd_conv_b[0], ssd_dt_bias[0], ssd_a_log[0], ssd_d[0], ssd_norm[0])
    lru_par = _lru_params(norm_mix[1], w_in_c[0], lru_conv_w[0], lru_conv_b[0], lru_w_a[0], lru_b_a[0], lru_w_x[0],
                          lru_b_x[0], lru_lambda[0], w_out_c[0])
    ffn_w = [(norm_ffn[l].reshape(1, d), w_ffn_gate[l].astype(BF), w_ffn_up[l].astype(BF), w_ffn_down[l].astype(BF))
             for l in range(2)]
    gfin = norm_final.reshape(1, d)

    tm_proj = 512
    xp = x_prompt.reshape(m, d)
    cos_p, sin_p = _rope_tables(jnp.arange(t, dtype=jnp.int32))
    q, qr, kvc, kvs_t, kvw_t, zx, small, kvc_t = _inproj_a(xp, norm_mix[0], wm, ws, cos_p, sin_p, tm_proj, b, True)
    cmp_p = _compress_prompt(kvc, b, t, *cmp_k, *cmp_v)
    o_nsa = _nsa_prompt(q, qr, cmp_p, kvs_t, kvw_t, small, b, t)
    y_ssd, ssm_p = _ssd_prompt(zx, small, ssd_par, b, t)
    tm_ffn = 512
    rowmap = lambda i: (i, 0)
    x2 = _ffn(xp, *ffn_w[0], gfin, tm_ffn, False, rowmap, rowmap, (m, d), mixer=(o_nsa, y_ssd, wo_nsa, wo_ssd))
    x3, lru_p, lru_conv_p = _lru_prompt(x2, *lru_par, b, 512)
    y_prompt = _ffn(x3, *ffn_w[1], gfin, tm_ffn, True, rowmap, rowmap, (m, d))
    w_keep = min(WINDOW, t)
    to_cache = lambda a: a.reshape((b,) + kv_shape + (a.shape[-1],)).transpose(0, 4, 1, 2, 3)[None]
    kv_cmp_p = to_cache(kvc_t)
    kv_slc_p = to_cache(kvs_t)
    kv_win_p = to_cache(kvw_t[:, :, t - w_keep:])
    ssd_conv_p = zx.reshape(b, t, -1)[:, t - (CONV_WIDTH - 1):, SSD_INNER:][None]
    lru_p = lru_p.reshape(b, -1)
    lru_conv_p = lru_conv_p[None]

    xs = x_sample.reshape(db, d)
    cos_s, sin_s = _rope_tables(jnp.full((db,), pos_s, dtype=jnp.int32))
    q_s, qr_s, kvc_s, kvs_s, kvw_s, zx_s, small_s = _inproj_a(xs, norm_mix[0], wm, ws, cos_s, sin_s, db, 1, False)
    pt_flat = page_table.reshape(-1)
    n_phys = cache_kv_cmp.shape[1]
    feature_major = lambda a: a.transpose(0, 2, 3, 4, 1).reshape(a.shape[0], KV_DIM, a.shape[1])
    cmp_s = _compress_decode(feature_major(cache_kv_cmp[0]), pt_flat, db, n_pages, *cmp_k, *cmp_v)
    win_buf = cache_kv_win[0].reshape(db, -1, KV_DIM)
    head3 = lambda a: a.reshape(db, N_HEADS, HEAD_DIM)
    o_cmp_s, o_win_s, idx = _nsa_decode_a(head3(q_s), head3(qr_s), cmp_s, feature_major(cache_kv_win[0]),
                                          kvw_s.reshape(db, 1, KV_DIM), pos_s)
    idx_flat = idx[:, :N_KV, :N_SELECT].reshape(-1)
    o_nsa_s = _nsa_decode_b(idx_flat, pt_flat, head3(qr_s), feature_major(cache_kv_slc[0]),
                            kvs_s.reshape(db, 1, KV_DIM), o_cmp_s, o_win_s, small_s.reshape(db, 1, LANES), pos_s,
                            n_pages)
    y_ssd_s, ssm_s = _ssd_decode(zx_s.reshape(db, 1, -1), state_ssd_conv[0], small_s.reshape(db, 1, LANES),
                                 state_ssm[0], ssd_par)
    x2_s = _ffn(xs, *ffn_w[0], gfin, db, False, lambda i: (i, 0), lambda i: (i, 0), (db, d),
                mixer=(o_nsa_s.reshape(db, Q_DIM), y_ssd_s.reshape(db, SSD_INNER), wo_nsa, wo_ssd))
    x3_s, lru_s, xbr_s = _lru_decode(x2_s, lru_par, state_lru_conv[0], state_lru[0])
    y_sample = _ffn(x3_s, *ffn_w[1], gfin, db, True, lambda i: (i, 0), lambda i: (i, 0), (db, d))
    kv_cmp_s = kvc_s.reshape((1, db, 1) + kv_shape)
    kv_slc_s = kvs_s.reshape((1, db, 1) + kv_shape)
    kv_win_s = jnp.concatenate([win_buf[:, 1:], kvw_s[:, None, :]], axis=1).reshape(
        (1, db, win_buf.shape[1]) + kv_shape)
    ssd_conv_s = jnp.concatenate([state_ssd_conv[0][:, 1:], zx_s[:, None, SSD_INNER:]], axis=1)[None]
    lru_conv_s = jnp.concatenate([state_lru_conv[0][:, 1:], xbr_s[:, None, :]], axis=1)[None]

    return (y_prompt.reshape(b, t, d), y_sample.reshape(db, 1, d),
            kv_cmp_p, kv_slc_p, kv_win_p, ssm_p[None], ssd_conv_p, lru_p[None], lru_conv_p,
            kv_cmp_s, kv_slc_s, kv_win_s, ssm_s[None], ssd_conv_s, lru_s[None], lru_conv_s)
```

```python
import functools
import math

import jax
import jax.numpy as jnp
from jax import lax
from jax.experimental import pallas as pl
from jax.experimental.pallas import tpu as pltpu

BF = jnp.bfloat16
F32 = jnp.float32

HEAD_DIM = 64
N_KV = 2
Q_PER_KV = 4
N_HEADS = N_KV * Q_PER_KV
CMP_STRIDE = 16
CMP_HIDDEN = 256
SLC_BLOCK = 64
N_SELECT = 16
WINDOW = 512
PAGE_SIZE = 128
ROPE_THETA = 10000.0
FORCE_SCORE = 1e9
NEG_BIG = -1e30
SSD_HEADS = 8
SSD_HEAD_DIM = 64
SSD_STATE = 128
SSD_INNER = SSD_HEADS * SSD_HEAD_DIM
SSD_CHUNK = 128
CONV_WIDTH = 4
LRU_BLOCK = 128
LRU_C = 8.0
RMS_EPS = 1e-6
SCALE = HEAD_DIM ** -0.5
LOG2E = math.log2(math.e)
Q_DIM = N_HEADS * HEAD_DIM
KV_DIM = 2 * N_KV * HEAD_DIM
GATE_DIM = 3 * N_HEADS
DT_COL = GATE_DIM
LANES = 128
VMEM_LIMIT = 56 * 1024 * 1024


def _cparams(sem):
    return pltpu.CompilerParams(dimension_semantics=sem, vmem_limit_bytes=VMEM_LIMIT)


def _dot(a, b):
    return jnp.dot(a, b, preferred_element_type=F32)


def _dot_nt(a, b):
    return lax.dot_general(a, b, (((1,), (1,)), ((), ())), preferred_element_type=F32)


def _dot_tn(a, b):
    return lax.dot_general(a, b, (((0,), (0,)), ((), ())), preferred_element_type=F32)


def _split3(x):
    hi = x.astype(BF)
    r = x - hi.astype(F32)
    mid = r.astype(BF)
    lo = (r - mid.astype(F32)).astype(BF)
    return hi, mid, lo


def _dot3(x, m01):
    hi, mid, lo = _split3(x)
    return _dot(hi, m01) + _dot(mid, m01) + _dot(lo, m01)


def _dot3_left(m01, x):
    hi, mid, lo = _split3(x)
    return _dot(m01, hi) + _dot(m01, mid) + _dot(m01, lo)


def _rms(x, g):
    y = x * lax.rsqrt(jnp.mean(x * x, axis=-1, keepdims=True) + RMS_EPS)
    return y * g


def _silu(x):
    return x * jax.nn.sigmoid(x)


def _softplus(x):
    return jnp.maximum(x, 0.0) + jnp.log1p(jnp.exp(-jnp.abs(x)))


def _msoftmax_parts(s, mask):
    s = jnp.where(mask, s, NEG_BIG)
    m = jnp.max(s, axis=-1, keepdims=True)
    e = jnp.where(mask, jnp.exp(s - m), 0.0)
    return m, e


def _msoftmax(s, mask):
    _, e = _msoftmax_parts(s, mask)
    d = jnp.sum(e, axis=-1, keepdims=True)
    return e / jnp.where(d > 0.0, d, 1.0)


def _rope_tables(pos):
    half = HEAD_DIM // 2
    inv_freq = ROPE_THETA ** (-jnp.arange(half, dtype=F32) / half)
    ang = pos.astype(F32)[:, None] * inv_freq[None, :]
    cos = jnp.cos(ang)
    sin = jnp.sin(ang)
    cos2 = jnp.tile(jnp.concatenate([cos, cos], axis=-1), (1, LANES // HEAD_DIM))
    sin2 = jnp.tile(jnp.concatenate([-sin, sin], axis=-1), (1, LANES // HEAD_DIM))
    return cos2, sin2


def _inproj_a_kernel(x_ref, g_ref, wm_ref, ws_ref, cos_ref, sin_ref,
                     q_ref, qr_ref, kvc_ref, kvs_ref, kvw_ref, zx_ref, sm_ref, *kvc_t_ref, feature_major):
    xn = _rms(x_ref[...], g_ref[...]).astype(BF)
    cos = cos_ref[...]
    sin = sin_ref[...]
    lane = lax.broadcasted_iota(jnp.int32, cos.shape, 1)
    first = (lane % HEAD_DIM) < (HEAD_DIM // 2)

    def rope(v):
        rot = jnp.where(first, pltpu.roll(v, LANES - HEAD_DIM // 2, 1), pltpu.roll(v, HEAD_DIM // 2, 1))
        return v * cos + rot * sin

    def store_kv(ref, k, v):
        if feature_major:
            ref[0:LANES, :] = k.T
            ref[LANES:KV_DIM, :] = v.T
        else:
            ref[:, 0:LANES] = k
            ref[:, LANES:KV_DIM] = v

    q = _dot(xn, wm_ref[:, 0:Q_DIM])
    for c in range(Q_DIM // LANES):
        cols = slice(c * LANES, (c + 1) * LANES)
        if feature_major:
            q_ref[cols, :] = q[:, cols].T
            qr_ref[cols, :] = rope(q[:, cols]).T
        else:
            q_ref[:, cols] = q[:, cols]
            qr_ref[:, cols] = rope(q[:, cols])
    o = Q_DIM
    kvc = _dot(xn, wm_ref[:, o:o + KV_DIM])
    kvc_ref[...] = kvc
    if feature_major:
        store_kv(kvc_t_ref[0], kvc[:, 0:LANES], kvc[:, LANES:KV_DIM])
    o += KV_DIM
    kvs = _dot(xn, wm_ref[:, o:o + KV_DIM])
    store_kv(kvs_ref, rope(kvs[:, 0:LANES]), kvs[:, LANES:KV_DIM])
    o += KV_DIM
    kvw = _dot(xn, wm_ref[:, o:o + KV_DIM])
    store_kv(kvw_ref, rope(kvw[:, 0:LANES]), kvw[:, LANES:KV_DIM])
    o += KV_DIM
    zx_ref[...] = _dot(xn, wm_ref[:, o:])
    sm_ref[...] = _dot(xn, ws_ref[...])


def _prep_w_in_a(w_in_a):
    a = Q_DIM + 3 * KV_DIM
    gate = w_in_a[:, a:a + GATE_DIM]
    rest = w_in_a[:, a + GATE_DIM:]
    zx_w = rest[:, :rest.shape[1] - SSD_HEADS]
    dt = rest[:, rest.shape[1] - SSD_HEADS:]
    main = jnp.concatenate([w_in_a[:, :a], zx_w], axis=1).astype(BF)
    small = jnp.concatenate([gate, dt], axis=1)
    small = jnp.pad(small, ((0, 0), (0, LANES - small.shape[1]))).astype(BF)
    return main, small


def _inproj_a(x2d, g, wm, ws, cos2, sin2, tm, nb, feature_major):
    m, d = x2d.shape
    nt = m // nb // tm
    zx_dim = wm.shape[1] - Q_DIM - 3 * KV_DIM
    row = lambda w: pl.BlockSpec((tm, w), lambda i, j: (i * nt + j, 0))
    row_shape = lambda w: jax.ShapeDtypeStruct((m, w), F32)
    const = lambda i, j: (0, 0)
    if feature_major:
        fm = lambda w: pl.BlockSpec((None, w, tm), lambda i, j: (i, 0, j))
        fm_shape = lambda w: jax.ShapeDtypeStruct((nb, w, m // nb), F32)
    else:
        fm, fm_shape = row, row_shape
    kv, kv_shape = fm(KV_DIM), fm_shape(KV_DIM)
    out_specs = [fm(Q_DIM), fm(Q_DIM), row(KV_DIM), kv, kv, row(zx_dim), row(LANES)]
    out_shape = [fm_shape(Q_DIM), fm_shape(Q_DIM), row_shape(KV_DIM), kv_shape, kv_shape, row_shape(zx_dim),
                 row_shape(LANES)]
    if feature_major:
        out_specs.append(kv)
        out_shape.append(kv_shape)
    pos = pl.BlockSpec((tm, LANES), lambda i, j: (j, 0))
    return pl.pallas_call(
        functools.partial(_inproj_a_kernel, feature_major=feature_major),
        grid=(nb, nt),
        in_specs=[row(d), pl.BlockSpec((1, d), const), pl.BlockSpec(wm.shape, const), pl.BlockSpec(ws.shape, const),
                  pos, pos],
        out_specs=out_specs,
        out_shape=out_shape,
        compiler_params=_cparams(("parallel", "parallel")),
        name="inproj_a",
    )(x2d, g.reshape(1, d), wm, ws, cos2, sin2)


def _compress_accumulate(load_rows, pe_ref, w1_ref, col):
    a0 = None
    a1 = None
    for s in range(CMP_STRIDE):
        x = load_rows(s)[:, col:col + HEAD_DIM]
        l = _dot((x + pe_ref[s:s + 1, :]).astype(BF), w1_ref[s])
        t = _dot((x + pe_ref[CMP_STRIDE + s:CMP_STRIDE + s + 1, :]).astype(BF), w1_ref[CMP_STRIDE + s])
        a0 = l if a0 is None else a0 + l
        a1 = t if a1 is None else a1 + t
    return a0, a1


def _compress_prompt_kernel(kc_ref, vc_ref, pek_ref, w1k_ref, w2k_ref, pev_ref, w1v_ref, w2v_ref, out_ref, *,
                            n_half):
    branches = ((kc_ref, pek_ref, w1k_ref, w2k_ref), (vc_ref, pev_ref, w1v_ref, w2v_ref))
    for c, (src_ref, pe_ref, w1_ref, w2_ref) in enumerate(branches):
        def load_rows(s, src_ref=src_ref):
            return src_ref[pl.ds(s, n_half, stride=CMP_STRIDE), :]

        for g in range(N_KV):
            a0, a1 = _compress_accumulate(load_rows, pe_ref, w1_ref, g * HEAD_DIM)
            pre = a0 + pltpu.roll(a1, n_half - 1, 0)
            tok = _dot(_silu(pre).astype(BF), w2_ref[...])
            out_ref[g, :, c * HEAD_DIM:(c + 1) * HEAD_DIM] = tok


def _compress_prompt(kvc2d, b, t, pek, w1k, w2k, pev, w1v, w2v):
    n_half = t // CMP_STRIDE
    full = lambda a: pl.BlockSpec(a.shape, lambda i: (0,) * a.ndim)
    return pl.pallas_call(
        functools.partial(_compress_prompt_kernel, n_half=n_half),
        grid=(b,),
        in_specs=[pl.BlockSpec((t, LANES), lambda i: (i, 0)), pl.BlockSpec((t, LANES), lambda i: (i, 1)),
                  full(pek), full(w1k), full(w2k), full(pev), full(w1v), full(w2v)],
        out_specs=pl.BlockSpec((None, N_KV, n_half, LANES), lambda i: (i, 0, 0, 0)),
        out_shape=jax.ShapeDtypeStruct((b, N_KV, n_half, LANES), F32),
        compiler_params=_cparams(("parallel",)),
        name="compress_prompt",
    )(kvc2d, kvc2d, pek, w1k, w2k, pev, w1v, w2v)


def _prep_cmp_w(pe, w1, w2):
    return pe, w1.reshape(2 * CMP_STRIDE, HEAD_DIM, CMP_HIDDEN).astype(BF), w2.astype(BF)


def _rank_select(imp_t):
    n_rows, tq = imp_t.shape
    jrow = lax.broadcasted_iota(jnp.int32, (n_rows, tq), 0)
    cnt = jnp.zeros((n_rows, tq), F32)
    for jp in range(n_rows):
        row = imp_t[jp:jp + 1, :]
        ahead = jnp.where(row == imp_t, jnp.where(jrow > jp, 1.0, 0.0), jnp.where(row > imp_t, 1.0, 0.0))
        cnt = cnt + ahead
    return jnp.where(cnt < float(N_SELECT), 1.0, 0.0)


def _heads_on_lanes(ref, g):
    rows = [(g * Q_PER_KV + r) * HEAD_DIM for r in range(Q_PER_KV)]
    return (jnp.concatenate([ref[c:c + HEAD_DIM, :] for c in rows], axis=1) * (SCALE * LOG2E)).astype(BF)


def _per_head(x):
    return jnp.concatenate([x] * Q_PER_KV, axis=1)


def _nsa_prompt_kernel(q_ref, qr_ref, cmp_ref, kvs_ref, kvw_ref, sm_ref, ovt_ref, o_ref, sel_ref, slc_ref, *, t_len,
                       tq, kc):
    t0 = pl.program_id(1) * tq
    hq = Q_PER_KV
    nq = hq * tq
    tcol = t0 + lax.broadcasted_iota(jnp.int32, (1, tq), 1)
    gates_t = jax.nn.sigmoid(sm_ref[...]).T
    tok = lax.broadcasted_iota(jnp.int32, (LANES, tq), 0)
    cmask = _per_head(jnp.where((tok * CMP_STRIDE + (2 * CMP_STRIDE - 1)) <= tcol, 1.0, 0.0)) > 0.5
    n_sel = t_len // SLC_BLOCK
    blk = lax.broadcasted_iota(jnp.int32, (n_sel, tq), 0)
    valid = (blk * SLC_BLOCK) <= tcol
    cur = tcol // SLC_BLOCK
    forced = (blk == 0) | (blk == cur) | (blk == cur - 1)
    krel = lax.broadcasted_iota(jnp.int32, (kc, tq), 0)
    n_chunks = (t0 + tq - 1) // kc + 1
    blocks_per_chunk = kc // SLC_BLOCK

    krows = [slice(g * HEAD_DIM, (g + 1) * HEAD_DIM) for g in range(N_KV)]
    vrows = [slice(LANES + g * HEAD_DIM, LANES + (g + 1) * HEAD_DIM) for g in range(N_KV)]

    o_cmp = []
    for g in range(N_KV):
        s = jnp.where(cmask, _dot(cmp_ref[g, :, 0:HEAD_DIM].astype(BF), _heads_on_lanes(q_ref, g)), NEG_BIG)
        e = jnp.where(cmask, jnp.exp2(s - jnp.max(s, axis=0, keepdims=True)), 0.0)
        d = jnp.sum(e, axis=0, keepdims=True)
        p = e / jnp.where(d > 0.0, d, 1.0)
        o_cmp.append(_dot_tn(cmp_ref[g, :, HEAD_DIM:LANES].astype(BF), p.astype(BF)))
        psum = p[:, 0:tq]
        for r in range(1, hq):
            psum = psum + p[:, r * tq:(r + 1) * tq]
        imp = _dot3_left(ovt_ref[...], psum)[0:n_sel, :]
        imp = jnp.where(valid, jnp.where(forced, FORCE_SCORE, imp), -FORCE_SCORE)
        sel_ref[g] = _rank_select(imp)

    qs = [_heads_on_lanes(qr_ref, g) for g in range(N_KV)]

    def attend(state, kv_ref, k0, n_keys, biases):
        scores = [_dot_tn(kv_ref[krows[g], pl.ds(k0, n_keys)].astype(BF), qs[g]) for g in range(N_KV)]
        out = []
        for g in range(N_KV):
            m, l, acc = state[g]
            vt = kv_ref[vrows[g], pl.ds(k0, n_keys)].astype(BF)
            s = scores[g] + _per_head(biases[g])
            m_new = jnp.maximum(m, jnp.max(s, axis=0, keepdims=True))
            alpha = jnp.exp2(m - m_new)
            e = jnp.exp2(s - m_new)
            l = alpha * l + jnp.sum(e, axis=0, keepdims=True)
            out.append((m_new, l, alpha * acc + _dot(vt, e.astype(BF))))
        return tuple(out)

    def chunk(ci, state, diagonal):
        k0 = ci * kc
        biases = []
        for g in range(N_KV):
            sel_c = sel_ref[g, ci * blocks_per_chunk:(ci + 1) * blocks_per_chunk, :]
            sel_k = jnp.concatenate([jnp.broadcast_to(sel_c[i:i + 1, :], (SLC_BLOCK, tq))
                                     for i in range(blocks_per_chunk)], axis=0)
            bias = jnp.where(sel_k > 0.5, 0.0, NEG_BIG)
            biases.append(jnp.where(k0 + krel <= tcol, bias, NEG_BIG) if diagonal else bias)
        return attend(state, kvs_ref, k0, kc, biases)

    init = (jnp.full((1, nq), NEG_BIG, F32), jnp.zeros((1, nq), F32), jnp.zeros((HEAD_DIM, nq), F32))
    for count in range(1, t_len // kc + 1):
        @pl.when(n_chunks == count)
        def _(count=count):
            state = (init,) * N_KV
            for ci in range(count):
                state = chunk(ci, state, ci == count - 1)
            for g in range(N_KV):
                slc_ref[g] = state[g][2] / state[g][1]

    tw = LANES
    n_win = WINDOW + tw
    o_win_parts = [[] for _ in range(N_KV)]
    for sub in range(tq // tw):
        start = pl.multiple_of(jnp.clip(t0 + sub * tw - WINDOW, 0, t_len - n_win), LANES)
        dist = tcol[:, sub * tw:(sub + 1) * tw] - (start + lax.broadcasted_iota(jnp.int32, (n_win, tw), 0))
        wbias = _per_head(jnp.where(dist >= 0, jnp.where(dist < WINDOW, 0.0, NEG_BIG), NEG_BIG))
        qsub = [jnp.concatenate([qs[g][:, r * tq + sub * tw:r * tq + (sub + 1) * tw] for r in range(hq)], axis=1)
                for g in range(N_KV)]
        wscores = [_dot_tn(kvw_ref[krows[g], pl.ds(start, n_win)].astype(BF), qsub[g]) for g in range(N_KV)]
        for g in range(N_KV):
            s = wscores[g] + wbias
            e = jnp.exp2(s - jnp.max(s, axis=0, keepdims=True))
            o_win_parts[g].append(_dot(kvw_ref[vrows[g], pl.ds(start, n_win)].astype(BF), e.astype(BF))
                                  / jnp.sum(e, axis=0, keepdims=True))
    for g in range(N_KV):
        o_slc = slc_ref[g]
        o_win = jnp.concatenate([o_win_parts[g][sub][:, r * tw:(r + 1) * tw]
                                 for r in range(hq) for sub in range(tq // tw)], axis=1)
        for r in range(hq):
            h = g * hq + r
            cols = slice(r * tq, (r + 1) * tq)
            o_ref[h * HEAD_DIM:(h + 1) * HEAD_DIM, :] = (
                gates_t[3 * h:3 * h + 1, :] * o_cmp[g][:, cols] + gates_t[3 * h + 1:3 * h + 2, :] * o_slc[:, cols]
                + gates_t[3 * h + 2:3 * h + 3, :] * o_win[:, cols])


def _overlap_matrix(n_rows, n_cols, row_shift):
    n = jnp.arange(n_rows)[:, None] - row_shift
    c_start = n * CMP_STRIDE
    s_start = jnp.arange(n_cols)[None, :] * SLC_BLOCK
    ov = (c_start < s_start + SLC_BLOCK) & (c_start + 2 * CMP_STRIDE > s_start) & (n >= 0)
    return ov.astype(BF)


def _nsa_prompt(q_t, qr_t, cmp, kvs_t, kvw_t, small, b, t):
    tq = 256
    kc = 512
    nq = t // tq
    n_half = t // CMP_STRIDE
    n_sel = t // SLC_BLOCK
    assert n_half == LANES and n_sel <= LANES and t % kc == 0
    ov_t = _overlap_matrix(LANES, LANES, 0).T
    qtile = pl.BlockSpec((None, Q_DIM, tq), lambda i, j: (i, 0, j))
    seq = pl.BlockSpec((None, KV_DIM, t), lambda i, j: (i, 0, 0))
    return pl.pallas_call(
        functools.partial(_nsa_prompt_kernel, t_len=t, tq=tq, kc=kc),
        grid=(b, nq),
        in_specs=[qtile, qtile,
                  pl.BlockSpec((None, N_KV, n_half, LANES), lambda i, j: (i, 0, 0, 0)),
                  seq, seq, pl.BlockSpec((tq, LANES), lambda i, j: (i * nq + j, 0)),
                  pl.BlockSpec(ov_t.shape, lambda i, j: (0, 0))],
        out_specs=qtile,
        out_shape=jax.ShapeDtypeStruct((b, Q_DIM, t), F32),
        scratch_shapes=[pltpu.VMEM((N_KV, n_sel, tq), F32), pltpu.VMEM((N_KV, HEAD_DIM, Q_PER_KV * tq), F32)],
        compiler_params=_cparams(("parallel", "parallel")),
        name="nsa_prompt",
    )(q_t, qr_t, cmp, kvs_t, kvw_t, small, ov_t)


def _ssd_prompt_kernel(zx_ref, sm_ref, cw_ref, cb_ref, dtb_ref, alog_ref, dskip_ref, nrm_ref, tri_ref,
                       y_ref, hout_ref, tail_ref, h_ref, *, n_sub):
    lc = SSD_CHUNK

    @pl.when(pl.program_id(1) == 0)
    def _():
        tail_ref[...] = jnp.zeros_like(tail_ref)
        h_ref[...] = jnp.zeros_like(h_ref)

    li = lax.broadcasted_iota(jnp.int32, (lc, lc), 0)
    si = lax.broadcasted_iota(jnp.int32, (lc, lc), 1)
    lower = li >= si
    heads_per_group = SSD_HEADS // 2
    tail = tail_ref[...]
    state = [h_ref[h] for h in range(SSD_HEADS)]
    for sc in range(n_sub):
        rows = slice(sc * lc, (sc + 1) * lc)
        z = zx_ref[rows, 0:SSD_INNER]
        xbc = zx_ref[rows, SSD_INNER:]
        u = jnp.concatenate([tail, xbc], axis=0)
        tail = xbc[lc - 8:lc, :]
        conv = cb_ref[...]
        for k in range(CONV_WIDTH):
            conv = conv + u[5 + k:5 + k + lc, :] * cw_ref[k:k + 1, :]
        xbc_c = _silu(conv)
        xs = xbc_c[:, 0:SSD_INNER]

        dt_full = _softplus(sm_ref[rows, :] + dtb_ref[...])
        a_full = dt_full * (-jnp.exp(alog_ref[...]))
        acs = _dot3_left(tri_ref[...], a_full)
        acs_t = acs.T

        bm_b = [xbc_c[:, SSD_INNER + g * SSD_STATE:SSD_INNER + (g + 1) * SSD_STATE].astype(BF) for g in range(2)]
        cm_b = [xbc_c[:, SSD_INNER + (2 + g) * SSD_STATE:SSD_INNER + (3 + g) * SSD_STATE].astype(BF)
                for g in range(2)]
        cb = [_dot_nt(cm_b[g], bm_b[g]) for g in range(2)]
        prep = []
        for h in range(SSD_HEADS):
            col = DT_COL + h
            acs_h = acs[:, col:col + 1]
            acs_last = acs[lc - 1:lc, col:col + 1]
            lmat = jnp.where(lower, jnp.exp(acs_h - acs_t[col:col + 1, :]), 0.0)
            xdt = xs[:, h * SSD_HEAD_DIM:(h + 1) * SSD_HEAD_DIM] * dt_full[:, col:col + 1]
            prep.append(((cb[h // heads_per_group] * lmat).astype(BF), xdt.astype(BF),
                         (xdt * jnp.exp(acs_last - acs_h)).astype(BF), jnp.exp(acs_h), jnp.exp(acs_last)))
        ys = []
        for h in range(SSD_HEADS):
            g = h // heads_per_group
            m_h, xdt_b, xdec_b, grow, glast = prep[h]
            ys.append(_dot(m_h, xdt_b) + _dot_nt(cm_b[g], state[h].astype(BF)) * grow)
            state[h] = glast * state[h] + _dot_tn(xdec_b, bm_b[g])
        y = jnp.concatenate(ys, axis=1) + dskip_ref[...] * xs
        y = y * _silu(z)
        y_ref[rows, :] = _rms(y, nrm_ref[...])
    tail_ref[...] = tail
    for h in range(SSD_HEADS):
        h_ref[h] = state[h]
        hout_ref[h] = state[h]


def _ssd_params(conv_w, conv_b, dt_bias, a_log, d_skip, ssd_norm):
    pad = lambda v: jnp.pad(v, (DT_COL, LANES - DT_COL - SSD_HEADS)).reshape(1, LANES)
    return (conv_w, conv_b.reshape(1, -1), pad(dt_bias), pad(a_log),
            jnp.repeat(d_skip, SSD_HEAD_DIM).reshape(1, SSD_INNER), ssd_norm.reshape(1, SSD_INNER))


def _ssd_prompt(zx, small, params, b, t):
    lc = SSD_CHUNK
    n_sub = 4
    nc = t // (lc * n_sub)
    cw, cb, dtb, alog, dskip, nrm = params
    tri = (jnp.arange(lc)[:, None] >= jnp.arange(lc)[None, :]).astype(BF)
    zx_dim = zx.shape[1]
    conv_dim = zx_dim - SSD_INNER
    full = lambda a: pl.BlockSpec(a.shape, lambda i, j: (0,) * a.ndim)
    tile = lambda w: pl.BlockSpec((lc * n_sub, w), lambda i, j: (i * nc + j, 0))
    return pl.pallas_call(
        functools.partial(_ssd_prompt_kernel, n_sub=n_sub),
        grid=(b, nc),
        in_specs=[tile(zx_dim), tile(LANES), full(cw), full(cb), full(dtb), full(alog), full(dskip), full(nrm),
                  full(tri)],
        out_specs=[tile(SSD_INNER),
                   pl.BlockSpec((None, SSD_HEADS, SSD_HEAD_DIM, SSD_STATE), lambda i, j: (i, 0, 0, 0))],
        out_shape=[jax.ShapeDtypeStruct((b * t, SSD_INNER), F32),
                   jax.ShapeDtypeStruct((b, SSD_HEADS, SSD_HEAD_DIM, SSD_STATE), F32)],
        scratch_shapes=[pltpu.VMEM((8, conv_dim), F32), pltpu.VMEM((SSD_HEADS, SSD_HEAD_DIM, SSD_STATE), F32)],
        compiler_params=_cparams(("parallel", "arbitrary")),
        name="ssd_prompt",
    )(zx, small, cw, cb, dtb, alog, dskip, nrm, tri)


def _ffn_kernel(x_ref, *refs, final_norm, mixer_proj, a_feature_major):
    x = x_ref[...]
    if mixer_proj:
        a_ref, b_ref, wa_ref, wb_ref = refs[:4]
        refs = refs[4:]
        a = a_ref[...].astype(BF)
        x = x + (_dot_tn(a, wa_ref[...]) if a_feature_major else _dot(a, wa_ref[...]))
        x = x + _dot(b_ref[...].astype(BF), wb_ref[...])
    g_ref, wg_ref, wu_ref, wd_ref, gf_ref, o_ref = refs
    xn = _rms(x, g_ref[...]).astype(BF)
    h = _silu(_dot(xn, wg_ref[...])) * _dot(xn, wu_ref[...])
    y = x + _dot(h.astype(BF), wd_ref[...])
    if final_norm:
        y = _rms(y, gf_ref[...])
    o_ref[...] = y


def _ffn(x2d, g, wg, wu, wd, gf, tm, final_norm, in_map, out_map, out_2d_shape, mixer=None):
    m, d = x2d.shape[0] * x2d.shape[1] // wg.shape[0], wg.shape[0]
    n_steps = m // tm
    full = lambda arr: pl.BlockSpec(arr.shape, lambda i: (0, 0), pipeline_mode=pl.Buffered(1))
    args = [x2d]
    in_specs = [pl.BlockSpec((tm, d), in_map)]
    a_feature_major = mixer is not None and mixer[0].ndim == 3
    if mixer is not None:
        a, bb, wa, wb = mixer
        args += [a, bb, wa, wb]
        if a_feature_major:
            nt = a.shape[2] // tm
            a_spec = pl.BlockSpec((None, a.shape[1], tm), lambda i: (i // nt, 0, i % nt))
        else:
            a_spec = pl.BlockSpec((tm, a.shape[1]), in_map)
        in_specs += [a_spec, pl.BlockSpec((tm, bb.shape[1]), in_map), full(wa), full(wb)]
    args += [g, wg, wu, wd, gf]
    in_specs += [full(g), full(wg), full(wu), full(wd), full(gf)]
    return pl.pallas_call(
        functools.partial(_ffn_kernel, final_norm=final_norm, mixer_proj=mixer is not None,
                          a_feature_major=a_feature_major),
        grid=(n_steps,),
        in_specs=in_specs,
        out_specs=pl.BlockSpec((tm, d), out_map),
        out_shape=jax.ShapeDtypeStruct(out_2d_shape, F32),
        compiler_params=_cparams(("parallel",)),
        name="ffn_final" if final_norm else "ffn",
    )(*args)


def _lru_gates(xc, wa_ref, ba_ref, wx_ref, bx_ref, sp):
    n_heads = xc.shape[1] // LRU_BLOCK
    a_parts = []
    gx_parts = []
    for h in range(n_heads):
        sl = slice(h * LRU_BLOCK, (h + 1) * LRU_BLOCK)
        xh = xc[:, sl]
        xb = xh.astype(BF)
        r = jax.nn.sigmoid(_dot(xb, wa_ref[h]) + ba_ref[:, sl])
        i = jax.nn.sigmoid(_dot(xb, wx_ref[h]) + bx_ref[:, sl])
        log_a = -LRU_C * r * sp[:, sl]
        a = jnp.exp(log_a)
        a_parts.append(a)
        gx_parts.append(jnp.sqrt(1.0 - a * a) * (i * xh))
    return jnp.concatenate(a_parts, axis=1), jnp.concatenate(gx_parts, axis=1)


SUBLANES = 8


def _lru_prompt_kernel(x_ref, g_ref, win_ref, cw_ref, cb_ref, wa_ref, ba_ref, wx_ref, bx_ref, lam_ref, wo_ref,
                       o_ref, hout_ref, cout_ref, tail_ref, h_ref, a_s, gx_s, hs_s, u_s, *, rows, w):
    @pl.when(pl.program_id(1) == 0)
    def _():
        tail_ref[...] = jnp.zeros_like(tail_ref)
        h_ref[...] = jnp.zeros_like(h_ref)

    x = x_ref[...]
    xn = _rms(x, g_ref[...]).astype(BF)
    proj = _dot(xn, win_ref[...])
    gate_br = proj[:, 0:w]
    x_br = proj[:, w:2 * w]
    u_s[0:SUBLANES, :] = tail_ref[...]
    u_s[SUBLANES:SUBLANES + rows, :] = x_br
    tail_ref[...] = x_br[rows - SUBLANES:rows, :]
    cout_ref[...] = x_br[rows - (CONV_WIDTH - 1):rows, :]
    xc = cb_ref[...]
    for k in range(CONV_WIDTH):
        off = SUBLANES - (CONV_WIDTH - 1) + k
        xc = xc + u_s[off:off + rows, :] * cw_ref[k:k + 1, :]
    sp = _softplus(-lam_ref[...])
    a, gx = _lru_gates(xc, wa_ref, ba_ref, wx_ref, bx_ref, sp)
    a_s[...] = a
    gx_s[...] = gx
    sub = lax.broadcasted_iota(jnp.int32, (SUBLANES, w), 0)
    h = h_ref[...]
    for i in range(rows // SUBLANES):
        blk = slice(i * SUBLANES, (i + 1) * SUBLANES)
        ac = a_s[blk, :]
        bc = gx_s[blk, :]
        for dd in (1, 2, 4):
            a_sh = jnp.where(sub >= dd, pltpu.roll(ac, dd, 0), 1.0)
            b_sh = jnp.where(sub >= dd, pltpu.roll(bc, dd, 0), 0.0)
            bc = ac * b_sh + bc
            ac = ac * a_sh
        hb = ac * h + bc
        hs_s[blk, :] = hb
        h = jnp.broadcast_to(hb[SUBLANES - 1:SUBLANES, :], (SUBLANES, w))
    h_ref[...] = h
    hout_ref[...] = h[0:1, :]
    y = (jax.nn.gelu(gate_br) * hs_s[...]).astype(BF)
    o_ref[...] = x + _dot(y, wo_ref[...])


def _lru_prompt(x2d, g, w_in, cw, cb, wa, ba, wx, bx, lam, wo, nb, rows):
    m, d = x2d.shape
    w = cw.shape[1]
    nt = m // nb // rows
    full = lambda a: pl.BlockSpec(a.shape, lambda i, j: (0,) * a.ndim)
    tile = pl.BlockSpec((rows, d), lambda i, j: (i * nt + j, 0))
    args = (g, w_in, cw, cb, wa, ba, wx, bx, lam, wo)
    return pl.pallas_call(
        functools.partial(_lru_prompt_kernel, rows=rows, w=w),
        grid=(nb, nt),
        in_specs=[tile] + [full(a) for a in args],
        out_specs=[tile,
                   pl.BlockSpec((None, 1, w), lambda i, j: (i, 0, 0)),
                   pl.BlockSpec((None, CONV_WIDTH - 1, w), lambda i, j: (i, 0, 0))],
        out_shape=[jax.ShapeDtypeStruct((m, d), F32),
                   jax.ShapeDtypeStruct((nb, 1, w), F32),
                   jax.ShapeDtypeStruct((nb, CONV_WIDTH - 1, w), F32)],
        scratch_shapes=[pltpu.VMEM((SUBLANES, w), F32), pltpu.VMEM((SUBLANES, w), F32),
                        pltpu.VMEM((rows, w), F32), pltpu.VMEM((rows, w), F32), pltpu.VMEM((rows, w), F32),
                        pltpu.VMEM((SUBLANES + rows, w), F32)],
        compiler_params=_cparams(("parallel", "arbitrary")),
        name="lru_prompt",
    )(x2d, *args)


def _lru_params(norm_g, w_in_c, conv_w, conv_b, w_a, b_a, w_x, b_x, lam, w_out_c):
    r = lambda v: v.reshape(1, -1)
    return (r(norm_g), w_in_c.astype(BF), conv_w, r(conv_b), w_a.astype(BF), r(b_a), w_x.astype(BF), r(b_x),
            r(lam), w_out_c.astype(BF))


def _lru_decode_kernel(x_ref, g_ref, win_ref, cw_ref, cb_ref, wa_ref, ba_ref, wx_ref, bx_ref, lam_ref, wo_ref,
                       c0_ref, c1_ref, c2_ref, h0_ref, o_ref, hout_ref, xbr_ref, *, w):
    x = x_ref[...]
    xn = _rms(x, g_ref[...]).astype(BF)
    proj = _dot(xn, win_ref[...])
    gate_br = proj[:, 0:w]
    x_br = proj[:, w:2 * w]
    xbr_ref[...] = x_br
    xc = (cb_ref[...] + c0_ref[...] * cw_ref[0:1, :] + c1_ref[...] * cw_ref[1:2, :]
          + c2_ref[...] * cw_ref[2:3, :] + x_br * cw_ref[3:4, :])
    sp = _softplus(-lam_ref[...])
    a, gx = _lru_gates(xc, wa_ref, ba_ref, wx_ref, bx_ref, sp)
    h = a * h0_ref[...] + gx
    hout_ref[...] = h
    y = (jax.nn.gelu(gate_br) * h).astype(BF)
    o_ref[...] = x + _dot(y, wo_ref[...])


def _lru_decode(x, params, conv_state, h0):
    m, d = x.shape
    w = h0.shape[1]
    args = (x,) + tuple(params) + (conv_state[:, 0], conv_state[:, 1], conv_state[:, 2], h0)
    full = lambda a: pl.BlockSpec(a.shape, lambda i: (0,) * a.ndim)
    return pl.pallas_call(
        functools.partial(_lru_decode_kernel, w=w),
        grid=(1,),
        in_specs=[full(a) for a in args],
        out_specs=[pl.BlockSpec((m, d), lambda i: (0, 0)), pl.BlockSpec((m, w), lambda i: (0, 0)),
                   pl.BlockSpec((m, w), lambda i: (0, 0))],
        out_shape=[jax.ShapeDtypeStruct((m, d), F32), jax.ShapeDtypeStruct((m, w), F32),
                   jax.ShapeDtypeStruct((m, w), F32)],
        compiler_params=_cparams(("arbitrary",)),
        name="lru_decode",
    )(*args)


PAGES_PER_STEP = 32
HALVES_PER_PAGE = PAGE_SIZE // CMP_STRIDE
CMP_PACK = 2 * LANES // HEAD_DIM


def _compress_decode_kernel(pt_ref, *refs):
    del pt_ref
    np_ = PAGES_PER_STEP
    k_pages = refs[0:np_]
    v_pages = refs[np_:2 * np_]
    pek_ref, w1k_ref, w2k_ref, pev_ref, w1v_ref, w2v_ref, out_ref, carry_ref, rows_ref = refs[2 * np_:]
    rows = np_ * HALVES_PER_PAGE

    @pl.when(pl.program_id(1) == 0)
    def _():
        carry_ref[...] = jnp.zeros_like(carry_ref)

    rowi = lax.broadcasted_iota(jnp.int32, (rows, CMP_HIDDEN), 0)
    low = lax.broadcasted_iota(jnp.int32, (rows, LANES), 1) < HEAD_DIM

    def split_groups(a, b):
        return (jnp.where(low, a, pltpu.roll(b, HEAD_DIM, 1)), jnp.where(low, pltpu.roll(a, HEAD_DIM, 1), b))

    branches = ((k_pages, pek_ref, w1k_ref, w2k_ref), (v_pages, pev_ref, w1v_ref, w2v_ref))
    for c, (pages, _, _, _) in enumerate(branches):
        for k, p in enumerate(pages):
            rows_ref[c, k * PAGE_SIZE:(k + 1) * PAGE_SIZE, :] = p[...].T

    for c, (pages, pe_ref, w1_ref, w2_ref) in enumerate(branches):
        operands = []
        for u in range(CMP_STRIDE // CMP_PACK):
            xs = [rows_ref[c, pl.ds(u * CMP_PACK + j, rows, stride=CMP_STRIDE), :] for j in range(CMP_PACK)]
            g01 = split_groups(xs[0], xs[1])
            g23 = split_groups(xs[2], xs[3])
            x = jnp.concatenate([jnp.concatenate([g01[0], g23[0]], axis=1),
                                 jnp.concatenate([g01[1], g23[1]], axis=1)], axis=0)
            operands.append(((x + pe_ref[0, u]).astype(BF), (x + pe_ref[1, u]).astype(BF)))
        lead = None
        trail = None
        for u, (xl, xt) in enumerate(operands):
            dl = _dot(xl, w1_ref[0, u])
            dt = _dot(xt, w1_ref[1, u])
            lead = dl if lead is None else lead + dl
            trail = dt if trail is None else trail + dt
        for g in range(N_KV):
            a0 = lead[g * rows:(g + 1) * rows]
            a1 = trail[g * rows:(g + 1) * rows]
            slot = c * N_KV + g
            prev = jnp.where(rowi == 0, carry_ref[slot, 7:8, :], pltpu.roll(a0, 1, 0))
            carry_ref[slot] = a0[rows - 8:rows, :]
            tok = _dot(_silu(prev + a1).astype(BF), w2_ref[...])
            out_ref[g, :, c * HEAD_DIM:(c + 1) * HEAD_DIM] = tok


def _compress_decode(cache_t, pt_flat, db, n_pages, pek, w1k, w2k, pev, w1v, w2v):
    np_ = PAGES_PER_STEP
    n_steps = n_pages // np_
    rows = np_ * HALVES_PER_PAGE
    n_half = n_pages * HALVES_PER_PAGE
    n_quads = CMP_STRIDE // CMP_PACK
    pack_pe = lambda pe: pe.reshape(2, n_quads, 1, CMP_PACK * HEAD_DIM)
    pack_w1 = lambda w1: w1.reshape(2, n_quads, CMP_PACK * HEAD_DIM, CMP_HIDDEN)
    pek, w1k, pev, w1v = pack_pe(pek), pack_w1(w1k), pack_pe(pev), pack_w1(w1v)

    def page_spec(k, rowblk):
        return pl.BlockSpec((None, LANES, PAGE_SIZE),
                            lambda i, j, pt: (pt[i * n_pages + j * np_ + k], rowblk, 0))

    full = lambda a: pl.BlockSpec(a.shape, lambda i, j, pt: (0,) * a.ndim)
    grid_spec = pltpu.PrefetchScalarGridSpec(
        num_scalar_prefetch=1,
        grid=(db, n_steps),
        in_specs=[page_spec(k, 0) for k in range(np_)] + [page_spec(k, 1) for k in range(np_)]
        + [full(pek), full(w1k), full(w2k), full(pev), full(w1v), full(w2v)],
        out_specs=pl.BlockSpec((None, N_KV, rows, LANES), lambda i, j, pt: (i, 0, j, 0)),
        scratch_shapes=[pltpu.VMEM((2 * N_KV, 8, CMP_HIDDEN), F32), pltpu.VMEM((2, np_ * PAGE_SIZE, LANES), F32)],
    )
    return pl.pallas_call(
        _compress_decode_kernel,
        grid_spec=grid_spec,
        out_shape=jax.ShapeDtypeStruct((db, N_KV, n_half, LANES), F32),
        compiler_params=_cparams(("parallel", "arbitrary")),
        name="compress_decode",
    )(pt_flat, *([cache_t] * (2 * np_)), pek, w1k, w2k, pev, w1v, w2v)


def _group_rows(top, a, b):
    return jnp.where(top, a, b)


def _nsa_decode_a_kernel(q_ref, qr_ref, cmp_ref, win_ref, new_ref, ov_ref, ocmp_ref, owin_ref, idx_ref, *,
                         pos, n_blocks):
    nh = N_HEADS
    top = lax.broadcasted_iota(jnp.int32, (nh, 1), 0) < Q_PER_KV
    q = q_ref[...].astype(BF)
    n_tok = cmp_ref.shape[1]
    lane = lax.broadcasted_iota(jnp.int32, (nh, n_tok), 1)
    cmask = (lane >= 1) & (((lane - 1) * CMP_STRIDE + (2 * CMP_STRIDE - 1)) <= pos)
    s = _group_rows(top, _dot_nt(q, cmp_ref[0, :, 0:HEAD_DIM].astype(BF)),
                    _dot_nt(q, cmp_ref[1, :, 0:HEAD_DIM].astype(BF))) * SCALE
    p = _msoftmax(s, cmask)
    pb = p.astype(BF)
    ocmp_ref[...] = _group_rows(top, _dot(pb, cmp_ref[0, :, HEAD_DIM:LANES].astype(BF)),
                                _dot(pb, cmp_ref[1, :, HEAD_DIM:LANES].astype(BF)))

    imp8 = _dot3(p, ov_ref[...])
    nj = ov_ref.shape[1]
    cur = pos // SLC_BLOCK
    j = lax.broadcasted_iota(jnp.int32, (1, nj), 1)
    valid = (j * SLC_BLOCK) <= pos
    forced = (j == 0) | (j == cur) | (j == cur - 1)
    ri = lax.broadcasted_iota(jnp.int32, (nj, nj), 0)
    ci = lax.broadcasted_iota(jnp.int32, (nj, nj), 1)
    k_lane = lax.broadcasted_iota(jnp.int32, (nj, LANES), 1).astype(F32)
    jvals = lax.broadcasted_iota(jnp.int32, (8, nj), 1).astype(F32).astype(BF)
    idx_rows = []
    for g in range(N_KV):
        imp = jnp.sum(imp8[g * Q_PER_KV:(g + 1) * Q_PER_KV, :], axis=0, keepdims=True)
        imp = jnp.where(valid & forced, FORCE_SCORE, imp)
        imp = jnp.where(valid, imp, -FORCE_SCORE)
        imp = jnp.where(j < n_blocks, imp, -3e38)
        impb = jnp.broadcast_to(imp, (nj, nj))
        col = jnp.sum(jnp.where(ri == ci, impb, 0.0), axis=1, keepdims=True)
        ahead = jnp.where(impb == col, jnp.where(ci < ri, 1.0, 0.0), jnp.where(impb > col, 1.0, 0.0))
        rank_col = jnp.sum(ahead, axis=1, keepdims=True)
        onehot = jnp.where(rank_col == k_lane, 1.0, 0.0).astype(BF)
        idx_rows.append(_dot(jvals, onehot)[0:1, :])
    idx = jnp.concatenate(idx_rows + [jnp.zeros((8 - N_KV, LANES), F32)], axis=0)
    idx_ref[...] = idx.astype(jnp.int32)

    qr = qr_ref[...]
    qrb = qr.astype(BF)
    n_win = win_ref.shape[1]
    s = _group_rows(top, _dot(qrb, win_ref[0:HEAD_DIM, :].astype(BF)),
                    _dot(qrb, win_ref[HEAD_DIM:LANES, :].astype(BF))) * SCALE
    wl = lax.broadcasted_iota(jnp.int32, (nh, n_win), 1)
    dist = n_win - wl
    wmask = (dist >= 0) & (dist < WINDOW)
    new = new_ref[...]
    knew = _group_rows(top, new[:, 0:HEAD_DIM], new[:, HEAD_DIM:LANES])
    vnew = _group_rows(top, new[:, LANES:LANES + HEAD_DIM], new[:, LANES + HEAD_DIM:KV_DIM])
    s_new = jnp.sum(qr * knew, axis=1, keepdims=True) * SCALE
    sm = jnp.where(wmask, s, NEG_BIG)
    m = jnp.maximum(jnp.max(sm, axis=1, keepdims=True), s_new)
    e = jnp.where(wmask, jnp.exp(sm - m), 0.0)
    e_new = jnp.exp(s_new - m)
    d = jnp.sum(e, axis=1, keepdims=True) + e_new
    eb = e.astype(BF)
    o = _group_rows(top, _dot_nt(eb, win_ref[LANES:LANES + HEAD_DIM, :].astype(BF)),
                    _dot_nt(eb, win_ref[LANES + HEAD_DIM:KV_DIM, :].astype(BF))) + e_new * vnew
    owin_ref[...] = o / d


def _nsa_decode_a(q3, qr3, cmp_s, win, kvw_new3, pos):
    db = q3.shape[0]
    n_tok = cmp_s.shape[2]
    n_blocks = -(-(pos + 1) // SLC_BLOCK)
    nj = -(-n_blocks // LANES) * LANES
    ov = _overlap_matrix(n_tok, nj, 1)
    per_b = lambda a: pl.BlockSpec((None,) + a.shape[1:], lambda i: (i,) + (0,) * (a.ndim - 1))
    head = pl.BlockSpec((None, N_HEADS, HEAD_DIM), lambda i: (i, 0, 0))
    return pl.pallas_call(
        functools.partial(_nsa_decode_a_kernel, pos=pos, n_blocks=n_blocks),
        grid=(db,),
        in_specs=[head, head, per_b(cmp_s), per_b(win), per_b(kvw_new3), pl.BlockSpec(ov.shape, lambda i: (0, 0))],
        out_specs=[head, head, pl.BlockSpec((None, 8, LANES), lambda i: (i, 0, 0))],
        out_shape=[jax.ShapeDtypeStruct((db, N_HEADS, HEAD_DIM), F32),
                   jax.ShapeDtypeStruct((db, N_HEADS, HEAD_DIM), F32),
                   jax.ShapeDtypeStruct((db, 8, LANES), jnp.int32)],
        compiler_params=_cparams(("parallel",)),
        name="nsa_decode_select",
    )(q3, qr3, cmp_s, win, kvw_new3, ov)


def _nsa_decode_b_kernel(idx_ref, pt_ref, qr_ref, *refs, n_past_blocks, cur):
    del pt_ref
    nb = N_KV * N_SELECT
    blk_refs = refs[0:nb]
    new_ref, ocmp_ref, owin_ref, sm_ref, o_ref = refs[nb:]
    b = pl.program_id(0)
    nh = N_HEADS
    top = lax.broadcasted_iota(jnp.int32, (nh, 1), 0) < Q_PER_KV
    qr = qr_ref[...]
    qrb = qr.astype(BF)
    per_page = PAGE_SIZE // SLC_BLOCK
    n_keys = N_SELECT * PAGE_SIZE
    lane = lax.broadcasted_iota(jnp.int32, (nh, n_keys), 1)
    lane_page = lane // PAGE_SIZE
    lane_sub = (lane // SLC_BLOCK) % per_page
    s_g = []
    v_g = []
    ok_g = []
    has_new = []
    for g in range(N_KV):
        blks = blk_refs[g * N_SELECT:(g + 1) * N_SELECT]
        kt = jnp.concatenate([r[g * HEAD_DIM:(g + 1) * HEAD_DIM, :] for r in blks], axis=1).astype(BF)
        v_g.append(jnp.concatenate([r[LANES + g * HEAD_DIM:LANES + (g + 1) * HEAD_DIM, :] for r in blks],
                                   axis=1).astype(BF))
        s_g.append(_dot(qrb, kt))
        ok = jnp.zeros((nh, n_keys), F32)
        new_sel = jnp.zeros((), F32)
        for k in range(N_SELECT):
            jk = idx_ref[(b * N_KV + g) * N_SELECT + k]
            hit = (lane_page == k) & (lane_sub == jk % per_page)
            ok = jnp.where(hit, jnp.where(jk < n_past_blocks, 1.0, 0.0), ok)
            new_sel = jnp.maximum(new_sel, jnp.where(jk == cur, 1.0, 0.0))
        ok_g.append(ok)
        has_new.append(new_sel)
    s = _group_rows(top, s_g[0], s_g[1]) * SCALE
    mask = _group_rows(top, ok_g[0], ok_g[1]) > 0.5
    new_on = _group_rows(top, has_new[0], has_new[1]) > 0.5
    new = new_ref[...]
    knew = _group_rows(top, new[:, 0:HEAD_DIM], new[:, HEAD_DIM:LANES])
    vnew = _group_rows(top, new[:, LANES:LANES + HEAD_DIM], new[:, LANES + HEAD_DIM:KV_DIM])
    s_new = jnp.where(new_on, jnp.sum(qr * knew, axis=1, keepdims=True) * SCALE, NEG_BIG)
    sm = jnp.where(mask, s, NEG_BIG)
    m = jnp.maximum(jnp.max(sm, axis=1, keepdims=True), s_new)
    e = jnp.where(mask, jnp.exp(sm - m), 0.0)
    e_new = jnp.where(new_on, jnp.exp(s_new - m), 0.0)
    d = jnp.sum(e, axis=1, keepdims=True) + e_new
    eb = e.astype(BF)
    o_slc = ((_group_rows(top, _dot_nt(eb, v_g[0]), _dot_nt(eb, v_g[1])) + e_new * vnew)
             / jnp.where(d > 0.0, d, 1.0))

    sig = jnp.broadcast_to(jax.nn.sigmoid(sm_ref[...]), (nh, LANES))
    hl = lax.broadcasted_iota(jnp.int32, (nh, LANES), 1)
    hr = lax.broadcasted_iota(jnp.int32, (nh, LANES), 0)

    def gate(br):
        return jnp.sum(jnp.where(hl == 3 * hr + br, sig, 0.0), axis=1, keepdims=True)

    o_ref[...] = gate(0) * ocmp_ref[...] + gate(1) * o_slc + gate(2) * owin_ref[...]


def _nsa_decode_b(idx_flat, pt_flat, qr3, slc_pages_t, kvs_new3, o_cmp, o_win, small3, pos, n_pages):
    db = qr3.shape[0]
    n_past_blocks = pos // SLC_BLOCK
    cur = pos // SLC_BLOCK
    per_page = PAGE_SIZE // SLC_BLOCK

    def blk_spec(g, k):
        def imap(i, idx, pt):
            jk = jnp.minimum(idx[(i * N_KV + g) * N_SELECT + k], n_past_blocks - 1)
            return (pt[i * n_pages + jk // per_page], 0, 0)
        return pl.BlockSpec((None, KV_DIM, PAGE_SIZE), imap)

    head = pl.BlockSpec((None, N_HEADS, HEAD_DIM), lambda i, idx, pt: (i, 0, 0))
    row3 = lambda a: pl.BlockSpec((None,) + a.shape[1:], lambda i, idx, pt: (i, 0, 0))
    grid_spec = pltpu.PrefetchScalarGridSpec(
        num_scalar_prefetch=2,
        grid=(db,),
        in_specs=[head] + [blk_spec(g, k) for g in range(N_KV) for k in range(N_SELECT)]
        + [row3(kvs_new3), head, head, row3(small3)],
        out_specs=head,
    )
    return pl.pallas_call(
        functools.partial(_nsa_decode_b_kernel, n_past_blocks=n_past_blocks, cur=cur),
        grid_spec=grid_spec,
        out_shape=jax.ShapeDtypeStruct((db, N_HEADS, HEAD_DIM), F32),
        compiler_params=_cparams(("arbitrary",)),
        name="nsa_decode_attend",
    )(idx_flat, pt_flat, qr3, *([slc_pages_t] * (N_KV * N_SELECT)), kvs_new3, o_cmp, o_win, small3)


def _ssd_decode_kernel(zx_ref, cst_ref, sm_ref, h0_ref, cw_ref, cb_ref, dtb_ref, alog_ref, dskip_ref, nrm_ref,
                       y_ref, hout_ref):
    z = zx_ref[:, 0:SSD_INNER]
    xbc = zx_ref[:, SSD_INNER:]
    conv = cb_ref[...] + xbc * cw_ref[CONV_WIDTH - 1:CONV_WIDTH, :]
    for k in range(CONV_WIDTH - 1):
        conv = conv + cst_ref[k:k + 1, :] * cw_ref[k:k + 1, :]
    xbc_c = _silu(conv)
    xs = xbc_c[:, 0:SSD_INNER]
    dt_full = _softplus(sm_ref[...] + dtb_ref[...])
    da_full = jnp.exp(dt_full * (-jnp.exp(alog_ref[...])))
    p = SSD_HEAD_DIM
    eye = lax.broadcasted_iota(jnp.int32, (p, p), 0) == lax.broadcasted_iota(jnp.int32, (p, p), 1)
    ys = []
    for h in range(SSD_HEADS):
        g = h // (SSD_HEADS // 2)
        col = DT_COL + h
        xdt = xs[:, h * p:(h + 1) * p] * dt_full[:, col:col + 1]
        xcol = jnp.sum(jnp.where(eye, jnp.broadcast_to(xdt, (p, p)), 0.0), axis=1, keepdims=True)
        bm = xbc_c[:, SSD_INNER + g * SSD_STATE:SSD_INNER + (g + 1) * SSD_STATE]
        cm = xbc_c[:, SSD_INNER + (2 + g) * SSD_STATE:SSD_INNER + (3 + g) * SSD_STATE]
        h_new = da_full[:, col:col + 1] * h0_ref[h] + xcol * bm
        hout_ref[h] = h_new
        ys.append(_dot_nt(jnp.broadcast_to(cm, (8, SSD_STATE)).astype(BF), h_new.astype(BF))[0:1, :])
    y = jnp.concatenate(ys, axis=1) + dskip_ref[...] * xs
    y = y * _silu(z)
    y_ref[...] = _rms(y, nrm_ref[...])


def _ssd_decode(zx3, conv_state, small3, h0, params):
    db = zx3.shape[0]
    cw, cb, dtb, alog, dskip, nrm = params
    per_b = lambda a: pl.BlockSpec((None,) + a.shape[1:], lambda i: (i,) + (0,) * (a.ndim - 1))
    full = lambda a: pl.BlockSpec(a.shape, lambda i: (0,) * a.ndim)
    return pl.pallas_call(
        _ssd_decode_kernel,
        grid=(db,),
        in_specs=[per_b(zx3), per_b(conv_state), per_b(small3), per_b(h0), full(cw), full(cb), full(dtb), full(alog),
                  full(dskip), full(nrm)],
        out_specs=[pl.BlockSpec((None, 1, SSD_INNER), lambda i: (i, 0, 0)), per_b(h0)],
        out_shape=[jax.ShapeDtypeStruct((db, 1, SSD_INNER), F32), jax.ShapeDtypeStruct(h0.shape, F32)],
        compiler_params=_cparams(("parallel",)),
        name="ssd_decode",
    )(zx3, conv_state, small3, h0, cw, cb, dtb, alog, dskip, nrm)


def kernel(x_prompt, x_sample, cache_kv_cmp, cache_kv_slc, cache_kv_win, state_ssm, state_ssd_conv, state_lru,
           state_lru_conv, page_table, norm_mix, norm_ffn, norm_final, w_ffn_gate, w_ffn_up, w_ffn_down, w_in_a,
           w_out_a, cmp_pe_k, cmp_w1_k, cmp_w2_k, cmp_pe_v, cmp_w1_v, cmp_w2_v, ssd_conv_w, ssd_conv_b, ssd_dt_bias,
           ssd_a_log, ssd_d, ssd_norm, w_in_c, lru_conv_w, lru_conv_b, lru_w_a, lru_b_a, lru_w_x, lru_b_x,
           lru_lambda, w_out_c):
    b, t, d = x_prompt.shape
    db = x_sample.shape[0]
    n_pages = page_table.shape[1]
    pos_s = n_pages * PAGE_SIZE
    m = b * t
    kv_shape = (2, N_KV, HEAD_DIM)

    wm, ws = _prep_w_in_a(w_in_a[0])
    wo_nsa = w_out_a[0, :Q_DIM].astype(BF)
    wo_ssd = w_out_a[0, Q_DIM:].astype(BF)
    cmp_k = _prep_cmp_w(cmp_pe_k[0], cmp_w1_k[0], cmp_w2_k[0])
    cmp_v = _prep_cmp_w(cmp_pe_v[0], cmp_w1_v[0], cmp_w2_v[0])
    ssd_par = _ssd_params(ssd_conv_w[0], ssd_conv_b[0], ssd_dt_bias[0], ssd_a_log[0], ssd_d[0], ssd_norm[0])
    lru_par = _lru_params(norm_mix[1], w_in_c[0], lru_conv_w[0], lru_conv_b[0], lru_w_a[0], lru_b_a[0], lru_w_x[0],
                          lru_b_x[0], lru_lambda[0], w_out_c[0])
    ffn_w = [(norm_ffn[l].reshape(1, d), w_ffn_gate[l].astype(BF), w_ffn_up[l].astype(BF), w_ffn_down[l].astype(BF))
             for l in range(2)]
    gfin = norm_final.reshape(1, d)

    tm_proj = 512
    xp = x_prompt.reshape(m, d)
    cos_p, sin_p = _rope_tables(jnp.arange(t, dtype=jnp.int32))
    q, qr, kvc, kvs_t, kvw_t, zx, small, kvc_t = _inproj_a(xp, norm_mix[0], wm, ws, cos_p, sin_p, tm_proj, b, True)
    cmp_p = _compress_prompt(kvc, b, t, *cmp_k, *cmp_v)
    o_nsa = _nsa_prompt(q, qr, cmp_p, kvs_t, kvw_t, small, b, t)
    y_ssd, ssm_p = _ssd_prompt(zx, small, ssd_par, b, t)
    tm_ffn = 512
    rowmap = lambda i: (i, 0)
    x2 = _ffn(xp, *ffn_w[0], gfin, tm_ffn, False, rowmap, rowmap, (m, d), mixer=(o_nsa, y_ssd, wo_nsa, wo_ssd))
    x3, lru_p, lru_conv_p = _lru_prompt(x2, *lru_par, b, 512)
    y_prompt = _ffn(x3, *ffn_w[1], gfin, tm_ffn, True, rowmap, rowmap, (m, d))
    w_keep = min(WINDOW, t)
    to_cache = lambda a: a.reshape((b,) + kv_shape + (a.shape[-1],)).transpose(0, 4, 1, 2, 3)[None]
    kv_cmp_p = to_cache(kvc_t)
    kv_slc_p = to_cache(kvs_t)
    kv_win_p = to_cache(kvw_t[:, :, t - w_keep:])
    ssd_conv_p = zx.reshape(b, t, -1)[:, t - (CONV_WIDTH - 1):, SSD_INNER:][None]
    lru_p = lru_p.reshape(b, -1)
    lru_conv_p = lru_conv_p[None]

    xs = x_sample.reshape(db, d)
    cos_s, sin_s = _rope_tables(jnp.full((db,), pos_s, dtype=jnp.int32))
    q_s, qr_s, kvc_s, kvs_s, kvw_s, zx_s, small_s = _inproj_a(xs, norm_mix[0], wm, ws, cos_s, sin_s, db, 1, False)
    pt_flat = page_table.reshape(-1)
    n_phys = cache_kv_cmp.shape[1]
    feature_major = lambda a: a.transpose(0, 2, 3, 4, 1).reshape(a.shape[0], KV_DIM, a.shape[1])
    cmp_s = _compress_decode(feature_major(cache_kv_cmp[0]), pt_flat, db, n_pages, *cmp_k, *cmp_v)
    win_buf = cache_kv_win[0].reshape(db, -1, KV_DIM)
    head3 = lambda a: a.reshape(db, N_HEADS, HEAD_DIM)
    o_cmp_s, o_win_s, idx = _nsa_decode_a(head3(q_s), head3(qr_s), cmp_s, feature_major(cache_kv_win[0]),
                                          kvw_s.reshape(db, 1, KV_DIM), pos_s)
    idx_flat = idx[:, :N_KV, :N_SELECT].reshape(-1)
    o_nsa_s = _nsa_decode_b(idx_flat, pt_flat, head3(qr_s), feature_major(cache_kv_slc[0]),
                            kvs_s.reshape(db, 1, KV_DIM), o_cmp_s, o_win_s, small_s.reshape(db, 1, LANES), pos_s,
                            n_pages)
    y_ssd_s, ssm_s = _ssd_decode(zx_s.reshape(db, 1, -1), state_ssd_conv[0], small_s.reshape(db, 1, LANES),
                                 state_ssm[0], ssd_par)
    x2_s = _ffn(xs, *ffn_w[0], gfin, db, False, lambda i: (i, 0), lambda i: (i, 0), (db, d),
                mixer=(o_nsa_s.reshape(db, Q_DIM), y_ssd_s.reshape(db, SSD_INNER), wo_nsa, wo_ssd))
    x3_s, lru_s, xbr_s = _lru_decode(x2_s, lru_par, state_lru_conv[0], state_lru[0])
    y_sample = _ffn(x3_s, *ffn_w[1], gfin, db, True, lambda i: (i, 0), lambda i: (i, 0), (db, d))
    kv_cmp_s = kvc_s.reshape((1, db, 1) + kv_shape)
    kv_slc_s = kvs_s.reshape((1, db, 1) + kv_shape)
    kv_win_s = jnp.concatenate([win_buf[:, 1:], kvw_s[:, None, :]], axis=1).reshape(
        (1, db, win_buf.shape[1]) + kv_shape)
    ssd_conv_s = jnp.concatenate([state_ssd_conv[0][:, 1:], zx_s[:, None, SSD_INNER:]], axis=1)[None]
    lru_conv_s = jnp.concatenate([state_lru_conv[0][:, 1:], xbr_s[:, None, :]], axis=1)[None]

    return (y_prompt.reshape(b, t, d), y_sample.reshape(db, 1, d),
            kv_cmp_p, kv_slc_p, kv_win_p, ssm_p[None], ssd_conv_p, lru_p[None], lru_conv_p,
            kv_cmp_s, kv_slc_s, kv_win_s, ssm_s[None], ssd_conv_s, lru_s[None], lru_conv_s)
```

```python
import functools
import math

import jax
import jax.numpy as jnp
from jax import lax
from jax.experimental import pallas as pl
from jax.experimental.pallas import tpu as pltpu

BF = jnp.bfloat16
F32 = jnp.float32

HEAD_DIM = 64
N_KV = 2
Q_PER_KV = 4
N_HEADS = N_KV * Q_PER_KV
CMP_STRIDE = 16
CMP_HIDDEN = 256
SLC_BLOCK = 64
N_SELECT = 16
WINDOW = 512
PAGE_SIZE = 128
ROPE_THETA = 10000.0
FORCE_SCORE = 1e9
NEG_BIG = -1e30
SSD_HEADS = 8
SSD_HEAD_DIM = 64
SSD_STATE = 128
SSD_INNER = SSD_HEADS * SSD_HEAD_DIM
SSD_CHUNK = 128
CONV_WIDTH = 4
LRU_BLOCK = 128
LRU_C = 8.0
RMS_EPS = 1e-6
SCALE = HEAD_DIM ** -0.5
LOG2E = math.log2(math.e)
Q_DIM = N_HEADS * HEAD_DIM
KV_DIM = 2 * N_KV * HEAD_DIM
GATE_DIM = 3 * N_HEADS
DT_COL = GATE_DIM
LANES = 128
VMEM_LIMIT = 56 * 1024 * 1024


def _cparams(sem):
    return pltpu.CompilerParams(dimension_semantics=sem, vmem_limit_bytes=VMEM_LIMIT)


def _dot(a, b):
    return jnp.dot(a, b, preferred_element_type=F32)


def _dot_nt(a, b):
    return lax.dot_general(a, b, (((1,), (1,)), ((), ())), preferred_element_type=F32)


def _dot_tn(a, b):
    return lax.dot_general(a, b, (((0,), (0,)), ((), ())), preferred_element_type=F32)


def _split3(x):
    hi = x.astype(BF)
    r = x - hi.astype(F32)
    mid = r.astype(BF)
    lo = (r - mid.astype(F32)).astype(BF)
    return hi, mid, lo


def _dot3(x, m01):
    hi, mid, lo = _split3(x)
    return _dot(hi, m01) + _dot(mid, m01) + _dot(lo, m01)


def _dot3_left(m01, x):
    hi, mid, lo = _split3(x)
    return _dot(m01, hi) + _dot(m01, mid) + _dot(m01, lo)


def _rms(x, g):
    y = x * lax.rsqrt(jnp.mean(x * x, axis=-1, keepdims=True) + RMS_EPS)
    return y * g


def _silu(x):
    return x * jax.nn.sigmoid(x)


def _softplus(x):
    return jnp.maximum(x, 0.0) + jnp.log1p(jnp.exp(-jnp.abs(x)))


def _msoftmax_parts(s, mask):
    s = jnp.where(mask, s, NEG_BIG)
    m = jnp.max(s, axis=-1, keepdims=True)
    e = jnp.where(mask, jnp.exp(s - m), 0.0)
    return m, e


def _msoftmax(s, mask):
    _, e = _msoftmax_parts(s, mask)
    d = jnp.sum(e, axis=-1, keepdims=True)
    return e / jnp.where(d > 0.0, d, 1.0)


def _rope_tables(pos):
    half = HEAD_DIM // 2
    inv_freq = ROPE_THETA ** (-jnp.arange(half, dtype=F32) / half)
    ang = pos.astype(F32)[:, None] * inv_freq[None, :]
    cos = jnp.cos(ang)
    sin = jnp.sin(ang)
    cos2 = jnp.tile(jnp.concatenate([cos, cos], axis=-1), (1, LANES // HEAD_DIM))
    sin2 = jnp.tile(jnp.concatenate([-sin, sin], axis=-1), (1, LANES // HEAD_DIM))
    return cos2, sin2


def _inproj_a_kernel(x_ref, g_ref, wm_ref, ws_ref, cos_ref, sin_ref,
                     q_ref, qr_ref, kvc_ref, kvs_ref, kvw_ref, zx_ref, sm_ref, *kvc_t_ref, feature_major):
    xn = _rms(x_ref[...], g_ref[...]).astype(BF)
    cos = cos_ref[...]
    sin = sin_ref[...]
    lane = lax.broadcasted_iota(jnp.int32, cos.shape, 1)
    first = (lane % HEAD_DIM) < (HEAD_DIM // 2)

    def rope(v):
        rot = jnp.where(first, pltpu.roll(v, LANES - HEAD_DIM // 2, 1), pltpu.roll(v, HEAD_DIM // 2, 1))
        return v * cos + rot * sin

    def store_kv(ref, k, v):
        if feature_major:
            ref[0:LANES, :] = k.T
            ref[LANES:KV_DIM, :] = v.T
        else:
            ref[:, 0:LANES] = k
            ref[:, LANES:KV_DIM] = v

    q = _dot(xn, wm_ref[:, 0:Q_DIM])
    for c in range(Q_DIM // LANES):
        cols = slice(c * LANES, (c + 1) * LANES)
        if feature_major:
            q_ref[cols, :] = q[:, cols].T
            qr_ref[cols, :] = rope(q[:, cols]).T
        else:
            q_ref[:, cols] = q[:, cols]
            qr_ref[:, cols] = rope(q[:, cols])
    o = Q_DIM
    kvc = _dot(xn, wm_ref[:, o:o + KV_DIM])
    kvc_ref[...] = kvc
    if feature_major:
        store_kv(kvc_t_ref[0], kvc[:, 0:LANES], kvc[:, LANES:KV_DIM])
    o += KV_DIM
    kvs = _dot(xn, wm_ref[:, o:o + KV_DIM])
    store_kv(kvs_ref, rope(kvs[:, 0:LANES]), kvs[:, LANES:KV_DIM])
    o += KV_DIM
    kvw = _dot(xn, wm_ref[:, o:o + KV_DIM])
    store_kv(kvw_ref, rope(kvw[:, 0:LANES]), kvw[:, LANES:KV_DIM])
    o += KV_DIM
    zx_ref[...] = _dot(xn, wm_ref[:, o:])
    sm_ref[...] = _dot(xn, ws_ref[...])


def _prep_w_in_a(w_in_a):
    a = Q_DIM + 3 * KV_DIM
    gate = w_in_a[:, a:a + GATE_DIM]
    rest = w_in_a[:, a + GATE_DIM:]
    zx_w = rest[:, :rest.shape[1] - SSD_HEADS]
    dt = rest[:, rest.shape[1] - SSD_HEADS:]
    main = jnp.concatenate([w_in_a[:, :a], zx_w], axis=1).astype(BF)
    small = jnp.concatenate([gate, dt], axis=1)
    small = jnp.pad(small, ((0, 0), (0, LANES - small.shape[1]))).astype(BF)
    return main, small


def _inproj_a(x2d, g, wm, ws, cos2, sin2, tm, nb, feature_major):
    m, d = x2d.shape
    nt = m // nb // tm
    zx_dim = wm.shape[1] - Q_DIM - 3 * KV_DIM
    row = lambda w: pl.BlockSpec((tm, w), lambda i, j: (i * nt + j, 0))
    row_shape = lambda w: jax.ShapeDtypeStruct((m, w), F32)
    const = lambda i, j: (0, 0)
    if feature_major:
        fm = lambda w: pl.BlockSpec((None, w, tm), lambda i, j: (i, 0, j))
        fm_shape = lambda w: jax.ShapeDtypeStruct((nb, w, m // nb), F32)
    else:
        fm, fm_shape = row, row_shape
    kv, kv_shape = fm(KV_DIM), fm_shape(KV_DIM)
    out_specs = [fm(Q_DIM), fm(Q_DIM), row(KV_DIM), kv, kv, row(zx_dim), row(LANES)]
    out_shape = [fm_shape(Q_DIM), fm_shape(Q_DIM), row_shape(KV_DIM), kv_shape, kv_shape, row_shape(zx_dim),
                 row_shape(LANES)]
    if feature_major:
        out_specs.append(kv)
        out_shape.append(kv_shape)
    pos = pl.BlockSpec((tm, LANES), lambda i, j: (j, 0))
    return pl.pallas_call(
        functools.partial(_inproj_a_kernel, feature_major=feature_major),
        grid=(nb, nt),
        in_specs=[row(d), pl.BlockSpec((1, d), const), pl.BlockSpec(wm.shape, const), pl.BlockSpec(ws.shape, const),
                  pos, pos],
        out_specs=out_specs,
        out_shape=out_shape,
        compiler_params=_cparams(("parallel", "parallel")),
        name="inproj_a",
    )(x2d, g.reshape(1, d), wm, ws, cos2, sin2)


CMP_PACK = 2 * LANES // HEAD_DIM


def _block_mlp_hidden(load_rows, n_rows, pe_ref, w1_ref):
    low = lax.broadcasted_iota(jnp.int32, (n_rows, LANES), 1) < HEAD_DIM

    def split_groups(a, b):
        return (jnp.where(low, a, pltpu.roll(b, HEAD_DIM, 1)), jnp.where(low, pltpu.roll(a, HEAD_DIM, 1), b))

    lead = None
    trail = None
    for u in range(CMP_STRIDE // CMP_PACK):
        xs = [load_rows(u * CMP_PACK + j) for j in range(CMP_PACK)]
        g01 = split_groups(xs[0], xs[1])
        g23 = split_groups(xs[2], xs[3])
        x = jnp.concatenate([jnp.concatenate([g01[0], g23[0]], axis=1),
                             jnp.concatenate([g01[1], g23[1]], axis=1)], axis=0)
        dl = _dot((x + pe_ref[0, u]).astype(BF), w1_ref[0, u])
        dt = _dot((x + pe_ref[1, u]).astype(BF), w1_ref[1, u])
        lead = dl if lead is None else lead + dl
        trail = dt if trail is None else trail + dt
    return lead, trail


def _compress_prompt_kernel(kc_ref, vc_ref, pek_ref, w1k_ref, w2k_ref, pev_ref, w1v_ref, w2v_ref, out_ref, *,
                            n_half):
    branches = ((kc_ref, pek_ref, w1k_ref, w2k_ref), (vc_ref, pev_ref, w1v_ref, w2v_ref))
    for c, (src_ref, pe_ref, w1_ref, w2_ref) in enumerate(branches):
        def load_rows(s, src_ref=src_ref):
            return src_ref[pl.ds(s, n_half, stride=CMP_STRIDE), :]

        lead, trail = _block_mlp_hidden(load_rows, n_half, pe_ref, w1_ref)
        for g in range(N_KV):
            a0 = lead[g * n_half:(g + 1) * n_half]
            a1 = trail[g * n_half:(g + 1) * n_half]
            pre = a0 + pltpu.roll(a1, n_half - 1, 0)
            tok = _dot(_silu(pre).astype(BF), w2_ref[...])
            out_ref[g, :, c * HEAD_DIM:(c + 1) * HEAD_DIM] = tok


def _compress_prompt(kvc2d, b, t, pek, w1k, w2k, pev, w1v, w2v):
    n_half = t // CMP_STRIDE
    full = lambda a: pl.BlockSpec(a.shape, lambda i: (0,) * a.ndim)
    return pl.pallas_call(
        functools.partial(_compress_prompt_kernel, n_half=n_half),
        grid=(b,),
        in_specs=[pl.BlockSpec((t, LANES), lambda i: (i, 0)), pl.BlockSpec((t, LANES), lambda i: (i, 1)),
                  full(pek), full(w1k), full(w2k), full(pev), full(w1v), full(w2v)],
        out_specs=pl.BlockSpec((None, N_KV, n_half, LANES), lambda i: (i, 0, 0, 0)),
        out_shape=jax.ShapeDtypeStruct((b, N_KV, n_half, LANES), F32),
        compiler_params=_cparams(("parallel",)),
        name="compress_prompt",
    )(kvc2d, kvc2d, pek, w1k, w2k, pev, w1v, w2v)


def _prep_cmp_w(pe, w1, w2):
    n_quads = CMP_STRIDE // CMP_PACK
    return (pe.reshape(2, n_quads, 1, CMP_PACK * HEAD_DIM),
            w1.reshape(2, n_quads, CMP_PACK * HEAD_DIM, CMP_HIDDEN).astype(BF), w2.astype(BF))


def _rank_select(imp_t):
    n_rows, tq = imp_t.shape
    jrow = lax.broadcasted_iota(jnp.int32, (n_rows, tq), 0)
    cnt = jnp.zeros((n_rows, tq), F32)
    for jp in range(n_rows):
        row = imp_t[jp:jp + 1, :]
        ahead = jnp.where(row == imp_t, jnp.where(jrow > jp, 1.0, 0.0), jnp.where(row > imp_t, 1.0, 0.0))
        cnt = cnt + ahead
    return jnp.where(cnt < float(N_SELECT), 1.0, 0.0)


def _heads_on_lanes(ref, g):
    rows = [(g * Q_PER_KV + r) * HEAD_DIM for r in range(Q_PER_KV)]
    return (jnp.concatenate([ref[c:c + HEAD_DIM, :] for c in rows], axis=1) * (SCALE * LOG2E)).astype(BF)


def _per_head(x):
    return jnp.concatenate([x] * Q_PER_KV, axis=1)


def _nsa_prompt_kernel(q_ref, qr_ref, cmp_ref, kvs_ref, kvw_ref, sm_ref, ovt_ref, o_ref, sel_ref, slc_ref, *, t_len,
                       tq, kc):
    t0 = pl.program_id(1) * tq
    hq = Q_PER_KV
    nq = hq * tq
    tcol = t0 + lax.broadcasted_iota(jnp.int32, (1, tq), 1)
    gates_t = jax.nn.sigmoid(sm_ref[...]).T
    tok = lax.broadcasted_iota(jnp.int32, (LANES, tq), 0)
    cmask = _per_head(jnp.where((tok * CMP_STRIDE + (2 * CMP_STRIDE - 1)) <= tcol, 1.0, 0.0)) > 0.5
    n_sel = t_len // SLC_BLOCK
    blk = lax.broadcasted_iota(jnp.int32, (n_sel, tq), 0)
    valid = (blk * SLC_BLOCK) <= tcol
    cur = tcol // SLC_BLOCK
    forced = (blk == 0) | (blk == cur) | (blk == cur - 1)
    krel = lax.broadcasted_iota(jnp.int32, (kc, tq), 0)
    n_chunks = (t0 + tq - 1) // kc + 1
    blocks_per_chunk = kc // SLC_BLOCK

    krows = [slice(g * HEAD_DIM, (g + 1) * HEAD_DIM) for g in range(N_KV)]
    vrows = [slice(LANES + g * HEAD_DIM, LANES + (g + 1) * HEAD_DIM) for g in range(N_KV)]

    o_cmp = []
    for g in range(N_KV):
        s = jnp.where(cmask, _dot(cmp_ref[g, :, 0:HEAD_DIM].astype(BF), _heads_on_lanes(q_ref, g)), NEG_BIG)
        e = jnp.where(cmask, jnp.exp2(s - jnp.max(s, axis=0, keepdims=True)), 0.0)
        d = jnp.sum(e, axis=0, keepdims=True)
        p = e / jnp.where(d > 0.0, d, 1.0)
        o_cmp.append(_dot_tn(cmp_ref[g, :, HEAD_DIM:LANES].astype(BF), p.astype(BF)))
        psum = p[:, 0:tq]
        for r in range(1, hq):
            psum = psum + p[:, r * tq:(r + 1) * tq]
        imp = _dot3_left(ovt_ref[...], psum)[0:n_sel, :]
        imp = jnp.where(valid, jnp.where(forced, FORCE_SCORE, imp), -FORCE_SCORE)
        sel_ref[g] = _rank_select(imp)

    qs = [_heads_on_lanes(qr_ref, g) for g in range(N_KV)]

    aug = 16
    blk_onehot = (lax.broadcasted_iota(jnp.int32, (aug, kc), 1) // SLC_BLOCK
                  == lax.broadcasted_iota(jnp.int32, (aug, kc), 0))
    blk_onehot = jnp.where(blk_onehot, 1.0, 0.0).astype(BF)
    pad_rows = jnp.zeros((aug - blocks_per_chunk, nq), F32)

    def chunk(ci, state, diagonal):
        k0 = ci * kc
        scores = []
        for g in range(N_KV):
            sel_c = sel_ref[g, ci * blocks_per_chunk:(ci + 1) * blocks_per_chunk, :]
            sel_bias = _per_head(jnp.where(sel_c > 0.5, 0.0, NEG_BIG))
            q_aug = jnp.concatenate([qs[g], jnp.concatenate([sel_bias, pad_rows], axis=0).astype(BF)], axis=0)
            k_aug = jnp.concatenate([kvs_ref[krows[g], k0:k0 + kc].astype(BF), blk_onehot], axis=0)
            scores.append(_dot_tn(k_aug, q_aug))
        causal_bias = _per_head(jnp.where(k0 + krel <= tcol, 0.0, NEG_BIG)) if diagonal else None
        out = []
        for g in range(N_KV):
            m, l, acc = state[g]
            vt = kvs_ref[vrows[g], k0:k0 + kc].astype(BF)
            s = scores[g] + causal_bias if diagonal else scores[g]
            m_new = jnp.maximum(m, jnp.max(s, axis=0, keepdims=True))
            alpha = jnp.exp2(m - m_new)
            e = jnp.exp2(s - m_new)
            l = alpha * l + jnp.sum(e, axis=0, keepdims=True)
            out.append((m_new, l, alpha * acc + _dot(vt, e.astype(BF))))
        return tuple(out)

    init = (jnp.full((1, nq), NEG_BIG, F32), jnp.zeros((1, nq), F32), jnp.zeros((HEAD_DIM, nq), F32))
    for count in range(1, t_len // kc + 1):
        @pl.when(n_chunks == count)
        def _(count=count):
            state = (init,) * N_KV
            for ci in range(count):
                state = chunk(ci, state, ci == count - 1)
            for g in range(N_KV):
                slc_ref[g] = state[g][2] / state[g][1]

    tw = LANES
    n_win = WINDOW + tw
    o_win_parts = [[] for _ in range(N_KV)]
    for sub in range(tq // tw):
        start = pl.multiple_of(jnp.clip(t0 + sub * tw - WINDOW, 0, t_len - n_win), LANES)
        dist = tcol[:, sub * tw:(sub + 1) * tw] - (start + lax.broadcasted_iota(jnp.int32, (n_win, tw), 0))
        wbias = _per_head(jnp.where(dist >= 0, jnp.where(dist < WINDOW, 0.0, NEG_BIG), NEG_BIG))
        qsub = [jnp.concatenate([qs[g][:, r * tq + sub * tw:r * tq + (sub + 1) * tw] for r in range(hq)], axis=1)
                for g in range(N_KV)]
        wscores = [_dot_tn(kvw_ref[krows[g], pl.ds(start, n_win)].astype(BF), qsub[g]) for g in range(N_KV)]
        for g in range(N_KV):
            s = wscores[g] + wbias
            e = jnp.exp2(s - jnp.max(s, axis=0, keepdims=True))
            o_win_parts[g].append(_dot(kvw_ref[vrows[g], pl.ds(start, n_win)].astype(BF), e.astype(BF))
                                  / jnp.sum(e, axis=0, keepdims=True))
    for g in range(N_KV):
        o_slc = slc_ref[g]
        o_win = jnp.concatenate([o_win_parts[g][sub][:, r * tw:(r + 1) * tw]
                                 for r in range(hq) for sub in range(tq // tw)], axis=1)
        for r in range(hq):
            h = g * hq + r
            cols = slice(r * tq, (r + 1) * tq)
            o_ref[h * HEAD_DIM:(h + 1) * HEAD_DIM, :] = (
                gates_t[3 * h:3 * h + 1, :] * o_cmp[g][:, cols] + gates_t[3 * h + 1:3 * h + 2, :] * o_slc[:, cols]
                + gates_t[3 * h + 2:3 * h + 3, :] * o_win[:, cols])


def _overlap_matrix(n_rows, n_cols, row_shift):
    n = jnp.arange(n_rows)[:, None] - row_shift
    c_start = n * CMP_STRIDE
    s_start = jnp.arange(n_cols)[None, :] * SLC_BLOCK
    ov = (c_start < s_start + SLC_BLOCK) & (c_start + 2 * CMP_STRIDE > s_start) & (n >= 0)
    return ov.astype(BF)


def _nsa_prompt(q_t, qr_t, cmp, kvs_t, kvw_t, small, b, t):
    tq = 256
    kc = 512
    nq = t // tq
    n_half = t // CMP_STRIDE
    n_sel = t // SLC_BLOCK
    assert n_half == LANES and n_sel <= LANES and t % kc == 0
    ov_t = _overlap_matrix(LANES, LANES, 0).T
    qtile = pl.BlockSpec((None, Q_DIM, tq), lambda i, j: (i, 0, j))
    seq = pl.BlockSpec((None, KV_DIM, t), lambda i, j: (i, 0, 0))
    return pl.pallas_call(
        functools.partial(_nsa_prompt_kernel, t_len=t, tq=tq, kc=kc),
        grid=(b, nq),
        in_specs=[qtile, qtile,
                  pl.BlockSpec((None, N_KV, n_half, LANES), lambda i, j: (i, 0, 0, 0)),
                  seq, seq, pl.BlockSpec((tq, LANES), lambda i, j: (i * nq + j, 0)),
                  pl.BlockSpec(ov_t.shape, lambda i, j: (0, 0))],
        out_specs=qtile,
        out_shape=jax.ShapeDtypeStruct((b, Q_DIM, t), F32),
        scratch_shapes=[pltpu.VMEM((N_KV, n_sel, tq), F32), pltpu.VMEM((N_KV, HEAD_DIM, Q_PER_KV * tq), F32)],
        compiler_params=_cparams(("parallel", "parallel")),
        name="nsa_prompt",
    )(q_t, qr_t, cmp, kvs_t, kvw_t, small, ov_t)


def _ssd_prompt_kernel(zx_ref, sm_ref, cw_ref, cb_ref, dtb_ref, alog_ref, dskip_ref, nrm_ref, tri_ref,
                       y_ref, hout_ref, tail_ref, h_ref, *, n_sub):
    lc = SSD_CHUNK

    @pl.when(pl.program_id(1) == 0)
    def _():
        tail_ref[...] = jnp.zeros_like(tail_ref)
        h_ref[...] = jnp.zeros_like(h_ref)

    li = lax.broadcasted_iota(jnp.int32, (lc, lc), 0)
    si = lax.broadcasted_iota(jnp.int32, (lc, lc), 1)
    lower = li >= si
    heads_per_group = SSD_HEADS // 2
    tail = tail_ref[...]
    state = [h_ref[h] for h in range(SSD_HEADS)]
    for sc in range(n_sub):
        rows = slice(sc * lc, (sc + 1) * lc)
        z = zx_ref[rows, 0:SSD_INNER]
        xbc = zx_ref[rows, SSD_INNER:]
        u = jnp.concatenate([tail, xbc], axis=0)
        tail = xbc[lc - 8:lc, :]
        conv = cb_ref[...]
        for k in range(CONV_WIDTH):
            conv = conv + u[5 + k:5 + k + lc, :] * cw_ref[k:k + 1, :]
        xbc_c = _silu(conv)
        xs = xbc_c[:, 0:SSD_INNER]

        dt_full = _softplus(sm_ref[rows, :] + dtb_ref[...])
        a_full = dt_full * (-jnp.exp(alog_ref[...]))
        acs = _dot3_left(tri_ref[...], a_full)
        acs_t = acs.T

        bm_b = [xbc_c[:, SSD_INNER + g * SSD_STATE:SSD_INNER + (g + 1) * SSD_STATE].astype(BF) for g in range(2)]
        cm_b = [xbc_c[:, SSD_INNER + (2 + g) * SSD_STATE:SSD_INNER + (3 + g) * SSD_STATE].astype(BF)
                for g in range(2)]
        cb = [_dot_nt(cm_b[g], bm_b[g]) for g in range(2)]
        prep = []
        for h in range(SSD_HEADS):
            col = DT_COL + h
            acs_h = acs[:, col:col + 1]
            acs_last = acs[lc - 1:lc, col:col + 1]
            lmat = jnp.where(lower, jnp.exp(acs_h - acs_t[col:col + 1, :]), 0.0)
            xdt = xs[:, h * SSD_HEAD_DIM:(h + 1) * SSD_HEAD_DIM] * dt_full[:, col:col + 1]
            prep.append(((cb[h // heads_per_group] * lmat).astype(BF), xdt.astype(BF),
                         (xdt * jnp.exp(acs_last - acs_h)).astype(BF), jnp.exp(acs_h), jnp.exp(acs_last)))
        ys = []
        for h in range(SSD_HEADS):
            g = h // heads_per_group
            m_h, xdt_b, xdec_b, grow, glast = prep[h]
            ys.append(_dot(m_h, xdt_b) + _dot_nt(cm_b[g], state[h].astype(BF)) * grow)
            state[h] = glast * state[h] + _dot_tn(xdec_b, bm_b[g])
        y = jnp.concatenate(ys, axis=1) + dskip_ref[...] * xs
        y = y * _silu(z)
        y_ref[rows, :] = _rms(y, nrm_ref[...])
    tail_ref[...] = tail
    for h in range(SSD_HEADS):
        h_ref[h] = state[h]
        hout_ref[h] = state[h]


def _ssd_params(conv_w, conv_b, dt_bias, a_log, d_skip, ssd_norm):
    pad = lambda v: jnp.pad(v, (DT_COL, LANES - DT_COL - SSD_HEADS)).reshape(1, LANES)
    return (conv_w, conv_b.reshape(1, -1), pad(dt_bias), pad(a_log),
            jnp.repeat(d_skip, SSD_HEAD_DIM).reshape(1, SSD_INNER), ssd_norm.reshape(1, SSD_INNER))


def _ssd_prompt(zx, small, params, b, t):
    lc = SSD_CHUNK
    n_sub = 4
    nc = t // (lc * n_sub)
    cw, cb, dtb, alog, dskip, nrm = params
    tri = (jnp.arange(lc)[:, None] >= jnp.arange(lc)[None, :]).astype(BF)
    zx_dim = zx.shape[1]
    conv_dim = zx_dim - SSD_INNER
    full = lambda a: pl.BlockSpec(a.shape, lambda i, j: (0,) * a.ndim)
    tile = lambda w: pl.BlockSpec((lc * n_sub, w), lambda i, j: (i * nc + j, 0))
    return pl.pallas_call(
        functools.partial(_ssd_prompt_kernel, n_sub=n_sub),
        grid=(b, nc),
        in_specs=[tile(zx_dim), tile(LANES), full(cw), full(cb), full(dtb), full(alog), full(dskip), full(nrm),
                  full(tri)],
        out_specs=[tile(SSD_INNER),
                   pl.BlockSpec((None, SSD_HEADS, SSD_HEAD_DIM, SSD_STATE), lambda i, j: (i, 0, 0, 0))],
        out_shape=[jax.ShapeDtypeStruct((b * t, SSD_INNER), F32),
                   jax.ShapeDtypeStruct((b, SSD_HEADS, SSD_HEAD_DIM, SSD_STATE), F32)],
        scratch_shapes=[pltpu.VMEM((8, conv_dim), F32), pltpu.VMEM((SSD_HEADS, SSD_HEAD_DIM, SSD_STATE), F32)],
        compiler_params=_cparams(("parallel", "arbitrary")),
        name="ssd_prompt",
    )(zx, small, cw, cb, dtb, alog, dskip, nrm, tri)


def _ffn_kernel(x_ref, *refs, final_norm, mixer_proj, a_feature_major):
    x = x_ref[...]
    if mixer_proj:
        a_ref, b_ref, wa_ref, wb_ref = refs[:4]
        refs = refs[4:]
        a = a_ref[...].astype(BF)
        x = x + (_dot_tn(a, wa_ref[...]) if a_feature_major else _dot(a, wa_ref[...]))
        x = x + _dot(b_ref[...].astype(BF), wb_ref[...])
    g_ref, wg_ref, wu_ref, wd_ref, gf_ref, o_ref = refs
    xn = _rms(x, g_ref[...]).astype(BF)
    h = _silu(_dot(xn, wg_ref[...])) * _dot(xn, wu_ref[...])
    y = x + _dot(h.astype(BF), wd_ref[...])
    if final_norm:
        y = _rms(y, gf_ref[...])
    o_ref[...] = y


def _ffn(x2d, g, wg, wu, wd, gf, tm, final_norm, in_map, out_map, out_2d_shape, mixer=None):
    m, d = x2d.shape[0] * x2d.shape[1] // wg.shape[0], wg.shape[0]
    n_steps = m // tm
    full = lambda arr: pl.BlockSpec(arr.shape, lambda i: (0, 0), pipeline_mode=pl.Buffered(1))
    args = [x2d]
    in_specs = [pl.BlockSpec((tm, d), in_map)]
    a_feature_major = mixer is not None and mixer[0].ndim == 3
    if mixer is not None:
        a, bb, wa, wb = mixer
        args += [a, bb, wa, wb]
        if a_feature_major:
            nt = a.shape[2] // tm
            a_spec = pl.BlockSpec((None, a.shape[1], tm), lambda i: (i // nt, 0, i % nt))
        else:
            a_spec = pl.BlockSpec((tm, a.shape[1]), in_map)
        in_specs += [a_spec, pl.BlockSpec((tm, bb.shape[1]), in_map), full(wa), full(wb)]
    args += [g, wg, wu, wd, gf]
    in_specs += [full(g), full(wg), full(wu), full(wd), full(gf)]
    return pl.pallas_call(
        functools.partial(_ffn_kernel, final_norm=final_norm, mixer_proj=mixer is not None,
                          a_feature_major=a_feature_major),
        grid=(n_steps,),
        in_specs=in_specs,
        out_specs=pl.BlockSpec((tm, d), out_map),
        out_shape=jax.ShapeDtypeStruct(out_2d_shape, F32),
        compiler_params=_cparams(("parallel",)),
        name="ffn_final" if final_norm else "ffn",
    )(*args)


def _lru_gates(xc, wa_ref, ba_ref, wx_ref, bx_ref, sp):
    n_heads = xc.shape[1] // LRU_BLOCK
    a_parts = []
    gx_parts = []
    for h in range(n_heads):
        sl = slice(h * LRU_BLOCK, (h + 1) * LRU_BLOCK)
        xh = xc[:, sl]
        xb = xh.astype(BF)
        r = jax.nn.sigmoid(_dot(xb, wa_ref[h]) + ba_ref[:, sl])
        i = jax.nn.sigmoid(_dot(xb, wx_ref[h]) + bx_ref[:, sl])
        log_a = -LRU_C * r * sp[:, sl]
        a = jnp.exp(log_a)
        a_parts.append(a)
        gx_parts.append(jnp.sqrt(1.0 - a * a) * (i * xh))
    return jnp.concatenate(a_parts, axis=1), jnp.concatenate(gx_parts, axis=1)


SUBLANES = 8


def _lru_prompt_kernel(x_ref, g_ref, win_ref, cw_ref, cb_ref, wa_ref, ba_ref, wx_ref, bx_ref, lam_ref, wo_ref,
                       o_ref, hout_ref, cout_ref, tail_ref, h_ref, a_s, gx_s, hs_s, u_s, *, rows, w):
    @pl.when(pl.program_id(1) == 0)
    def _():
        tail_ref[...] = jnp.zeros_like(tail_ref)
        h_ref[...] = jnp.zeros_like(h_ref)

    x = x_ref[...]
    xn = _rms(x, g_ref[...]).astype(BF)
    proj = _dot(xn, win_ref[...])
    gate_br = proj[:, 0:w]
    x_br = proj[:, w:2 * w]
    u_s[0:SUBLANES, :] = tail_ref[...]
    u_s[SUBLANES:SUBLANES + rows, :] = x_br
    tail_ref[...] = x_br[rows - SUBLANES:rows, :]
    cout_ref[...] = x_br[rows - (CONV_WIDTH - 1):rows, :]
    xc = cb_ref[...]
    for k in range(CONV_WIDTH):
        off = SUBLANES - (CONV_WIDTH - 1) + k
        xc = xc + u_s[off:off + rows, :] * cw_ref[k:k + 1, :]
    sp = _softplus(-lam_ref[...])
    a, gx = _lru_gates(xc, wa_ref, ba_ref, wx_ref, bx_ref, sp)
    a_s[...] = a
    gx_s[...] = gx
    sub = lax.broadcasted_iota(jnp.int32, (SUBLANES, w), 0)
    h = h_ref[...]
    for i in range(rows // SUBLANES):
        blk = slice(i * SUBLANES, (i + 1) * SUBLANES)
        ac = a_s[blk, :]
        bc = gx_s[blk, :]
        for dd in (1, 2, 4):
            a_sh = jnp.where(sub >= dd, pltpu.roll(ac, dd, 0), 1.0)
            b_sh = jnp.where(sub >= dd, pltpu.roll(bc, dd, 0), 0.0)
            bc = ac * b_sh + bc
            ac = ac * a_sh
        hb = ac * h + bc
        hs_s[blk, :] = hb
        h = jnp.broadcast_to(hb[SUBLANES - 1:SUBLANES, :], (SUBLANES, w))
    h_ref[...] = h
    hout_ref[...] = h[0:1, :]
    y = (jax.nn.gelu(gate_br) * hs_s[...]).astype(BF)
    o_ref[...] = x + _dot(y, wo_ref[...])


def _lru_prompt(x2d, g, w_in, cw, cb, wa, ba, wx, bx, lam, wo, nb, rows):
    m, d = x2d.shape
    w = cw.shape[1]
    nt = m // nb // rows
    full = lambda a: pl.BlockSpec(a.shape, lambda i, j: (0,) * a.ndim)
    tile = pl.BlockSpec((rows, d), lambda i, j: (i * nt + j, 0))
    args = (g, w_in, cw, cb, wa, ba, wx, bx, lam, wo)
    return pl.pallas_call(
        functools.partial(_lru_prompt_kernel, rows=rows, w=w),
        grid=(nb, nt),
        in_specs=[tile] + [full(a) for a in args],
        out_specs=[tile,
                   pl.BlockSpec((None, 1, w), lambda i, j: (i, 0, 0)),
                   pl.BlockSpec((None, CONV_WIDTH - 1, w), lambda i, j: (i, 0, 0))],
        out_shape=[jax.ShapeDtypeStruct((m, d), F32),
                   jax.ShapeDtypeStruct((nb, 1, w), F32),
                   jax.ShapeDtypeStruct((nb, CONV_WIDTH - 1, w), F32)],
        scratch_shapes=[pltpu.VMEM((SUBLANES, w), F32), pltpu.VMEM((SUBLANES, w), F32),
                        pltpu.VMEM((rows, w), F32), pltpu.VMEM((rows, w), F32), pltpu.VMEM((rows, w), F32),
                        pltpu.VMEM((SUBLANES + rows, w), F32)],
        compiler_params=_cparams(("parallel", "arbitrary")),
        name="lru_prompt",
    )(x2d, *args)


def _lru_params(norm_g, w_in_c, conv_w, conv_b, w_a, b_a, w_x, b_x, lam, w_out_c):
    r = lambda v: v.reshape(1, -1)
    return (r(norm_g), w_in_c.astype(BF), conv_w, r(conv_b), w_a.astype(BF), r(b_a), w_x.astype(BF), r(b_x),
            r(lam), w_out_c.astype(BF))


def _lru_decode_kernel(x_ref, g_ref, win_ref, cw_ref, cb_ref, wa_ref, ba_ref, wx_ref, bx_ref, lam_ref, wo_ref,
                       c0_ref, c1_ref, c2_ref, h0_ref, o_ref, hout_ref, xbr_ref, *, w):
    x = x_ref[...]
    xn = _rms(x, g_ref[...]).astype(BF)
    proj = _dot(xn, win_ref[...])
    gate_br = proj[:, 0:w]
    x_br = proj[:, w:2 * w]
    xbr_ref[...] = x_br
    xc = (cb_ref[...] + c0_ref[...] * cw_ref[0:1, :] + c1_ref[...] * cw_ref[1:2, :]
          + c2_ref[...] * cw_ref[2:3, :] + x_br * cw_ref[3:4, :])
    sp = _softplus(-lam_ref[...])
    a, gx = _lru_gates(xc, wa_ref, ba_ref, wx_ref, bx_ref, sp)
    h = a * h0_ref[...] + gx
    hout_ref[...] = h
    y = (jax.nn.gelu(gate_br) * h).astype(BF)
    o_ref[...] = x + _dot(y, wo_ref[...])


def _lru_decode(x, params, conv_state, h0):
    m, d = x.shape
    w = h0.shape[1]
    args = (x,) + tuple(params) + (conv_state[:, 0], conv_state[:, 1], conv_state[:, 2], h0)
    full = lambda a: pl.BlockSpec(a.shape, lambda i: (0,) * a.ndim)
    return pl.pallas_call(
        functools.partial(_lru_decode_kernel, w=w),
        grid=(1,),
        in_specs=[full(a) for a in args],
        out_specs=[pl.BlockSpec((m, d), lambda i: (0, 0)), pl.BlockSpec((m, w), lambda i: (0, 0)),
                   pl.BlockSpec((m, w), lambda i: (0, 0))],
        out_shape=[jax.ShapeDtypeStruct((m, d), F32), jax.ShapeDtypeStruct((m, w), F32),
                   jax.ShapeDtypeStruct((m, w), F32)],
        compiler_params=_cparams(("arbitrary",)),
        name="lru_decode",
    )(*args)


PAGES_PER_STEP = 32
HALVES_PER_PAGE = PAGE_SIZE // CMP_STRIDE


def _compress_decode_kernel(pt_ref, *refs):
    del pt_ref
    np_ = PAGES_PER_STEP
    k_pages = refs[0:np_]
    v_pages = refs[np_:2 * np_]
    pek_ref, w1k_ref, w2k_ref, pev_ref, w1v_ref, w2v_ref, out_ref, carry_ref, rows_ref = refs[2 * np_:]
    rows = np_ * HALVES_PER_PAGE

    @pl.when(pl.program_id(1) == 0)
    def _():
        carry_ref[...] = jnp.zeros_like(carry_ref)

    rowi = lax.broadcasted_iota(jnp.int32, (rows, CMP_HIDDEN), 0)
    branches = ((k_pages, pek_ref, w1k_ref, w2k_ref), (v_pages, pev_ref, w1v_ref, w2v_ref))
    for c, (pages, _, _, _) in enumerate(branches):
        for k, p in enumerate(pages):
            rows_ref[c, k * PAGE_SIZE:(k + 1) * PAGE_SIZE, :] = p[...].T

    for c, (pages, pe_ref, w1_ref, w2_ref) in enumerate(branches):
        def load_rows(s, c=c):
            return rows_ref[c, pl.ds(s, rows, stride=CMP_STRIDE), :]

        lead, trail = _block_mlp_hidden(load_rows, rows, pe_ref, w1_ref)
        for g in range(N_KV):
            a0 = lead[g * rows:(g + 1) * rows]
            a1 = trail[g * rows:(g + 1) * rows]
            slot = c * N_KV + g
            prev = jnp.where(rowi == 0, carry_ref[slot, 7:8, :], pltpu.roll(a0, 1, 0))
            carry_ref[slot] = a0[rows - 8:rows, :]
            tok = _dot(_silu(prev + a1).astype(BF), w2_ref[...])
            out_ref[g, :, c * HEAD_DIM:(c + 1) * HEAD_DIM] = tok


def _compress_decode(cache_t, pt_flat, db, n_pages, pek, w1k, w2k, pev, w1v, w2v):
    np_ = PAGES_PER_STEP
    n_steps = n_pages // np_
    rows = np_ * HALVES_PER_PAGE
    n_half = n_pages * HALVES_PER_PAGE

    def page_spec(k, rowblk):
        return pl.BlockSpec((None, LANES, PAGE_SIZE),
                            lambda i, j, pt: (pt[i * n_pages + j * np_ + k], rowblk, 0))

    full = lambda a: pl.BlockSpec(a.shape, lambda i, j, pt: (0,) * a.ndim)
    grid_spec = pltpu.PrefetchScalarGridSpec(
        num_scalar_prefetch=1,
        grid=(db, n_steps),
        in_specs=[page_spec(k, 0) for k in range(np_)] + [page_spec(k, 1) for k in range(np_)]
        + [full(pek), full(w1k), full(w2k), full(pev), full(w1v), full(w2v)],
        out_specs=pl.BlockSpec((None, N_KV, rows, LANES), lambda i, j, pt: (i, 0, j, 0)),
        scratch_shapes=[pltpu.VMEM((2 * N_KV, 8, CMP_HIDDEN), F32), pltpu.VMEM((2, np_ * PAGE_SIZE, LANES), F32)],
    )
    return pl.pallas_call(
        _compress_decode_kernel,
        grid_spec=grid_spec,
        out_shape=jax.ShapeDtypeStruct((db, N_KV, n_half, LANES), F32),
        compiler_params=_cparams(("parallel", "arbitrary")),
        name="compress_decode",
    )(pt_flat, *([cache_t] * (2 * np_)), pek, w1k, w2k, pev, w1v, w2v)


def _group_rows(top, a, b):
    return jnp.where(top, a, b)


def _nsa_decode_a_kernel(q_ref, qr_ref, cmp_ref, win_ref, new_ref, ov_ref, ocmp_ref, owin_ref, idx_ref, *,
                         pos, n_blocks):
    nh = N_HEADS
    top = lax.broadcasted_iota(jnp.int32, (nh, 1), 0) < Q_PER_KV
    q = q_ref[...].astype(BF)
    n_tok = cmp_ref.shape[1]
    lane = lax.broadcasted_iota(jnp.int32, (nh, n_tok), 1)
    cmask = (lane >= 1) & (((lane - 1) * CMP_STRIDE + (2 * CMP_STRIDE - 1)) <= pos)
    s = _group_rows(top, _dot_nt(q, cmp_ref[0, :, 0:HEAD_DIM].astype(BF)),
                    _dot_nt(q, cmp_ref[1, :, 0:HEAD_DIM].astype(BF))) * SCALE
    p = _msoftmax(s, cmask)
    pb = p.astype(BF)
    ocmp_ref[...] = _group_rows(top, _dot(pb, cmp_ref[0, :, HEAD_DIM:LANES].astype(BF)),
                                _dot(pb, cmp_ref[1, :, HEAD_DIM:LANES].astype(BF)))

    imp8 = _dot3(p, ov_ref[...])
    nj = ov_ref.shape[1]
    cur = pos // SLC_BLOCK
    j = lax.broadcasted_iota(jnp.int32, (1, nj), 1)
    valid = (j * SLC_BLOCK) <= pos
    forced = (j == 0) | (j == cur) | (j == cur - 1)
    ri = lax.broadcasted_iota(jnp.int32, (nj, nj), 0)
    ci = lax.broadcasted_iota(jnp.int32, (nj, nj), 1)
    k_lane = lax.broadcasted_iota(jnp.int32, (nj, LANES), 1).astype(F32)
    jvals = lax.broadcasted_iota(jnp.int32, (8, nj), 1).astype(F32).astype(BF)
    idx_rows = []
    for g in range(N_KV):
        imp = jnp.sum(imp8[g * Q_PER_KV:(g + 1) * Q_PER_KV, :], axis=0, keepdims=True)
        imp = jnp.where(valid & forced, FORCE_SCORE, imp)
        imp = jnp.where(valid, imp, -FORCE_SCORE)
        imp = jnp.where(j < n_blocks, imp, -3e38)
        impb = jnp.broadcast_to(imp, (nj, nj))
        col = jnp.sum(jnp.where(ri == ci, impb, 0.0), axis=1, keepdims=True)
        ahead = jnp.where(impb == col, jnp.where(ci < ri, 1.0, 0.0), jnp.where(impb > col, 1.0, 0.0))
        rank_col = jnp.sum(ahead, axis=1, keepdims=True)
        onehot = jnp.where(rank_col == k_lane, 1.0, 0.0).astype(BF)
        idx_rows.append(_dot(jvals, onehot)[0:1, :])
    idx = jnp.concatenate(idx_rows + [jnp.zeros((8 - N_KV, LANES), F32)], axis=0)
    idx_ref[...] = idx.astype(jnp.int32)

    qr = qr_ref[...]
    qrb = qr.astype(BF)
    n_win = win_ref.shape[1]
    s = _group_rows(top, _dot(qrb, win_ref[0:HEAD_DIM, :].astype(BF)),
                    _dot(qrb, win_ref[HEAD_DIM:LANES, :].astype(BF))) * SCALE
    wl = lax.broadcasted_iota(jnp.int32, (nh, n_win), 1)
    dist = n_win - wl
    wmask = (dist >= 0) & (dist < WINDOW)
    new = new_ref[...]
    knew = _group_rows(top, new[:, 0:HEAD_DIM], new[:, HEAD_DIM:LANES])
    vnew = _group_rows(top, new[:, LANES:LANES + HEAD_DIM], new[:, LANES + HEAD_DIM:KV_DIM])
    s_new = jnp.sum(qr * knew, axis=1, keepdims=True) * SCALE
    sm = jnp.where(wmask, s, NEG_BIG)
    m = jnp.maximum(jnp.max(sm, axis=1, keepdims=True), s_new)
    e = jnp.where(wmask, jnp.exp(sm - m), 0.0)
    e_new = jnp.exp(s_new - m)
    d = jnp.sum(e, axis=1, keepdims=True) + e_new
    eb = e.astype(BF)
    o = _group_rows(top, _dot_nt(eb, win_ref[LANES:LANES + HEAD_DIM, :].astype(BF)),
                    _dot_nt(eb, win_ref[LANES + HEAD_DIM:KV_DIM, :].astype(BF))) + e_new * vnew
    owin_ref[...] = o / d


def _nsa_decode_a(q3, qr3, cmp_s, win, kvw_new3, pos):
    db = q3.shape[0]
    n_tok = cmp_s.shape[2]
    n_blocks = -(-(pos + 1) // SLC_BLOCK)
    nj = -(-n_blocks // LANES) * LANES
    ov = _overlap_matrix(n_tok, nj, 1)
    per_b = lambda a: pl.BlockSpec((None,) + a.shape[1:], lambda i: (i,) + (0,) * (a.ndim - 1))
    head = pl.BlockSpec((None, N_HEADS, HEAD_DIM), lambda i: (i, 0, 0))
    return pl.pallas_call(
        functools.partial(_nsa_decode_a_kernel, pos=pos, n_blocks=n_blocks),
        grid=(db,),
        in_specs=[head, head, per_b(cmp_s), per_b(win), per_b(kvw_new3), pl.BlockSpec(ov.shape, lambda i: (0, 0))],
        out_specs=[head, head, pl.BlockSpec((None, 8, LANES), lambda i: (i, 0, 0))],
        out_shape=[jax.ShapeDtypeStruct((db, N_HEADS, HEAD_DIM), F32),
                   jax.ShapeDtypeStruct((db, N_HEADS, HEAD_DIM), F32),
                   jax.ShapeDtypeStruct((db, 8, LANES), jnp.int32)],
        compiler_params=_cparams(("parallel",)),
        name="nsa_decode_select",
    )(q3, qr3, cmp_s, win, kvw_new3, ov)


def _nsa_decode_b_kernel(idx_ref, pt_ref, qr_ref, *refs, n_past_blocks, cur):
    del pt_ref
    nb = N_KV * N_SELECT
    blk_refs = refs[0:nb]
    new_ref, ocmp_ref, owin_ref, sm_ref, o_ref = refs[nb:]
    b = pl.program_id(0)
    nh = N_HEADS
    top = lax.broadcasted_iota(jnp.int32, (nh, 1), 0) < Q_PER_KV
    qr = qr_ref[...]
    qrb = qr.astype(BF)
    per_page = PAGE_SIZE // SLC_BLOCK
    n_keys = N_SELECT * PAGE_SIZE
    lane = lax.broadcasted_iota(jnp.int32, (nh, n_keys), 1)
    lane_page = lane // PAGE_SIZE
    lane_sub = (lane // SLC_BLOCK) % per_page
    s_g = []
    v_g = []
    ok_g = []
    has_new = []
    for g in range(N_KV):
        blks = blk_refs[g * N_SELECT:(g + 1) * N_SELECT]
        kt = jnp.concatenate([r[g * HEAD_DIM:(g + 1) * HEAD_DIM, :] for r in blks], axis=1).astype(BF)
        v_g.append(jnp.concatenate([r[LANES + g * HEAD_DIM:LANES + (g + 1) * HEAD_DIM, :] for r in blks],
                                   axis=1).astype(BF))
        s_g.append(_dot(qrb, kt))
        ok = jnp.zeros((nh, n_keys), F32)
        new_sel = jnp.zeros((), F32)
        for k in range(N_SELECT):
            jk = idx_ref[(b * N_KV + g) * N_SELECT + k]
            hit = (lane_page == k) & (lane_sub == jk % per_page)
            ok = jnp.where(hit, jnp.where(jk < n_past_blocks, 1.0, 0.0), ok)
            new_sel = jnp.maximum(new_sel, jnp.where(jk == cur, 1.0, 0.0))
        ok_g.append(ok)
        has_new.append(new_sel)
    s = _group_rows(top, s_g[0], s_g[1]) * SCALE
    mask = _group_rows(top, ok_g[0], ok_g[1]) > 0.5
    new_on = _group_rows(top, has_new[0], has_new[1]) > 0.5
    new = new_ref[...]
    knew = _group_rows(top, new[:, 0:HEAD_DIM], new[:, HEAD_DIM:LANES])
    vnew = _group_rows(top, new[:, LANES:LANES + HEAD_DIM], new[:, LANES + HEAD_DIM:KV_DIM])
    s_new = jnp.where(new_on, jnp.sum(qr * knew, axis=1, keepdims=True) * SCALE, NEG_BIG)
    sm = jnp.where(mask, s, NEG_BIG)
    m = jnp.maximum(jnp.max(sm, axis=1, keepdims=True), s_new)
    e = jnp.where(mask, jnp.exp(sm - m), 0.0)
    e_new = jnp.where(new_on, jnp.exp(s_new - m), 0.0)
    d = jnp.sum(e, axis=1, keepdims=True) + e_new
    eb = e.astype(BF)
    o_slc = ((_group_rows(top, _dot_nt(eb, v_g[0]), _dot_nt(eb, v_g[1])) + e_new * vnew)
             / jnp.where(d > 0.0, d, 1.0))

    sig = jnp.broadcast_to(jax.nn.sigmoid(sm_ref[...]), (nh, LANES))
    hl = lax.broadcasted_iota(jnp.int32, (nh, LANES), 1)
    hr = lax.broadcasted_iota(jnp.int32, (nh, LANES), 0)

    def gate(br):
        return jnp.sum(jnp.where(hl == 3 * hr + br, sig, 0.0), axis=1, keepdims=True)

    o_ref[...] = gate(0) * ocmp_ref[...] + gate(1) * o_slc + gate(2) * owin_ref[...]


def _nsa_decode_b(idx_flat, pt_flat, qr3, slc_pages_t, kvs_new3, o_cmp, o_win, small3, pos, n_pages):
    db = qr3.shape[0]
    n_past_blocks = pos // SLC_BLOCK
    cur = pos // SLC_BLOCK
    per_page = PAGE_SIZE // SLC_BLOCK

    def blk_spec(g, k):
        def imap(i, idx, pt):
            jk = jnp.minimum(idx[(i * N_KV + g) * N_SELECT + k], n_past_blocks - 1)
            return (pt[i * n_pages + jk // per_page], 0, 0)
        return pl.BlockSpec((None, KV_DIM, PAGE_SIZE), imap)

    head = pl.BlockSpec((None, N_HEADS, HEAD_DIM), lambda i, idx, pt: (i, 0, 0))
    row3 = lambda a: pl.BlockSpec((None,) + a.shape[1:], lambda i, idx, pt: (i, 0, 0))
    grid_spec = pltpu.PrefetchScalarGridSpec(
        num_scalar_prefetch=2,
        grid=(db,),
        in_specs=[head] + [blk_spec(g, k) for g in range(N_KV) for k in range(N_SELECT)]
        + [row3(kvs_new3), head, head, row3(small3)],
        out_specs=head,
    )
    return pl.pallas_call(
        functools.partial(_nsa_decode_b_kernel, n_past_blocks=n_past_blocks, cur=cur),
        grid_spec=grid_spec,
        out_shape=jax.ShapeDtypeStruct((db, N_HEADS, HEAD_DIM), F32),
        compiler_params=_cparams(("arbitrary",)),
        name="nsa_decode_attend",
    )(idx_flat, pt_flat, qr3, *([slc_pages_t] * (N_KV * N_SELECT)), kvs_new3, o_cmp, o_win, small3)


def _ssd_decode_kernel(zx_ref, cst_ref, sm_ref, h0_ref, cw_ref, cb_ref, dtb_ref, alog_ref, dskip_ref, nrm_ref,
                       y_ref, hout_ref):
    z = zx_ref[:, 0:SSD_INNER]
    xbc = zx_ref[:, SSD_INNER:]
    conv = cb_ref[...] + xbc * cw_ref[CONV_WIDTH - 1:CONV_WIDTH, :]
    for k in range(CONV_WIDTH - 1):
        conv = conv + cst_ref[k:k + 1, :] * cw_ref[k:k + 1, :]
    xbc_c = _silu(conv)
    xs = xbc_c[:, 0:SSD_INNER]
    dt_full = _softplus(sm_ref[...] + dtb_ref[...])
    da_full = jnp.exp(dt_full * (-jnp.exp(alog_ref[...])))
    p = SSD_HEAD_DIM
    eye = lax.broadcasted_iota(jnp.int32, (p, p), 0) == lax.broadcasted_iota(jnp.int32, (p, p), 1)
    ys = []
    for h in range(SSD_HEADS):
        g = h // (SSD_HEADS // 2)
        col = DT_COL + h
        xdt = xs[:, h * p:(h + 1) * p] * dt_full[:, col:col + 1]
        xcol = jnp.sum(jnp.where(eye, jnp.broadcast_to(xdt, (p, p)), 0.0), axis=1, keepdims=True)
        bm = xbc_c[:, SSD_INNER + g * SSD_STATE:SSD_INNER + (g + 1) * SSD_STATE]
        cm = xbc_c[:, SSD_INNER + (2 + g) * SSD_STATE:SSD_INNER + (3 + g) * SSD_STATE]
        h_new = da_full[:, col:col + 1] * h0_ref[h] + xcol * bm
        hout_ref[h] = h_new
        ys.append(_dot_nt(jnp.broadcast_to(cm, (8, SSD_STATE)).astype(BF), h_new.astype(BF))[0:1, :])
    y = jnp.concatenate(ys, axis=1) + dskip_ref[...] * xs
    y = y * _silu(z)
    y_ref[...] = _rms(y, nrm_ref[...])


def _ssd_decode(zx3, conv_state, small3, h0, params):
    db = zx3.shape[0]
    cw, cb, dtb, alog, dskip, nrm = params
    per_b = lambda a: pl.BlockSpec((None,) + a.shape[1:], lambda i: (i,) + (0,) * (a.ndim - 1))
    full = lambda a: pl.BlockSpec(a.shape, lambda i: (0,) * a.ndim)
    return pl.pallas_call(
        _ssd_decode_kernel,
        grid=(db,),
        in_specs=[per_b(zx3), per_b(conv_state), per_b(small3), per_b(h0), full(cw), full(cb), full(dtb), full(alog),
                  full(dskip), full(nrm)],
        out_specs=[pl.BlockSpec((None, 1, SSD_INNER), lambda i: (i, 0, 0)), per_b(h0)],
        out_shape=[jax.ShapeDtypeStruct((db, 1, SSD_INNER), F32), jax.ShapeDtypeStruct(h0.shape, F32)],
        compiler_params=_cparams(("parallel",)),
        name="ssd_decode",
    )(zx3, conv_state, small3, h0, cw, cb, dtb, alog, dskip, nrm)


def kernel(x_prompt, x_sample, cache_kv_cmp, cache_kv_slc, cache_kv_win, state_ssm, state_ssd_conv, state_lru,
           state_lru_conv, page_table, norm_mix, norm_ffn, norm_final, w_ffn_gate, w_ffn_up, w_ffn_down, w_in_a,
           w_out_a, cmp_pe_k, cmp_w1_k, cmp_w2_k, cmp_pe_v, cmp_w1_v, cmp_w2_v, ssd_conv_w, ssd_conv_b, ssd_dt_bias,
           ssd_a_log, ssd_d, ssd_norm, w_in_c, lru_conv_w, lru_conv_b, lru_w_a, lru_b_a, lru_w_x, lru_b_x,
           lru_lambda, w_out_c):
    b, t, d = x_prompt.shape
    db = x_sample.shape[0]
    n_pages = page_table.shape[1]
    pos_s = n_pages * PAGE_SIZE
    m = b * t
    kv_shape = (2, N_KV, HEAD_DIM)

    wm, ws = _prep_w_in_a(w_in_a[0])
    wo_nsa = w_out_a[0, :Q_DIM].astype(BF)
    wo_ssd = w_out_a[0, Q_DIM:].astype(BF)
    cmp_k = _prep_cmp_w(cmp_pe_k[0], cmp_w1_k[0], cmp_w2_k[0])
    cmp_v = _prep_cmp_w(cmp_pe_v[0], cmp_w1_v[0], cmp_w2_v[0])
    ssd_par = _ssd_params(ssd_conv_w[0], ssd_conv_b[0], ssd_dt_bias[0], ssd_a_log[0], ssd_d[0], ssd_norm[0])
    lru_par = _lru_params(norm_mix[1], w_in_c[0], lru_conv_w[0], lru_conv_b[0], lru_w_a[0], lru_b_a[0], lru_w_x[0],
                          lru_b_x[0], lru_lambda[0], w_out_c[0])
    ffn_w = [(norm_ffn[l].reshape(1, d), w_ffn_gate[l].astype(BF), w_ffn_up[l].astype(BF), w_ffn_down[l].astype(BF))
             for l in range(2)]
    gfin = norm_final.reshape(1, d)

    tm_proj = 512
    xp = x_prompt.reshape(m, d)
    cos_p, sin_p = _rope_tables(jnp.arange(t, dtype=jnp.int32))
    q, qr, kvc, kvs_t, kvw_t, zx, small, kvc_t = _inproj_a(xp, norm_mix[0], wm, ws, cos_p, sin_p, tm_proj, b, True)
    cmp_p = _compress_prompt(kvc, b, t, *cmp_k, *cmp_v)
    o_nsa = _nsa_prompt(q, qr, cmp_p, kvs_t, kvw_t, small, b, t)
    y_ssd, ssm_p = _ssd_prompt(zx, small, ssd_par, b, t)
    tm_ffn = 512
    rowmap = lambda i: (i, 0)
    x2 = _ffn(xp, *ffn_w[0], gfin, tm_ffn, False, rowmap, rowmap, (m, d), mixer=(o_nsa, y_ssd, wo_nsa, wo_ssd))
    x3, lru_p, lru_conv_p = _lru_prompt(x2, *lru_par, b, 512)
    y_prompt = _ffn(x3, *ffn_w[1], gfin, tm_ffn, True, rowmap, rowmap, (m, d))
    w_keep = min(WINDOW, t)
    to_cache = lambda a: a.reshape((b,) + kv_shape + (a.shape[-1],)).transpose(0, 4, 1, 2, 3)[None]
    kv_cmp_p = to_cache(kvc_t)
    kv_slc_p = to_cache(kvs_t)
    kv_win_p = to_cache(kvw_t[:, :, t - w_keep:])
    ssd_conv_p = zx.reshape(b, t, -1)[:, t - (CONV_WIDTH - 1):, SSD_INNER:][None]
    lru_p = lru_p.reshape(b, -1)
    lru_conv_p = lru_conv_p[None]

    xs = x_sample.reshape(db, d)
    cos_s, sin_s = _rope_tables(jnp.full((db,), pos_s, dtype=jnp.int32))
    q_s, qr_s, kvc_s, kvs_s, kvw_s, zx_s, small_s = _inproj_a(xs, norm_mix[0], wm, ws, cos_s, sin_s, db, 1, False)
    pt_flat = page_table.reshape(-1)
    n_phys = cache_kv_cmp.shape[1]
    feature_major = lambda a: a.transpose(0, 2, 3, 4, 1).reshape(a.shape[0], KV_DIM, a.shape[1])
    cmp_s = _compress_decode(feature_major(cache_kv_cmp[0]), pt_flat, db, n_pages, *cmp_k, *cmp_v)
    win_buf = cache_kv_win[0].reshape(db, -1, KV_DIM)
    head3 = lambda a: a.reshape(db, N_HEADS, HEAD_DIM)
    o_cmp_s, o_win_s, idx = _nsa_decode_a(head3(q_s), head3(qr_s), cmp_s, feature_major(cache_kv_win[0]),
                                          kvw_s.reshape(db, 1, KV_DIM), pos_s)
    idx_flat = idx[:, :N_KV, :N_SELECT].reshape(-1)
    o_nsa_s = _nsa_decode_b(idx_flat, pt_flat, head3(qr_s), feature_major(cache_kv_slc[0]),
                            kvs_s.reshape(db, 1, KV_DIM), o_cmp_s, o_win_s, small_s.reshape(db, 1, LANES), pos_s,
                            n_pages)
    y_ssd_s, ssm_s = _ssd_decode(zx_s.reshape(db, 1, -1), state_ssd_conv[0], small_s.reshape(db, 1, LANES),
                                 state_ssm[0], ssd_par)
    x2_s = _ffn(xs, *ffn_w[0], gfin, db, False, lambda i: (i, 0), lambda i: (i, 0), (db, d),
                mixer=(o_nsa_s.reshape(db, Q_DIM), y_ssd_s.reshape(db, SSD_INNER), wo_nsa, wo_ssd))
    x3_s, lru_s, xbr_s = _lru_decode(x2_s, lru_par, state_lru_conv[0], state_lru[0])
    y_sample = _ffn(x3_s, *ffn_w[1], gfin, db, True, lambda i: (i, 0), lambda i: (i, 0), (db, d))
    kv_cmp_s = kvc_s.reshape((1, db, 1) + kv_shape)
    kv_slc_s = kvs_s.reshape((1, db, 1) + kv_shape)
    kv_win_s = jnp.concatenate([win_buf[:, 1:], kvw_s[:, None, :]], axis=1).reshape(
        (1, db, win_buf.shape[1]) + kv_shape)
    ssd_conv_s = jnp.concatenate([state_ssd_conv[0][:, 1:], zx_s[:, None, SSD_INNER:]], axis=1)[None]
    lru_conv_s = jnp.concatenate([state_lru_conv[0][:, 1:], xbr_s[:, None, :]], axis=1)[None]

    return (y_prompt.reshape(b, t, d), y_sample.reshape(db, 1, d),
            kv_cmp_p, kv_slc_p, kv_win_p, ssm_p[None], ssd_conv_p, lru_p[None], lru_conv_p,
            kv_cmp_s, kv_slc_s, kv_win_s, ssm_s[None], ssd_conv_s, lru_s[None], lru_conv_s)
```

```python
import functools
import math

import jax
import jax.numpy as jnp
from jax import lax
from jax.experimental import pallas as pl
from jax.experimental.pallas import tpu as pltpu

BF = jnp.bfloat16
F32 = jnp.float32

HEAD_DIM = 64
N_KV = 2
Q_PER_KV = 4
N_HEADS = N_KV * Q_PER_KV
CMP_STRIDE = 16
CMP_HIDDEN = 256
SLC_BLOCK = 64
N_SELECT = 16
WINDOW = 512
PAGE_SIZE = 128
ROPE_THETA = 10000.0
FORCE_SCORE = 1e9
NEG_BIG = -1e30
SSD_HEADS = 8
SSD_HEAD_DIM = 64
SSD_STATE = 128
SSD_INNER = SSD_HEADS * SSD_HEAD_DIM
SSD_CHUNK = 128
CONV_WIDTH = 4
LRU_BLOCK = 128
LRU_C = 8.0
RMS_EPS = 1e-6
SCALE = HEAD_DIM ** -0.5
LOG2E = math.log2(math.e)
Q_DIM = N_HEADS * HEAD_DIM
KV_DIM = 2 * N_KV * HEAD_DIM
GATE_DIM = 3 * N_HEADS
DT_COL = GATE_DIM
N_BRANCH = 3
LANES = 128
SUBLANES = 8
BF16_SUBLANES = 16
F32_LOWEST = -3e38
VMEM_LIMIT = 56 * 1024 * 1024

TILE_PROJ = 512
TILE_FFN = 512
TILE_LRU = 512
TILE_NSA_Q = 256
TILE_NSA_K = 512
SSD_CHUNKS_PER_STEP = 4
PAGES_PER_STEP = 64


def _cparams(sem):
    return pltpu.CompilerParams(dimension_semantics=sem, vmem_limit_bytes=VMEM_LIMIT)


def _dot(a, b):
    return jnp.dot(a, b, preferred_element_type=F32)


def _dot_nt(a, b):
    return lax.dot_general(a, b, (((1,), (1,)), ((), ())), preferred_element_type=F32)


def _dot_tn(a, b):
    return lax.dot_general(a, b, (((0,), (0,)), ((), ())), preferred_element_type=F32)


def _split3(x):
    hi = x.astype(BF)
    r = x - hi.astype(F32)
    mid = r.astype(BF)
    lo = (r - mid.astype(F32)).astype(BF)
    return hi, mid, lo


def _dot3(x, m01):
    hi, mid, lo = _split3(x)
    return _dot(hi, m01) + _dot(mid, m01) + _dot(lo, m01)


def _dot3_left(m01, x):
    hi, mid, lo = _split3(x)
    return _dot(m01, hi) + _dot(m01, mid) + _dot(m01, lo)


def _rms(x, g):
    y = x * lax.rsqrt(jnp.mean(x * x, axis=-1, keepdims=True) + RMS_EPS)
    return y * g


def _silu(x):
    return x * jax.nn.sigmoid(x)


def _softplus(x):
    return jnp.maximum(x, 0.0) + jnp.log1p(jnp.exp(-jnp.abs(x)))


def _msoftmax_parts(s, mask):
    s = jnp.where(mask, s, NEG_BIG)
    m = jnp.max(s, axis=-1, keepdims=True)
    e = jnp.where(mask, jnp.exp(s - m), 0.0)
    return m, e


def _msoftmax(s, mask):
    _, e = _msoftmax_parts(s, mask)
    d = jnp.sum(e, axis=-1, keepdims=True)
    return e / jnp.where(d > 0.0, d, 1.0)


def _rope_tables(pos):
    half = HEAD_DIM // 2
    inv_freq = ROPE_THETA ** (-jnp.arange(half, dtype=F32) / half)
    ang = pos.astype(F32)[:, None] * inv_freq[None, :]
    cos = jnp.cos(ang)
    sin = jnp.sin(ang)
    cos2 = jnp.tile(jnp.concatenate([cos, cos], axis=-1), (1, LANES // HEAD_DIM))
    sin2 = jnp.tile(jnp.concatenate([-sin, sin], axis=-1), (1, LANES // HEAD_DIM))
    return cos2, sin2


def _inproj_a_kernel(x_ref, g_ref, wm_ref, ws_ref, cos_ref, sin_ref,
                     q_ref, qr_ref, kvc_ref, kvs_ref, kvw_ref, zx_ref, sm_ref, *kvc_t_ref, feature_major):
    xn = _rms(x_ref[...], g_ref[...]).astype(BF)
    cos = cos_ref[...]
    sin = sin_ref[...]
    lane = lax.broadcasted_iota(jnp.int32, cos.shape, 1)
    first = (lane % HEAD_DIM) < (HEAD_DIM // 2)

    def rope(v):
        rot = jnp.where(first, pltpu.roll(v, LANES - HEAD_DIM // 2, 1), pltpu.roll(v, HEAD_DIM // 2, 1))
        return v * cos + rot * sin

    def store_kv(ref, k, v):
        if feature_major:
            ref[0:LANES, :] = k.T
            ref[LANES:KV_DIM, :] = v.T
        else:
            ref[:, 0:LANES] = k
            ref[:, LANES:KV_DIM] = v

    q = _dot(xn, wm_ref[:, 0:Q_DIM])
    for c in range(Q_DIM // LANES):
        cols = slice(c * LANES, (c + 1) * LANES)
        if feature_major:
            q_ref[cols, :] = q[:, cols].T
            qr_ref[cols, :] = rope(q[:, cols]).T
        else:
            q_ref[:, cols] = q[:, cols]
            qr_ref[:, cols] = rope(q[:, cols])
    o = Q_DIM
    kvc = _dot(xn, wm_ref[:, o:o + KV_DIM])
    kvc_ref[...] = kvc
    if feature_major:
        store_kv(kvc_t_ref[0], kvc[:, 0:LANES], kvc[:, LANES:KV_DIM])
    o += KV_DIM
    kvs = _dot(xn, wm_ref[:, o:o + KV_DIM])
    store_kv(kvs_ref, rope(kvs[:, 0:LANES]), kvs[:, LANES:KV_DIM])
    o += KV_DIM
    kvw = _dot(xn, wm_ref[:, o:o + KV_DIM])
    store_kv(kvw_ref, rope(kvw[:, 0:LANES]), kvw[:, LANES:KV_DIM])
    o += KV_DIM
    zx_ref[...] = _dot(xn, wm_ref[:, o:])
    sm_ref[...] = _dot(xn, ws_ref[...])


def _prep_w_in_a(w_in_a):
    a = Q_DIM + 3 * KV_DIM
    gate = w_in_a[:, a:a + GATE_DIM]
    rest = w_in_a[:, a + GATE_DIM:]
    zx_w = rest[:, :rest.shape[1] - SSD_HEADS]
    dt = rest[:, rest.shape[1] - SSD_HEADS:]
    main = jnp.concatenate([w_in_a[:, :a], zx_w], axis=1).astype(BF)
    small = jnp.concatenate([gate, dt], axis=1)
    small = jnp.pad(small, ((0, 0), (0, LANES - small.shape[1]))).astype(BF)
    return main, small


def _inproj_a(x2d, g, wm, ws, cos2, sin2, tm, nb, feature_major):
    m, d = x2d.shape
    nt = m // nb // tm
    zx_dim = wm.shape[1] - Q_DIM - 3 * KV_DIM
    row = lambda w: pl.BlockSpec((tm, w), lambda i, j: (i * nt + j, 0))
    row_shape = lambda w: jax.ShapeDtypeStruct((m, w), F32)
    const = lambda i, j: (0, 0)
    if feature_major:
        fm = lambda w: pl.BlockSpec((None, w, tm), lambda i, j: (i, 0, j))
        fm_shape = lambda w: jax.ShapeDtypeStruct((nb, w, m // nb), F32)
    else:
        fm, fm_shape = row, row_shape
    kv, kv_shape = fm(KV_DIM), fm_shape(KV_DIM)
    out_specs = [fm(Q_DIM), fm(Q_DIM), row(KV_DIM), kv, kv, row(zx_dim), row(LANES)]
    out_shape = [fm_shape(Q_DIM), fm_shape(Q_DIM), row_shape(KV_DIM), kv_shape, kv_shape, row_shape(zx_dim),
                 row_shape(LANES)]
    if feature_major:
        out_specs.append(kv)
        out_shape.append(kv_shape)
    pos = pl.BlockSpec((tm, LANES), lambda i, j: (j, 0))
    return pl.pallas_call(
        functools.partial(_inproj_a_kernel, feature_major=feature_major),
        grid=(nb, nt),
        in_specs=[row(d), pl.BlockSpec((1, d), const), pl.BlockSpec(wm.shape, const), pl.BlockSpec(ws.shape, const),
                  pos, pos],
        out_specs=out_specs,
        out_shape=out_shape,
        compiler_params=_cparams(("parallel", "parallel")),
        name="inproj_a",
    )(x2d, g.reshape(1, d), wm, ws, cos2, sin2)


CMP_PACK = 2 * LANES // HEAD_DIM


def _block_mlp_hidden(load_rows, n_rows, pe_ref, w1_ref):
    low = lax.broadcasted_iota(jnp.int32, (n_rows, LANES), 1) < HEAD_DIM

    def split_groups(a, b):
        return (jnp.where(low, a, pltpu.roll(b, HEAD_DIM, 1)), jnp.where(low, pltpu.roll(a, HEAD_DIM, 1), b))

    lead = None
    trail = None
    for u in range(CMP_STRIDE // CMP_PACK):
        xs = [load_rows(u * CMP_PACK + j) for j in range(CMP_PACK)]
        g01 = split_groups(xs[0], xs[1])
        g23 = split_groups(xs[2], xs[3])
        x = jnp.concatenate([jnp.concatenate([g01[0], g23[0]], axis=1),
                             jnp.concatenate([g01[1], g23[1]], axis=1)], axis=0)
        dl = _dot((x + pe_ref[0, u]).astype(BF), w1_ref[0, u])
        dt = _dot((x + pe_ref[1, u]).astype(BF), w1_ref[1, u])
        lead = dl if lead is None else lead + dl
        trail = dt if trail is None else trail + dt
    return lead, trail


def _compress_prompt_kernel(kc_ref, vc_ref, pek_ref, w1k_ref, w2k_ref, pev_ref, w1v_ref, w2v_ref, out_ref, *,
                            n_half):
    branches = ((kc_ref, pek_ref, w1k_ref, w2k_ref), (vc_ref, pev_ref, w1v_ref, w2v_ref))
    for c, (src_ref, pe_ref, w1_ref, w2_ref) in enumerate(branches):
        def load_rows(s, src_ref=src_ref):
            return src_ref[pl.ds(s, n_half, stride=CMP_STRIDE), :]

        lead, trail = _block_mlp_hidden(load_rows, n_half, pe_ref, w1_ref)
        for g in range(N_KV):
            a0 = lead[g * n_half:(g + 1) * n_half]
            a1 = trail[g * n_half:(g + 1) * n_half]
            pre = a0 + pltpu.roll(a1, n_half - 1, 0)
            tok = _dot(_silu(pre).astype(BF), w2_ref[...])
            out_ref[g, :, c * HEAD_DIM:(c + 1) * HEAD_DIM] = tok


def _compress_prompt(kvc2d, b, t, pek, w1k, w2k, pev, w1v, w2v):
    n_half = t // CMP_STRIDE
    full = lambda a: pl.BlockSpec(a.shape, lambda i: (0,) * a.ndim)
    return pl.pallas_call(
        functools.partial(_compress_prompt_kernel, n_half=n_half),
        grid=(b,),
        in_specs=[pl.BlockSpec((t, LANES), lambda i: (i, 0)), pl.BlockSpec((t, LANES), lambda i: (i, 1)),
                  full(pek), full(w1k), full(w2k), full(pev), full(w1v), full(w2v)],
        out_specs=pl.BlockSpec((None, N_KV, n_half, LANES), lambda i: (i, 0, 0, 0)),
        out_shape=jax.ShapeDtypeStruct((b, N_KV, n_half, LANES), F32),
        compiler_params=_cparams(("parallel",)),
        name="compress_prompt",
    )(kvc2d, kvc2d, pek, w1k, w2k, pev, w1v, w2v)


def _prep_cmp_w(pe, w1, w2):
    n_quads = CMP_STRIDE // CMP_PACK
    return (pe.reshape(2, n_quads, 1, CMP_PACK * HEAD_DIM),
            w1.reshape(2, n_quads, CMP_PACK * HEAD_DIM, CMP_HIDDEN).astype(BF), w2.astype(BF))


def _rank_select(imp_t):
    n_rows, tq = imp_t.shape
    jrow = lax.broadcasted_iota(jnp.int32, (n_rows, tq), 0)
    cnt = jnp.zeros((n_rows, tq), F32)
    for jp in range(n_rows):
        row = imp_t[jp:jp + 1, :]
        ahead = jnp.where(row == imp_t, jnp.where(jrow > jp, 1.0, 0.0), jnp.where(row > imp_t, 1.0, 0.0))
        cnt = cnt + ahead
    return jnp.where(cnt < float(N_SELECT), 1.0, 0.0)


def _heads_on_lanes(ref, g):
    rows = [(g * Q_PER_KV + r) * HEAD_DIM for r in range(Q_PER_KV)]
    return (jnp.concatenate([ref[c:c + HEAD_DIM, :] for c in rows], axis=1) * (SCALE * LOG2E)).astype(BF)


def _per_head(x):
    return jnp.concatenate([x] * Q_PER_KV, axis=1)


def _nsa_prompt_kernel(q_ref, qr_ref, cmp_ref, kvs_ref, kvw_ref, sm_ref, ovt_ref, o_ref, sel_ref, slc_ref, *, t_len,
                       tq, kc):
    t0 = pl.program_id(1) * tq
    hq = Q_PER_KV
    nq = hq * tq
    tcol = t0 + lax.broadcasted_iota(jnp.int32, (1, tq), 1)
    gates_t = jax.nn.sigmoid(sm_ref[...]).T
    tok = lax.broadcasted_iota(jnp.int32, (LANES, tq), 0)
    cmask = _per_head(jnp.where((tok * CMP_STRIDE + (2 * CMP_STRIDE - 1)) <= tcol, 1.0, 0.0)) > 0.5
    n_sel = t_len // SLC_BLOCK
    blk = lax.broadcasted_iota(jnp.int32, (n_sel, tq), 0)
    valid = (blk * SLC_BLOCK) <= tcol
    cur = tcol // SLC_BLOCK
    forced = (blk == 0) | (blk == cur) | (blk == cur - 1)
    krel = lax.broadcasted_iota(jnp.int32, (kc, tq), 0)
    blocks_per_chunk = kc // SLC_BLOCK

    krows = [slice(g * HEAD_DIM, (g + 1) * HEAD_DIM) for g in range(N_KV)]
    vrows = [slice(LANES + g * HEAD_DIM, LANES + (g + 1) * HEAD_DIM) for g in range(N_KV)]

    o_cmp = []
    for g in range(N_KV):
        s = jnp.where(cmask, _dot(cmp_ref[g, :, 0:HEAD_DIM].astype(BF), _heads_on_lanes(q_ref, g)), NEG_BIG)
        e = jnp.where(cmask, jnp.exp2(s - jnp.max(s, axis=0, keepdims=True)), 0.0)
        d = jnp.sum(e, axis=0, keepdims=True)
        p = e / jnp.where(d > 0.0, d, 1.0)
        o_cmp.append(_dot_tn(cmp_ref[g, :, HEAD_DIM:LANES].astype(BF), p.astype(BF)))
        psum = p[:, 0:tq]
        for r in range(1, hq):
            psum = psum + p[:, r * tq:(r + 1) * tq]
        imp = _dot3_left(ovt_ref[...], psum)[0:n_sel, :]
        imp = jnp.where(valid, jnp.where(forced, FORCE_SCORE, imp), -FORCE_SCORE)
        sel_ref[g] = _rank_select(imp)

    qs = [_heads_on_lanes(qr_ref, g) for g in range(N_KV)]

    aug = BF16_SUBLANES
    blk_onehot = (lax.broadcasted_iota(jnp.int32, (aug, kc), 1) // SLC_BLOCK
                  == lax.broadcasted_iota(jnp.int32, (aug, kc), 0))
    blk_onehot = jnp.where(blk_onehot, 1.0, 0.0).astype(BF)

    pad_rows = jnp.zeros((aug - blocks_per_chunk, nq), F32)

    def chunk(ci, state, diagonal):
        k0 = ci * kc
        scores = []
        for g in range(N_KV):
            sel_c = sel_ref[g, ci * blocks_per_chunk:(ci + 1) * blocks_per_chunk, :]
            sel_bias = _per_head(jnp.where(sel_c > 0.5, 0.0, NEG_BIG))
            q_aug = jnp.concatenate([qs[g], jnp.concatenate([sel_bias, pad_rows], axis=0).astype(BF)], axis=0)
            k_aug = jnp.concatenate([kvs_ref[krows[g], k0:k0 + kc].astype(BF), blk_onehot], axis=0)
            scores.append(_dot_tn(k_aug, q_aug))
        causal_bias = _per_head(jnp.where(k0 + krel <= tcol, 0.0, NEG_BIG)) if diagonal else None
        out = []
        for g in range(N_KV):
            m, l, acc = state[g]
            vt = kvs_ref[vrows[g], k0:k0 + kc].astype(BF)
            s = scores[g] + causal_bias if diagonal else scores[g]
            m_new = jnp.maximum(m, jnp.max(s, axis=0, keepdims=True))
            alpha = jnp.exp2(m - m_new)
            e = jnp.exp2(s - m_new)
            l = alpha * l + jnp.sum(e, axis=0, keepdims=True)
            out.append((m_new, l, alpha * acc + _dot(vt, e.astype(BF))))
        return tuple(out)

    init = (jnp.full((1, nq), NEG_BIG, F32), jnp.zeros((1, nq), F32), jnp.zeros((HEAD_DIM, nq), F32))
    n_chunks = (t0 + tq - 1) // kc + 1
    for count in range(1, t_len // kc + 1):
        @pl.when(n_chunks == count)
        def _(count=count):
            state = (init,) * N_KV
            for ci in range(count):
                state = chunk(ci, state, ci == count - 1)
            for g in range(N_KV):
                slc_ref[g] = state[g][2] / state[g][1]

    tw = LANES
    n_win = WINDOW + tw
    o_win_parts = [[] for _ in range(N_KV)]
    for sub in range(tq // tw):
        start = pl.multiple_of(jnp.clip(t0 + sub * tw - WINDOW, 0, t_len - n_win), LANES)
        dist = tcol[:, sub * tw:(sub + 1) * tw] - (start + lax.broadcasted_iota(jnp.int32, (n_win, tw), 0))
        wbias = _per_head(jnp.where(dist >= 0, jnp.where(dist < WINDOW, 0.0, NEG_BIG), NEG_BIG))
        qsub = [jnp.concatenate([qs[g][:, r * tq + sub * tw:r * tq + (sub + 1) * tw] for r in range(hq)], axis=1)
                for g in range(N_KV)]
        wscores = [_dot_tn(kvw_ref[krows[g], pl.ds(start, n_win)].astype(BF), qsub[g]) for g in range(N_KV)]
        for g in range(N_KV):
            s = wscores[g] + wbias
            e = jnp.exp2(s - jnp.max(s, axis=0, keepdims=True))
            o_win_parts[g].append(_dot(kvw_ref[vrows[g], pl.ds(start, n_win)].astype(BF), e.astype(BF))
                                  / jnp.sum(e, axis=0, keepdims=True))
    def gate(h, branch):
        row = N_BRANCH * h + branch
        return gates_t[row:row + 1, :]

    for g in range(N_KV):
        o_slc = slc_ref[g]
        o_win = jnp.concatenate([o_win_parts[g][sub][:, r * tw:(r + 1) * tw]
                                 for r in range(hq) for sub in range(tq // tw)], axis=1)
        for r in range(hq):
            h = g * hq + r
            cols = slice(r * tq, (r + 1) * tq)
            o_ref[h * HEAD_DIM:(h + 1) * HEAD_DIM, :] = (
                gate(h, 0) * o_cmp[g][:, cols] + gate(h, 1) * o_slc[:, cols] + gate(h, 2) * o_win[:, cols])


def _overlap_matrix(n_rows, n_cols, row_shift):
    n = jnp.arange(n_rows)[:, None] - row_shift
    c_start = n * CMP_STRIDE
    s_start = jnp.arange(n_cols)[None, :] * SLC_BLOCK
    ov = (c_start < s_start + SLC_BLOCK) & (c_start + 2 * CMP_STRIDE > s_start) & (n >= 0)
    return ov.astype(BF)


def _nsa_prompt(q_t, qr_t, cmp, kvs_t, kvw_t, small, b, t):
    tq = TILE_NSA_Q
    kc = TILE_NSA_K
    nq = t // tq
    n_half = t // CMP_STRIDE
    n_sel = t // SLC_BLOCK
    assert n_half == LANES and n_sel <= LANES and t % kc == 0 and kc % tq == 0 and tq % LANES == 0
    ov_t = _overlap_matrix(LANES, LANES, 0).T
    qtile = pl.BlockSpec((None, Q_DIM, tq), lambda i, j: (i, 0, j))
    seq = pl.BlockSpec((None, KV_DIM, t), lambda i, j: (i, 0, 0))
    return pl.pallas_call(
        functools.partial(_nsa_prompt_kernel, t_len=t, tq=tq, kc=kc),
        grid=(b, nq),
        in_specs=[qtile, qtile,
                  pl.BlockSpec((None, N_KV, n_half, LANES), lambda i, j: (i, 0, 0, 0)),
                  seq, seq, pl.BlockSpec((tq, LANES), lambda i, j: (i * nq + j, 0)),
                  pl.BlockSpec(ov_t.shape, lambda i, j: (0, 0))],
        out_specs=qtile,
        out_shape=jax.ShapeDtypeStruct((b, Q_DIM, t), F32),
        scratch_shapes=[pltpu.VMEM((N_KV, n_sel, tq), F32), pltpu.VMEM((N_KV, HEAD_DIM, Q_PER_KV * tq), F32)],
        compiler_params=_cparams(("parallel", "parallel")),
        name="nsa_prompt",
    )(q_t, qr_t, cmp, kvs_t, kvw_t, small, ov_t)


def _ssd_prompt_kernel(zx_ref, sm_ref, cw_ref, cb_ref, dtb_ref, alog_ref, dskip_ref, nrm_ref, tri_ref,
                       y_ref, hout_ref, tail_ref, h_ref, *, n_sub):
    lc = SSD_CHUNK

    @pl.when(pl.program_id(1) == 0)
    def _():
        tail_ref[...] = jnp.zeros_like(tail_ref)
        h_ref[...] = jnp.zeros_like(h_ref)

    li = lax.broadcasted_iota(jnp.int32, (lc, lc), 0)
    si = lax.broadcasted_iota(jnp.int32, (lc, lc), 1)
    lower = li >= si
    heads_per_group = SSD_HEADS // 2
    tail = tail_ref[...]
    state = [h_ref[h] for h in range(SSD_HEADS)]
    for sc in range(n_sub):
        rows = slice(sc * lc, (sc + 1) * lc)
        z = zx_ref[rows, 0:SSD_INNER]
        xbc = zx_ref[rows, SSD_INNER:]
        u = jnp.concatenate([tail, xbc], axis=0)
        tail = xbc[lc - SUBLANES:lc, :]
        conv = cb_ref[...]
        for k in range(CONV_WIDTH):
            off = SUBLANES - (CONV_WIDTH - 1) + k
            conv = conv + u[off:off + lc, :] * cw_ref[k:k + 1, :]
        xbc_c = _silu(conv)
        xs = xbc_c[:, 0:SSD_INNER]

        dt_full = _softplus(sm_ref[rows, :] + dtb_ref[...])
        a_full = dt_full * (-jnp.exp(alog_ref[...]))
        acs = _dot3_left(tri_ref[...], a_full)
        acs_t = acs.T

        bm_b = [xbc_c[:, SSD_INNER + g * SSD_STATE:SSD_INNER + (g + 1) * SSD_STATE].astype(BF) for g in range(2)]
        cm_b = [xbc_c[:, SSD_INNER + (2 + g) * SSD_STATE:SSD_INNER + (3 + g) * SSD_STATE].astype(BF)
                for g in range(2)]
        cb = [_dot_nt(cm_b[g], bm_b[g]) for g in range(2)]
        prep = []
        for h in range(SSD_HEADS):
            col = DT_COL + h
            acs_h = acs[:, col:col + 1]
            acs_last = acs[lc - 1:lc, col:col + 1]
            lmat = jnp.where(lower, jnp.exp(acs_h - acs_t[col:col + 1, :]), 0.0)
            xdt = xs[:, h * SSD_HEAD_DIM:(h + 1) * SSD_HEAD_DIM] * dt_full[:, col:col + 1]
            prep.append(((cb[h // heads_per_group] * lmat).astype(BF), xdt.astype(BF),
                         (xdt * jnp.exp(acs_last - acs_h)).astype(BF), jnp.exp(acs_h), jnp.exp(acs_last)))
        ys = []
        for h in range(SSD_HEADS):
            g = h // heads_per_group
            m_h, xdt_b, xdec_b, grow, glast = prep[h]
            ys.append(_dot(m_h, xdt_b) + _dot_nt(cm_b[g], state[h].astype(BF)) * grow)
            state[h] = glast * state[h] + _dot_tn(xdec_b, bm_b[g])
        y = jnp.concatenate(ys, axis=1) + dskip_ref[...] * xs
        y = y * _silu(z)
        y_ref[rows, :] = _rms(y, nrm_ref[...])
    tail_ref[...] = tail
    for h in range(SSD_HEADS):
        h_ref[h] = state[h]
        hout_ref[h] = state[h]


def _ssd_params(conv_w, conv_b, dt_bias, a_log, d_skip, ssd_norm):
    pad = lambda v: jnp.pad(v, (DT_COL, LANES - DT_COL - SSD_HEADS)).reshape(1, LANES)
    return (conv_w, conv_b.reshape(1, -1), pad(dt_bias), pad(a_log),
            jnp.repeat(d_skip, SSD_HEAD_DIM).reshape(1, SSD_INNER), ssd_norm.reshape(1, SSD_INNER))


def _ssd_prompt(zx, small, params, b, t):
    lc = SSD_CHUNK
    n_sub = SSD_CHUNKS_PER_STEP
    nc = t // (lc * n_sub)
    cw, cb, dtb, alog, dskip, nrm = params
    tri = (jnp.arange(lc)[:, None] >= jnp.arange(lc)[None, :]).astype(BF)
    zx_dim = zx.shape[1]
    conv_dim = zx_dim - SSD_INNER
    full = lambda a: pl.BlockSpec(a.shape, lambda i, j: (0,) * a.ndim)
    tile = lambda w: pl.BlockSpec((lc * n_sub, w), lambda i, j: (i * nc + j, 0))
    return pl.pallas_call(
        functools.partial(_ssd_prompt_kernel, n_sub=n_sub),
        grid=(b, nc),
        in_specs=[tile(zx_dim), tile(LANES), full(cw), full(cb), full(dtb), full(alog), full(dskip), full(nrm),
                  full(tri)],
        out_specs=[tile(SSD_INNER),
                   pl.BlockSpec((None, SSD_HEADS, SSD_HEAD_DIM, SSD_STATE), lambda i, j: (i, 0, 0, 0))],
        out_shape=[jax.ShapeDtypeStruct((b * t, SSD_INNER), F32),
                   jax.ShapeDtypeStruct((b, SSD_HEADS, SSD_HEAD_DIM, SSD_STATE), F32)],
        scratch_shapes=[pltpu.VMEM((SUBLANES, conv_dim), F32),
                        pltpu.VMEM((SSD_HEADS, SSD_HEAD_DIM, SSD_STATE), F32)],
        compiler_params=_cparams(("parallel", "arbitrary")),
        name="ssd_prompt",
    )(zx, small, cw, cb, dtb, alog, dskip, nrm, tri)


def _ffn_kernel(x_ref, *refs, final_norm, mixer_proj, a_feature_major):
    x = x_ref[...]
    if mixer_proj:
        a_ref, b_ref, wa_ref, wb_ref = refs[:4]
        refs = refs[4:]
        a = a_ref[...].astype(BF)
        x = x + (_dot_tn(a, wa_ref[...]) if a_feature_major else _dot(a, wa_ref[...]))
        x = x + _dot(b_ref[...].astype(BF), wb_ref[...])
    g_ref, wg_ref, wu_ref, wd_ref, gf_ref, o_ref = refs
    xn = _rms(x, g_ref[...]).astype(BF)
    h = _silu(_dot(xn, wg_ref[...])) * _dot(xn, wu_ref[...])
    y = x + _dot(h.astype(BF), wd_ref[...])
    if final_norm:
        y = _rms(y, gf_ref[...])
    o_ref[...] = y


def _ffn(x2d, g, wg, wu, wd, gf, tm, final_norm, in_map, out_map, out_2d_shape, mixer=None):
    m, d = x2d.shape[0] * x2d.shape[1] // wg.shape[0], wg.shape[0]
    n_steps = m // tm
    full = lambda arr: pl.BlockSpec(arr.shape, lambda i: (0, 0), pipeline_mode=pl.Buffered(1))
    args = [x2d]
    in_specs = [pl.BlockSpec((tm, d), in_map)]
    a_feature_major = mixer is not None and mixer[0].ndim == 3
    if mixer is not None:
        a, bb, wa, wb = mixer
        args += [a, bb, wa, wb]
        if a_feature_major:
            nt = a.shape[2] // tm
            a_spec = pl.BlockSpec((None, a.shape[1], tm), lambda i: (i // nt, 0, i % nt))
        else:
            a_spec = pl.BlockSpec((tm, a.shape[1]), in_map)
        in_specs += [a_spec, pl.BlockSpec((tm, bb.shape[1]), in_map), full(wa), full(wb)]
    args += [g, wg, wu, wd, gf]
    in_specs += [full(g), full(wg), full(wu), full(wd), full(gf)]
    return pl.pallas_call(
        functools.partial(_ffn_kernel, final_norm=final_norm, mixer_proj=mixer is not None,
                          a_feature_major=a_feature_major),
        grid=(n_steps,),
        in_specs=in_specs,
        out_specs=pl.BlockSpec((tm, d), out_map),
        out_shape=jax.ShapeDtypeStruct(out_2d_shape, F32),
        compiler_params=_cparams(("parallel",)),
        name="ffn_final" if final_norm else "ffn",
    )(*args)


def _lru_gates(xc, wa_ref, ba_ref, wx_ref, bx_ref, sp):
    n_heads = xc.shape[1] // LRU_BLOCK
    a_parts = []
    gx_parts = []
    for h in range(n_heads):
        sl = slice(h * LRU_BLOCK, (h + 1) * LRU_BLOCK)
        xh = xc[:, sl]
        xb = xh.astype(BF)
        r = jax.nn.sigmoid(_dot(xb, wa_ref[h]) + ba_ref[:, sl])
        i = jax.nn.sigmoid(_dot(xb, wx_ref[h]) + bx_ref[:, sl])
        log_a = -LRU_C * r * sp[:, sl]
        a = jnp.exp(log_a)
        a_parts.append(a)
        gx_parts.append(jnp.sqrt(1.0 - a * a) * (i * xh))
    return jnp.concatenate(a_parts, axis=1), jnp.concatenate(gx_parts, axis=1)


def _lru_prompt_kernel(x_ref, g_ref, win_ref, cw_ref, cb_ref, wa_ref, ba_ref, wx_ref, bx_ref, lam_ref, wo_ref,
                       o_ref, hout_ref, cout_ref, tail_ref, h_ref, a_s, gx_s, hs_s, u_s, *, rows, w):
    @pl.when(pl.program_id(1) == 0)
    def _():
        tail_ref[...] = jnp.zeros_like(tail_ref)
        h_ref[...] = jnp.zeros_like(h_ref)

    x = x_ref[...]
    xn = _rms(x, g_ref[...]).astype(BF)
    proj = _dot(xn, win_ref[...])
    gate_br = proj[:, 0:w]
    x_br = proj[:, w:2 * w]
    u_s[0:SUBLANES, :] = tail_ref[...]
    u_s[SUBLANES:SUBLANES + rows, :] = x_br
    tail_ref[...] = x_br[rows - SUBLANES:rows, :]
    cout_ref[...] = x_br[rows - (CONV_WIDTH - 1):rows, :]
    xc = cb_ref[...]
    for k in range(CONV_WIDTH):
        off = SUBLANES - (CONV_WIDTH - 1) + k
        xc = xc + u_s[off:off + rows, :] * cw_ref[k:k + 1, :]
    sp = _softplus(-lam_ref[...])
    a, gx = _lru_gates(xc, wa_ref, ba_ref, wx_ref, bx_ref, sp)
    a_s[...] = a
    gx_s[...] = gx
    sub = lax.broadcasted_iota(jnp.int32, (SUBLANES, w), 0)
    h = h_ref[...]
    for i in range(rows // SUBLANES):
        blk = slice(i * SUBLANES, (i + 1) * SUBLANES)
        ac = a_s[blk, :]
        bc = gx_s[blk, :]
        for dd in (1, 2, 4):
            a_sh = jnp.where(sub >= dd, pltpu.roll(ac, dd, 0), 1.0)
            b_sh = jnp.where(sub >= dd, pltpu.roll(bc, dd, 0), 0.0)
            bc = ac * b_sh + bc
            ac = ac * a_sh
        hb = ac * h + bc
        hs_s[blk, :] = hb
        h = jnp.broadcast_to(hb[SUBLANES - 1:SUBLANES, :], (SUBLANES, w))
    h_ref[...] = h
    hout_ref[...] = h[0:1, :]
    y = (jax.nn.gelu(gate_br) * hs_s[...]).astype(BF)
    o_ref[...] = x + _dot(y, wo_ref[...])


def _lru_prompt(x2d, g, w_in, cw, cb, wa, ba, wx, bx, lam, wo, nb, rows):
    m, d = x2d.shape
    w = cw.shape[1]
    nt = m // nb // rows
    full = lambda a: pl.BlockSpec(a.shape, lambda i, j: (0,) * a.ndim)
    tile = pl.BlockSpec((rows, d), lambda i, j: (i * nt + j, 0))
    args = (g, w_in, cw, cb, wa, ba, wx, bx, lam, wo)
    return pl.pallas_call(
        functools.partial(_lru_prompt_kernel, rows=rows, w=w),
        grid=(nb, nt),
        in_specs=[tile] + [full(a) for a in args],
        out_specs=[tile,
                   pl.BlockSpec((None, 1, w), lambda i, j: (i, 0, 0)),
                   pl.BlockSpec((None, CONV_WIDTH - 1, w), lambda i, j: (i, 0, 0))],
        out_shape=[jax.ShapeDtypeStruct((m, d), F32),
                   jax.ShapeDtypeStruct((nb, 1, w), F32),
                   jax.ShapeDtypeStruct((nb, CONV_WIDTH - 1, w), F32)],
        scratch_shapes=[pltpu.VMEM((SUBLANES, w), F32), pltpu.VMEM((SUBLANES, w), F32),
                        pltpu.VMEM((rows, w), F32), pltpu.VMEM((rows, w), F32), pltpu.VMEM((rows, w), F32),
                        pltpu.VMEM((SUBLANES + rows, w), F32)],
        compiler_params=_cparams(("parallel", "arbitrary")),
        name="lru_prompt",
    )(x2d, *args)


def _lru_params(norm_g, w_in_c, conv_w, conv_b, w_a, b_a, w_x, b_x, lam, w_out_c):
    r = lambda v: v.reshape(1, -1)
    return (r(norm_g), w_in_c.astype(BF), conv_w, r(conv_b), w_a.astype(BF), r(b_a), w_x.astype(BF), r(b_x),
            r(lam), w_out_c.astype(BF))


def _lru_decode_kernel(x_ref, g_ref, win_ref, cw_ref, cb_ref, wa_ref, ba_ref, wx_ref, bx_ref, lam_ref, wo_ref,
                       c0_ref, c1_ref, c2_ref, h0_ref, o_ref, hout_ref, xbr_ref, *, w):
    x = x_ref[...]
    xn = _rms(x, g_ref[...]).astype(BF)
    proj = _dot(xn, win_ref[...])
    gate_br = proj[:, 0:w]
    x_br = proj[:, w:2 * w]
    xbr_ref[...] = x_br
    xc = (cb_ref[...] + c0_ref[...] * cw_ref[0:1, :] + c1_ref[...] * cw_ref[1:2, :]
          + c2_ref[...] * cw_ref[2:3, :] + x_br * cw_ref[3:4, :])
    sp = _softplus(-lam_ref[...])
    a, gx = _lru_gates(xc, wa_ref, ba_ref, wx_ref, bx_ref, sp)
    h = a * h0_ref[...] + gx
    hout_ref[...] = h
    y = (jax.nn.gelu(gate_br) * h).astype(BF)
    o_ref[...] = x + _dot(y, wo_ref[...])


def _lru_decode(x, params, conv_state, h0):
    m, d = x.shape
    w = h0.shape[1]
    args = (x,) + tuple(params) + (conv_state[:, 0], conv_state[:, 1], conv_state[:, 2], h0)
    full = lambda a: pl.BlockSpec(a.shape, lambda i: (0,) * a.ndim)
    return pl.pallas_call(
        functools.partial(_lru_decode_kernel, w=w),
        grid=(1,),
        in_specs=[full(a) for a in args],
        out_specs=[pl.BlockSpec((m, d), lambda i: (0, 0)), pl.BlockSpec((m, w), lambda i: (0, 0)),
                   pl.BlockSpec((m, w), lambda i: (0, 0))],
        out_shape=[jax.ShapeDtypeStruct((m, d), F32), jax.ShapeDtypeStruct((m, w), F32),
                   jax.ShapeDtypeStruct((m, w), F32)],
        compiler_params=_cparams(("arbitrary",)),
        name="lru_decode",
    )(*args)


HALVES_PER_PAGE = PAGE_SIZE // CMP_STRIDE


def _compress_decode_kernel(pt_ref, *refs):
    del pt_ref
    np_ = PAGES_PER_STEP
    k_pages = refs[0:np_]
    v_pages = refs[np_:2 * np_]
    pek_ref, w1k_ref, w2k_ref, pev_ref, w1v_ref, w2v_ref, out_ref, carry_ref, rows_ref = refs[2 * np_:]
    rows = np_ * HALVES_PER_PAGE

    @pl.when(pl.program_id(1) == 0)
    def _():
        carry_ref[...] = jnp.zeros_like(carry_ref)

    rowi = lax.broadcasted_iota(jnp.int32, (rows, CMP_HIDDEN), 0)
    branches = ((k_pages, pek_ref, w1k_ref, w2k_ref), (v_pages, pev_ref, w1v_ref, w2v_ref))
    for c, (pages, _, _, _) in enumerate(branches):
        for k, p in enumerate(pages):
            rows_ref[c, k * PAGE_SIZE:(k + 1) * PAGE_SIZE, :] = p[...].T

    for c, (pages, pe_ref, w1_ref, w2_ref) in enumerate(branches):
        def load_rows(s, c=c):
            return rows_ref[c, pl.ds(s, rows, stride=CMP_STRIDE), :]

        lead, trail = _block_mlp_hidden(load_rows, rows, pe_ref, w1_ref)
        for g in range(N_KV):
            a0 = lead[g * rows:(g + 1) * rows]
            a1 = trail[g * rows:(g + 1) * rows]
            slot = c * N_KV + g
            prev = jnp.where(rowi == 0, carry_ref[slot, SUBLANES - 1:SUBLANES, :], pltpu.roll(a0, 1, 0))
            carry_ref[slot] = a0[rows - SUBLANES:rows, :]
            tok = _dot(_silu(prev + a1).astype(BF), w2_ref[...])
            out_ref[g, :, c * HEAD_DIM:(c + 1) * HEAD_DIM] = tok


def _compress_decode(cache_t, pt_flat, db, n_pages, pek, w1k, w2k, pev, w1v, w2v):
    np_ = PAGES_PER_STEP
    n_steps = n_pages // np_
    rows = np_ * HALVES_PER_PAGE
    n_half = n_pages * HALVES_PER_PAGE

    def page_spec(k, rowblk):
        return pl.BlockSpec((None, LANES, PAGE_SIZE),
                            lambda i, j, pt: (pt[i * n_pages + j * np_ + k], rowblk, 0))

    full = lambda a: pl.BlockSpec(a.shape, lambda i, j, pt: (0,) * a.ndim)
    grid_spec = pltpu.PrefetchScalarGridSpec(
        num_scalar_prefetch=1,
        grid=(db, n_steps),
        in_specs=[page_spec(k, 0) for k in range(np_)] + [page_spec(k, 1) for k in range(np_)]
        + [full(pek), full(w1k), full(w2k), full(pev), full(w1v), full(w2v)],
        out_specs=pl.BlockSpec((None, N_KV, rows, LANES), lambda i, j, pt: (i, 0, j, 0)),
        scratch_shapes=[pltpu.VMEM((2 * N_KV, SUBLANES, CMP_HIDDEN), F32),
                        pltpu.VMEM((2, np_ * PAGE_SIZE, LANES), F32)],
    )
    return pl.pallas_call(
        _compress_decode_kernel,
        grid_spec=grid_spec,
        out_shape=jax.ShapeDtypeStruct((db, N_KV, n_half, LANES), F32),
        compiler_params=_cparams(("parallel", "arbitrary")),
        name="compress_decode",
    )(pt_flat, *([cache_t] * (2 * np_)), pek, w1k, w2k, pev, w1v, w2v)


def _group_rows(top, a, b):
    return jnp.where(top, a, b)


def _nsa_decode_a_kernel(q_ref, qr_ref, cmp_ref, win_ref, new_ref, ov_ref, ocmp_ref, owin_ref, idx_ref, *,
                         pos, n_blocks):
    nh = N_HEADS
    top = lax.broadcasted_iota(jnp.int32, (nh, 1), 0) < Q_PER_KV
    q = q_ref[...].astype(BF)
    n_tok = cmp_ref.shape[1]
    lane = lax.broadcasted_iota(jnp.int32, (nh, n_tok), 1)
    cmask = (lane >= 1) & (((lane - 1) * CMP_STRIDE + (2 * CMP_STRIDE - 1)) <= pos)
    s = _group_rows(top, _dot_nt(q, cmp_ref[0, :, 0:HEAD_DIM].astype(BF)),
                    _dot_nt(q, cmp_ref[1, :, 0:HEAD_DIM].astype(BF))) * SCALE
    p = _msoftmax(s, cmask)
    pb = p.astype(BF)
    ocmp_ref[...] = _group_rows(top, _dot(pb, cmp_ref[0, :, HEAD_DIM:LANES].astype(BF)),
                                _dot(pb, cmp_ref[1, :, HEAD_DIM:LANES].astype(BF)))

    imp8 = _dot3(p, ov_ref[...])
    nj = ov_ref.shape[1]
    cur = pos // SLC_BLOCK
    j = lax.broadcasted_iota(jnp.int32, (1, nj), 1)
    valid = (j * SLC_BLOCK) <= pos
    forced = (j == 0) | (j == cur) | (j == cur - 1)
    ri = lax.broadcasted_iota(jnp.int32, (nj, nj), 0)
    ci = lax.broadcasted_iota(jnp.int32, (nj, nj), 1)
    k_lane = lax.broadcasted_iota(jnp.int32, (nj, LANES), 1).astype(F32)
    jvals = lax.broadcasted_iota(jnp.int32, (8, nj), 1).astype(F32).astype(BF)
    idx_rows = []
    for g in range(N_KV):
        imp = jnp.sum(imp8[g * Q_PER_KV:(g + 1) * Q_PER_KV, :], axis=0, keepdims=True)
        imp = jnp.where(valid & forced, FORCE_SCORE, imp)
        imp = jnp.where(valid, imp, -FORCE_SCORE)
        imp = jnp.where(j < n_blocks, imp, F32_LOWEST)
        impb = jnp.broadcast_to(imp, (nj, nj))
        col = jnp.sum(jnp.where(ri == ci, impb, 0.0), axis=1, keepdims=True)
        ahead = jnp.where(impb == col, jnp.where(ci < ri, 1.0, 0.0), jnp.where(impb > col, 1.0, 0.0))
        rank_col = jnp.sum(ahead, axis=1, keepdims=True)
        onehot = jnp.where(rank_col == k_lane, 1.0, 0.0).astype(BF)
        idx_rows.append(_dot(jvals, onehot)[0:1, :])
    idx = jnp.concatenate(idx_rows + [jnp.zeros((SUBLANES - N_KV, LANES), F32)], axis=0)
    idx_ref[...] = idx.astype(jnp.int32)

    qr = qr_ref[...]
    qrb = qr.astype(BF)
    n_win = win_ref.shape[1]
    s = _group_rows(top, _dot(qrb, win_ref[0:HEAD_DIM, :].astype(BF)),
                    _dot(qrb, win_ref[HEAD_DIM:LANES, :].astype(BF))) * SCALE
    wl = lax.broadcasted_iota(jnp.int32, (nh, n_win), 1)
    dist = n_win - wl
    wmask = (dist >= 0) & (dist < WINDOW)
    new = new_ref[...]
    knew = _group_rows(top, new[:, 0:HEAD_DIM], new[:, HEAD_DIM:LANES])
    vnew = _group_rows(top, new[:, LANES:LANES + HEAD_DIM], new[:, LANES + HEAD_DIM:KV_DIM])
    s_new = jnp.sum(qr * knew, axis=1, keepdims=True) * SCALE
    sm = jnp.where(wmask, s, NEG_BIG)
    m = jnp.maximum(jnp.max(sm, axis=1, keepdims=True), s_new)
    e = jnp.where(wmask, jnp.exp(sm - m), 0.0)
    e_new = jnp.exp(s_new - m)
    d = jnp.sum(e, axis=1, keepdims=True) + e_new
    eb = e.astype(BF)
    o = _group_rows(top, _dot_nt(eb, win_ref[LANES:LANES + HEAD_DIM, :].astype(BF)),
                    _dot_nt(eb, win_ref[LANES + HEAD_DIM:KV_DIM, :].astype(BF))) + e_new * vnew
    owin_ref[...] = o / d


def _nsa_decode_a(q3, qr3, cmp_s, win, kvw_new3, pos):
    db = q3.shape[0]
    n_tok = cmp_s.shape[2]
    n_blocks = -(-(pos + 1) // SLC_BLOCK)
    nj = -(-n_blocks // LANES) * LANES
    ov = _overlap_matrix(n_tok, nj, 1)
    per_b = lambda a: pl.BlockSpec((None,) + a.shape[1:], lambda i: (i,) + (0,) * (a.ndim - 1))
    head = pl.BlockSpec((None, N_HEADS, HEAD_DIM), lambda i: (i, 0, 0))
    return pl.pallas_call(
        functools.partial(_nsa_decode_a_kernel, pos=pos, n_blocks=n_blocks),
        grid=(db,),
        in_specs=[head, head, per_b(cmp_s), per_b(win), per_b(kvw_new3), pl.BlockSpec(ov.shape, lambda i: (0, 0))],
        out_specs=[head, head, pl.BlockSpec((None, SUBLANES, LANES), lambda i: (i, 0, 0))],
        out_shape=[jax.ShapeDtypeStruct((db, N_HEADS, HEAD_DIM), F32),
                   jax.ShapeDtypeStruct((db, N_HEADS, HEAD_DIM), F32),
                   jax.ShapeDtypeStruct((db, SUBLANES, LANES), jnp.int32)],
        compiler_params=_cparams(("parallel",)),
        name="nsa_decode_select",
    )(q3, qr3, cmp_s, win, kvw_new3, ov)


def _nsa_decode_b_kernel(idx_ref, pt_ref, qr_ref, *refs, n_past_blocks, cur):
    del pt_ref
    nb = N_KV * N_SELECT
    blk_refs = refs[0:nb]
    new_ref, ocmp_ref, owin_ref, sm_ref, o_ref = refs[nb:]
    b = pl.program_id(0)
    nh = N_HEADS
    top = lax.broadcasted_iota(jnp.int32, (nh, 1), 0) < Q_PER_KV
    qr = qr_ref[...]
    qrb = qr.astype(BF)
    per_page = PAGE_SIZE // SLC_BLOCK
    n_keys = N_SELECT * PAGE_SIZE
    lane = lax.broadcasted_iota(jnp.int32, (nh, n_keys), 1)
    lane_page = lane // PAGE_SIZE
    lane_sub = (lane // SLC_BLOCK) % per_page
    s_g = []
    v_g = []
    ok_g = []
    has_new = []
    for g in range(N_KV):
        blks = blk_refs[g * N_SELECT:(g + 1) * N_SELECT]
        kt = jnp.concatenate([r[g * HEAD_DIM:(g + 1) * HEAD_DIM, :] for r in blks], axis=1).astype(BF)
        v_g.append(jnp.concatenate([r[LANES + g * HEAD_DIM:LANES + (g + 1) * HEAD_DIM, :] for r in blks],
                                   axis=1).astype(BF))
        s_g.append(_dot(qrb, kt))
        ok = jnp.zeros((nh, n_keys), F32)
        new_sel = jnp.zeros((), F32)
        for k in range(N_SELECT):
            jk = idx_ref[(b * N_KV + g) * N_SELECT + k]
            hit = (lane_page == k) & (lane_sub == jk % per_page)
            ok = jnp.where(hit, jnp.where(jk < n_past_blocks, 1.0, 0.0), ok)
            new_sel = jnp.maximum(new_sel, jnp.where(jk == cur, 1.0, 0.0))
        ok_g.append(ok)
        has_new.append(new_sel)
    s = _group_rows(top, s_g[0], s_g[1]) * SCALE
    mask = _group_rows(top, ok_g[0], ok_g[1]) > 0.5
    new_on = _group_rows(top, has_new[0], has_new[1]) > 0.5
    new = new_ref[...]
    knew = _group_rows(top, new[:, 0:HEAD_DIM], new[:, HEAD_DIM:LANES])
    vnew = _group_rows(top, new[:, LANES:LANES + HEAD_DIM], new[:, LANES + HEAD_DIM:KV_DIM])
    s_new = jnp.where(new_on, jnp.sum(qr * knew, axis=1, keepdims=True) * SCALE, NEG_BIG)
    sm = jnp.where(mask, s, NEG_BIG)
    m = jnp.maximum(jnp.max(sm, axis=1, keepdims=True), s_new)
    e = jnp.where(mask, jnp.exp(sm - m), 0.0)
    e_new = jnp.where(new_on, jnp.exp(s_new - m), 0.0)
    d = jnp.sum(e, axis=1, keepdims=True) + e_new
    eb = e.astype(BF)
    o_slc = ((_group_rows(top, _dot_nt(eb, v_g[0]), _dot_nt(eb, v_g[1])) + e_new * vnew)
             / jnp.where(d > 0.0, d, 1.0))

    sig = jnp.broadcast_to(jax.nn.sigmoid(sm_ref[...]), (nh, LANES))
    hl = lax.broadcasted_iota(jnp.int32, (nh, LANES), 1)
    hr = lax.broadcasted_iota(jnp.int32, (nh, LANES), 0)

    def gate(br):
        return jnp.sum(jnp.where(hl == N_BRANCH * hr + br, sig, 0.0), axis=1, keepdims=True)

    o_ref[...] = gate(0) * ocmp_ref[...] + gate(1) * o_slc + gate(2) * owin_ref[...]


def _nsa_decode_b(idx_flat, pt_flat, qr3, slc_pages_t, kvs_new3, o_cmp, o_win, small3, pos, n_pages):
    db = qr3.shape[0]
    n_past_blocks = pos // SLC_BLOCK
    cur = pos // SLC_BLOCK
    per_page = PAGE_SIZE // SLC_BLOCK

    def blk_spec(g, k):
        def imap(i, idx, pt):
            jk = jnp.minimum(idx[(i * N_KV + g) * N_SELECT + k], n_past_blocks - 1)
            return (pt[i * n_pages + jk // per_page], 0, 0)
        return pl.BlockSpec((None, KV_DIM, PAGE_SIZE), imap)

    head = pl.BlockSpec((None, N_HEADS, HEAD_DIM), lambda i, idx, pt: (i, 0, 0))
    row3 = lambda a: pl.BlockSpec((None,) + a.shape[1:], lambda i, idx, pt: (i, 0, 0))
    grid_spec = pltpu.PrefetchScalarGridSpec(
        num_scalar_prefetch=2,
        grid=(db,),
        in_specs=[head] + [blk_spec(g, k) for g in range(N_KV) for k in range(N_SELECT)]
        + [row3(kvs_new3), head, head, row3(small3)],
        out_specs=head,
    )
    return pl.pallas_call(
        functools.partial(_nsa_decode_b_kernel, n_past_blocks=n_past_blocks, cur=cur),
        grid_spec=grid_spec,
        out_shape=jax.ShapeDtypeStruct((db, N_HEADS, HEAD_DIM), F32),
        compiler_params=_cparams(("arbitrary",)),
        name="nsa_decode_attend",
    )(idx_flat, pt_flat, qr3, *([slc_pages_t] * (N_KV * N_SELECT)), kvs_new3, o_cmp, o_win, small3)


def _ssd_decode_kernel(zx_ref, cst_ref, sm_ref, h0_ref, cw_ref, cb_ref, dtb_ref, alog_ref, dskip_ref, nrm_ref,
                       y_ref, hout_ref):
    z = zx_ref[:, 0:SSD_INNER]
    xbc = zx_ref[:, SSD_INNER:]
    conv = cb_ref[...] + xbc * cw_ref[CONV_WIDTH - 1:CONV_WIDTH, :]
    for k in range(CONV_WIDTH - 1):
        conv = conv + cst_ref[k:k + 1, :] * cw_ref[k:k + 1, :]
    xbc_c = _silu(conv)
    xs = xbc_c[:, 0:SSD_INNER]
    dt_full = _softplus(sm_ref[...] + dtb_ref[...])
    da_full = jnp.exp(dt_full * (-jnp.exp(alog_ref[...])))
    p = SSD_HEAD_DIM
    eye = lax.broadcasted_iota(jnp.int32, (p, p), 0) == lax.broadcasted_iota(jnp.int32, (p, p), 1)
    ys = []
    for h in range(SSD_HEADS):
        g = h // (SSD_HEADS // 2)
        col = DT_COL + h
        xdt = xs[:, h * p:(h + 1) * p] * dt_full[:, col:col + 1]
        xcol = jnp.sum(jnp.where(eye, jnp.broadcast_to(xdt, (p, p)), 0.0), axis=1, keepdims=True)
        bm = xbc_c[:, SSD_INNER + g * SSD_STATE:SSD_INNER + (g + 1) * SSD_STATE]
        cm = xbc_c[:, SSD_INNER + (2 + g) * SSD_STATE:SSD_INNER + (3 + g) * SSD_STATE]
        h_new = da_full[:, col:col + 1] * h0_ref[h] + xcol * bm
        hout_ref[h] = h_new
        ys.append(_dot_nt(jnp.broadcast_to(cm, (8, SSD_STATE)).astype(BF), h_new.astype(BF))[0:1, :])
    y = jnp.concatenate(ys, axis=1) + dskip_ref[...] * xs
    y = y * _silu(z)
    y_ref[...] = _rms(y, nrm_ref[...])


def _ssd_decode(zx3, conv_state, small3, h0, params):
    db = zx3.shape[0]
    cw, cb, dtb, alog, dskip, nrm = params
    per_b = lambda a: pl.BlockSpec((None,) + a.shape[1:], lambda i: (i,) + (0,) * (a.ndim - 1))
    full = lambda a: pl.BlockSpec(a.shape, lambda i: (0,) * a.ndim)
    return pl.pallas_call(
        _ssd_decode_kernel,
        grid=(db,),
        in_specs=[per_b(zx3), per_b(conv_state), per_b(small3), per_b(h0), full(cw), full(cb), full(dtb), full(alog),
                  full(dskip), full(nrm)],
        out_specs=[pl.BlockSpec((None, 1, SSD_INNER), lambda i: (i, 0, 0)), per_b(h0)],
        out_shape=[jax.ShapeDtypeStruct((db, 1, SSD_INNER), F32), jax.ShapeDtypeStruct(h0.shape, F32)],
        compiler_params=_cparams(("parallel",)),
        name="ssd_decode",
    )(zx3, conv_state, small3, h0, cw, cb, dtb, alog, dskip, nrm)


def kernel(x_prompt, x_sample, cache_kv_cmp, cache_kv_slc, cache_kv_win, state_ssm, state_ssd_conv, state_lru,
           state_lru_conv, page_table, norm_mix, norm_ffn, norm_final, w_ffn_gate, w_ffn_up, w_ffn_down, w_in_a,
           w_out_a, cmp_pe_k, cmp_w1_k, cmp_w2_k, cmp_pe_v, cmp_w1_v, cmp_w2_v, ssd_conv_w, ssd_conv_b, ssd_dt_bias,
           ssd_a_log, ssd_d, ssd_norm, w_in_c, lru_conv_w, lru_conv_b, lru_w_a, lru_b_a, lru_w_x, lru_b_x,
           lru_lambda, w_out_c):
    b, t, d = x_prompt.shape
    db = x_sample.shape[0]
    n_pages = page_table.shape[1]
    pos_s = n_pages * PAGE_SIZE
    m = b * t
    kv_shape = (2, N_KV, HEAD_DIM)

    wm, ws = _prep_w_in_a(w_in_a[0])
    wo_nsa = w_out_a[0, :Q_DIM].astype(BF)
    wo_ssd = w_out_a[0, Q_DIM:].astype(BF)
    cmp_k = _prep_cmp_w(cmp_pe_k[0], cmp_w1_k[0], cmp_w2_k[0])
    cmp_v = _prep_cmp_w(cmp_pe_v[0], cmp_w1_v[0], cmp_w2_v[0])
    ssd_par = _ssd_params(ssd_conv_w[0], ssd_conv_b[0], ssd_dt_bias[0], ssd_a_log[0], ssd_d[0], ssd_norm[0])
    lru_par = _lru_params(norm_mix[1], w_in_c[0], lru_conv_w[0], lru_conv_b[0], lru_w_a[0], lru_b_a[0], lru_w_x[0],
                          lru_b_x[0], lru_lambda[0], w_out_c[0])
    ffn_w = [(norm_ffn[l].reshape(1, d), w_ffn_gate[l].astype(BF), w_ffn_up[l].astype(BF), w_ffn_down[l].astype(BF))
             for l in range(2)]
    gfin = norm_final.reshape(1, d)

    xp = x_prompt.reshape(m, d)
    cos_p, sin_p = _rope_tables(jnp.arange(t, dtype=jnp.int32))
    q, qr, kvc, kvs_t, kvw_t, zx, small, kvc_t = _inproj_a(xp, norm_mix[0], wm, ws, cos_p, sin_p, TILE_PROJ, b, True)
    cmp_p = _compress_prompt(kvc, b, t, *cmp_k, *cmp_v)
    o_nsa = _nsa_prompt(q, qr, cmp_p, kvs_t, kvw_t, small, b, t)
    y_ssd, ssm_p = _ssd_prompt(zx, small, ssd_par, b, t)
    rowmap = lambda i: (i, 0)
    x2 = _ffn(xp, *ffn_w[0], gfin, TILE_FFN, False, rowmap, rowmap, (m, d), mixer=(o_nsa, y_ssd, wo_nsa, wo_ssd))
    x3, lru_p, lru_conv_p = _lru_prompt(x2, *lru_par, b, TILE_LRU)
    y_prompt = _ffn(x3, *ffn_w[1], gfin, TILE_FFN, True, rowmap, rowmap, (m, d))
    w_keep = min(WINDOW, t)
    to_cache = lambda a: a.reshape((b,) + kv_shape + (a.shape[-1],)).transpose(0, 4, 1, 2, 3)[None]
    kv_cmp_p = to_cache(kvc_t)
    kv_slc_p = to_cache(kvs_t)
    kv_win_p = to_cache(kvw_t[:, :, t - w_keep:])
    ssd_conv_p = zx.reshape(b, t, -1)[:, t - (CONV_WIDTH - 1):, SSD_INNER:][None]
    lru_p = lru_p.reshape(b, -1)
    lru_conv_p = lru_conv_p[None]

    xs = x_sample.reshape(db, d)
    cos_s, sin_s = _rope_tables(jnp.full((db,), pos_s, dtype=jnp.int32))
    q_s, qr_s, kvc_s, kvs_s, kvw_s, zx_s, small_s = _inproj_a(xs, norm_mix[0], wm, ws, cos_s, sin_s, db, 1, False)
    pt_flat = page_table.reshape(-1)
    n_phys = cache_kv_cmp.shape[1]
    feature_major = lambda a: a.transpose(0, 2, 3, 4, 1).reshape(a.shape[0], KV_DIM, a.shape[1])
    cmp_s = _compress_decode(feature_major(cache_kv_cmp[0]), pt_flat, db, n_pages, *cmp_k, *cmp_v)
    win_buf = cache_kv_win[0].reshape(db, -1, KV_DIM)
    head3 = lambda a: a.reshape(db, N_HEADS, HEAD_DIM)
    o_cmp_s, o_win_s, idx = _nsa_decode_a(head3(q_s), head3(qr_s), cmp_s, feature_major(cache_kv_win[0]),
                                          kvw_s.reshape(db, 1, KV_DIM), pos_s)
    idx_flat = idx[:, :N_KV, :N_SELECT].reshape(-1)
    o_nsa_s = _nsa_decode_b(idx_flat, pt_flat, head3(qr_s), feature_major(cache_kv_slc[0]),
                            kvs_s.reshape(db, 1, KV_DIM), o_cmp_s, o_win_s, small_s.reshape(db, 1, LANES), pos_s,
                            n_pages)
    y_ssd_s, ssm_s = _ssd_decode(zx_s.reshape(db, 1, -1), state_ssd_conv[0], small_s.reshape(db, 1, LANES),
                                 state_ssm[0], ssd_par)
    x2_s = _ffn(xs, *ffn_w[0], gfin, db, False, lambda i: (i, 0), lambda i: (i, 0), (db, d),
                mixer=(o_nsa_s.reshape(db, Q_DIM), y_ssd_s.reshape(db, SSD_INNER), wo_nsa, wo_ssd))
    x3_s, lru_s, xbr_s = _lru_decode(x2_s, lru_par, state_lru_conv[0], state_lru[0])
    y_sample = _ffn(x3_s, *ffn_w[1], gfin, db, True, lambda i: (i, 0), lambda i: (i, 0), (db, d))
    kv_cmp_s = kvc_s.reshape((1, db, 1) + kv_shape)
    kv_slc_s = kvs_s.reshape((1, db, 1) + kv_shape)
    kv_win_s = jnp.concatenate([win_buf[:, 1:], kvw_s[:, None, :]], axis=1).reshape(
        (1, db, win_buf.shape[1]) + kv_shape)
    ssd_conv_s = jnp.concatenate([state_ssd_conv[0][:, 1:], zx_s[:, None, SSD_INNER:]], axis=1)[None]
    lru_conv_s = jnp.concatenate([state_lru_conv[0][:, 1:], xbr_s[:, None, :]], axis=1)[None]

    return (y_prompt.reshape(b, t, d), y_sample.reshape(db, 1, d),
            kv_cmp_p, kv_slc_p, kv_win_p, ssm_p[None], ssd_conv_p, lru_p[None], lru_conv_p,
            kv_cmp_s, kv_slc_s, kv_win_s, ssm_s[None], ssd_conv_s, lru_s[None], lru_conv_s)
```

```python
import functools
import math

import jax
import jax.numpy as jnp
from jax import lax
from jax.experimental import pallas as pl
from jax.experimental.pallas import tpu as pltpu

BF = jnp.bfloat16
F32 = jnp.float32

HEAD_DIM = 64
N_KV = 2
Q_PER_KV = 4
N_HEADS = N_KV * Q_PER_KV
CMP_STRIDE = 16
CMP_HIDDEN = 256
SLC_BLOCK = 64
N_SELECT = 16
WINDOW = 512
PAGE_SIZE = 128
ROPE_THETA = 10000.0
FORCE_SCORE = 1e9
NEG_BIG = -1e30
SSD_HEADS = 8
SSD_HEAD_DIM = 64
SSD_STATE = 128
SSD_INNER = SSD_HEADS * SSD_HEAD_DIM
SSD_CHUNK = 128
CONV_WIDTH = 4
LRU_BLOCK = 128
LRU_C = 8.0
RMS_EPS = 1e-6
SCALE = HEAD_DIM ** -0.5
LOG2E = math.log2(math.e)
Q_DIM = N_HEADS * HEAD_DIM
KV_DIM = 2 * N_KV * HEAD_DIM
GATE_DIM = 3 * N_HEADS
DT_COL = GATE_DIM
N_BRANCH = 3
LANES = 128
SUBLANES = 8
BF16_SUBLANES = 16
F32_LOWEST = -3e38
VMEM_LIMIT = 56 * 1024 * 1024

TILE_PROJ = 512
TILE_FFN = 512
TILE_LRU = 512
TILE_NSA_Q = 256
TILE_NSA_K = 512
SSD_CHUNKS_PER_STEP = 8
PAGES_PER_STEP = 64


def _cparams(sem):
    return pltpu.CompilerParams(dimension_semantics=sem, vmem_limit_bytes=VMEM_LIMIT)


def _dot(a, b):
    return jnp.dot(a, b, preferred_element_type=F32)


def _dot_nt(a, b):
    return lax.dot_general(a, b, (((1,), (1,)), ((), ())), preferred_element_type=F32)


def _dot_tn(a, b):
    return lax.dot_general(a, b, (((0,), (0,)), ((), ())), preferred_element_type=F32)


def _split3(x):
    hi = x.astype(BF)
    r = x - hi.astype(F32)
    mid = r.astype(BF)
    lo = (r - mid.astype(F32)).astype(BF)
    return hi, mid, lo


def _dot3(x, m01):
    hi, mid, lo = _split3(x)
    return _dot(hi, m01) + _dot(mid, m01) + _dot(lo, m01)


def _dot3_left(m01, x):
    hi, mid, lo = _split3(x)
    return _dot(m01, hi) + _dot(m01, mid) + _dot(m01, lo)


def _rms(x, g):
    y = x * lax.rsqrt(jnp.mean(x * x, axis=-1, keepdims=True) + RMS_EPS)
    return y * g


def _silu(x):
    return x * jax.nn.sigmoid(x)


def _softplus(x):
    return jnp.maximum(x, 0.0) + jnp.log1p(jnp.exp(-jnp.abs(x)))


def _msoftmax_parts(s, mask):
    s = jnp.where(mask, s, NEG_BIG)
    m = jnp.max(s, axis=-1, keepdims=True)
    e = jnp.where(mask, jnp.exp(s - m), 0.0)
    return m, e


def _msoftmax(s, mask):
    _, e = _msoftmax_parts(s, mask)
    d = jnp.sum(e, axis=-1, keepdims=True)
    return e / jnp.where(d > 0.0, d, 1.0)


def _rope_tables(pos):
    half = HEAD_DIM // 2
    inv_freq = ROPE_THETA ** (-jnp.arange(half, dtype=F32) / half)
    ang = pos.astype(F32)[:, None] * inv_freq[None, :]
    cos = jnp.cos(ang)
    sin = jnp.sin(ang)
    cos2 = jnp.tile(jnp.concatenate([cos, cos], axis=-1), (1, LANES // HEAD_DIM))
    sin2 = jnp.tile(jnp.concatenate([-sin, sin], axis=-1), (1, LANES // HEAD_DIM))
    return cos2, sin2


def _inproj_a_kernel(x_ref, g_ref, wm_ref, ws_ref, cos_ref, sin_ref,
                     q_ref, qr_ref, kvc_ref, kvs_ref, kvw_ref, zx_ref, sm_ref, *kvc_t_ref, feature_major):
    xn = _rms(x_ref[...], g_ref[...]).astype(BF)
    cos = cos_ref[...]
    sin = sin_ref[...]
    lane = lax.broadcasted_iota(jnp.int32, cos.shape, 1)
    first = (lane % HEAD_DIM) < (HEAD_DIM // 2)

    def rope(v):
        rot = jnp.where(first, pltpu.roll(v, LANES - HEAD_DIM // 2, 1), pltpu.roll(v, HEAD_DIM // 2, 1))
        return v * cos + rot * sin

    def store_kv(ref, k, v):
        if feature_major:
            ref[0:LANES, :] = k.T
            ref[LANES:KV_DIM, :] = v.T
        else:
            ref[:, 0:LANES] = k
            ref[:, LANES:KV_DIM] = v

    q = _dot(xn, wm_ref[:, 0:Q_DIM])
    for c in range(Q_DIM // LANES):
        cols = slice(c * LANES, (c + 1) * LANES)
        if feature_major:
            q_ref[cols, :] = q[:, cols].T
            qr_ref[cols, :] = rope(q[:, cols]).T
        else:
            q_ref[:, cols] = q[:, cols]
            qr_ref[:, cols] = rope(q[:, cols])
    o = Q_DIM
    kvc = _dot(xn, wm_ref[:, o:o + KV_DIM])
    kvc_ref[...] = kvc
    if feature_major:
        store_kv(kvc_t_ref[0], kvc[:, 0:LANES], kvc[:, LANES:KV_DIM])
    o += KV_DIM
    kvs = _dot(xn, wm_ref[:, o:o + KV_DIM])
    store_kv(kvs_ref, rope(kvs[:, 0:LANES]), kvs[:, LANES:KV_DIM])
    o += KV_DIM
    kvw = _dot(xn, wm_ref[:, o:o + KV_DIM])
    store_kv(kvw_ref, rope(kvw[:, 0:LANES]), kvw[:, LANES:KV_DIM])
    o += KV_DIM
    zx_ref[...] = _dot(xn, wm_ref[:, o:])
    sm_ref[...] = _dot(xn, ws_ref[...])


def _prep_w_in_a(w_in_a):
    a = Q_DIM + 3 * KV_DIM
    gate = w_in_a[:, a:a + GATE_DIM]
    rest = w_in_a[:, a + GATE_DIM:]
    zx_w = rest[:, :rest.shape[1] - SSD_HEADS]
    dt = rest[:, rest.shape[1] - SSD_HEADS:]
    main = jnp.concatenate([w_in_a[:, :a], zx_w], axis=1).astype(BF)
    small = jnp.concatenate([gate, dt], axis=1)
    small = jnp.pad(small, ((0, 0), (0, LANES - small.shape[1]))).astype(BF)
    return main, small


def _inproj_a(x2d, g, wm, ws, cos2, sin2, tm, nb, feature_major):
    m, d = x2d.shape
    nt = m // nb // tm
    zx_dim = wm.shape[1] - Q_DIM - 3 * KV_DIM
    row = lambda w: pl.BlockSpec((tm, w), lambda i, j: (i * nt + j, 0))
    row_shape = lambda w: jax.ShapeDtypeStruct((m, w), F32)
    const = lambda i, j: (0, 0)
    if feature_major:
        fm = lambda w: pl.BlockSpec((None, w, tm), lambda i, j: (i, 0, j))
        fm_shape = lambda w: jax.ShapeDtypeStruct((nb, w, m // nb), F32)
    else:
        fm, fm_shape = row, row_shape
    kv, kv_shape = fm(KV_DIM), fm_shape(KV_DIM)
    out_specs = [fm(Q_DIM), fm(Q_DIM), row(KV_DIM), kv, kv, row(zx_dim), row(LANES)]
    out_shape = [fm_shape(Q_DIM), fm_shape(Q_DIM), row_shape(KV_DIM), kv_shape, kv_shape, row_shape(zx_dim),
                 row_shape(LANES)]
    if feature_major:
        out_specs.append(kv)
        out_shape.append(kv_shape)
    pos = pl.BlockSpec((tm, LANES), lambda i, j: (j, 0))
    return pl.pallas_call(
        functools.partial(_inproj_a_kernel, feature_major=feature_major),
        grid=(nb, nt),
        in_specs=[row(d), pl.BlockSpec((1, d), const), pl.BlockSpec(wm.shape, const), pl.BlockSpec(ws.shape, const),
                  pos, pos],
        out_specs=out_specs,
        out_shape=out_shape,
        compiler_params=_cparams(("parallel", "parallel")),
        name="inproj_a",
    )(x2d, g.reshape(1, d), wm, ws, cos2, sin2)


CMP_PACK = 2 * LANES // HEAD_DIM


def _block_mlp_hidden(load_rows, n_rows, pe_ref, w1_ref):
    low = lax.broadcasted_iota(jnp.int32, (n_rows, LANES), 1) < HEAD_DIM

    def split_groups(a, b):
        return (jnp.where(low, a, pltpu.roll(b, HEAD_DIM, 1)), jnp.where(low, pltpu.roll(a, HEAD_DIM, 1), b))

    lead = None
    trail = None
    for u in range(CMP_STRIDE // CMP_PACK):
        xs = [load_rows(u * CMP_PACK + j) for j in range(CMP_PACK)]
        g01 = split_groups(xs[0], xs[1])
        g23 = split_groups(xs[2], xs[3])
        x = jnp.concatenate([jnp.concatenate([g01[0], g23[0]], axis=1),
                             jnp.concatenate([g01[1], g23[1]], axis=1)], axis=0)
        dl = _dot((x + pe_ref[0, u]).astype(BF), w1_ref[0, u])
        dt = _dot((x + pe_ref[1, u]).astype(BF), w1_ref[1, u])
        lead = dl if lead is None else lead + dl
        trail = dt if trail is None else trail + dt
    return lead, trail


def _compress_prompt_kernel(kc_ref, vc_ref, pek_ref, w1k_ref, w2k_ref, pev_ref, w1v_ref, w2v_ref, out_ref, *,
                            n_half):
    branches = ((kc_ref, pek_ref, w1k_ref, w2k_ref), (vc_ref, pev_ref, w1v_ref, w2v_ref))
    for c, (src_ref, pe_ref, w1_ref, w2_ref) in enumerate(branches):
        def load_rows(s, src_ref=src_ref):
            return src_ref[pl.ds(s, n_half, stride=CMP_STRIDE), :]

        lead, trail = _block_mlp_hidden(load_rows, n_half, pe_ref, w1_ref)
        for g in range(N_KV):
            a0 = lead[g * n_half:(g + 1) * n_half]
            a1 = trail[g * n_half:(g + 1) * n_half]
            pre = a0 + pltpu.roll(a1, n_half - 1, 0)
            tok = _dot(_silu(pre).astype(BF), w2_ref[...])
            out_ref[g, :, c * HEAD_DIM:(c + 1) * HEAD_DIM] = tok


def _compress_prompt(kvc2d, b, t, pek, w1k, w2k, pev, w1v, w2v):
    n_half = t // CMP_STRIDE
    full = lambda a: pl.BlockSpec(a.shape, lambda i: (0,) * a.ndim)
    return pl.pallas_call(
        functools.partial(_compress_prompt_kernel, n_half=n_half),
        grid=(b,),
        in_specs=[pl.BlockSpec((t, LANES), lambda i: (i, 0)), pl.BlockSpec((t, LANES), lambda i: (i, 1)),
                  full(pek), full(w1k), full(w2k), full(pev), full(w1v), full(w2v)],
        out_specs=pl.BlockSpec((None, N_KV, n_half, LANES), lambda i: (i, 0, 0, 0)),
        out_shape=jax.ShapeDtypeStruct((b, N_KV, n_half, LANES), F32),
        compiler_params=_cparams(("parallel",)),
        name="compress_prompt",
    )(kvc2d, kvc2d, pek, w1k, w2k, pev, w1v, w2v)


def _prep_cmp_w(pe, w1, w2):
    n_quads = CMP_STRIDE // CMP_PACK
    return (pe.reshape(2, n_quads, 1, CMP_PACK * HEAD_DIM),
            w1.reshape(2, n_quads, CMP_PACK * HEAD_DIM, CMP_HIDDEN).astype(BF), w2.astype(BF))


def _rank_select(imp_t):
    n_rows, tq = imp_t.shape
    jrow = lax.broadcasted_iota(jnp.int32, (n_rows, tq), 0)
    cnt = jnp.zeros((n_rows, tq), F32)
    for jp in range(n_rows):
        row = imp_t[jp:jp + 1, :]
        ahead = jnp.where(row == imp_t, jnp.where(jrow > jp, 1.0, 0.0), jnp.where(row > imp_t, 1.0, 0.0))
        cnt = cnt + ahead
    return jnp.where(cnt < float(N_SELECT), 1.0, 0.0)


def _heads_on_lanes(ref, g):
    rows = [(g * Q_PER_KV + r) * HEAD_DIM for r in range(Q_PER_KV)]
    return (jnp.concatenate([ref[c:c + HEAD_DIM, :] for c in rows], axis=1) * (SCALE * LOG2E)).astype(BF)


def _per_head(x):
    return jnp.concatenate([x] * Q_PER_KV, axis=1)


def _nsa_prompt_kernel(q_ref, qr_ref, cmp_ref, kvs_ref, kvw_ref, sm_ref, ovt_ref, o_ref, sel_ref, slc_ref, *, t_len,
                       tq, kc):
    t0 = pl.program_id(1) * tq
    hq = Q_PER_KV
    nq = hq * tq
    tcol = t0 + lax.broadcasted_iota(jnp.int32, (1, tq), 1)
    gates_t = jax.nn.sigmoid(sm_ref[...]).T
    tok = lax.broadcasted_iota(jnp.int32, (LANES, tq), 0)
    cmask = _per_head(jnp.where((tok * CMP_STRIDE + (2 * CMP_STRIDE - 1)) <= tcol, 1.0, 0.0)) > 0.5
    n_sel = t_len // SLC_BLOCK
    blk = lax.broadcasted_iota(jnp.int32, (n_sel, tq), 0)
    valid = (blk * SLC_BLOCK) <= tcol
    cur = tcol // SLC_BLOCK
    forced = (blk == 0) | (blk == cur) | (blk == cur - 1)
    krel = lax.broadcasted_iota(jnp.int32, (kc, tq), 0)
    blocks_per_chunk = kc // SLC_BLOCK

    krows = [slice(g * HEAD_DIM, (g + 1) * HEAD_DIM) for g in range(N_KV)]
    vrows = [slice(LANES + g * HEAD_DIM, LANES + (g + 1) * HEAD_DIM) for g in range(N_KV)]

    o_cmp = []
    for g in range(N_KV):
        s = jnp.where(cmask, _dot(cmp_ref[g, :, 0:HEAD_DIM].astype(BF), _heads_on_lanes(q_ref, g)), NEG_BIG)
        e = jnp.where(cmask, jnp.exp2(s - jnp.max(s, axis=0, keepdims=True)), 0.0)
        d = jnp.sum(e, axis=0, keepdims=True)
        p = e / jnp.where(d > 0.0, d, 1.0)
        o_cmp.append(_dot_tn(cmp_ref[g, :, HEAD_DIM:LANES].astype(BF), p.astype(BF)))
        psum = p[:, 0:tq]
        for r in range(1, hq):
            psum = psum + p[:, r * tq:(r + 1) * tq]
        imp = _dot3_left(ovt_ref[...], psum)[0:n_sel, :]
        imp = jnp.where(valid, jnp.where(forced, FORCE_SCORE, imp), -FORCE_SCORE)
        sel_ref[g] = _rank_select(imp)

    qs = [_heads_on_lanes(qr_ref, g) for g in range(N_KV)]

    aug = BF16_SUBLANES
    blk_onehot = (lax.broadcasted_iota(jnp.int32, (aug, kc), 1) // SLC_BLOCK
                  == lax.broadcasted_iota(jnp.int32, (aug, kc), 0))
    blk_onehot = jnp.where(blk_onehot, 1.0, 0.0).astype(BF)

    pad_rows = jnp.zeros((aug - blocks_per_chunk, nq), F32)

    def chunk(ci, state, diagonal):
        k0 = ci * kc
        scores = []
        for g in range(N_KV):
            sel_c = sel_ref[g, ci * blocks_per_chunk:(ci + 1) * blocks_per_chunk, :]
            sel_bias = _per_head(jnp.where(sel_c > 0.5, 0.0, NEG_BIG))
            q_aug = jnp.concatenate([qs[g], jnp.concatenate([sel_bias, pad_rows], axis=0).astype(BF)], axis=0)
            k_aug = jnp.concatenate([kvs_ref[krows[g], k0:k0 + kc].astype(BF), blk_onehot], axis=0)
            scores.append(_dot_tn(k_aug, q_aug))
        causal_bias = _per_head(jnp.where(k0 + krel <= tcol, 0.0, NEG_BIG)) if diagonal else None
        out = []
        for g in range(N_KV):
            m, l, acc = state[g]
            vt = kvs_ref[vrows[g], k0:k0 + kc].astype(BF)
            s = scores[g] + causal_bias if diagonal else scores[g]
            m_new = jnp.maximum(m, jnp.max(s, axis=0, keepdims=True))
            alpha = jnp.exp2(m - m_new)
            e = jnp.exp2(s - m_new)
            l = alpha * l + jnp.sum(e, axis=0, keepdims=True)
            out.append((m_new, l, alpha * acc + _dot(vt, e.astype(BF))))
        return tuple(out)

    init = (jnp.full((1, nq), NEG_BIG, F32), jnp.zeros((1, nq), F32), jnp.zeros((HEAD_DIM, nq), F32))
    n_chunks = (t0 + tq - 1) // kc + 1
    for count in range(1, t_len // kc + 1):
        @pl.when(n_chunks == count)
        def _(count=count):
            state = (init,) * N_KV
            for ci in range(count):
                state = chunk(ci, state, ci == count - 1)
            for g in range(N_KV):
                slc_ref[g] = state[g][2] / state[g][1]

    tw = LANES
    n_win = WINDOW + tw
    o_win_parts = [[] for _ in range(N_KV)]
    for sub in range(tq // tw):
        start = pl.multiple_of(jnp.clip(t0 + sub * tw - WINDOW, 0, t_len - n_win), LANES)
        dist = tcol[:, sub * tw:(sub + 1) * tw] - (start + lax.broadcasted_iota(jnp.int32, (n_win, tw), 0))
        wbias = _per_head(jnp.where(dist >= 0, jnp.where(dist < WINDOW, 0.0, NEG_BIG), NEG_BIG))
        qsub = [jnp.concatenate([qs[g][:, r * tq + sub * tw:r * tq + (sub + 1) * tw] for r in range(hq)], axis=1)
                for g in range(N_KV)]
        wscores = [_dot_tn(kvw_ref[krows[g], pl.ds(start, n_win)].astype(BF), qsub[g]) for g in range(N_KV)]
        for g in range(N_KV):
            s = wscores[g] + wbias
            e = jnp.exp2(s - jnp.max(s, axis=0, keepdims=True))
            o_win_parts[g].append(_dot(kvw_ref[vrows[g], pl.ds(start, n_win)].astype(BF), e.astype(BF))
                                  / jnp.sum(e, axis=0, keepdims=True))
    def gate(h, branch):
        row = N_BRANCH * h + branch
        return gates_t[row:row + 1, :]

    for g in range(N_KV):
        o_slc = slc_ref[g]
        o_win = jnp.concatenate([o_win_parts[g][sub][:, r * tw:(r + 1) * tw]
                                 for r in range(hq) for sub in range(tq // tw)], axis=1)
        for r in range(hq):
            h = g * hq + r
            cols = slice(r * tq, (r + 1) * tq)
            o_ref[h * HEAD_DIM:(h + 1) * HEAD_DIM, :] = (
                gate(h, 0) * o_cmp[g][:, cols] + gate(h, 1) * o_slc[:, cols] + gate(h, 2) * o_win[:, cols])


def _overlap_matrix(n_rows, n_cols, row_shift):
    n = jnp.arange(n_rows)[:, None] - row_shift
    c_start = n * CMP_STRIDE
    s_start = jnp.arange(n_cols)[None, :] * SLC_BLOCK
    ov = (c_start < s_start + SLC_BLOCK) & (c_start + 2 * CMP_STRIDE > s_start) & (n >= 0)
    return ov.astype(BF)


def _nsa_prompt(q_t, qr_t, cmp, kvs_t, kvw_t, small, b, t):
    tq = TILE_NSA_Q
    kc = TILE_NSA_K
    nq = t // tq
    n_half = t // CMP_STRIDE
    n_sel = t // SLC_BLOCK
    assert n_half == LANES and n_sel <= LANES and t % kc == 0 and kc % tq == 0 and tq % LANES == 0
    ov_t = _overlap_matrix(LANES, LANES, 0).T
    qtile = pl.BlockSpec((None, Q_DIM, tq), lambda i, j: (i, 0, j))
    seq = pl.BlockSpec((None, KV_DIM, t), lambda i, j: (i, 0, 0))
    return pl.pallas_call(
        functools.partial(_nsa_prompt_kernel, t_len=t, tq=tq, kc=kc),
        grid=(b, nq),
        in_specs=[qtile, qtile,
                  pl.BlockSpec((None, N_KV, n_half, LANES), lambda i, j: (i, 0, 0, 0)),
                  seq, seq, pl.BlockSpec((tq, LANES), lambda i, j: (i * nq + j, 0)),
                  pl.BlockSpec(ov_t.shape, lambda i, j: (0, 0))],
        out_specs=qtile,
        out_shape=jax.ShapeDtypeStruct((b, Q_DIM, t), F32),
        scratch_shapes=[pltpu.VMEM((N_KV, n_sel, tq), F32), pltpu.VMEM((N_KV, HEAD_DIM, Q_PER_KV * tq), F32)],
        compiler_params=_cparams(("parallel", "parallel")),
        name="nsa_prompt",
    )(q_t, qr_t, cmp, kvs_t, kvw_t, small, ov_t)


def _ssd_prompt_kernel(zx_ref, sm_ref, cw_ref, cb_ref, dtb_ref, alog_ref, dskip_ref, nrm_ref, tri_ref,
                       y_ref, hout_ref, tail_ref, h_ref, *, n_sub):
    lc = SSD_CHUNK

    @pl.when(pl.program_id(1) == 0)
    def _():
        tail_ref[...] = jnp.zeros_like(tail_ref)
        h_ref[...] = jnp.zeros_like(h_ref)

    li = lax.broadcasted_iota(jnp.int32, (lc, lc), 0)
    si = lax.broadcasted_iota(jnp.int32, (lc, lc), 1)
    lower = li >= si
    heads_per_group = SSD_HEADS // 2
    tail = tail_ref[...]
    state = [h_ref[h] for h in range(SSD_HEADS)]
    for sc in range(n_sub):
        rows = slice(sc * lc, (sc + 1) * lc)
        z = zx_ref[rows, 0:SSD_INNER]
        xbc = zx_ref[rows, SSD_INNER:]
        u = jnp.concatenate([tail, xbc], axis=0)
        tail = xbc[lc - SUBLANES:lc, :]
        conv = cb_ref[...]
        for k in range(CONV_WIDTH):
            off = SUBLANES - (CONV_WIDTH - 1) + k
            conv = conv + u[off:off + lc, :] * cw_ref[k:k + 1, :]
        xbc_c = _silu(conv)
        xs = xbc_c[:, 0:SSD_INNER]

        dt_full = _softplus(sm_ref[rows, :] + dtb_ref[...])
        a_full = dt_full * (-jnp.exp(alog_ref[...]))
        acs = _dot3_left(tri_ref[...], a_full)
        acs_t = acs.T

        bm_b = [xbc_c[:, SSD_INNER + g * SSD_STATE:SSD_INNER + (g + 1) * SSD_STATE].astype(BF) for g in range(2)]
        cm_b = [xbc_c[:, SSD_INNER + (2 + g) * SSD_STATE:SSD_INNER + (3 + g) * SSD_STATE].astype(BF)
                for g in range(2)]
        cb = [_dot_nt(cm_b[g], bm_b[g]) for g in range(2)]
        prep = []
        for h in range(SSD_HEADS):
            col = DT_COL + h
            acs_h = acs[:, col:col + 1]
            acs_last = acs[lc - 1:lc, col:col + 1]
            lmat = jnp.where(lower, jnp.exp(acs_h - acs_t[col:col + 1, :]), 0.0)
            xdt = xs[:, h * SSD_HEAD_DIM:(h + 1) * SSD_HEAD_DIM] * dt_full[:, col:col + 1]
            prep.append(((cb[h // heads_per_group] * lmat).astype(BF), xdt.astype(BF),
                         (xdt * jnp.exp(acs_last - acs_h)).astype(BF), jnp.exp(acs_h), jnp.exp(acs_last)))
        ys = []
        for h in range(SSD_HEADS):
            g = h // heads_per_group
            m_h, xdt_b, xdec_b, grow, glast = prep[h]
            ys.append(_dot(m_h, xdt_b) + _dot_nt(cm_b[g], state[h].astype(BF)) * grow)
            state[h] = glast * state[h] + _dot_tn(xdec_b, bm_b[g])
        y = jnp.concatenate(ys, axis=1) + dskip_ref[...] * xs
        y = y * _silu(z)
        y_ref[rows, :] = _rms(y, nrm_ref[...])
    tail_ref[...] = tail
    for h in range(SSD_HEADS):
        h_ref[h] = state[h]
        hout_ref[h] = state[h]


def _ssd_params(conv_w, conv_b, dt_bias, a_log, d_skip, ssd_norm):
    pad = lambda v: jnp.pad(v, (DT_COL, LANES - DT_COL - SSD_HEADS)).reshape(1, LANES)
    return (conv_w, conv_b.reshape(1, -1), pad(dt_bias), pad(a_log),
            jnp.repeat(d_skip, SSD_HEAD_DIM).reshape(1, SSD_INNER), ssd_norm.reshape(1, SSD_INNER))


def _ssd_prompt(zx, small, params, b, t):
    lc = SSD_CHUNK
    n_sub = SSD_CHUNKS_PER_STEP
    nc = t // (lc * n_sub)
    cw, cb, dtb, alog, dskip, nrm = params
    tri = (jnp.arange(lc)[:, None] >= jnp.arange(lc)[None, :]).astype(BF)
    zx_dim = zx.shape[1]
    conv_dim = zx_dim - SSD_INNER
    full = lambda a: pl.BlockSpec(a.shape, lambda i, j: (0,) * a.ndim)
    tile = lambda w: pl.BlockSpec((lc * n_sub, w), lambda i, j: (i * nc + j, 0))
    return pl.pallas_call(
        functools.partial(_ssd_prompt_kernel, n_sub=n_sub),
        grid=(b, nc),
        in_specs=[tile(zx_dim), tile(LANES), full(cw), full(cb), full(dtb), full(alog), full(dskip), full(nrm),
                  full(tri)],
        out_specs=[tile(SSD_INNER),
                   pl.BlockSpec((None, SSD_HEADS, SSD_HEAD_DIM, SSD_STATE), lambda i, j: (i, 0, 0, 0))],
        out_shape=[jax.ShapeDtypeStruct((b * t, SSD_INNER), F32),
                   jax.ShapeDtypeStruct((b, SSD_HEADS, SSD_HEAD_DIM, SSD_STATE), F32)],
        scratch_shapes=[pltpu.VMEM((SUBLANES, conv_dim), F32),
                        pltpu.VMEM((SSD_HEADS, SSD_HEAD_DIM, SSD_STATE), F32)],
        compiler_params=_cparams(("parallel", "arbitrary")),
        name="ssd_prompt",
    )(zx, small, cw, cb, dtb, alog, dskip, nrm, tri)


def _ffn_kernel(x_ref, *refs, final_norm, mixer_proj, a_feature_major):
    x = x_ref[...]
    if mixer_proj:
        a_ref, b_ref, wa_ref, wb_ref = refs[:4]
        refs = refs[4:]
        a = a_ref[...].astype(BF)
        x = x + (_dot_tn(a, wa_ref[...]) if a_feature_major else _dot(a, wa_ref[...]))
        x = x + _dot(b_ref[...].astype(BF), wb_ref[...])
    g_ref, wg_ref, wu_ref, wd_ref, gf_ref, o_ref = refs
    xn = _rms(x, g_ref[...]).astype(BF)
    h = _silu(_dot(xn, wg_ref[...])) * _dot(xn, wu_ref[...])
    y = x + _dot(h.astype(BF), wd_ref[...])
    if final_norm:
        y = _rms(y, gf_ref[...])
    o_ref[...] = y


def _ffn(x2d, g, wg, wu, wd, gf, tm, final_norm, in_map, out_map, out_2d_shape, mixer=None):
    m, d = x2d.shape[0] * x2d.shape[1] // wg.shape[0], wg.shape[0]
    n_steps = m // tm
    full = lambda arr: pl.BlockSpec(arr.shape, lambda i: (0, 0), pipeline_mode=pl.Buffered(1))
    args = [x2d]
    in_specs = [pl.BlockSpec((tm, d), in_map)]
    a_feature_major = mixer is not None and mixer[0].ndim == 3
    if mixer is not None:
        a, bb, wa, wb = mixer
        args += [a, bb, wa, wb]
        if a_feature_major:
            nt = a.shape[2] // tm
            a_spec = pl.BlockSpec((None, a.shape[1], tm), lambda i: (i // nt, 0, i % nt))
        else:
            a_spec = pl.BlockSpec((tm, a.shape[1]), in_map)
        in_specs += [a_spec, pl.BlockSpec((tm, bb.shape[1]), in_map), full(wa), full(wb)]
    args += [g, wg, wu, wd, gf]
    in_specs += [full(g), full(wg), full(wu), full(wd), full(gf)]
    return pl.pallas_call(
        functools.partial(_ffn_kernel, final_norm=final_norm, mixer_proj=mixer is not None,
                          a_feature_major=a_feature_major),
        grid=(n_steps,),
        in_specs=in_specs,
        out_specs=pl.BlockSpec((tm, d), out_map),
        out_shape=jax.ShapeDtypeStruct(out_2d_shape, F32),
        compiler_params=_cparams(("parallel",)),
        name="ffn_final" if final_norm else "ffn",
    )(*args)


def _lru_gates(xc, wa_ref, ba_ref, wx_ref, bx_ref, sp):
    n_heads = xc.shape[1] // LRU_BLOCK
    a_parts = []
    gx_parts = []
    for h in range(n_heads):
        sl = slice(h * LRU_BLOCK, (h + 1) * LRU_BLOCK)
        xh = xc[:, sl]
        xb = xh.astype(BF)
        r = jax.nn.sigmoid(_dot(xb, wa_ref[h]) + ba_ref[:, sl])
        i = jax.nn.sigmoid(_dot(xb, wx_ref[h]) + bx_ref[:, sl])
        log_a = -LRU_C * r * sp[:, sl]
        a = jnp.exp(log_a)
        a_parts.append(a)
        gx_parts.append(jnp.sqrt(1.0 - a * a) * (i * xh))
    return jnp.concatenate(a_parts, axis=1), jnp.concatenate(gx_parts, axis=1)


def _lru_prompt_kernel(x_ref, g_ref, win_ref, cw_ref, cb_ref, wa_ref, ba_ref, wx_ref, bx_ref, lam_ref, wo_ref,
                       o_ref, hout_ref, cout_ref, tail_ref, h_ref, a_s, gx_s, hs_s, u_s, *, rows, w):
    @pl.when(pl.program_id(1) == 0)
    def _():
        tail_ref[...] = jnp.zeros_like(tail_ref)
        h_ref[...] = jnp.zeros_like(h_ref)

    x = x_ref[...]
    xn = _rms(x, g_ref[...]).astype(BF)
    proj = _dot(xn, win_ref[...])
    gate_br = proj[:, 0:w]
    x_br = proj[:, w:2 * w]
    u_s[0:SUBLANES, :] = tail_ref[...]
    u_s[SUBLANES:SUBLANES + rows, :] = x_br
    tail_ref[...] = x_br[rows - SUBLANES:rows, :]
    cout_ref[...] = x_br[rows - (CONV_WIDTH - 1):rows, :]
    xc = cb_ref[...]
    for k in range(CONV_WIDTH):
        off = SUBLANES - (CONV_WIDTH - 1) + k
        xc = xc + u_s[off:off + rows, :] * cw_ref[k:k + 1, :]
    sp = _softplus(-lam_ref[...])
    a, gx = _lru_gates(xc, wa_ref, ba_ref, wx_ref, bx_ref, sp)
    a_s[...] = a
    gx_s[...] = gx
    sub = lax.broadcasted_iota(jnp.int32, (SUBLANES, w), 0)
    h = h_ref[...]
    for i in range(rows // SUBLANES):
        blk = slice(i * SUBLANES, (i + 1) * SUBLANES)
        ac = a_s[blk, :]
        bc = gx_s[blk, :]
        for dd in (1, 2, 4):
            a_sh = jnp.where(sub >= dd, pltpu.roll(ac, dd, 0), 1.0)
            b_sh = jnp.where(sub >= dd, pltpu.roll(bc, dd, 0), 0.0)
            bc = ac * b_sh + bc
            ac = ac * a_sh
        hb = ac * h + bc
        hs_s[blk, :] = hb
        h = jnp.broadcast_to(hb[SUBLANES - 1:SUBLANES, :], (SUBLANES, w))
    h_ref[...] = h
    hout_ref[...] = h[0:1, :]
    y = (jax.nn.gelu(gate_br) * hs_s[...]).astype(BF)
    o_ref[...] = x + _dot(y, wo_ref[...])


def _lru_prompt(x2d, g, w_in, cw, cb, wa, ba, wx, bx, lam, wo, nb, rows):
    m, d = x2d.shape
    w = cw.shape[1]
    nt = m // nb // rows
    full = lambda a: pl.BlockSpec(a.shape, lambda i, j: (0,) * a.ndim)
    tile = pl.BlockSpec((rows, d), lambda i, j: (i * nt + j, 0))
    args = (g, w_in, cw, cb, wa, ba, wx, bx, lam, wo)
    return pl.pallas_call(
        functools.partial(_lru_prompt_kernel, rows=rows, w=w),
        grid=(nb, nt),
        in_specs=[tile] + [full(a) for a in args],
        out_specs=[tile,
                   pl.BlockSpec((None, 1, w), lambda i, j: (i, 0, 0)),
                   pl.BlockSpec((None, CONV_WIDTH - 1, w), lambda i, j: (i, 0, 0))],
        out_shape=[jax.ShapeDtypeStruct((m, d), F32),
                   jax.ShapeDtypeStruct((nb, 1, w), F32),
                   jax.ShapeDtypeStruct((nb, CONV_WIDTH - 1, w), F32)],
        scratch_shapes=[pltpu.VMEM((SUBLANES, w), F32), pltpu.VMEM((SUBLANES, w), F32),
                        pltpu.VMEM((rows, w), F32), pltpu.VMEM((rows, w), F32), pltpu.VMEM((rows, w), F32),
                        pltpu.VMEM((SUBLANES + rows, w), F32)],
        compiler_params=_cparams(("parallel", "arbitrary")),
        name="lru_prompt",
    )(x2d, *args)


def _lru_params(norm_g, w_in_c, conv_w, conv_b, w_a, b_a, w_x, b_x, lam, w_out_c):
    r = lambda v: v.reshape(1, -1)
    return (r(norm_g), w_in_c.astype(BF), conv_w, r(conv_b), w_a.astype(BF), r(b_a), w_x.astype(BF), r(b_x),
            r(lam), w_out_c.astype(BF))


def _lru_decode_kernel(x_ref, g_ref, win_ref, cw_ref, cb_ref, wa_ref, ba_ref, wx_ref, bx_ref, lam_ref, wo_ref,
                       c0_ref, c1_ref, c2_ref, h0_ref, o_ref, hout_ref, xbr_ref, *, w):
    x = x_ref[...]
    xn = _rms(x, g_ref[...]).astype(BF)
    proj = _dot(xn, win_ref[...])
    gate_br = proj[:, 0:w]
    x_br = proj[:, w:2 * w]
    xbr_ref[...] = x_br
    xc = (cb_ref[...] + c0_ref[...] * cw_ref[0:1, :] + c1_ref[...] * cw_ref[1:2, :]
          + c2_ref[...] * cw_ref[2:3, :] + x_br * cw_ref[3:4, :])
    sp = _softplus(-lam_ref[...])
    a, gx = _lru_gates(xc, wa_ref, ba_ref, wx_ref, bx_ref, sp)
    h = a * h0_ref[...] + gx
    hout_ref[...] = h
    y = (jax.nn.gelu(gate_br) * h).astype(BF)
    o_ref[...] = x + _dot(y, wo_ref[...])


def _lru_decode(x, params, conv_state, h0):
    m, d = x.shape
    w = h0.shape[1]
    args = (x,) + tuple(params) + (conv_state[:, 0], conv_state[:, 1], conv_state[:, 2], h0)
    full = lambda a: pl.BlockSpec(a.shape, lambda i: (0,) * a.ndim)
    return pl.pallas_call(
        functools.partial(_lru_decode_kernel, w=w),
        grid=(1,),
        in_specs=[full(a) for a in args],
        out_specs=[pl.BlockSpec((m, d), lambda i: (0, 0)), pl.BlockSpec((m, w), lambda i: (0, 0)),
                   pl.BlockSpec((m, w), lambda i: (0, 0))],
        out_shape=[jax.ShapeDtypeStruct((m, d), F32), jax.ShapeDtypeStruct((m, w), F32),
                   jax.ShapeDtypeStruct((m, w), F32)],
        compiler_params=_cparams(("arbitrary",)),
        name="lru_decode",
    )(*args)


HALVES_PER_PAGE = PAGE_SIZE // CMP_STRIDE


def _compress_decode_kernel(pt_ref, *refs):
    del pt_ref
    np_ = PAGES_PER_STEP
    k_pages = refs[0:np_]
    v_pages = refs[np_:2 * np_]
    pek_ref, w1k_ref, w2k_ref, pev_ref, w1v_ref, w2v_ref, out_ref, carry_ref, rows_ref = refs[2 * np_:]
    rows = np_ * HALVES_PER_PAGE

    @pl.when(pl.program_id(1) == 0)
    def _():
        carry_ref[...] = jnp.zeros_like(carry_ref)

    rowi = lax.broadcasted_iota(jnp.int32, (rows, CMP_HIDDEN), 0)
    branches = ((k_pages, pek_ref, w1k_ref, w2k_ref), (v_pages, pev_ref, w1v_ref, w2v_ref))
    for c, (pages, _, _, _) in enumerate(branches):
        for k, p in enumerate(pages):
            rows_ref[c, k * PAGE_SIZE:(k + 1) * PAGE_SIZE, :] = p[...].T

    for c, (pages, pe_ref, w1_ref, w2_ref) in enumerate(branches):
        def load_rows(s, c=c):
            return rows_ref[c, pl.ds(s, rows, stride=CMP_STRIDE), :]

        lead, trail = _block_mlp_hidden(load_rows, rows, pe_ref, w1_ref)
        for g in range(N_KV):
            a0 = lead[g * rows:(g + 1) * rows]
            a1 = trail[g * rows:(g + 1) * rows]
            slot = c * N_KV + g
            prev = jnp.where(rowi == 0, carry_ref[slot, SUBLANES - 1:SUBLANES, :], pltpu.roll(a0, 1, 0))
            carry_ref[slot] = a0[rows - SUBLANES:rows, :]
            tok = _dot(_silu(prev + a1).astype(BF), w2_ref[...])
            out_ref[g, :, c * HEAD_DIM:(c + 1) * HEAD_DIM] = tok


def _compress_decode(cache_t, pt_flat, db, n_pages, pek, w1k, w2k, pev, w1v, w2v):
    np_ = PAGES_PER_STEP
    n_steps = n_pages // np_
    rows = np_ * HALVES_PER_PAGE
    n_half = n_pages * HALVES_PER_PAGE

    def page_spec(k, rowblk):
        return pl.BlockSpec((None, LANES, PAGE_SIZE),
                            lambda i, j, pt: (pt[i * n_pages + j * np_ + k], rowblk, 0))

    full = lambda a: pl.BlockSpec(a.shape, lambda i, j, pt: (0,) * a.ndim)
    grid_spec = pltpu.PrefetchScalarGridSpec(
        num_scalar_prefetch=1,
        grid=(db, n_steps),
        in_specs=[page_spec(k, 0) for k in range(np_)] + [page_spec(k, 1) for k in range(np_)]
        + [full(pek), full(w1k), full(w2k), full(pev), full(w1v), full(w2v)],
        out_specs=pl.BlockSpec((None, N_KV, rows, LANES), lambda i, j, pt: (i, 0, j, 0)),
        scratch_shapes=[pltpu.VMEM((2 * N_KV, SUBLANES, CMP_HIDDEN), F32),
                        pltpu.VMEM((2, np_ * PAGE_SIZE, LANES), F32)],
    )
    return pl.pallas_call(
        _compress_decode_kernel,
        grid_spec=grid_spec,
        out_shape=jax.ShapeDtypeStruct((db, N_KV, n_half, LANES), F32),
        compiler_params=_cparams(("parallel", "arbitrary")),
        name="compress_decode",
    )(pt_flat, *([cache_t] * (2 * np_)), pek, w1k, w2k, pev, w1v, w2v)


def _group_rows(top, a, b):
    return jnp.where(top, a, b)


def _nsa_decode_a_kernel(q_ref, qr_ref, cmp_ref, win_ref, new_ref, ov_ref, ocmp_ref, owin_ref, idx_ref, *,
                         pos, n_blocks):
    nh = N_HEADS
    top = lax.broadcasted_iota(jnp.int32, (nh, 1), 0) < Q_PER_KV
    q = q_ref[...].astype(BF)
    n_tok = cmp_ref.shape[1]
    lane = lax.broadcasted_iota(jnp.int32, (nh, n_tok), 1)
    cmask = (lane >= 1) & (((lane - 1) * CMP_STRIDE + (2 * CMP_STRIDE - 1)) <= pos)
    s = _group_rows(top, _dot_nt(q, cmp_ref[0, :, 0:HEAD_DIM].astype(BF)),
                    _dot_nt(q, cmp_ref[1, :, 0:HEAD_DIM].astype(BF))) * SCALE
    p = _msoftmax(s, cmask)
    pb = p.astype(BF)
    ocmp_ref[...] = _group_rows(top, _dot(pb, cmp_ref[0, :, HEAD_DIM:LANES].astype(BF)),
                                _dot(pb, cmp_ref[1, :, HEAD_DIM:LANES].astype(BF)))

    imp8 = _dot3(p, ov_ref[...])
    nj = ov_ref.shape[1]
    cur = pos // SLC_BLOCK
    j = lax.broadcasted_iota(jnp.int32, (1, nj), 1)
    valid = (j * SLC_BLOCK) <= pos
    forced = (j == 0) | (j == cur) | (j == cur - 1)
    ri = lax.broadcasted_iota(jnp.int32, (nj, nj), 0)
    ci = lax.broadcasted_iota(jnp.int32, (nj, nj), 1)
    k_lane = lax.broadcasted_iota(jnp.int32, (nj, LANES), 1).astype(F32)
    jvals = lax.broadcasted_iota(jnp.int32, (8, nj), 1).astype(F32).astype(BF)
    idx_rows = []
    for g in range(N_KV):
        imp = jnp.sum(imp8[g * Q_PER_KV:(g + 1) * Q_PER_KV, :], axis=0, keepdims=True)
        imp = jnp.where(valid & forced, FORCE_SCORE, imp)
        imp = jnp.where(valid, imp, -FORCE_SCORE)
        imp = jnp.where(j < n_blocks, imp, F32_LOWEST)
        impb = jnp.broadcast_to(imp, (nj, nj))
        col = jnp.sum(jnp.where(ri == ci, impb, 0.0), axis=1, keepdims=True)
        ahead = jnp.where(impb == col, jnp.where(ci < ri, 1.0, 0.0), jnp.where(impb > col, 1.0, 0.0))
        rank_col = jnp.sum(ahead, axis=1, keepdims=True)
        onehot = jnp.where(rank_col == k_lane, 1.0, 0.0).astype(BF)
        idx_rows.append(_dot(jvals, onehot)[0:1, :])
    idx = jnp.concatenate(idx_rows + [jnp.zeros((SUBLANES - N_KV, LANES), F32)], axis=0)
    idx_ref[...] = idx.astype(jnp.int32)

    qr = qr_ref[...]
    qrb = qr.astype(BF)
    n_win = win_ref.shape[1]
    s = _group_rows(top, _dot(qrb, win_ref[0:HEAD_DIM, :].astype(BF)),
                    _dot(qrb, win_ref[HEAD_DIM:LANES, :].astype(BF))) * SCALE
    wl = lax.broadcasted_iota(jnp.int32, (nh, n_win), 1)
    dist = n_win - wl
    wmask = (dist >= 0) & (dist < WINDOW)
    new = new_ref[...]
    knew = _group_rows(top, new[:, 0:HEAD_DIM], new[:, HEAD_DIM:LANES])
    vnew = _group_rows(top, new[:, LANES:LANES + HEAD_DIM], new[:, LANES + HEAD_DIM:KV_DIM])
    s_new = jnp.sum(qr * knew, axis=1, keepdims=True) * SCALE
    sm = jnp.where(wmask, s, NEG_BIG)
    m = jnp.maximum(jnp.max(sm, axis=1, keepdims=True), s_new)
    e = jnp.where(wmask, jnp.exp(sm - m), 0.0)
    e_new = jnp.exp(s_new - m)
    d = jnp.sum(e, axis=1, keepdims=True) + e_new
    eb = e.astype(BF)
    o = _group_rows(top, _dot_nt(eb, win_ref[LANES:LANES + HEAD_DIM, :].astype(BF)),
                    _dot_nt(eb, win_ref[LANES + HEAD_DIM:KV_DIM, :].astype(BF))) + e_new * vnew
    owin_ref[...] = o / d


def _nsa_decode_a(q3, qr3, cmp_s, win, kvw_new3, pos):
    db = q3.shape[0]
    n_tok = cmp_s.shape[2]
    n_blocks = -(-(pos + 1) // SLC_BLOCK)
    nj = -(-n_blocks // LANES) * LANES
    ov = _overlap_matrix(n_tok, nj, 1)
    per_b = lambda a: pl.BlockSpec((None,) + a.shape[1:], lambda i: (i,) + (0,) * (a.ndim - 1))
    head = pl.BlockSpec((None, N_HEADS, HEAD_DIM), lambda i: (i, 0, 0))
    return pl.pallas_call(
        functools.partial(_nsa_decode_a_kernel, pos=pos, n_blocks=n_blocks),
        grid=(db,),
        in_specs=[head, head, per_b(cmp_s), per_b(win), per_b(kvw_new3), pl.BlockSpec(ov.shape, lambda i: (0, 0))],
        out_specs=[head, head, pl.BlockSpec((None, SUBLANES, LANES), lambda i: (i, 0, 0))],
        out_shape=[jax.ShapeDtypeStruct((db, N_HEADS, HEAD_DIM), F32),
                   jax.ShapeDtypeStruct((db, N_HEADS, HEAD_DIM), F32),
                   jax.ShapeDtypeStruct((db, SUBLANES, LANES), jnp.int32)],
        compiler_params=_cparams(("parallel",)),
        name="nsa_decode_select",
    )(q3, qr3, cmp_s, win, kvw_new3, ov)


def _nsa_decode_b_kernel(idx_ref, pt_ref, qr_ref, *refs, n_past_blocks, cur):
    del pt_ref
    nb = N_KV * N_SELECT
    blk_refs = refs[0:nb]
    new_ref, ocmp_ref, owin_ref, sm_ref, o_ref = refs[nb:]
    b = pl.program_id(0)
    nh = N_HEADS
    top = lax.broadcasted_iota(jnp.int32, (nh, 1), 0) < Q_PER_KV
    qr = qr_ref[...]
    qrb = qr.astype(BF)
    per_page = PAGE_SIZE // SLC_BLOCK
    n_keys = N_SELECT * PAGE_SIZE
    lane = lax.broadcasted_iota(jnp.int32, (nh, n_keys), 1)
    lane_page = lane // PAGE_SIZE
    lane_sub = (lane // SLC_BLOCK) % per_page
    s_g = []
    v_g = []
    ok_g = []
    has_new = []
    for g in range(N_KV):
        blks = blk_refs[g * N_SELECT:(g + 1) * N_SELECT]
        kt = jnp.concatenate([r[g * HEAD_DIM:(g + 1) * HEAD_DIM, :] for r in blks], axis=1).astype(BF)
        v_g.append(jnp.concatenate([r[LANES + g * HEAD_DIM:LANES + (g + 1) * HEAD_DIM, :] for r in blks],
                                   axis=1).astype(BF))
        s_g.append(_dot(qrb, kt))
        ok = jnp.zeros((nh, n_keys), F32)
        new_sel = jnp.zeros((), F32)
        for k in range(N_SELECT):
            jk = idx_ref[(b * N_KV + g) * N_SELECT + k]
            hit = (lane_page == k) & (lane_sub == jk % per_page)
            ok = jnp.where(hit, jnp.where(jk < n_past_blocks, 1.0, 0.0), ok)
            new_sel = jnp.maximum(new_sel, jnp.where(jk == cur, 1.0, 0.0))
        ok_g.append(ok)
        has_new.append(new_sel)
    s = _group_rows(top, s_g[0], s_g[1]) * SCALE
    mask = _group_rows(top, ok_g[0], ok_g[1]) > 0.5
    new_on = _group_rows(top, has_new[0], has_new[1]) > 0.5
    new = new_ref[...]
    knew = _group_rows(top, new[:, 0:HEAD_DIM], new[:, HEAD_DIM:LANES])
    vnew = _group_rows(top, new[:, LANES:LANES + HEAD_DIM], new[:, LANES + HEAD_DIM:KV_DIM])
    s_new = jnp.where(new_on, jnp.sum(qr * knew, axis=1, keepdims=True) * SCALE, NEG_BIG)
    sm = jnp.where(mask, s, NEG_BIG)
    m = jnp.maximum(jnp.max(sm, axis=1, keepdims=True), s_new)
    e = jnp.where(mask, jnp.exp(sm - m), 0.0)
    e_new = jnp.where(new_on, jnp.exp(s_new - m), 0.0)
    d = jnp.sum(e, axis=1, keepdims=True) + e_new
    eb = e.astype(BF)
    o_slc = ((_group_rows(top, _dot_nt(eb, v_g[0]), _dot_nt(eb, v_g[1])) + e_new * vnew)
             / jnp.where(d > 0.0, d, 1.0))

    sig = jnp.broadcast_to(jax.nn.sigmoid(sm_ref[...]), (nh, LANES))
    hl = lax.broadcasted_iota(jnp.int32, (nh, LANES), 1)
    hr = lax.broadcasted_iota(jnp.int32, (nh, LANES), 0)

    def gate(br):
        return jnp.sum(jnp.where(hl == N_BRANCH * hr + br, sig, 0.0), axis=1, keepdims=True)

    o_ref[...] = gate(0) * ocmp_ref[...] + gate(1) * o_slc + gate(2) * owin_ref[...]


def _nsa_decode_b(idx_flat, pt_flat, qr3, slc_pages_t, kvs_new3, o_cmp, o_win, small3, pos, n_pages):
    db = qr3.shape[0]
    n_past_blocks = pos // SLC_BLOCK
    cur = pos // SLC_BLOCK
    per_page = PAGE_SIZE // SLC_BLOCK

    def blk_spec(g, k):
        def imap(i, idx, pt):
            jk = jnp.minimum(idx[(i * N_KV + g) * N_SELECT + k], n_past_blocks - 1)
            return (pt[i * n_pages + jk // per_page], 0, 0)
        return pl.BlockSpec((None, KV_DIM, PAGE_SIZE), imap)

    head = pl.BlockSpec((None, N_HEADS, HEAD_DIM), lambda i, idx, pt: (i, 0, 0))
    row3 = lambda a: pl.BlockSpec((None,) + a.shape[1:], lambda i, idx, pt: (i, 0, 0))
    grid_spec = pltpu.PrefetchScalarGridSpec(
        num_scalar_prefetch=2,
        grid=(db,),
        in_specs=[head] + [blk_spec(g, k) for g in range(N_KV) for k in range(N_SELECT)]
        + [row3(kvs_new3), head, head, row3(small3)],
        out_specs=head,
    )
    return pl.pallas_call(
        functools.partial(_nsa_decode_b_kernel, n_past_blocks=n_past_blocks, cur=cur),
        grid_spec=grid_spec,
        out_shape=jax.ShapeDtypeStruct((db, N_HEADS, HEAD_DIM), F32),
        compiler_params=_cparams(("arbitrary",)),
        name="nsa_decode_attend",
    )(idx_flat, pt_flat, qr3, *([slc_pages_t] * (N_KV * N_SELECT)), kvs_new3, o_cmp, o_win, small3)


def _ssd_decode_kernel(zx_ref, cst_ref, sm_ref, h0_ref, cw_ref, cb_ref, dtb_ref, alog_ref, dskip_ref, nrm_ref,
                       y_ref, hout_ref):
    z = zx_ref[:, 0:SSD_INNER]
    xbc = zx_ref[:, SSD_INNER:]
    conv = cb_ref[...] + xbc * cw_ref[CONV_WIDTH - 1:CONV_WIDTH, :]
    for k in range(CONV_WIDTH - 1):
        conv = conv + cst_ref[k:k + 1, :] * cw_ref[k:k + 1, :]
    xbc_c = _silu(conv)
    xs = xbc_c[:, 0:SSD_INNER]
    dt_full = _softplus(sm_ref[...] + dtb_ref[...])
    da_full = jnp.exp(dt_full * (-jnp.exp(alog_ref[...])))
    p = SSD_HEAD_DIM
    eye = lax.broadcasted_iota(jnp.int32, (p, p), 0) == lax.broadcasted_iota(jnp.int32, (p, p), 1)
    ys = []
    for h in range(SSD_HEADS):
        g = h // (SSD_HEADS // 2)
        col = DT_COL + h
        xdt = xs[:, h * p:(h + 1) * p] * dt_full[:, col:col + 1]
        xcol = jnp.sum(jnp.where(eye, jnp.broadcast_to(xdt, (p, p)), 0.0), axis=1, keepdims=True)
        bm = xbc_c[:, SSD_INNER + g * SSD_STATE:SSD_INNER + (g + 1) * SSD_STATE]
        cm = xbc_c[:, SSD_INNER + (2 + g) * SSD_STATE:SSD_INNER + (3 + g) * SSD_STATE]
        h_new = da_full[:, col:col + 1] * h0_ref[h] + xcol * bm
        hout_ref[h] = h_new
        ys.append(_dot_nt(jnp.broadcast_to(cm, (8, SSD_STATE)).astype(BF), h_new.astype(BF))[0:1, :])
    y = jnp.concatenate(ys, axis=1) + dskip_ref[...] * xs
    y = y * _silu(z)
    y_ref[...] = _rms(y, nrm_ref[...])


def _ssd_decode(zx3, conv_state, small3, h0, params):
    db = zx3.shape[0]
    cw, cb, dtb, alog, dskip, nrm = params
    per_b = lambda a: pl.BlockSpec((None,) + a.shape[1:], lambda i: (i,) + (0,) * (a.ndim - 1))
    full = lambda a: pl.BlockSpec(a.shape, lambda i: (0,) * a.ndim)
    return pl.pallas_call(
        _ssd_decode_kernel,
        grid=(db,),
        in_specs=[per_b(zx3), per_b(conv_state), per_b(small3), per_b(h0), full(cw), full(cb), full(dtb), full(alog),
                  full(dskip), full(nrm)],
        out_specs=[pl.BlockSpec((None, 1, SSD_INNER), lambda i: (i, 0, 0)), per_b(h0)],
        out_shape=[jax.ShapeDtypeStruct((db, 1, SSD_INNER), F32), jax.ShapeDtypeStruct(h0.shape, F32)],
        compiler_params=_cparams(("parallel",)),
        name="ssd_decode",
    )(zx3, conv_state, small3, h0, cw, cb, dtb, alog, dskip, nrm)


def kernel(x_prompt, x_sample, cache_kv_cmp, cache_kv_slc, cache_kv_win, state_ssm, state_ssd_conv, state_lru,
           state_lru_conv, page_table, norm_mix, norm_ffn, norm_final, w_ffn_gate, w_ffn_up, w_ffn_down, w_in_a,
           w_out_a, cmp_pe_k, cmp_w1_k, cmp_w2_k, cmp_pe_v, cmp_w1_v, cmp_w2_v, ssd_conv_w, ssd_conv_b, ssd_dt_bias,
           ssd_a_log, ssd_d, ssd_norm, w_in_c, lru_conv_w, lru_conv_b, lru_w_a, lru_b_a, lru_w_x, lru_b_x,
           lru_lambda, w_out_c):
    b, t, d = x_prompt.shape
    db = x_sample.shape[0]
    n_pages = page_table.shape[1]
    pos_s = n_pages * PAGE_SIZE
    m = b * t
    kv_shape = (2, N_KV, HEAD_DIM)

    wm, ws = _prep_w_in_a(w_in_a[0])
    wo_nsa = w_out_a[0, :Q_DIM].astype(BF)
    wo_ssd = w_out_a[0, Q_DIM:].astype(BF)
    cmp_k = _prep_cmp_w(cmp_pe_k[0], cmp_w1_k[0], cmp_w2_k[0])
    cmp_v = _prep_cmp_w(cmp_pe_v[0], cmp_w1_v[0], cmp_w2_v[0])
    ssd_par = _ssd_params(ssd_conv_w[0], ssd_conv_b[0], ssd_dt_bias[0], ssd_a_log[0], ssd_d[0], ssd_norm[0])
    lru_par = _lru_params(norm_mix[1], w_in_c[0], lru_conv_w[0], lru_conv_b[0], lru_w_a[0], lru_b_a[0], lru_w_x[0],
                          lru_b_x[0], lru_lambda[0], w_out_c[0])
    ffn_w = [(norm_ffn[l].reshape(1, d), w_ffn_gate[l].astype(BF), w_ffn_up[l].astype(BF), w_ffn_down[l].astype(BF))
             for l in range(2)]
    gfin = norm_final.reshape(1, d)

    xp = x_prompt.reshape(m, d)
    cos_p, sin_p = _rope_tables(jnp.arange(t, dtype=jnp.int32))
    q, qr, kvc, kvs_t, kvw_t, zx, small, kvc_t = _inproj_a(xp, norm_mix[0], wm, ws, cos_p, sin_p, TILE_PROJ, b, True)
    cmp_p = _compress_prompt(kvc, b, t, *cmp_k, *cmp_v)
    o_nsa = _nsa_prompt(q, qr, cmp_p, kvs_t, kvw_t, small, b, t)
    y_ssd, ssm_p = _ssd_prompt(zx, small, ssd_par, b, t)
    rowmap = lambda i: (i, 0)
    x2 = _ffn(xp, *ffn_w[0], gfin, TILE_FFN, False, rowmap, rowmap, (m, d), mixer=(o_nsa, y_ssd, wo_nsa, wo_ssd))
    x3, lru_p, lru_conv_p = _lru_prompt(x2, *lru_par, b, TILE_LRU)
    y_prompt = _ffn(x3, *ffn_w[1], gfin, TILE_FFN, True, rowmap, rowmap, (m, d))
    w_keep = min(WINDOW, t)
    to_cache = lambda a: a.reshape((b,) + kv_shape + (a.shape[-1],)).transpose(0, 4, 1, 2, 3)[None]
    kv_cmp_p = to_cache(kvc_t)
    kv_slc_p = to_cache(kvs_t)
    kv_win_p = to_cache(kvw_t[:, :, t - w_keep:])
    ssd_conv_p = zx.reshape(b, t, -1)[:, t - (CONV_WIDTH - 1):, SSD_INNER:][None]
    lru_p = lru_p.reshape(b, -1)
    lru_conv_p = lru_conv_p[None]

    xs = x_sample.reshape(db, d)
    cos_s, sin_s = _rope_tables(jnp.full((db,), pos_s, dtype=jnp.int32))
    q_s, qr_s, kvc_s, kvs_s, kvw_s, zx_s, small_s = _inproj_a(xs, norm_mix[0], wm, ws, cos_s, sin_s, db, 1, False)
    pt_flat = page_table.reshape(-1)
    n_phys = cache_kv_cmp.shape[1]
    feature_major = lambda a: a.transpose(0, 2, 3, 4, 1).reshape(a.shape[0], KV_DIM, a.shape[1])
    cmp_s = _compress_decode(feature_major(cache_kv_cmp[0]), pt_flat, db, n_pages, *cmp_k, *cmp_v)
    win_buf = cache_kv_win[0].reshape(db, -1, KV_DIM)
    head3 = lambda a: a.reshape(db, N_HEADS, HEAD_DIM)
    o_cmp_s, o_win_s, idx = _nsa_decode_a(head3(q_s), head3(qr_s), cmp_s, feature_major(cache_kv_win[0]),
                                          kvw_s.reshape(db, 1, KV_DIM), pos_s)
    idx_flat = idx[:, :N_KV, :N_SELECT].reshape(-1)
    o_nsa_s = _nsa_decode_b(idx_flat, pt_flat, head3(qr_s), feature_major(cache_kv_slc[0]),
                            kvs_s.reshape(db, 1, KV_DIM), o_cmp_s, o_win_s, small_s.reshape(db, 1, LANES), pos_s,
                            n_pages)
    y_ssd_s, ssm_s = _ssd_decode(zx_s.reshape(db, 1, -1), state_ssd_conv[0], small_s.reshape(db, 1, LANES),
                                 state_ssm[0], ssd_par)
    x2_s = _ffn(xs, *ffn_w[0], gfin, db, False, lambda i: (i, 0), lambda i: (i, 0), (db, d),
                mixer=(o_nsa_s.reshape(db, Q_DIM), y_ssd_s.reshape(db, SSD_INNER), wo_nsa, wo_ssd))
    x3_s, lru_s, xbr_s = _lru_decode(x2_s, lru_par, state_lru_conv[0], state_lru[0])
    y_sample = _ffn(x3_s, *ffn_w[1], gfin, db, True, lambda i: (i, 0), lambda i: (i, 0), (db, d))
    kv_cmp_s = kvc_s.reshape((1, db, 1) + kv_shape)
    kv_slc_s = kvs_s.reshape((1, db, 1) + kv_shape)
    kv_win_s = jnp.concatenate([win_buf[:, 1:], kvw_s[:, None, :]], axis=1).reshape(
        (1, db, win_buf.shape[1]) + kv_shape)
    ssd_conv_s = jnp.concatenate([state_ssd_conv[0][:, 1:], zx_s[:, None, SSD_INNER:]], axis=1)[None]
    lru_conv_s = jnp.concatenate([state_lru_conv[0][:, 1:], xbr_s[:, None, :]], axis=1)[None]

    return (y_prompt.reshape(b, t, d), y_sample.reshape(db, 1, d),
            kv_cmp_p, kv_slc_p, kv_win_p, ssm_p[None], ssd_conv_p, lru_p[None], lru_conv_p,
            kv_cmp_s, kv_slc_s, kv_win_s, ssm_s[None], ssd_conv_s, lru_s[None], lru_conv_s)
```

```python
import functools
import math

import jax
import jax.numpy as jnp
from jax import lax
from jax.experimental import pallas as pl
from jax.experimental.pallas import tpu as pltpu

BF = jnp.bfloat16
F32 = jnp.float32

HEAD_DIM = 64
N_KV = 2
Q_PER_KV = 4
N_HEADS = N_KV * Q_PER_KV
CMP_STRIDE = 16
CMP_HIDDEN = 256
SLC_BLOCK = 64
N_SELECT = 16
WINDOW = 512
PAGE_SIZE = 128
ROPE_THETA = 10000.0
FORCE_SCORE = 1e9
NEG_BIG = -1e30
SSD_HEADS = 8
SSD_HEAD_DIM = 64
SSD_STATE = 128
SSD_INNER = SSD_HEADS * SSD_HEAD_DIM
SSD_CHUNK = 128
CONV_WIDTH = 4
LRU_BLOCK = 128
LRU_C = 8.0
RMS_EPS = 1e-6
SCALE = HEAD_DIM ** -0.5
LOG2E = math.log2(math.e)
Q_DIM = N_HEADS * HEAD_DIM
KV_DIM = 2 * N_KV * HEAD_DIM
GATE_DIM = 3 * N_HEADS
DT_COL = GATE_DIM
N_BRANCH = 3
LANES = 128
SUBLANES = 8
BF16_SUBLANES = 16
F32_LOWEST = -3e38
VMEM_LIMIT = 56 * 1024 * 1024

TILE_PROJ = 512
TILE_FFN = 512
TILE_LRU = 512
TILE_NSA_Q = 512
TILE_NSA_K = 512
SSD_CHUNKS_PER_STEP = 4
PAGES_PER_STEP = 64


def _cparams(sem):
    return pltpu.CompilerParams(dimension_semantics=sem, vmem_limit_bytes=VMEM_LIMIT)


def _dot(a, b):
    return jnp.dot(a, b, preferred_element_type=F32)


def _dot_nt(a, b):
    return lax.dot_general(a, b, (((1,), (1,)), ((), ())), preferred_element_type=F32)


def _dot_tn(a, b):
    return lax.dot_general(a, b, (((0,), (0,)), ((), ())), preferred_element_type=F32)


def _split3(x):
    hi = x.astype(BF)
    r = x - hi.astype(F32)
    mid = r.astype(BF)
    lo = (r - mid.astype(F32)).astype(BF)
    return hi, mid, lo


def _dot3(x, m01):
    hi, mid, lo = _split3(x)
    return _dot(hi, m01) + _dot(mid, m01) + _dot(lo, m01)


def _dot3_left(m01, x):
    hi, mid, lo = _split3(x)
    return _dot(m01, hi) + _dot(m01, mid) + _dot(m01, lo)


def _rms(x, g):
    y = x * lax.rsqrt(jnp.mean(x * x, axis=-1, keepdims=True) + RMS_EPS)
    return y * g


def _silu(x):
    return x * jax.nn.sigmoid(x)


def _softplus(x):
    return jnp.maximum(x, 0.0) + jnp.log1p(jnp.exp(-jnp.abs(x)))


def _msoftmax_parts(s, mask):
    s = jnp.where(mask, s, NEG_BIG)
    m = jnp.max(s, axis=-1, keepdims=True)
    e = jnp.where(mask, jnp.exp(s - m), 0.0)
    return m, e


def _msoftmax(s, mask):
    _, e = _msoftmax_parts(s, mask)
    d = jnp.sum(e, axis=-1, keepdims=True)
    return e / jnp.where(d > 0.0, d, 1.0)


def _rope_tables(pos):
    half = HEAD_DIM // 2
    inv_freq = ROPE_THETA ** (-jnp.arange(half, dtype=F32) / half)
    ang = pos.astype(F32)[:, None] * inv_freq[None, :]
    cos = jnp.cos(ang)
    sin = jnp.sin(ang)
    cos2 = jnp.tile(jnp.concatenate([cos, cos], axis=-1), (1, LANES // HEAD_DIM))
    sin2 = jnp.tile(jnp.concatenate([-sin, sin], axis=-1), (1, LANES // HEAD_DIM))
    return cos2, sin2


def _inproj_a_kernel(x_ref, g_ref, wm_ref, ws_ref, cos_ref, sin_ref,
                     q_ref, qr_ref, kvc_ref, kvs_ref, kvw_ref, zx_ref, sm_ref, *kvc_t_ref, feature_major):
    xn = _rms(x_ref[...], g_ref[...]).astype(BF)
    cos = cos_ref[...]
    sin = sin_ref[...]
    lane = lax.broadcasted_iota(jnp.int32, cos.shape, 1)
    first = (lane % HEAD_DIM) < (HEAD_DIM // 2)

    def rope(v):
        rot = jnp.where(first, pltpu.roll(v, LANES - HEAD_DIM // 2, 1), pltpu.roll(v, HEAD_DIM // 2, 1))
        return v * cos + rot * sin

    def store_kv(ref, k, v):
        if feature_major:
            ref[0:LANES, :] = k.T
            ref[LANES:KV_DIM, :] = v.T
        else:
            ref[:, 0:LANES] = k
            ref[:, LANES:KV_DIM] = v

    q = _dot(xn, wm_ref[:, 0:Q_DIM])
    for c in range(Q_DIM // LANES):
        cols = slice(c * LANES, (c + 1) * LANES)
        if feature_major:
            q_ref[cols, :] = q[:, cols].T
            qr_ref[cols, :] = rope(q[:, cols]).T
        else:
            q_ref[:, cols] = q[:, cols]
            qr_ref[:, cols] = rope(q[:, cols])
    o = Q_DIM
    kvc = _dot(xn, wm_ref[:, o:o + KV_DIM])
    kvc_ref[...] = kvc
    if feature_major:
        store_kv(kvc_t_ref[0], kvc[:, 0:LANES], kvc[:, LANES:KV_DIM])
    o += KV_DIM
    kvs = _dot(xn, wm_ref[:, o:o + KV_DIM])
    store_kv(kvs_ref, rope(kvs[:, 0:LANES]), kvs[:, LANES:KV_DIM])
    o += KV_DIM
    kvw = _dot(xn, wm_ref[:, o:o + KV_DIM])
    store_kv(kvw_ref, rope(kvw[:, 0:LANES]), kvw[:, LANES:KV_DIM])
    o += KV_DIM
    zx_ref[...] = _dot(xn, wm_ref[:, o:])
    sm_ref[...] = _dot(xn, ws_ref[...])


def _prep_w_in_a(w_in_a):
    a = Q_DIM + 3 * KV_DIM
    gate = w_in_a[:, a:a + GATE_DIM]
    rest = w_in_a[:, a + GATE_DIM:]
    zx_w = rest[:, :rest.shape[1] - SSD_HEADS]
    dt = rest[:, rest.shape[1] - SSD_HEADS:]
    main = jnp.concatenate([w_in_a[:, :a], zx_w], axis=1).astype(BF)
    small = jnp.concatenate([gate, dt], axis=1)
    small = jnp.pad(small, ((0, 0), (0, LANES - small.shape[1]))).astype(BF)
    return main, small


def _inproj_a(x2d, g, wm, ws, cos2, sin2, tm, nb, feature_major):
    m, d = x2d.shape
    nt = m // nb // tm
    zx_dim = wm.shape[1] - Q_DIM - 3 * KV_DIM
    row = lambda w: pl.BlockSpec((tm, w), lambda i, j: (i * nt + j, 0))
    row_shape = lambda w: jax.ShapeDtypeStruct((m, w), F32)
    const = lambda i, j: (0, 0)
    if feature_major:
        fm = lambda w: pl.BlockSpec((None, w, tm), lambda i, j: (i, 0, j))
        fm_shape = lambda w: jax.ShapeDtypeStruct((nb, w, m // nb), F32)
    else:
        fm, fm_shape = row, row_shape
    kv, kv_shape = fm(KV_DIM), fm_shape(KV_DIM)
    out_specs = [fm(Q_DIM), fm(Q_DIM), row(KV_DIM), kv, kv, row(zx_dim), row(LANES)]
    out_shape = [fm_shape(Q_DIM), fm_shape(Q_DIM), row_shape(KV_DIM), kv_shape, kv_shape, row_shape(zx_dim),
                 row_shape(LANES)]
    if feature_major:
        out_specs.append(kv)
        out_shape.append(kv_shape)
    pos = pl.BlockSpec((tm, LANES), lambda i, j: (j, 0))
    return pl.pallas_call(
        functools.partial(_inproj_a_kernel, feature_major=feature_major),
        grid=(nb, nt),
        in_specs=[row(d), pl.BlockSpec((1, d), const), pl.BlockSpec(wm.shape, const), pl.BlockSpec(ws.shape, const),
                  pos, pos],
        out_specs=out_specs,
        out_shape=out_shape,
        compiler_params=_cparams(("parallel", "parallel")),
        name="inproj_a",
    )(x2d, g.reshape(1, d), wm, ws, cos2, sin2)


CMP_PACK = 2 * LANES // HEAD_DIM


def _block_mlp_hidden(load_rows, n_rows, pe_ref, w1_ref):
    low = lax.broadcasted_iota(jnp.int32, (n_rows, LANES), 1) < HEAD_DIM

    def split_groups(a, b):
        return (jnp.where(low, a, pltpu.roll(b, HEAD_DIM, 1)), jnp.where(low, pltpu.roll(a, HEAD_DIM, 1), b))

    lead = None
    trail = None
    for u in range(CMP_STRIDE // CMP_PACK):
        xs = [load_rows(u * CMP_PACK + j) for j in range(CMP_PACK)]
        g01 = split_groups(xs[0], xs[1])
        g23 = split_groups(xs[2], xs[3])
        x = jnp.concatenate([jnp.concatenate([g01[0], g23[0]], axis=1),
                             jnp.concatenate([g01[1], g23[1]], axis=1)], axis=0)
        dl = _dot((x + pe_ref[0, u]).astype(BF), w1_ref[0, u])
        dt = _dot((x + pe_ref[1, u]).astype(BF), w1_ref[1, u])
        lead = dl if lead is None else lead + dl
        trail = dt if trail is None else trail + dt
    return lead, trail


def _compress_prompt_kernel(kc_ref, vc_ref, pek_ref, w1k_ref, w2k_ref, pev_ref, w1v_ref, w2v_ref, out_ref, *,
                            n_half):
    branches = ((kc_ref, pek_ref, w1k_ref, w2k_ref), (vc_ref, pev_ref, w1v_ref, w2v_ref))
    for c, (src_ref, pe_ref, w1_ref, w2_ref) in enumerate(branches):
        def load_rows(s, src_ref=src_ref):
            return src_ref[pl.ds(s, n_half, stride=CMP_STRIDE), :]

        lead, trail = _block_mlp_hidden(load_rows, n_half, pe_ref, w1_ref)
        for g in range(N_KV):
            a0 = lead[g * n_half:(g + 1) * n_half]
            a1 = trail[g * n_half:(g + 1) * n_half]
            pre = a0 + pltpu.roll(a1, n_half - 1, 0)
            tok = _dot(_silu(pre).astype(BF), w2_ref[...])
            out_ref[g, :, c * HEAD_DIM:(c + 1) * HEAD_DIM] = tok


def _compress_prompt(kvc2d, b, t, pek, w1k, w2k, pev, w1v, w2v):
    n_half = t // CMP_STRIDE
    full = lambda a: pl.BlockSpec(a.shape, lambda i: (0,) * a.ndim)
    return pl.pallas_call(
        functools.partial(_compress_prompt_kernel, n_half=n_half),
        grid=(b,),
        in_specs=[pl.BlockSpec((t, LANES), lambda i: (i, 0)), pl.BlockSpec((t, LANES), lambda i: (i, 1)),
                  full(pek), full(w1k), full(w2k), full(pev), full(w1v), full(w2v)],
        out_specs=pl.BlockSpec((None, N_KV, n_half, LANES), lambda i: (i, 0, 0, 0)),
        out_shape=jax.ShapeDtypeStruct((b, N_KV, n_half, LANES), F32),
        compiler_params=_cparams(("parallel",)),
        name="compress_prompt",
    )(kvc2d, kvc2d, pek, w1k, w2k, pev, w1v, w2v)


def _prep_cmp_w(pe, w1, w2):
    n_quads = CMP_STRIDE // CMP_PACK
    return (pe.reshape(2, n_quads, 1, CMP_PACK * HEAD_DIM),
            w1.reshape(2, n_quads, CMP_PACK * HEAD_DIM, CMP_HIDDEN).astype(BF), w2.astype(BF))


def _rank_select(imp_t):
    n_rows, tq = imp_t.shape
    jrow = lax.broadcasted_iota(jnp.int32, (n_rows, tq), 0)
    cnt = jnp.zeros((n_rows, tq), F32)
    for jp in range(n_rows):
        row = imp_t[jp:jp + 1, :]
        ahead = jnp.where(row == imp_t, jnp.where(jrow > jp, 1.0, 0.0), jnp.where(row > imp_t, 1.0, 0.0))
        cnt = cnt + ahead
    return jnp.where(cnt < float(N_SELECT), 1.0, 0.0)


def _heads_on_lanes(ref, g):
    rows = [(g * Q_PER_KV + r) * HEAD_DIM for r in range(Q_PER_KV)]
    return (jnp.concatenate([ref[c:c + HEAD_DIM, :] for c in rows], axis=1) * (SCALE * LOG2E)).astype(BF)


def _per_head(x):
    return jnp.concatenate([x] * Q_PER_KV, axis=1)


def _nsa_prompt_kernel(q_ref, qr_ref, cmp_ref, kvs_ref, kvw_ref, sm_ref, ovt_ref, o_ref, sel_ref, slc_ref, *, t_len,
                       tq, kc):
    t0 = pl.program_id(1) * tq
    hq = Q_PER_KV
    nq = hq * tq
    tcol = t0 + lax.broadcasted_iota(jnp.int32, (1, tq), 1)
    gates_t = jax.nn.sigmoid(sm_ref[...]).T
    tok = lax.broadcasted_iota(jnp.int32, (LANES, tq), 0)
    cmask = _per_head(jnp.where((tok * CMP_STRIDE + (2 * CMP_STRIDE - 1)) <= tcol, 1.0, 0.0)) > 0.5
    n_sel = t_len // SLC_BLOCK
    blk = lax.broadcasted_iota(jnp.int32, (n_sel, tq), 0)
    valid = (blk * SLC_BLOCK) <= tcol
    cur = tcol // SLC_BLOCK
    forced = (blk == 0) | (blk == cur) | (blk == cur - 1)
    krel = lax.broadcasted_iota(jnp.int32, (kc, tq), 0)
    blocks_per_chunk = kc // SLC_BLOCK

    krows = [slice(g * HEAD_DIM, (g + 1) * HEAD_DIM) for g in range(N_KV)]
    vrows = [slice(LANES + g * HEAD_DIM, LANES + (g + 1) * HEAD_DIM) for g in range(N_KV)]

    o_cmp = []
    for g in range(N_KV):
        s = jnp.where(cmask, _dot(cmp_ref[g, :, 0:HEAD_DIM].astype(BF), _heads_on_lanes(q_ref, g)), NEG_BIG)
        e = jnp.where(cmask, jnp.exp2(s - jnp.max(s, axis=0, keepdims=True)), 0.0)
        d = jnp.sum(e, axis=0, keepdims=True)
        p = e / jnp.where(d > 0.0, d, 1.0)
        o_cmp.append(_dot_tn(cmp_ref[g, :, HEAD_DIM:LANES].astype(BF), p.astype(BF)))
        psum = p[:, 0:tq]
        for r in range(1, hq):
            psum = psum + p[:, r * tq:(r + 1) * tq]
        imp = _dot3_left(ovt_ref[...], psum)[0:n_sel, :]
        imp = jnp.where(valid, jnp.where(forced, FORCE_SCORE, imp), -FORCE_SCORE)
        sel_ref[g] = _rank_select(imp)

    qs = [_heads_on_lanes(qr_ref, g) for g in range(N_KV)]

    aug = BF16_SUBLANES
    blk_onehot = (lax.broadcasted_iota(jnp.int32, (aug, kc), 1) // SLC_BLOCK
                  == lax.broadcasted_iota(jnp.int32, (aug, kc), 0))
    blk_onehot = jnp.where(blk_onehot, 1.0, 0.0).astype(BF)

    pad_rows = jnp.zeros((aug - blocks_per_chunk, nq), F32)

    def chunk(ci, state, diagonal):
        k0 = ci * kc
        scores = []
        for g in range(N_KV):
            sel_c = sel_ref[g, ci * blocks_per_chunk:(ci + 1) * blocks_per_chunk, :]
            sel_bias = _per_head(jnp.where(sel_c > 0.5, 0.0, NEG_BIG))
            q_aug = jnp.concatenate([qs[g], jnp.concatenate([sel_bias, pad_rows], axis=0).astype(BF)], axis=0)
            k_aug = jnp.concatenate([kvs_ref[krows[g], k0:k0 + kc].astype(BF), blk_onehot], axis=0)
            scores.append(_dot_tn(k_aug, q_aug))
        causal_bias = _per_head(jnp.where(k0 + krel <= tcol, 0.0, NEG_BIG)) if diagonal else None
        out = []
        for g in range(N_KV):
            m, l, acc = state[g]
            vt = kvs_ref[vrows[g], k0:k0 + kc].astype(BF)
            s = scores[g] + causal_bias if diagonal else scores[g]
            m_new = jnp.maximum(m, jnp.max(s, axis=0, keepdims=True))
            alpha = jnp.exp2(m - m_new)
            e = jnp.exp2(s - m_new)
            l = alpha * l + jnp.sum(e, axis=0, keepdims=True)
            out.append((m_new, l, alpha * acc + _dot(vt, e.astype(BF))))
        return tuple(out)

    init = (jnp.full((1, nq), NEG_BIG, F32), jnp.zeros((1, nq), F32), jnp.zeros((HEAD_DIM, nq), F32))
    n_chunks = (t0 + tq - 1) // kc + 1
    for count in range(1, t_len // kc + 1):
        @pl.when(n_chunks == count)
        def _(count=count):
            state = (init,) * N_KV
            for ci in range(count):
                state = chunk(ci, state, ci == count - 1)
            for g in range(N_KV):
                slc_ref[g] = state[g][2] / state[g][1]

    tw = LANES
    n_win = WINDOW + tw
    o_win_parts = [[] for _ in range(N_KV)]
    for sub in range(tq // tw):
        start = pl.multiple_of(jnp.clip(t0 + sub * tw - WINDOW, 0, t_len - n_win), LANES)
        dist = tcol[:, sub * tw:(sub + 1) * tw] - (start + lax.broadcasted_iota(jnp.int32, (n_win, tw), 0))
        wbias = _per_head(jnp.where(dist >= 0, jnp.where(dist < WINDOW, 0.0, NEG_BIG), NEG_BIG))
        qsub = [jnp.concatenate([qs[g][:, r * tq + sub * tw:r * tq + (sub + 1) * tw] for r in range(hq)], axis=1)
                for g in range(N_KV)]
        wscores = [_dot_tn(kvw_ref[krows[g], pl.ds(start, n_win)].astype(BF), qsub[g]) for g in range(N_KV)]
        for g in range(N_KV):
            s = wscores[g] + wbias
            e = jnp.exp2(s - jnp.max(s, axis=0, keepdims=True))
            o_win_parts[g].append(_dot(kvw_ref[vrows[g], pl.ds(start, n_win)].astype(BF), e.astype(BF))
                                  / jnp.sum(e, axis=0, keepdims=True))
    def gate(h, branch):
        row = N_BRANCH * h + branch
        return gates_t[row:row + 1, :]

    for g in range(N_KV):
        o_slc = slc_ref[g]
        o_win = jnp.concatenate([o_win_parts[g][sub][:, r * tw:(r + 1) * tw]
                                 for r in range(hq) for sub in range(tq // tw)], axis=1)
        for r in range(hq):
            h = g * hq + r
            cols = slice(r * tq, (r + 1) * tq)
            o_ref[h * HEAD_DIM:(h + 1) * HEAD_DIM, :] = (
                gate(h, 0) * o_cmp[g][:, cols] + gate(h, 1) * o_slc[:, cols] + gate(h, 2) * o_win[:, cols])


def _overlap_matrix(n_rows, n_cols, row_shift):
    n = jnp.arange(n_rows)[:, None] - row_shift
    c_start = n * CMP_STRIDE
    s_start = jnp.arange(n_cols)[None, :] * SLC_BLOCK
    ov = (c_start < s_start + SLC_BLOCK) & (c_start + 2 * CMP_STRIDE > s_start) & (n >= 0)
    return ov.astype(BF)


def _nsa_prompt(q_t, qr_t, cmp, kvs_t, kvw_t, small, b, t):
    tq = TILE_NSA_Q
    kc = TILE_NSA_K
    nq = t // tq
    n_half = t // CMP_STRIDE
    n_sel = t // SLC_BLOCK
    assert n_half == LANES and n_sel <= LANES and t % kc == 0 and kc % tq == 0 and tq % LANES == 0
    ov_t = _overlap_matrix(LANES, LANES, 0).T
    qtile = pl.BlockSpec((None, Q_DIM, tq), lambda i, j: (i, 0, j))
    seq = pl.BlockSpec((None, KV_DIM, t), lambda i, j: (i, 0, 0))
    return pl.pallas_call(
        functools.partial(_nsa_prompt_kernel, t_len=t, tq=tq, kc=kc),
        grid=(b, nq),
        in_specs=[qtile, qtile,
                  pl.BlockSpec((None, N_KV, n_half, LANES), lambda i, j: (i, 0, 0, 0)),
                  seq, seq, pl.BlockSpec((tq, LANES), lambda i, j: (i * nq + j, 0)),
                  pl.BlockSpec(ov_t.shape, lambda i, j: (0, 0))],
        out_specs=qtile,
        out_shape=jax.ShapeDtypeStruct((b, Q_DIM, t), F32),
        scratch_shapes=[pltpu.VMEM((N_KV, n_sel, tq), F32), pltpu.VMEM((N_KV, HEAD_DIM, Q_PER_KV * tq), F32)],
        compiler_params=_cparams(("parallel", "parallel")),
        name="nsa_prompt",
    )(q_t, qr_t, cmp, kvs_t, kvw_t, small, ov_t)


def _ssd_prompt_kernel(zx_ref, sm_ref, cw_ref, cb_ref, dtb_ref, alog_ref, dskip_ref, nrm_ref, tri_ref,
                       y_ref, hout_ref, tail_ref, h_ref, *, n_sub):
    lc = SSD_CHUNK

    @pl.when(pl.program_id(1) == 0)
    def _():
        tail_ref[...] = jnp.zeros_like(tail_ref)
        h_ref[...] = jnp.zeros_like(h_ref)

    li = lax.broadcasted_iota(jnp.int32, (lc, lc), 0)
    si = lax.broadcasted_iota(jnp.int32, (lc, lc), 1)
    lower = li >= si
    heads_per_group = SSD_HEADS // 2
    tail = tail_ref[...]
    state = [h_ref[h] for h in range(SSD_HEADS)]
    for sc in range(n_sub):
        rows = slice(sc * lc, (sc + 1) * lc)
        z = zx_ref[rows, 0:SSD_INNER]
        xbc = zx_ref[rows, SSD_INNER:]
        u = jnp.concatenate([tail, xbc], axis=0)
        tail = xbc[lc - SUBLANES:lc, :]
        conv = cb_ref[...]
        for k in range(CONV_WIDTH):
            off = SUBLANES - (CONV_WIDTH - 1) + k
            conv = conv + u[off:off + lc, :] * cw_ref[k:k + 1, :]
        xbc_c = _silu(conv)
        xs = xbc_c[:, 0:SSD_INNER]

        dt_full = _softplus(sm_ref[rows, :] + dtb_ref[...])
        a_full = dt_full * (-jnp.exp(alog_ref[...]))
        acs = _dot3_left(tri_ref[...], a_full)
        acs_t = acs.T

        bm_b = [xbc_c[:, SSD_INNER + g * SSD_STATE:SSD_INNER + (g + 1) * SSD_STATE].astype(BF) for g in range(2)]
        cm_b = [xbc_c[:, SSD_INNER + (2 + g) * SSD_STATE:SSD_INNER + (3 + g) * SSD_STATE].astype(BF)
                for g in range(2)]
        cb = [_dot_nt(cm_b[g], bm_b[g]) for g in range(2)]
        prep = []
        for h in range(SSD_HEADS):
            col = DT_COL + h
            acs_h = acs[:, col:col + 1]
            acs_last = acs[lc - 1:lc, col:col + 1]
            lmat = jnp.where(lower, jnp.exp(acs_h - acs_t[col:col + 1, :]), 0.0)
            xdt = xs[:, h * SSD_HEAD_DIM:(h + 1) * SSD_HEAD_DIM] * dt_full[:, col:col + 1]
            prep.append(((cb[h // heads_per_group] * lmat).astype(BF), xdt.astype(BF),
                         (xdt * jnp.exp(acs_last - acs_h)).astype(BF), jnp.exp(acs_h), jnp.exp(acs_last)))
        ys = []
        for h in range(SSD_HEADS):
            g = h // heads_per_group
            m_h, xdt_b, xdec_b, grow, glast = prep[h]
            ys.append(_dot(m_h, xdt_b) + _dot_nt(cm_b[g], state[h].astype(BF)) * grow)
            state[h] = glast * state[h] + _dot_tn(xdec_b, bm_b[g])
        y = jnp.concatenate(ys, axis=1) + dskip_ref[...] * xs
        y = y * _silu(z)
        y_ref[rows, :] = _rms(y, nrm_ref[...])
    tail_ref[...] = tail
    for h in range(SSD_HEADS):
        h_ref[h] = state[h]
        hout_ref[h] = state[h]


def _ssd_params(conv_w, conv_b, dt_bias, a_log, d_skip, ssd_norm):
    pad = lambda v: jnp.pad(v, (DT_COL, LANES - DT_COL - SSD_HEADS)).reshape(1, LANES)
    return (conv_w, conv_b.reshape(1, -1), pad(dt_bias), pad(a_log),
            jnp.repeat(d_skip, SSD_HEAD_DIM).reshape(1, SSD_INNER), ssd_norm.reshape(1, SSD_INNER))


def _ssd_prompt(zx, small, params, b, t):
    lc = SSD_CHUNK
    n_sub = SSD_CHUNKS_PER_STEP
    nc = t // (lc * n_sub)
    cw, cb, dtb, alog, dskip, nrm = params
    tri = (jnp.arange(lc)[:, None] >= jnp.arange(lc)[None, :]).astype(BF)
    zx_dim = zx.shape[1]
    conv_dim = zx_dim - SSD_INNER
    full = lambda a: pl.BlockSpec(a.shape, lambda i, j: (0,) * a.ndim)
    tile = lambda w: pl.BlockSpec((lc * n_sub, w), lambda i, j: (i * nc + j, 0))
    return pl.pallas_call(
        functools.partial(_ssd_prompt_kernel, n_sub=n_sub),
        grid=(b, nc),
        in_specs=[tile(zx_dim), tile(LANES), full(cw), full(cb), full(dtb), full(alog), full(dskip), full(nrm),
                  full(tri)],
        out_specs=[tile(SSD_INNER),
                   pl.BlockSpec((None, SSD_HEADS, SSD_HEAD_DIM, SSD_STATE), lambda i, j: (i, 0, 0, 0))],
        out_shape=[jax.ShapeDtypeStruct((b * t, SSD_INNER), F32),
                   jax.ShapeDtypeStruct((b, SSD_HEADS, SSD_HEAD_DIM, SSD_STATE), F32)],
        scratch_shapes=[pltpu.VMEM((SUBLANES, conv_dim), F32),
                        pltpu.VMEM((SSD_HEADS, SSD_HEAD_DIM, SSD_STATE), F32)],
        compiler_params=_cparams(("parallel", "arbitrary")),
        name="ssd_prompt",
    )(zx, small, cw, cb, dtb, alog, dskip, nrm, tri)


def _ffn_kernel(x_ref, *refs, final_norm, mixer_proj, a_feature_major):
    x = x_ref[...]
    if mixer_proj:
        a_ref, b_ref, wa_ref, wb_ref = refs[:4]
        refs = refs[4:]
        a = a_ref[...].astype(BF)
        x = x + (_dot_tn(a, wa_ref[...]) if a_feature_major else _dot(a, wa_ref[...]))
        x = x + _dot(b_ref[...].astype(BF), wb_ref[...])
    g_ref, wg_ref, wu_ref, wd_ref, gf_ref, o_ref = refs
    xn = _rms(x, g_ref[...]).astype(BF)
    h = _silu(_dot(xn, wg_ref[...])) * _dot(xn, wu_ref[...])
    y = x + _dot(h.astype(BF), wd_ref[...])
    if final_norm:
        y = _rms(y, gf_ref[...])
    o_ref[...] = y


def _ffn(x2d, g, wg, wu, wd, gf, tm, final_norm, in_map, out_map, out_2d_shape, mixer=None):
    m, d = x2d.shape[0] * x2d.shape[1] // wg.shape[0], wg.shape[0]
    n_steps = m // tm
    full = lambda arr: pl.BlockSpec(arr.shape, lambda i: (0, 0), pipeline_mode=pl.Buffered(1))
    args = [x2d]
    in_specs = [pl.BlockSpec((tm, d), in_map)]
    a_feature_major = mixer is not None and mixer[0].ndim == 3
    if mixer is not None:
        a, bb, wa, wb = mixer
        args += [a, bb, wa, wb]
        if a_feature_major:
            nt = a.shape[2] // tm
            a_spec = pl.BlockSpec((None, a.shape[1], tm), lambda i: (i // nt, 0, i % nt))
        else:
            a_spec = pl.BlockSpec((tm, a.shape[1]), in_map)
        in_specs += [a_spec, pl.BlockSpec((tm, bb.shape[1]), in_map), full(wa), full(wb)]
    args += [g, wg, wu, wd, gf]
    in_specs += [full(g), full(wg), full(wu), full(wd), full(gf)]
    return pl.pallas_call(
        functools.partial(_ffn_kernel, final_norm=final_norm, mixer_proj=mixer is not None,
                          a_feature_major=a_feature_major),
        grid=(n_steps,),
        in_specs=in_specs,
        out_specs=pl.BlockSpec((tm, d), out_map),
        out_shape=jax.ShapeDtypeStruct(out_2d_shape, F32),
        compiler_params=_cparams(("parallel",)),
        name="ffn_final" if final_norm else "ffn",
    )(*args)


def _lru_gates(xc, wa_ref, ba_ref, wx_ref, bx_ref, sp):
    n_heads = xc.shape[1] // LRU_BLOCK
    a_parts = []
    gx_parts = []
    for h in range(n_heads):
        sl = slice(h * LRU_BLOCK, (h + 1) * LRU_BLOCK)
        xh = xc[:, sl]
        xb = xh.astype(BF)
        r = jax.nn.sigmoid(_dot(xb, wa_ref[h]) + ba_ref[:, sl])
        i = jax.nn.sigmoid(_dot(xb, wx_ref[h]) + bx_ref[:, sl])
        log_a = -LRU_C * r * sp[:, sl]
        a = jnp.exp(log_a)
        a_parts.append(a)
        gx_parts.append(jnp.sqrt(1.0 - a * a) * (i * xh))
    return jnp.concatenate(a_parts, axis=1), jnp.concatenate(gx_parts, axis=1)


def _lru_prompt_kernel(x_ref, g_ref, win_ref, cw_ref, cb_ref, wa_ref, ba_ref, wx_ref, bx_ref, lam_ref, wo_ref,
                       o_ref, hout_ref, cout_ref, tail_ref, h_ref, a_s, gx_s, hs_s, u_s, *, rows, w):
    @pl.when(pl.program_id(1) == 0)
    def _():
        tail_ref[...] = jnp.zeros_like(tail_ref)
        h_ref[...] = jnp.zeros_like(h_ref)

    x = x_ref[...]
    xn = _rms(x, g_ref[...]).astype(BF)
    proj = _dot(xn, win_ref[...])
    gate_br = proj[:, 0:w]
    x_br = proj[:, w:2 * w]
    u_s[0:SUBLANES, :] = tail_ref[...]
    u_s[SUBLANES:SUBLANES + rows, :] = x_br
    tail_ref[...] = x_br[rows - SUBLANES:rows, :]
    cout_ref[...] = x_br[rows - (CONV_WIDTH - 1):rows, :]
    xc = cb_ref[...]
    for k in range(CONV_WIDTH):
        off = SUBLANES - (CONV_WIDTH - 1) + k
        xc = xc + u_s[off:off + rows, :] * cw_ref[k:k + 1, :]
    sp = _softplus(-lam_ref[...])
    a, gx = _lru_gates(xc, wa_ref, ba_ref, wx_ref, bx_ref, sp)
    a_s[...] = a
    gx_s[...] = gx
    sub = lax.broadcasted_iota(jnp.int32, (SUBLANES, w), 0)
    h = h_ref[...]
    for i in range(rows // SUBLANES):
        blk = slice(i * SUBLANES, (i + 1) * SUBLANES)
        ac = a_s[blk, :]
        bc = gx_s[blk, :]
        for dd in (1, 2, 4):
            a_sh = jnp.where(sub >= dd, pltpu.roll(ac, dd, 0), 1.0)
            b_sh = jnp.where(sub >= dd, pltpu.roll(bc, dd, 0), 0.0)
            bc = ac * b_sh + bc
            ac = ac * a_sh
        hb = ac * h + bc
        hs_s[blk, :] = hb
        h = jnp.broadcast_to(hb[SUBLANES - 1:SUBLANES, :], (SUBLANES, w))
    h_ref[...] = h
    hout_ref[...] = h[0:1, :]
    y = (jax.nn.gelu(gate_br) * hs_s[...]).astype(BF)
    o_ref[...] = x + _dot(y, wo_ref[...])


def _lru_prompt(x2d, g, w_in, cw, cb, wa, ba, wx, bx, lam, wo, nb, rows):
    m, d = x2d.shape
    w = cw.shape[1]
    nt = m // nb // rows
    full = lambda a: pl.BlockSpec(a.shape, lambda i, j: (0,) * a.ndim)
    tile = pl.BlockSpec((rows, d), lambda i, j: (i * nt + j, 0))
    args = (g, w_in, cw, cb, wa, ba, wx, bx, lam, wo)
    return pl.pallas_call(
        functools.partial(_lru_prompt_kernel, rows=rows, w=w),
        grid=(nb, nt),
        in_specs=[tile] + [full(a) for a in args],
        out_specs=[tile,
                   pl.BlockSpec((None, 1, w), lambda i, j: (i, 0, 0)),
                   pl.BlockSpec((None, CONV_WIDTH - 1, w), lambda i, j: (i, 0, 0))],
        out_shape=[jax.ShapeDtypeStruct((m, d), F32),
                   jax.ShapeDtypeStruct((nb, 1, w), F32),
                   jax.ShapeDtypeStruct((nb, CONV_WIDTH - 1, w), F32)],
        scratch_shapes=[pltpu.VMEM((SUBLANES, w), F32), pltpu.VMEM((SUBLANES, w), F32),
                        pltpu.VMEM((rows, w), F32), pltpu.VMEM((rows, w), F32), pltpu.VMEM((rows, w), F32),
                        pltpu.VMEM((SUBLANES + rows, w), F32)],
        compiler_params=_cparams(("parallel", "arbitrary")),
        name="lru_prompt",
    )(x2d, *args)


def _lru_params(norm_g, w_in_c, conv_w, conv_b, w_a, b_a, w_x, b_x, lam, w_out_c):
    r = lambda v: v.reshape(1, -1)
    return (r(norm_g), w_in_c.astype(BF), conv_w, r(conv_b), w_a.astype(BF), r(b_a), w_x.astype(BF), r(b_x),
            r(lam), w_out_c.astype(BF))


def _lru_decode_kernel(x_ref, g_ref, win_ref, cw_ref, cb_ref, wa_ref, ba_ref, wx_ref, bx_ref, lam_ref, wo_ref,
                       c0_ref, c1_ref, c2_ref, h0_ref, o_ref, hout_ref, xbr_ref, *, w):
    x = x_ref[...]
    xn = _rms(x, g_ref[...]).astype(BF)
    proj = _dot(xn, win_ref[...])
    gate_br = proj[:, 0:w]
    x_br = proj[:, w:2 * w]
    xbr_ref[...] = x_br
    xc = (cb_ref[...] + c0_ref[...] * cw_ref[0:1, :] + c1_ref[...] * cw_ref[1:2, :]
          + c2_ref[...] * cw_ref[2:3, :] + x_br * cw_ref[3:4, :])
    sp = _softplus(-lam_ref[...])
    a, gx = _lru_gates(xc, wa_ref, ba_ref, wx_ref, bx_ref, sp)
    h = a * h0_ref[...] + gx
    hout_ref[...] = h
    y = (jax.nn.gelu(gate_br) * h).astype(BF)
    o_ref[...] = x + _dot(y, wo_ref[...])


def _lru_decode(x, params, conv_state, h0):
    m, d = x.shape
    w = h0.shape[1]
    args = (x,) + tuple(params) + (conv_state[:, 0], conv_state[:, 1], conv_state[:, 2], h0)
    full = lambda a: pl.BlockSpec(a.shape, lambda i: (0,) * a.ndim)
    return pl.pallas_call(
        functools.partial(_lru_decode_kernel, w=w),
        grid=(1,),
        in_specs=[full(a) for a in args],
        out_specs=[pl.BlockSpec((m, d), lambda i: (0, 0)), pl.BlockSpec((m, w), lambda i: (0, 0)),
                   pl.BlockSpec((m, w), lambda i: (0, 0))],
        out_shape=[jax.ShapeDtypeStruct((m, d), F32), jax.ShapeDtypeStruct((m, w), F32),
                   jax.ShapeDtypeStruct((m, w), F32)],
        compiler_params=_cparams(("arbitrary",)),
        name="lru_decode",
    )(*args)


HALVES_PER_PAGE = PAGE_SIZE // CMP_STRIDE


def _compress_decode_kernel(pt_ref, *refs):
    del pt_ref
    np_ = PAGES_PER_STEP
    k_pages = refs[0:np_]
    v_pages = refs[np_:2 * np_]
    pek_ref, w1k_ref, w2k_ref, pev_ref, w1v_ref, w2v_ref, out_ref, carry_ref, rows_ref = refs[2 * np_:]
    rows = np_ * HALVES_PER_PAGE

    @pl.when(pl.program_id(1) == 0)
    def _():
        carry_ref[...] = jnp.zeros_like(carry_ref)

    rowi = lax.broadcasted_iota(jnp.int32, (rows, CMP_HIDDEN), 0)
    branches = ((k_pages, pek_ref, w1k_ref, w2k_ref), (v_pages, pev_ref, w1v_ref, w2v_ref))
    for c, (pages, _, _, _) in enumerate(branches):
        for k, p in enumerate(pages):
            rows_ref[c, k * PAGE_SIZE:(k + 1) * PAGE_SIZE, :] = p[...].T

    for c, (pages, pe_ref, w1_ref, w2_ref) in enumerate(branches):
        def load_rows(s, c=c):
            return rows_ref[c, pl.ds(s, rows, stride=CMP_STRIDE), :]

        lead, trail = _block_mlp_hidden(load_rows, rows, pe_ref, w1_ref)
        for g in range(N_KV):
            a0 = lead[g * rows:(g + 1) * rows]
            a1 = trail[g * rows:(g + 1) * rows]
            slot = c * N_KV + g
            prev = jnp.where(rowi == 0, carry_ref[slot, SUBLANES - 1:SUBLANES, :], pltpu.roll(a0, 1, 0))
            carry_ref[slot] = a0[rows - SUBLANES:rows, :]
            tok = _dot(_silu(prev + a1).astype(BF), w2_ref[...])
            out_ref[g, :, c * HEAD_DIM:(c + 1) * HEAD_DIM] = tok


def _compress_decode(cache_t, pt_flat, db, n_pages, pek, w1k, w2k, pev, w1v, w2v):
    np_ = PAGES_PER_STEP
    n_steps = n_pages // np_
    rows = np_ * HALVES_PER_PAGE
    n_half = n_pages * HALVES_PER_PAGE

    def page_spec(k, rowblk):
        return pl.BlockSpec((None, LANES, PAGE_SIZE),
                            lambda i, j, pt: (pt[i * n_pages + j * np_ + k], rowblk, 0))

    full = lambda a: pl.BlockSpec(a.shape, lambda i, j, pt: (0,) * a.ndim)
    grid_spec = pltpu.PrefetchScalarGridSpec(
        num_scalar_prefetch=1,
        grid=(db, n_steps),
        in_specs=[page_spec(k, 0) for k in range(np_)] + [page_spec(k, 1) for k in range(np_)]
        + [full(pek), full(w1k), full(w2k), full(pev), full(w1v), full(w2v)],
        out_specs=pl.BlockSpec((None, N_KV, rows, LANES), lambda i, j, pt: (i, 0, j, 0)),
        scratch_shapes=[pltpu.VMEM((2 * N_KV, SUBLANES, CMP_HIDDEN), F32),
                        pltpu.VMEM((2, np_ * PAGE_SIZE, LANES), F32)],
    )
    return pl.pallas_call(
        _compress_decode_kernel,
        grid_spec=grid_spec,
        out_shape=jax.ShapeDtypeStruct((db, N_KV, n_half, LANES), F32),
        compiler_params=_cparams(("parallel", "arbitrary")),
        name="compress_decode",
    )(pt_flat, *([cache_t] * (2 * np_)), pek, w1k, w2k, pev, w1v, w2v)


def _group_rows(top, a, b):
    return jnp.where(top, a, b)


def _nsa_decode_a_kernel(q_ref, qr_ref, cmp_ref, win_ref, new_ref, ov_ref, ocmp_ref, owin_ref, idx_ref, *,
                         pos, n_blocks):
    nh = N_HEADS
    top = lax.broadcasted_iota(jnp.int32, (nh, 1), 0) < Q_PER_KV
    q = q_ref[...].astype(BF)
    n_tok = cmp_ref.shape[1]
    lane = lax.broadcasted_iota(jnp.int32, (nh, n_tok), 1)
    cmask = (lane >= 1) & (((lane - 1) * CMP_STRIDE + (2 * CMP_STRIDE - 1)) <= pos)
    s = _group_rows(top, _dot_nt(q, cmp_ref[0, :, 0:HEAD_DIM].astype(BF)),
                    _dot_nt(q, cmp_ref[1, :, 0:HEAD_DIM].astype(BF))) * SCALE
    p = _msoftmax(s, cmask)
    pb = p.astype(BF)
    ocmp_ref[...] = _group_rows(top, _dot(pb, cmp_ref[0, :, HEAD_DIM:LANES].astype(BF)),
                                _dot(pb, cmp_ref[1, :, HEAD_DIM:LANES].astype(BF)))

    imp8 = _dot3(p, ov_ref[...])
    nj = ov_ref.shape[1]
    cur = pos // SLC_BLOCK
    j = lax.broadcasted_iota(jnp.int32, (1, nj), 1)
    valid = (j * SLC_BLOCK) <= pos
    forced = (j == 0) | (j == cur) | (j == cur - 1)
    ri = lax.broadcasted_iota(jnp.int32, (nj, nj), 0)
    ci = lax.broadcasted_iota(jnp.int32, (nj, nj), 1)
    k_lane = lax.broadcasted_iota(jnp.int32, (nj, LANES), 1).astype(F32)
    jvals = lax.broadcasted_iota(jnp.int32, (8, nj), 1).astype(F32).astype(BF)
    idx_rows = []
    for g in range(N_KV):
        imp = jnp.sum(imp8[g * Q_PER_KV:(g + 1) * Q_PER_KV, :], axis=0, keepdims=True)
        imp = jnp.where(valid & forced, FORCE_SCORE, imp)
        imp = jnp.where(valid, imp, -FORCE_SCORE)
        imp = jnp.where(j < n_blocks, imp, F32_LOWEST)
        impb = jnp.broadcast_to(imp, (nj, nj))
        col = jnp.sum(jnp.where(ri == ci, impb, 0.0), axis=1, keepdims=True)
        ahead = jnp.where(impb == col, jnp.where(ci < ri, 1.0, 0.0), jnp.where(impb > col, 1.0, 0.0))
        rank_col = jnp.sum(ahead, axis=1, keepdims=True)
        onehot = jnp.where(rank_col == k_lane, 1.0, 0.0).astype(BF)
        idx_rows.append(_dot(jvals, onehot)[0:1, :])
    idx = jnp.concatenate(idx_rows + [jnp.zeros((SUBLANES - N_KV, LANES), F32)], axis=0)
    idx_ref[...] = idx.astype(jnp.int32)

    qr = qr_ref[...]
    qrb = qr.astype(BF)
    n_win = win_ref.shape[1]
    s = _group_rows(top, _dot(qrb, win_ref[0:HEAD_DIM, :].astype(BF)),
                    _dot(qrb, win_ref[HEAD_DIM:LANES, :].astype(BF))) * SCALE
    wl = lax.broadcasted_iota(jnp.int32, (nh, n_win), 1)
    dist = n_win - wl
    wmask = (dist >= 0) & (dist < WINDOW)
    new = new_ref[...]
    knew = _group_rows(top, new[:, 0:HEAD_DIM], new[:, HEAD_DIM:LANES])
    vnew = _group_rows(top, new[:, LANES:LANES + HEAD_DIM], new[:, LANES + HEAD_DIM:KV_DIM])
    s_new = jnp.sum(qr * knew, axis=1, keepdims=True) * SCALE
    sm = jnp.where(wmask, s, NEG_BIG)
    m = jnp.maximum(jnp.max(sm, axis=1, keepdims=True), s_new)
    e = jnp.where(wmask, jnp.exp(sm - m), 0.0)
    e_new = jnp.exp(s_new - m)
    d = jnp.sum(e, axis=1, keepdims=True) + e_new
    eb = e.astype(BF)
    o = _group_rows(top, _dot_nt(eb, win_ref[LANES:LANES + HEAD_DIM, :].astype(BF)),
                    _dot_nt(eb, win_ref[LANES + HEAD_DIM:KV_DIM, :].astype(BF))) + e_new * vnew
    owin_ref[...] = o / d


def _nsa_decode_a(q3, qr3, cmp_s, win, kvw_new3, pos):
    db = q3.shape[0]
    n_tok = cmp_s.shape[2]
    n_blocks = -(-(pos + 1) // SLC_BLOCK)
    nj = -(-n_blocks // LANES) * LANES
    ov = _overlap_matrix(n_tok, nj, 1)
    per_b = lambda a: pl.BlockSpec((None,) + a.shape[1:], lambda i: (i,) + (0,) * (a.ndim - 1))
    head = pl.BlockSpec((None, N_HEADS, HEAD_DIM), lambda i: (i, 0, 0))
    return pl.pallas_call(
        functools.partial(_nsa_decode_a_kernel, pos=pos, n_blocks=n_blocks),
        grid=(db,),
        in_specs=[head, head, per_b(cmp_s), per_b(win), per_b(kvw_new3), pl.BlockSpec(ov.shape, lambda i: (0, 0))],
        out_specs=[head, head, pl.BlockSpec((None, SUBLANES, LANES), lambda i: (i, 0, 0))],
        out_shape=[jax.ShapeDtypeStruct((db, N_HEADS, HEAD_DIM), F32),
                   jax.ShapeDtypeStruct((db, N_HEADS, HEAD_DIM), F32),
                   jax.ShapeDtypeStruct((db, SUBLANES, LANES), jnp.int32)],
        compiler_params=_cparams(("parallel",)),
        name="nsa_decode_select",
    )(q3, qr3, cmp_s, win, kvw_new3, ov)


def _nsa_decode_b_kernel(idx_ref, pt_ref, qr_ref, *refs, n_past_blocks, cur):
    del pt_ref
    nb = N_KV * N_SELECT
    blk_refs = refs[0:nb]
    new_ref, ocmp_ref, owin_ref, sm_ref, o_ref = refs[nb:]
    b = pl.program_id(0)
    nh = N_HEADS
    top = lax.broadcasted_iota(jnp.int32, (nh, 1), 0) < Q_PER_KV
    qr = qr_ref[...]
    qrb = qr.astype(BF)
    per_page = PAGE_SIZE // SLC_BLOCK
    n_keys = N_SELECT * PAGE_SIZE
    lane = lax.broadcasted_iota(jnp.int32, (nh, n_keys), 1)
    lane_page = lane // PAGE_SIZE
    lane_sub = (lane // SLC_BLOCK) % per_page
    s_g = []
    v_g = []
    ok_g = []
    has_new = []
    for g in range(N_KV):
        blks = blk_refs[g * N_SELECT:(g + 1) * N_SELECT]
        kt = jnp.concatenate([r[g * HEAD_DIM:(g + 1) * HEAD_DIM, :] for r in blks], axis=1).astype(BF)
        v_g.append(jnp.concatenate([r[LANES + g * HEAD_DIM:LANES + (g + 1) * HEAD_DIM, :] for r in blks],
                                   axis=1).astype(BF))
        s_g.append(_dot(qrb, kt))
        ok = jnp.zeros((nh, n_keys), F32)
        new_sel = jnp.zeros((), F32)
        for k in range(N_SELECT):
            jk = idx_ref[(b * N_KV + g) * N_SELECT + k]
            hit = (lane_page == k) & (lane_sub == jk % per_page)
            ok = jnp.where(hit, jnp.where(jk < n_past_blocks, 1.0, 0.0), ok)
            new_sel = jnp.maximum(new_sel, jnp.where(jk == cur, 1.0, 0.0))
        ok_g.append(ok)
        has_new.append(new_sel)
    s = _group_rows(top, s_g[0], s_g[1]) * SCALE
    mask = _group_rows(top, ok_g[0], ok_g[1]) > 0.5
    new_on = _group_rows(top, has_new[0], has_new[1]) > 0.5
    new = new_ref[...]
    knew = _group_rows(top, new[:, 0:HEAD_DIM], new[:, HEAD_DIM:LANES])
    vnew = _group_rows(top, new[:, LANES:LANES + HEAD_DIM], new[:, LANES + HEAD_DIM:KV_DIM])
    s_new = jnp.where(new_on, jnp.sum(qr * knew, axis=1, keepdims=True) * SCALE, NEG_BIG)
    sm = jnp.where(mask, s, NEG_BIG)
    m = jnp.maximum(jnp.max(sm, axis=1, keepdims=True), s_new)
    e = jnp.where(mask, jnp.exp(sm - m), 0.0)
    e_new = jnp.where(new_on, jnp.exp(s_new - m), 0.0)
    d = jnp.sum(e, axis=1, keepdims=True) + e_new
    eb = e.astype(BF)
    o_slc = ((_group_rows(top, _dot_nt(eb, v_g[0]), _dot_nt(eb, v_g[1])) + e_new * vnew)
             / jnp.where(d > 0.0, d, 1.0))

    sig = jnp.broadcast_to(jax.nn.sigmoid(sm_ref[...]), (nh, LANES))
    hl = lax.broadcasted_iota(jnp.int32, (nh, LANES), 1)
    hr = lax.broadcasted_iota(jnp.int32, (nh, LANES), 0)

    def gate(br):
        return jnp.sum(jnp.where(hl == N_BRANCH * hr + br, sig, 0.0), axis=1, keepdims=True)

    o_ref[...] = gate(0) * ocmp_ref[...] + gate(1) * o_slc + gate(2) * owin_ref[...]


def _nsa_decode_b(idx_flat, pt_flat, qr3, slc_pages_t, kvs_new3, o_cmp, o_win, small3, pos, n_pages):
    db = qr3.shape[0]
    n_past_blocks = pos // SLC_BLOCK
    cur = pos // SLC_BLOCK
    per_page = PAGE_SIZE // SLC_BLOCK

    def blk_spec(g, k):
        def imap(i, idx, pt):
            jk = jnp.minimum(idx[(i * N_KV + g) * N_SELECT + k], n_past_blocks - 1)
            return (pt[i * n_pages + jk // per_page], 0, 0)
        return pl.BlockSpec((None, KV_DIM, PAGE_SIZE), imap)

    head = pl.BlockSpec((None, N_HEADS, HEAD_DIM), lambda i, idx, pt: (i, 0, 0))
    row3 = lambda a: pl.BlockSpec((None,) + a.shape[1:], lambda i, idx, pt: (i, 0, 0))
    grid_spec = pltpu.PrefetchScalarGridSpec(
        num_scalar_prefetch=2,
        grid=(db,),
        in_specs=[head] + [blk_spec(g, k) for g in range(N_KV) for k in range(N_SELECT)]
        + [row3(kvs_new3), head, head, row3(small3)],
        out_specs=head,
    )
    return pl.pallas_call(
        functools.partial(_nsa_decode_b_kernel, n_past_blocks=n_past_blocks, cur=cur),
        grid_spec=grid_spec,
        out_shape=jax.ShapeDtypeStruct((db, N_HEADS, HEAD_DIM), F32),
        compiler_params=_cparams(("arbitrary",)),
        name="nsa_decode_attend",
    )(idx_flat, pt_flat, qr3, *([slc_pages_t] * (N_KV * N_SELECT)), kvs_new3, o_cmp, o_win, small3)


def _ssd_decode_kernel(zx_ref, cst_ref, sm_ref, h0_ref, cw_ref, cb_ref, dtb_ref, alog_ref, dskip_ref, nrm_ref,
                       y_ref, hout_ref):
    z = zx_ref[:, 0:SSD_INNER]
    xbc = zx_ref[:, SSD_INNER:]
    conv = cb_ref[...] + xbc * cw_ref[CONV_WIDTH - 1:CONV_WIDTH, :]
    for k in range(CONV_WIDTH - 1):
        conv = conv + cst_ref[k:k + 1, :] * cw_ref[k:k + 1, :]
    xbc_c = _silu(conv)
    xs = xbc_c[:, 0:SSD_INNER]
    dt_full = _softplus(sm_ref[...] + dtb_ref[...])
    da_full = jnp.exp(dt_full * (-jnp.exp(alog_ref[...])))
    p = SSD_HEAD_DIM
    eye = lax.broadcasted_iota(jnp.int32, (p, p), 0) == lax.broadcasted_iota(jnp.int32, (p, p), 1)
    ys = []
    for h in range(SSD_HEADS):
        g = h // (SSD_HEADS // 2)
        col = DT_COL + h
        xdt = xs[:, h * p:(h + 1) * p] * dt_full[:, col:col + 1]
        xcol = jnp.sum(jnp.where(eye, jnp.broadcast_to(xdt, (p, p)), 0.0), axis=1, keepdims=True)
        bm = xbc_c[:, SSD_INNER + g * SSD_STATE:SSD_INNER + (g + 1) * SSD_STATE]
        cm = xbc_c[:, SSD_INNER + (2 + g) * SSD_STATE:SSD_INNER + (3 + g) * SSD_STATE]
        h_new = da_full[:, col:col + 1] * h0_ref[h] + xcol * bm
        hout_ref[h] = h_new
        ys.append(_dot_nt(jnp.broadcast_to(cm, (8, SSD_STATE)).astype(BF), h_new.astype(BF))[0:1, :])
    y = jnp.concatenate(ys, axis=1) + dskip_ref[...] * xs
    y = y * _silu(z)
    y_ref[...] = _rms(y, nrm_ref[...])


def _ssd_decode(zx3, conv_state, small3, h0, params):
    db = zx3.shape[0]
    cw, cb, dtb, alog, dskip, nrm = params
    per_b = lambda a: pl.BlockSpec((None,) + a.shape[1:], lambda i: (i,) + (0,) * (a.ndim - 1))
    full = lambda a: pl.BlockSpec(a.shape, lambda i: (0,) * a.ndim)
    return pl.pallas_call(
        _ssd_decode_kernel,
        grid=(db,),
        in_specs=[per_b(zx3), per_b(conv_state), per_b(small3), per_b(h0), full(cw), full(cb), full(dtb), full(alog),
                  full(dskip), full(nrm)],
        out_specs=[pl.BlockSpec((None, 1, SSD_INNER), lambda i: (i, 0, 0)), per_b(h0)],
        out_shape=[jax.ShapeDtypeStruct((db, 1, SSD_INNER), F32), jax.ShapeDtypeStruct(h0.shape, F32)],
        compiler_params=_cparams(("parallel",)),
        name="ssd_decode",
    )(zx3, conv_state, small3, h0, cw, cb, dtb, alog, dskip, nrm)


def kernel(x_prompt, x_sample, cache_kv_cmp, cache_kv_slc, cache_kv_win, state_ssm, state_ssd_conv, state_lru,
           state_lru_conv, page_table, norm_mix, norm_ffn, norm_final, w_ffn_gate, w_ffn_up, w_ffn_down, w_in_a,
           w_out_a, cmp_pe_k, cmp_w1_k, cmp_w2_k, cmp_pe_v, cmp_w1_v, cmp_w2_v, ssd_conv_w, ssd_conv_b, ssd_dt_bias,
           ssd_a_log, ssd_d, ssd_norm, w_in_c, lru_conv_w, lru_conv_b, lru_w_a, lru_b_a, lru_w_x, lru_b_x,
           lru_lambda, w_out_c):
    b, t, d = x_prompt.shape
    db = x_sample.shape[0]
    n_pages = page_table.shape[1]
    pos_s = n_pages * PAGE_SIZE
    m = b * t
    kv_shape = (2, N_KV, HEAD_DIM)

    wm, ws = _prep_w_in_a(w_in_a[0])
    wo_nsa = w_out_a[0, :Q_DIM].astype(BF)
    wo_ssd = w_out_a[0, Q_DIM:].astype(BF)
    cmp_k = _prep_cmp_w(cmp_pe_k[0], cmp_w1_k[0], cmp_w2_k[0])
    cmp_v = _prep_cmp_w(cmp_pe_v[0], cmp_w1_v[0], cmp_w2_v[0])
    ssd_par = _ssd_params(ssd_conv_w[0], ssd_conv_b[0], ssd_dt_bias[0], ssd_a_log[0], ssd_d[0], ssd_norm[0])
    lru_par = _lru_params(norm_mix[1], w_in_c[0], lru_conv_w[0], lru_conv_b[0], lru_w_a[0], lru_b_a[0], lru_w_x[0],
                          lru_b_x[0], lru_lambda[0], w_out_c[0])
    ffn_w = [(norm_ffn[l].reshape(1, d), w_ffn_gate[l].astype(BF), w_ffn_up[l].astype(BF), w_ffn_down[l].astype(BF))
             for l in range(2)]
    gfin = norm_final.reshape(1, d)

    xp = x_prompt.reshape(m, d)
    cos_p, sin_p = _rope_tables(jnp.arange(t, dtype=jnp.int32))
    q, qr, kvc, kvs_t, kvw_t, zx, small, kvc_t = _inproj_a(xp, norm_mix[0], wm, ws, cos_p, sin_p, TILE_PROJ, b, True)
    cmp_p = _compress_prompt(kvc, b, t, *cmp_k, *cmp_v)
    o_nsa = _nsa_prompt(q, qr, cmp_p, kvs_t, kvw_t, small, b, t)
    y_ssd, ssm_p = _ssd_prompt(zx, small, ssd_par, b, t)
    rowmap = lambda i: (i, 0)
    x2 = _ffn(xp, *ffn_w[0], gfin, TILE_FFN, False, rowmap, rowmap, (m, d), mixer=(o_nsa, y_ssd, wo_nsa, wo_ssd))
    x3, lru_p, lru_conv_p = _lru_prompt(x2, *lru_par, b, TILE_LRU)
    y_prompt = _ffn(x3, *ffn_w[1], gfin, TILE_FFN, True, rowmap, rowmap, (m, d))
    w_keep = min(WINDOW, t)
    to_cache = lambda a: a.reshape((b,) + kv_shape + (a.shape[-1],)).transpose(0, 4, 1, 2, 3)[None]
    kv_cmp_p = to_cache(kvc_t)
    kv_slc_p = to_cache(kvs_t)
    kv_win_p = to_cache(kvw_t[:, :, t - w_keep:])
    ssd_conv_p = zx.reshape(b, t, -1)[:, t - (CONV_WIDTH - 1):, SSD_INNER:][None]
    lru_p = lru_p.reshape(b, -1)
    lru_conv_p = lru_conv_p[None]

    xs = x_sample.reshape(db, d)
    cos_s, sin_s = _rope_tables(jnp.full((db,), pos_s, dtype=jnp.int32))
    q_s, qr_s, kvc_s, kvs_s, kvw_s, zx_s, small_s = _inproj_a(xs, norm_mix[0], wm, ws, cos_s, sin_s, db, 1, False)
    pt_flat = page_table.reshape(-1)
    n_phys = cache_kv_cmp.shape[1]
    feature_major = lambda a: a.transpose(0, 2, 3, 4, 1).reshape(a.shape[0], KV_DIM, a.shape[1])
    cmp_s = _compress_decode(feature_major(cache_kv_cmp[0]), pt_flat, db, n_pages, *cmp_k, *cmp_v)
    win_buf = cache_kv_win[0].reshape(db, -1, KV_DIM)
    head3 = lambda a: a.reshape(db, N_HEADS, HEAD_DIM)
    o_cmp_s, o_win_s, idx = _nsa_decode_a(head3(q_s), head3(qr_s), cmp_s, feature_major(cache_kv_win[0]),
                                          kvw_s.reshape(db, 1, KV_DIM), pos_s)
    idx_flat = idx[:, :N_KV, :N_SELECT].reshape(-1)
    o_nsa_s = _nsa_decode_b(idx_flat, pt_flat, head3(qr_s), feature_major(cache_kv_slc[0]),
                            kvs_s.reshape(db, 1, KV_DIM), o_cmp_s, o_win_s, small_s.reshape(db, 1, LANES), pos_s,
                            n_pages)
    y_ssd_s, ssm_s = _ssd_decode(zx_s.reshape(db, 1, -1), state_ssd_conv[0], small_s.reshape(db, 1, LANES),
                                 state_ssm[0], ssd_par)
    x2_s = _ffn(xs, *ffn_w[0], gfin, db, False, lambda i: (i, 0), lambda i: (i, 0), (db, d),
                mixer=(o_nsa_s.reshape(db, Q_DIM), y_ssd_s.reshape(db, SSD_INNER), wo_nsa, wo_ssd))
    x3_s, lru_s, xbr_s = _lru_decode(x2_s, lru_par, state_lru_conv[0], state_lru[0])
    y_sample = _ffn(x3_s, *ffn_w[1], gfin, db, True, lambda i: (i, 0), lambda i: (i, 0), (db, d))
    kv_cmp_s = kvc_s.reshape((1, db, 1) + kv_shape)
    kv_slc_s = kvs_s.reshape((1, db, 1) + kv_shape)
    kv_win_s = jnp.concatenate([win_buf[:, 1:], kvw_s[:, None, :]], axis=1).reshape(
        (1, db, win_buf.shape[1]) + kv_shape)
    ssd_conv_s = jnp.concatenate([state_ssd_conv[0][:, 1:], zx_s[:, None, SSD_INNER:]], axis=1)[None]
    lru_conv_s = jnp.concatenate([state_lru_conv[0][:, 1:], xbr_s[:, None, :]], axis=1)[None]

    return (y_prompt.reshape(b, t, d), y_sample.reshape(db, 1, d),
            kv_cmp_p, kv_slc_p, kv_win_p, ssm_p[None], ssd_conv_p, lru_p[None], lru_conv_p,
            kv_cmp_s, kv_slc_s, kv_win_s, ssm_s[None], ssd_conv_s, lru_s[None], lru_conv_s)
```
